```python
import jax, jax.numpy as jnp
from jax import lax
import numpy as np

D_MODEL = 1024
BATCH = 16
SEQ = 256
DEPTH = 1
DEC_BATCH = 8
DEC_SEQ = 2048
PAST_LEN = 512

GRID_W = 64
D_RNN = D_MODEL
LRU_BLOCKS = 16
LRU_BW = D_RNN // LRU_BLOCKS
CONV_W = 4
LRU_C = 8.0
GLA_HEADS = 4
GLA_DK = D_MODEL // 2 // GLA_HEADS
GLA_DV = D_MODEL // GLA_HEADS
GLA_DK_TOT = GLA_HEADS * GLA_DK
GLA_DV_TOT = GLA_HEADS * GLA_DV
GLA_RANK = 16
GLA_TAU = 16.0
GLA_CHUNK = 64
D_FF = 4 * D_MODEL
N_MOD = 6
EPS = 1e-6
IN_SPLITS = (D_RNN, D_RNN, GLA_DK_TOT, GLA_DK_TOT, GLA_DV_TOT, GLA_DV_TOT, 2 * GLA_RANK, D_MODEL, D_MODEL)
IN_TOTAL = D_RNN * 2 + GLA_DK_TOT * 2 + GLA_DV_TOT * 2 + 2 * GLA_RANK + 2 * D_MODEL

kernel_name = "hybrid_lru_gla_diffusion_step"


def rmsnorm(x, g):
    xf = x.astype(jnp.float32)
    y = xf * lax.rsqrt(jnp.mean(xf * xf, axis=-1, keepdims=True) + EPS)
    return (y * g.astype(jnp.float32)).astype(x.dtype)


def conv_centred(x, w, b):
    T = x.shape[-2]
    pad = [(0, 0)] * (x.ndim - 2) + [((CONV_W - 1) // 2, CONV_W // 2), (0, 0)]
    xp = jnp.pad(x, pad)
    return b + sum(xp[..., k:k + T, :] * w[k] for k in range(CONV_W))


def blockdiag(x, w):
    xb = x.reshape(*x.shape[:-1], LRU_BLOCKS, LRU_BW)
    return jnp.einsum('...nc,ncd->...nd', xb, w).reshape(x.shape)


def rg_lru_scan(xc, wa, ba, wx, bx, L, h0, reverse):
    if reverse:
        xc = jnp.flip(xc, axis=1)
    r = jax.nn.sigmoid(blockdiag(xc, wa) + ba)
    i = jax.nn.sigmoid(blockdiag(xc, wx) + bx)
    log_a = (-LRU_C * r * jax.nn.softplus(-L)).astype(jnp.float32)
    a = jnp.exp(log_a)
    u = jnp.sqrt(-jnp.expm1(2.0 * log_a)) * (i * xc).astype(jnp.float32)
    u = u.at[:, 0].add(a[:, 0] * h0.astype(jnp.float32))

    def comb(lft, rgt):
        al, bl = lft
        ar, br = rgt
        return al * ar, ar * bl + br

    _, h = lax.associative_scan(comb, (a, u), axis=1)
    final = h[:, -1]
    if reverse:
        h = jnp.flip(h, axis=1)
    return h.astype(xc.dtype), final.astype(xc.dtype)


def gla_chunked(q, k, v, log_a, s0):
    B, T, H, DK = q.shape
    DV = v.shape[-1]
    C = GLA_CHUNK
    N = T // C
    f32 = jnp.float32
    qc = q.astype(f32).reshape(B, N, C, H, DK) * (DK ** -0.5)
    kc = k.astype(f32).reshape(B, N, C, H, DK)
    vc = v.astype(f32).reshape(B, N, C, H, DV)
    bcum = jnp.cumsum(log_a.astype(f32).reshape(B, N, C, H, DK), axis=2)
    b_last = bcum[:, :, -1]
    q_e = qc * jnp.exp(bcum)
    k_e = kc * jnp.exp(-bcum)
    k_tail = kc * jnp.exp(b_last[:, :, None] - bcum)
    mask = jnp.tril(jnp.ones((C, C), dtype=bool))
    scores = jnp.where(mask, jnp.einsum('bnchk,bnshk->bnhcs', q_e, k_e), 0.0)
    o_intra = jnp.einsum('bnhcs,bnshv->bnchv', scores, vc)
    kv = jnp.einsum('bnshk,bnshv->bnhkv', k_tail, vc)
    decay = jnp.exp(b_last)

    def step(S, inp):
        d, upd = inp
        return d[..., None] * S + upd, S

    s_final, s_starts = lax.scan(step, s0.astype(f32), (jnp.moveaxis(decay, 1, 0), jnp.moveaxis(kv, 1, 0)))
    s_starts = jnp.moveaxis(s_starts, 0, 1)
    o_inter = jnp.einsum('bnchk,bnhkv->bnchv', q_e, s_starts)
    o = (o_intra + o_inter).reshape(B, T, H, DV)
    return o.astype(q.dtype), s_final.astype(q.dtype)


def to_col_major(t, rows):
    B, T = t.shape[:2]
    rest = t.shape[2:]
    return t.reshape(B, rows, GRID_W, *rest).swapaxes(1, 2).reshape(B, T, *rest)


def from_col_major(t, rows):
    B, T = t.shape[:2]
    rest = t.shape[2:]
    return t.reshape(B, GRID_W, rows, *rest).swapaxes(1, 2).reshape(B, T, *rest)


def token_mix(h, p, lru_h0, gla_s0, rows):
    B, T, _ = h.shape
    z = h @ p['w_in']
    splits = np.cumsum(IN_SPLITS)[:-1].tolist()
    zx, zg, q, k, v, g, lr, ga, gb = jnp.split(z, splits, axis=-1)

    if rows is None:
        xc = conv_centred(zx, p['conv_w'], p['conv_b'])
    else:
        xc = conv_centred(zx.reshape(B, rows, GRID_W, D_RNN), p['conv_w'], p['conv_b']).reshape(B, T, D_RNN)
    hf, sf = rg_lru_scan(xc, p['lru_wa'][0], p['lru_ba'][0], p['lru_wx'][0], p['lru_bx'][0], p['lru_L'][0], lru_h0[:, 0], False)
    hb, sb = rg_lru_scan(xc, p['lru_wa'][1], p['lru_ba'][1], p['lru_wx'][1], p['lru_bx'][1], p['lru_L'][1], lru_h0[:, 1], True)
    y_a = ((hf + hb) * jax.nn.gelu(zg)) @ p['lru_up']

    q = q.reshape(B, T, GLA_HEADS, GLA_DK)
    k = k.reshape(B, T, GLA_HEADS, GLA_DK)
    v = v.reshape(B, T, GLA_HEADS, GLA_DV)
    la_f = (jax.nn.log_sigmoid(lr[..., :GLA_RANK] @ p['gla_w2'][0] + p['gla_b2'][0]) / GLA_TAU).reshape(B, T, GLA_HEADS, GLA_DK)
    la_b = (jax.nn.log_sigmoid(lr[..., GLA_RANK:] @ p['gla_w2'][1] + p['gla_b2'][1]) / GLA_TAU).reshape(B, T, GLA_HEADS, GLA_DK)
    if rows is not None:
        q, k, v, la_f, la_b = (to_col_major(t, rows) for t in (q, k, v, la_f, la_b))
    o_f, gf = gla_chunked(q, k, v, la_f, gla_s0[:, 0])
    o_b, gbs = gla_chunked(jnp.flip(q, 1), jnp.flip(k, 1), jnp.flip(v, 1), jnp.flip(la_b, 1), gla_s0[:, 1])
    o = o_f + jnp.flip(o_b, 1)
    if rows is not None:
        o = from_col_major(o, rows)
    o = rmsnorm(o, p['gla_norm_g']).reshape(B, T, GLA_DV_TOT) * jax.nn.silu(g)
    y_b = o @ p['gla_up']

    m = (jax.nn.sigmoid(ga) * y_a + jax.nn.sigmoid(gb) * y_b) @ p['w_out']
    return m, jnp.stack([sf, sb], axis=1), jnp.stack([gf, gbs], axis=1)


def layer(x, mod, p, lru_h0, gla_s0, rows):
    sh1, sc1, g1, sh2, sc2, g2 = jnp.split(mod, N_MOD, axis=-1)
    ng = p['norm_g']
    h = rmsnorm(x, ng[0]) * (1.0 + sc1) + sh1
    m, s_lru, s_gla = token_mix(h, p, lru_h0, gla_s0, rows)
    x = x + g1 * rmsnorm(m, ng[1])
    h = rmsnorm(x, ng[2]) * (1.0 + sc2) + sh2
    f = jnp.square(jax.nn.relu(h @ p['mlp_w1'])) @ p['mlp_w2']
    x = x + g2 * rmsnorm(f, ng[3])
    return x, s_lru, s_gla


def setup_inputs(seed: int = 0) -> dict:
    key = jax.random.key(seed)
    ks = jax.random.split(key, 26)
    nrm = jax.random.normal
    D = D_MODEL
    u = jax.random.uniform(ks[15], (DEPTH, 2, D_RNN), minval=0.9, maxval=0.999)
    return {
        'x_prompt': nrm(ks[0], (BATCH, SEQ, D), jnp.float32),
        'x_sample': nrm(ks[1], (DEC_BATCH, DEC_SEQ, D), jnp.float32),
        'state_lru': 0.5 * nrm(ks[2], (DEC_BATCH, DEPTH, 2, D_RNN), jnp.float32),
        'state_gla': nrm(ks[3], (DEC_BATCH, DEPTH, 2, GLA_HEADS, GLA_DK, GLA_DV), jnp.float32),
        'c': nrm(ks[4], (DEC_BATCH, D), jnp.float32),
        'c_ctx': nrm(ks[5], (D,), jnp.float32),
        'w_mod': 0.5 * D ** -0.5 * nrm(ks[6], (DEPTH, D, N_MOD * D), jnp.float32),
        'b_mod': 0.02 * nrm(ks[7], (DEPTH, N_MOD * D), jnp.float32),
        'norm_g': 1.0 + 0.02 * nrm(ks[8], (DEPTH, 4, D), jnp.float32),
        'w_in': D ** -0.5 * nrm(ks[9], (DEPTH, D, IN_TOTAL), jnp.float32),
        'conv_w': CONV_W ** -0.5 * nrm(ks[10], (DEPTH, CONV_W, D_RNN), jnp.float32),
        'conv_b': 0.02 * nrm(ks[11], (DEPTH, D_RNN), jnp.float32),
        'lru_wa': LRU_BW ** -0.5 * nrm(ks[12], (DEPTH, 2, LRU_BLOCKS, LRU_BW, LRU_BW), jnp.float32),
        'lru_ba': 0.02 * nrm(ks[13], (DEPTH, 2, D_RNN), jnp.float32),
        'lru_wx': LRU_BW ** -0.5 * nrm(ks[14], (DEPTH, 2, LRU_BLOCKS, LRU_BW, LRU_BW), jnp.float32),
        'lru_bx': 0.02 * nrm(ks[16], (DEPTH, 2, D_RNN), jnp.float32),
        'lru_L': jnp.log(u) - jnp.log1p(-u),
        'lru_up': D_RNN ** -0.5 * nrm(ks[17], (DEPTH, D_RNN, D), jnp.float32),
        'gla_w2': GLA_RANK ** -0.5 * nrm(ks[18], (DEPTH, 2, GLA_RANK, GLA_DK_TOT), jnp.float32),
        'gla_b2': 0.02 * nrm(ks[19], (DEPTH, 2, GLA_DK_TOT), jnp.float32),
        'gla_norm_g': 1.0 + 0.02 * nrm(ks[20], (DEPTH, GLA_DV), jnp.float32),
        'gla_up': GLA_DV_TOT ** -0.5 * nrm(ks[21], (DEPTH, GLA_DV_TOT, D), jnp.float32),
        'w_out': D ** -0.5 * nrm(ks[22], (DEPTH, D, D), jnp.float32),
        'mlp_w1': D ** -0.5 * nrm(ks[23], (DEPTH, D, D_FF), jnp.float32),
        'mlp_w2': D_FF ** -0.5 * nrm(ks[24], (DEPTH, D_FF, D), jnp.float32),
    }


def reference(x_prompt, x_sample, state_lru, state_gla, c, c_ctx, w_mod, b_mod, norm_g, w_in, conv_w, conv_b,
              lru_wa, lru_ba, lru_wx, lru_bx, lru_L, lru_up, gla_w2, gla_b2, gla_norm_g, gla_up, w_out, mlp_w1, mlp_w2):
    rows = x_sample.shape[1] // GRID_W
    B0 = x_prompt.shape[0]
    xp = x_prompt
    xs = x_sample
    lru_states = []
    gla_states = []
    for l in range(DEPTH):
        p = {
            'norm_g': norm_g[l], 'w_in': w_in[l], 'conv_w': conv_w[l], 'conv_b': conv_b[l],
            'lru_wa': lru_wa[l], 'lru_ba': lru_ba[l], 'lru_wx': lru_wx[l], 'lru_bx': lru_bx[l], 'lru_L': lru_L[l],
            'lru_up': lru_up[l], 'gla_w2': gla_w2[l], 'gla_b2': gla_b2[l], 'gla_norm_g': gla_norm_g[l],
            'gla_up': gla_up[l], 'w_out': w_out[l], 'mlp_w1': mlp_w1[l], 'mlp_w2': mlp_w2[l],
        }
        mod_ctx = (jax.nn.silu(c_ctx) @ w_mod[l] + b_mod[l])[None, None, :]
        mod_lat = (jax.nn.silu(c) @ w_mod[l] + b_mod[l])[:, None, :]
        h0_lru = jnp.zeros((B0, 2, D_RNN), xp.dtype)
        h0_gla = jnp.zeros((B0, 2, GLA_HEADS, GLA_DK, GLA_DV), xp.dtype)
        xp, s_lru, s_gla = layer(xp, mod_ctx, p, h0_lru, h0_gla, None)
        lru_states.append(s_lru)
        gla_states.append(s_gla)
        xs, _, _ = layer(xs, mod_lat, p, state_lru[:, l], state_gla[:, l], rows)
    new_state_lru = jnp.stack(lru_states, axis=1)
    new_state_gla = jnp.stack(gla_states, axis=1)
    return (xp, xs, new_state_lru, new_state_gla)
```

```python
import functools

import numpy as np
import jax
import jax.numpy as jnp
from jax import lax
from jax.experimental import pallas as pl
from jax.experimental.pallas import tpu as pltpu

F32 = jnp.float32
BF16 = jnp.bfloat16

D_MODEL = 1024
GRID_W = 64
LRU_BLOCKS = 16
LRU_BW = D_MODEL // LRU_BLOCKS
LRU_C = 8.0
GLA_HEADS = 4
GLA_DK = 128
GLA_DV = 256
GLA_RANK = 16
GLA_TAU = 16.0
GLA_CHUNK = 64
D_FF = 4 * D_MODEL
N_MOD = 6
EPS = 1e-6

V7X_LANES = 128
V7X_SUBLANES = 8
V7X_MXU_DIM = 256
V7X_VMEM_BYTES = 64 * 1024 * 1024
VMEM_LIMIT = V7X_VMEM_BYTES - 8 * 1024 * 1024

LRU_COLB = V7X_MXU_DIM
LRU_NCOLB = D_MODEL // LRU_COLB
LR_PAD = V7X_LANES
COL_BLOCK = 8


def _cparams(sem):
    return pltpu.CompilerParams(dimension_semantics=sem, vmem_limit_bytes=VMEM_LIMIT)


def _rms(x, g):
    ms = jnp.mean(x * x, axis=-1, keepdims=True)
    return x * lax.rsqrt(ms + EPS) * g


def _sigmoid(x):
    return 1.0 / (1.0 + jnp.exp(-x))


def _softplus(x):
    return jnp.maximum(x, 0.0) + jnp.log1p(jnp.exp(-jnp.abs(x)))


def _bdot(a, b):
    return jnp.dot(a, b, preferred_element_type=F32)


def _mod_kernel(c_ref, w_ref, b_ref, o_ref):
    c = c_ref[...]
    s = (c * _sigmoid(c)).astype(BF16)
    o_ref[...] = _bdot(s, w_ref[...].astype(BF16)) + b_ref[...]


def _modulation(cc, w_mod, b_mod):
    rows = cc.shape[0]
    return pl.pallas_call(
        _mod_kernel,
        grid=(N_MOD,),
        in_specs=[
            pl.BlockSpec((rows, D_MODEL), lambda n: (0, 0)),
            pl.BlockSpec((D_MODEL, D_MODEL), lambda n: (0, n)),
            pl.BlockSpec((1, D_MODEL), lambda n: (0, n)),
        ],
        out_specs=pl.BlockSpec((rows, D_MODEL), lambda n: (0, n)),
        out_shape=jax.ShapeDtypeStruct((rows, N_MOD * D_MODEL), F32),
        compiler_params=_cparams(("arbitrary",)),
        name="modulation",
    )(cc, w_mod, b_mod.reshape(1, N_MOD * D_MODEL)).reshape(rows, N_MOD, D_MODEL)


def _tok_view(a, grid_tile):
    if grid_tile is None:
        return a
    b, t, n = a.shape
    rows = t // GRID_W
    return a.reshape(b, rows, GRID_W // COL_BLOCK, COL_BLOCK, n)


def _tok_spec(n, tm, grid_tile):
    if grid_tile is None:
        return pl.BlockSpec((None, tm, n), lambda b, j: (b, j, 0))
    rows = grid_tile
    return pl.BlockSpec((None, rows, None, COL_BLOCK, n), lambda b, j: (b, 0, j, 0, 0))


def _tok_shape(b, t, n, grid_tile):
    if grid_tile is None:
        return (b, t, n)
    return (b, grid_tile, GRID_W // COL_BLOCK, COL_BLOCK, n)


def _const_spec(shape):
    nd = len(shape)
    return pl.BlockSpec(shape, lambda b, j: (0,) * nd)


def _mod_spec(per_batch):
    if per_batch:
        return pl.BlockSpec((None, N_MOD, D_MODEL), lambda b, j: (b + 1, 0, 0))
    return pl.BlockSpec((None, N_MOD, D_MODEL), lambda b, j: (0, 0, 0))


def _perm_matrix(rows):
    n = rows * COL_BLOCK
    p = np.zeros((n, n), np.float32)
    for r in range(rows):
        for c in range(COL_BLOCK):
            p[c * rows + r, r * COL_BLOCK + c] = 1.0
    return p


def _inproj_kernel(*refs, n_out, permute):
    x_ref, mod_ref, ng_ref = refs[:3]
    k = 3
    if permute:
        perm_ref = refs[3]
        k = 4
    w_refs = refs[k:k + n_out]
    o_refs = refs[k + n_out:]
    x = x_ref[...].reshape(-1, D_MODEL)
    h = _rms(x, ng_ref[0:1, :]) * (1.0 + mod_ref[1:2, :]) + mod_ref[0:1, :]
    hb = h.astype(BF16)
    if permute:
        hb = _bdot(perm_ref[...], hb).astype(BF16)
    for w_ref, o_ref in zip(w_refs, o_refs):
        z = _bdot(hb, w_ref[...])
        o_ref[...] = z.astype(o_ref.dtype).reshape(o_ref.shape)


def _inproj(x, mod, norm_g, weights, out_dtypes, *, per_batch_mod, tm, col_major, name):
    b, t, _ = x.shape
    if col_major:
        rows = t // GRID_W
        tm = rows * COL_BLOCK
        x_in = _tok_view(x, rows)
        x_spec = _tok_spec(D_MODEL, tm, rows)
        extra = [jnp.asarray(_perm_matrix(rows), BF16)]
        extra_specs = [_const_spec((tm, tm))]
    else:
        x_in = x
        x_spec = _tok_spec(D_MODEL, tm, None)
        extra, extra_specs = [], []
    n_tiles = t // tm
    return pl.pallas_call(
        functools.partial(_inproj_kernel, n_out=len(weights), permute=col_major),
        grid=(b, n_tiles),
        in_specs=[x_spec, _mod_spec(per_batch_mod), _const_spec((4, D_MODEL))]
        + extra_specs + [_const_spec(w.shape) for w in weights],
        out_specs=[_tok_spec(w.shape[1], tm, None) for w in weights],
        out_shape=[jax.ShapeDtypeStruct((b, t, w.shape[1]), dt) for w, dt in zip(weights, out_dtypes)],
        compiler_params=_cparams(("parallel", "parallel")),
        name=name,
    )(x_in, mod, norm_g, *extra, *weights)


def _lru_kernel(zf_ref, zb_ref, h0_ref, cw_ref, cb_ref, wa_ref, wx_ref, ba_ref, bx_ref, l_ref,
                hf_ref, hb_ref, fin_ref, a_s, u_s, carry_s, *, tc, period):
    j = pl.program_id(1)

    @pl.when(j == 0)
    def _():
        carry_s[...] = h0_ref[...]

    t = lax.broadcasted_iota(jnp.int32, (tc, LRU_COLB), 0)
    tp = jnp.bitwise_and(t, period - 1)
    for d, z_ref in ((0, zf_ref), (1, zb_ref)):
        for cb in range(LRU_NCOLB):
            cs = slice(cb * LRU_COLB, (cb + 1) * LRU_COLB)
            x = z_ref[:, cs]
            xm1 = jnp.where(tp >= 1, pltpu.roll(x, 1, 0), 0.0)
            xp1 = jnp.where(tp <= period - 2, pltpu.roll(x, tc - 1, 0), 0.0)
            xp2 = jnp.where(tp <= period - 3, pltpu.roll(x, tc - 2, 0), 0.0)
            xc = (cb_ref[0:1, cs] + xm1 * cw_ref[0:1, cs] + x * cw_ref[1:2, cs]
                  + xp1 * cw_ref[2:3, cs] + xp2 * cw_ref[3:4, cs])
            xcb = xc.astype(BF16)
            r = _sigmoid(_bdot(xcb, wa_ref[d, cb]) + ba_ref[d:d + 1, cs])
            i = _sigmoid(_bdot(xcb, wx_ref[d, cb]) + bx_ref[d:d + 1, cs])
            log_a = (-LRU_C * r) * _softplus(-l_ref[d:d + 1, cs])
            a = jnp.exp(log_a)
            u = jnp.sqrt(1.0 - a * a) * (i * xc)
            a_s[d, :, cs] = a
            u_s[d, :, cs] = u

    sub = lax.broadcasted_iota(jnp.int32, (V7X_SUBLANES, D_MODEL), 0)
    n_tiles = tc // V7X_SUBLANES

    def tile_scan(a, u, forward):
        for s in (1, 2, 4):
            if forward:
                m = sub >= s
                shift = s
            else:
                m = sub < V7X_SUBLANES - s
                shift = V7X_SUBLANES - s
            a_sh = jnp.where(m, pltpu.roll(a, shift, 0), 1.0)
            u_sh = jnp.where(m, pltpu.roll(u, shift, 0), 0.0)
            u = a * u_sh + u
            a = a * a_sh
        return a, u

    def body(i, carry):
        cf, cbk = carry
        rf = pl.multiple_of(i * V7X_SUBLANES, V7X_SUBLANES)
        rb = pl.multiple_of((n_tiles - 1 - i) * V7X_SUBLANES, V7X_SUBLANES)
        af, uf = tile_scan(a_s[0, pl.ds(rf, V7X_SUBLANES), :], u_s[0, pl.ds(rf, V7X_SUBLANES), :], True)
        ab, ub = tile_scan(a_s[1, pl.ds(rb, V7X_SUBLANES), :], u_s[1, pl.ds(rb, V7X_SUBLANES), :], False)
        hf = uf + af * cf
        hb = ub + ab * cbk
        hf_ref[pl.ds(rf, V7X_SUBLANES), :] = hf
        hb_ref[pl.ds(rb, V7X_SUBLANES), :] = hb
        return hf[V7X_SUBLANES - 1:V7X_SUBLANES, :], hb[0:1, :]

    cf, cbk = lax.fori_loop(0, n_tiles, body, (carry_s[0:1, :], carry_s[1:2, :]))
    carry_s[0:1, :] = cf
    carry_s[1:2, :] = cbk
    fin_ref[0:1, :] = cf
    fin_ref[1:2, :] = cbk


def _lru(zx, h0, conv_w, conv_b, wa_bd, wx_bd, ba, bx, lru_l, *, period, name):
    b, t, _ = zx.shape
    tc = min(256, t)
    nc = t // tc
    const = lambda shape: pl.BlockSpec(shape, lambda bb, j: (0,) * len(shape))
    return pl.pallas_call(
        functools.partial(_lru_kernel, tc=tc, period=period),
        grid=(b, nc),
        in_specs=[
            pl.BlockSpec((None, tc, D_MODEL), lambda bb, j: (bb, j, 0)),
            pl.BlockSpec((None, tc, D_MODEL), lambda bb, j: (bb, nc - 1 - j, 0)),
            pl.BlockSpec((None, 2, D_MODEL), lambda bb, j: (bb, 0, 0)),
            const((4, D_MODEL)), const((1, D_MODEL)),
            const((2, LRU_NCOLB, LRU_COLB, LRU_COLB)), const((2, LRU_NCOLB, LRU_COLB, LRU_COLB)),
            const((2, D_MODEL)), const((2, D_MODEL)), const((2, D_MODEL)),
        ],
        out_specs=[
            pl.BlockSpec((None, tc, D_MODEL), lambda bb, j: (bb, j, 0)),
            pl.BlockSpec((None, tc, D_MODEL), lambda bb, j: (bb, nc - 1 - j, 0)),
            pl.BlockSpec((None, 2, D_MODEL), lambda bb, j: (bb, 0, 0)),
        ],
        out_shape=[
            jax.ShapeDtypeStruct((b, t, D_MODEL), F32),
            jax.ShapeDtypeStruct((b, t, D_MODEL), F32),
            jax.ShapeDtypeStruct((b, 2, D_MODEL), F32),
        ],
        scratch_shapes=[
            pltpu.VMEM((2, tc, D_MODEL), F32),
            pltpu.VMEM((2, tc, D_MODEL), F32),
            pltpu.VMEM((2, D_MODEL), F32),
        ],
        compiler_params=_cparams(("parallel", "arbitrary")),
        name=name,
    )(zx, zx, h0, conv_w, conv_b, wa_bd, wx_bd, ba, bx, lru_l)


def _gla_kernel(*refs, t, hps, has_s0, want_final):
    q_ref, k_ref, v_ref, g_ref, lr_ref, w2_ref, b2_ref, gn_ref = refs[:8]
    n = 8
    if has_s0:
        s0_ref = refs[n]
        n += 1
    on_ref = refs[n]
    n += 1
    if want_final:
        fin_ref = refs[n]
        n += 1
    oacc, st = refs[n:]

    c = GLA_CHUNK
    n_chunks = t // c
    for d in range(2):
        for hh in range(hps):
            if has_s0:
                st[d, hh] = s0_ref[d, hh].T
            else:
                st[d, hh] = jnp.zeros((GLA_DV, GLA_DK), F32)

    ti = lax.broadcasted_iota(jnp.int32, (c, c), 0)
    si = lax.broadcasted_iota(jnp.int32, (c, c), 1)
    masks = (si <= ti, si >= ti)
    tris = tuple(m.astype(BF16) for m in masks)
    scale = GLA_DK ** -0.5

    def chunk(c0, d, hh, finalize):
        rows = pl.ds(c0, c)
        ks = slice(hh * GLA_DK, (hh + 1) * GLA_DK)
        vs = slice(hh * GLA_DV, (hh + 1) * GLA_DV)
        pre = _bdot(lr_ref[rows, :], w2_ref[d, :, ks]) + b2_ref[d:d + 1, ks]
        la = -_softplus(-pre) * (1.0 / GLA_TAU)
        hi = la.astype(BF16)
        r1 = la - hi.astype(F32)
        mid = r1.astype(BF16)
        lo = (r1 - mid.astype(F32)).astype(BF16)
        tri = tris[d]
        bcum = _bdot(tri, hi) + _bdot(tri, mid) + _bdot(tri, lo)
        b_last = bcum[c - 1:c, :] if d == 0 else bcum[0:1, :]
        q = q_ref[rows, ks]
        k = k_ref[rows, ks]
        qe = ((q * scale) * jnp.exp(bcum)).astype(BF16)
        ke = (k * jnp.exp(-bcum)).astype(BF16)
        kt = (k * jnp.exp(b_last - bcum)).astype(BF16)
        sc = lax.dot_general(qe, ke, (((1,), (1,)), ((), ())), preferred_element_type=F32)
        sc = jnp.where(masks[d], sc, 0.0).astype(BF16)
        v = v_ref[rows, vs]
        s_t = st[d, hh]
        o = _bdot(sc, v) + lax.dot_general(qe, s_t.astype(BF16), (((1,), (1,)), ((), ())),
                                           preferred_element_type=F32)
        upd = lax.dot_general(v, kt, (((0,), (0,)), ((), ())), preferred_element_type=F32)
        st[d, hh] = s_t * jnp.exp(b_last) + upd
        if finalize:
            tot = oacc[rows, vs] + o
            y = _rms(tot, gn_ref[...])
            gg = g_ref[rows, vs]
            on_ref[rows, vs] = (y * (gg * _sigmoid(gg))).astype(on_ref.dtype)
        else:
            oacc[rows, vs] = o

    def make_body(finalize):
        def body(jj, carry):
            cf = pl.multiple_of(jj * c, c)
            cbk = pl.multiple_of((n_chunks - 1 - jj) * c, c)
            for hh in range(hps):
                chunk(cf, 0, hh, finalize)
                chunk(cbk, 1, hh, finalize)
            return carry
        return body

    half = n_chunks // 2
    lax.fori_loop(0, half, make_body(False), 0)
    lax.fori_loop(half, n_chunks, make_body(True), 0)

    if want_final:
        for d in range(2):
            for hh in range(hps):
                fin_ref[d, hh] = st[d, hh].T


def _gla(q, k, v, g, lr, w2p, b2, gn, s0, *, want_final, hps, name):
    b, t, _ = q.shape
    assert (t // GLA_CHUNK) % 2 == 0
    ng = GLA_HEADS // hps
    has_s0 = s0 is not None
    in_specs = [
        pl.BlockSpec((None, t, hps * GLA_DK), lambda bb, h: (bb, 0, h)),
        pl.BlockSpec((None, t, hps * GLA_DK), lambda bb, h: (bb, 0, h)),
        pl.BlockSpec((None, t, hps * GLA_DV), lambda bb, h: (bb, 0, h)),
        pl.BlockSpec((None, t, hps * GLA_DV), lambda bb, h: (bb, 0, h)),
        pl.BlockSpec((None, t, LR_PAD), lambda bb, h: (bb, 0, 0)),
        pl.BlockSpec((2, LR_PAD, hps * GLA_DK), lambda bb, h: (0, 0, h)),
        pl.BlockSpec((2, hps * GLA_DK), lambda bb, h: (0, h)),
        pl.BlockSpec((1, GLA_DV), lambda bb, h: (0, 0)),
    ]
    args = [q, k, v, g, lr, w2p, b2, gn]
    state_spec = pl.BlockSpec((None, 2, hps, GLA_DK, GLA_DV), lambda bb, h: (bb, 0, h, 0, 0))
    if has_s0:
        in_specs.append(state_spec)
        args.append(s0)
    out_specs = [pl.BlockSpec((None, t, hps * GLA_DV), lambda bb, h: (bb, 0, h))]
    out_shape = [jax.ShapeDtypeStruct((b, t, GLA_HEADS * GLA_DV), BF16)]
    if want_final:
        out_specs.append(state_spec)
        out_shape.append(jax.ShapeDtypeStruct((b, 2, GLA_HEADS, GLA_DK, GLA_DV), F32))
    return pl.pallas_call(
        functools.partial(_gla_kernel, t=t, hps=hps, has_s0=has_s0, want_final=want_final),
        grid=(b, ng),
        in_specs=in_specs,
        out_specs=out_specs,
        out_shape=out_shape,
        scratch_shapes=[
            pltpu.VMEM((t, hps * GLA_DV), F32),
            pltpu.VMEM((2, hps, GLA_DV, GLA_DK), F32),
        ],
        compiler_params=_cparams(("parallel", "parallel")),
        name=name,
    )(*args)


def _gelu_tanh(x):
    return x * (0.5 * (1.0 + jnp.tanh(0.7978845608028654 * (x + 0.044715 * (x * x * x)))))


def _merge_kernel(*refs, permute):
    x_ref, hf_ref, hb_ref, zg_ref, ga_ref, gb_ref, on_ref, mod_ref, ng_ref = refs[:9]
    n = 9
    if permute:
        perm_ref = refs[n]
        n += 1
    lup_ref, gup_ref, wo_ref, o_ref = refs[n:]
    ld = lambda r: r[...].reshape(-1, r.shape[-1])
    hsum = ld(hf_ref) + ld(hb_ref)
    y_a = _bdot((hsum * _gelu_tanh(ld(zg_ref))).astype(BF16), lup_ref[...])
    on = on_ref[...]
    if permute:
        on = _bdot(perm_ref[...], on).astype(BF16)
    y_b = _bdot(on, gup_ref[...])
    mm = _sigmoid(ld(ga_ref)) * y_a + _sigmoid(ld(gb_ref)) * y_b
    m = _bdot(mm.astype(BF16), wo_ref[...])
    x1 = ld(x_ref) + mod_ref[2:3, :] * _rms(m, ng_ref[1:2, :])
    o_ref[...] = x1.reshape(o_ref.shape)


def _merge(x, hf, hb, zg, ga, gb, on, mod, norm_g, lru_up, gla_up, w_out, *, per_batch_mod, tm,
           col_major, name):
    b, t, _ = x.shape
    if col_major:
        rows = t // GRID_W
        tm = rows * COL_BLOCK
        gt = rows
        extra = [jnp.asarray(_perm_matrix(rows).T, BF16)]
        extra_specs = [_const_spec((tm, tm))]
    else:
        gt = None
        extra, extra_specs = [], []
    tok = lambda a: _tok_view(a, gt)
    spec = _tok_spec(D_MODEL, tm, gt)
    wspec = _const_spec((D_MODEL, D_MODEL))
    out = pl.pallas_call(
        functools.partial(_merge_kernel, permute=col_major),
        grid=(b, t // tm),
        in_specs=[spec] * 6 + [_tok_spec(D_MODEL, tm, None), _mod_spec(per_batch_mod),
                               _const_spec((4, D_MODEL))] + extra_specs + [wspec] * 3,
        out_specs=spec,
        out_shape=jax.ShapeDtypeStruct(_tok_shape(b, t, D_MODEL, gt), F32),
        compiler_params=_cparams(("parallel", "parallel")),
        name=name,
    )(tok(x), tok(hf), tok(hb), tok(zg), tok(ga), tok(gb), on, mod, norm_g, *extra,
      lru_up, gla_up, w_out)
    return out.reshape(b, t, D_MODEL)


def _ffn_kernel(x_ref, mod_ref, ng_ref, w1_ref, w2_ref, o_ref, *, n_split):
    x = x_ref[...]
    h = (_rms(x, ng_ref[2:3, :]) * (1.0 + mod_ref[4:5, :]) + mod_ref[3:4, :]).astype(BF16)
    fs = D_FF // n_split
    acc = jnp.zeros(x.shape, F32)
    for kf in range(n_split):
        hid = jnp.maximum(_bdot(h, w1_ref[:, kf * fs:(kf + 1) * fs]), 0.0)
        acc = acc + _bdot((hid * hid).astype(BF16), w2_ref[kf * fs:(kf + 1) * fs, :])
    o_ref[...] = x + mod_ref[5:6, :] * _rms(acc, ng_ref[3:4, :])


def _ffn(x, mod, norm_g, w1, w2, *, per_batch_mod, tm, name):
    b, t, _ = x.shape
    spec = _tok_spec(D_MODEL, tm, None)
    return pl.pallas_call(
        functools.partial(_ffn_kernel, n_split=4),
        grid=(b, t // tm),
        in_specs=[spec, _mod_spec(per_batch_mod), _const_spec((4, D_MODEL)),
                  _const_spec((D_MODEL, D_FF)), _const_spec((D_FF, D_MODEL))],
        out_specs=spec,
        out_shape=jax.ShapeDtypeStruct((b, t, D_MODEL), F32),
        compiler_params=_cparams(("parallel", "parallel")),
        name=name,
    )(x, mod, norm_g, w1, w2)


def _blockdiag_slabs(w):
    per = LRU_COLB // LRU_BW
    w = w.reshape(2, LRU_NCOLB, per, LRU_BW, LRU_BW)
    eye = jnp.eye(per, dtype=w.dtype)
    slab = jnp.einsum('dspij,pq->dspiqj', w, eye)
    return slab.reshape(2, LRU_NCOLB, LRU_COLB, LRU_COLB).astype(BF16)


def _layer_group(x, mod, p, lru_h0, gla_s0, *, latent, want_state, tag):
    b, t, _ = x.shape
    per_batch = latent
    tm = 256 if t % 256 == 0 else t
    zx, zg, ga, gb = _inproj(x, mod, p['norm_g'], p['w_rm'], (F32,) * 4, per_batch_mod=per_batch,
                             tm=tm, col_major=False, name=f"inproj_rm_{tag}")
    q, k, v, g, lr = _inproj(x, mod, p['norm_g'], p['w_cm'], (F32, F32, BF16, F32, BF16),
                             per_batch_mod=per_batch, tm=tm, col_major=latent,
                             name=f"inproj_cm_{tag}")
    period = GRID_W if latent else t
    hf, hb, s_lru = _lru(zx, lru_h0, p['conv_w'], p['conv_b'], p['wa_bd'], p['wx_bd'], p['lru_ba'],
                         p['lru_bx'], p['lru_L'], period=period, name=f"lru_{tag}")
    res = _gla(q, k, v, g, lr, p['w2p'], p['gla_b2'], p['gla_norm_g'], gla_s0,
               want_final=want_state, hps=2, name=f"gla_{tag}")
    on = res[0]
    s_gla = res[1] if want_state else None
    x1 = _merge(x, hf, hb, zg, ga, gb, on, mod, p['norm_g'], p['lru_up'], p['gla_up'], p['w_out'],
                per_batch_mod=per_batch, tm=tm, col_major=latent, name=f"merge_{tag}")
    y = _ffn(x1, mod, p['norm_g'], p['mlp_w1'], p['mlp_w2'], per_batch_mod=per_batch, tm=tm,
             name=f"ffn_{tag}")
    return y, s_lru, s_gla


def kernel(x_prompt, x_sample, state_lru, state_gla, c, c_ctx, w_mod, b_mod, norm_g, w_in, conv_w, conv_b,
           lru_wa, lru_ba, lru_wx, lru_bx, lru_L, lru_up, gla_w2, gla_b2, gla_norm_g, gla_up, w_out,
           mlp_w1, mlp_w2):
    depth = w_in.shape[0]
    assert depth == 1
    dec_b = x_sample.shape[0]
    b0 = x_prompt.shape[0]
    xp, xs = x_prompt, x_sample
    lru_states, gla_states = [], []
    mod_rows = -(-(1 + dec_b) // V7X_SUBLANES) * V7X_SUBLANES
    cc = jnp.zeros((mod_rows, D_MODEL), F32).at[0].set(c_ctx).at[1:1 + dec_b].set(c)
    for l in range(depth):
        wl = w_in[l]
        o = np.cumsum((0, D_MODEL, D_MODEL, 512, 512, 1024, 1024, 2 * GLA_RANK, D_MODEL, D_MODEL))
        col = lambda i: wl[:, o[i]:o[i + 1]].astype(BF16)
        lr_w = jnp.pad(col(6), ((0, 0), (0, LR_PAD - 2 * GLA_RANK)))
        w2 = gla_w2[l].astype(BF16)
        w2p = jnp.zeros((2, LR_PAD, GLA_HEADS * GLA_DK), BF16)
        w2p = w2p.at[0, 0:GLA_RANK].set(w2[0]).at[1, GLA_RANK:2 * GLA_RANK].set(w2[1])
        p = {
            'norm_g': norm_g[l],
            'w_rm': [col(0), col(1), col(7), col(8)],
            'w_cm': [col(2), col(3), col(4), col(5), lr_w],
            'conv_w': conv_w[l], 'conv_b': conv_b[l].reshape(1, D_MODEL),
            'wa_bd': _blockdiag_slabs(lru_wa[l]), 'wx_bd': _blockdiag_slabs(lru_wx[l]),
            'lru_ba': lru_ba[l], 'lru_bx': lru_bx[l], 'lru_L': lru_L[l],
            'lru_up': lru_up[l].astype(BF16),
            'w2p': w2p, 'gla_b2': gla_b2[l], 'gla_norm_g': gla_norm_g[l].reshape(1, GLA_DV),
            'gla_up': gla_up[l].astype(BF16), 'w_out': w_out[l].astype(BF16),
            'mlp_w1': mlp_w1[l].astype(BF16), 'mlp_w2': mlp_w2[l].astype(BF16),
        }
        mod = _modulation(cc, w_mod[l], b_mod[l])
        h0_ctx = jnp.zeros((b0, 2, D_MODEL), F32)
        xp, s_lru, s_gla = _layer_group(xp, mod, p, h0_ctx, None, latent=False, want_state=True,
                                        tag="ctx")
        lru_states.append(s_lru)
        gla_states.append(s_gla)
        xs, _, _ = _layer_group(xs, mod, p, state_lru[:, l], state_gla[:, l], latent=True,
                                want_state=False, tag="lat")
    return (xp, xs, jnp.stack(lru_states, axis=1), jnp.stack(gla_states, axis=1))
```

```python
import functools

import numpy as np
import jax
import jax.numpy as jnp
from jax import lax
from jax.experimental import pallas as pl
from jax.experimental.pallas import tpu as pltpu

F32 = jnp.float32
BF16 = jnp.bfloat16

D_MODEL = 1024
GRID_W = 64
LRU_BLOCKS = 16
LRU_BW = D_MODEL // LRU_BLOCKS
LRU_C = 8.0
GLA_HEADS = 4
GLA_DK = 128
GLA_DV = 256
GLA_RANK = 16
GLA_TAU = 16.0
GLA_CHUNK = 64
D_FF = 4 * D_MODEL
N_MOD = 6
EPS = 1e-6

V7X_LANES = 128
V7X_SUBLANES = 8
V7X_MXU_DIM = 256
V7X_VMEM_BYTES = 64 * 1024 * 1024
VMEM_LIMIT = V7X_VMEM_BYTES - 8 * 1024 * 1024

LRU_COLB = V7X_MXU_DIM
LRU_NCOLB = D_MODEL // LRU_COLB
LR_PAD = V7X_LANES
COL_BLOCK = 8
GLA_BLOCK = V7X_MXU_DIM


def _cparams(sem):
    return pltpu.CompilerParams(dimension_semantics=sem, vmem_limit_bytes=VMEM_LIMIT)


def _rms(x, g):
    ms = jnp.mean(x * x, axis=-1, keepdims=True)
    return x * lax.rsqrt(ms + EPS) * g


def _sigmoid(x):
    return 1.0 / (1.0 + jnp.exp(-x))


def _softplus(x):
    return jnp.maximum(x, 0.0) + jnp.log1p(jnp.exp(-jnp.abs(x)))


def _bdot(a, b):
    return jnp.dot(a, b, preferred_element_type=F32)


def _mod_kernel(c_ref, w_ref, b_ref, o_ref):
    c = c_ref[...]
    s = (c * _sigmoid(c)).astype(BF16)
    o_ref[...] = _bdot(s, w_ref[...].astype(BF16)) + b_ref[...]


def _modulation(cc, w_mod, b_mod):
    rows = cc.shape[0]
    return pl.pallas_call(
        _mod_kernel,
        grid=(N_MOD,),
        in_specs=[
            pl.BlockSpec((rows, D_MODEL), lambda n: (0, 0)),
            pl.BlockSpec((D_MODEL, D_MODEL), lambda n: (0, n)),
            pl.BlockSpec((1, D_MODEL), lambda n: (0, n)),
        ],
        out_specs=pl.BlockSpec((rows, D_MODEL), lambda n: (0, n)),
        out_shape=jax.ShapeDtypeStruct((rows, N_MOD * D_MODEL), F32),
        compiler_params=_cparams(("arbitrary",)),
        name="modulation",
    )(cc, w_mod, b_mod.reshape(1, N_MOD * D_MODEL)).reshape(rows, N_MOD, D_MODEL)


def _tok_view(a, grid_tile):
    if grid_tile is None:
        return a
    b, t, n = a.shape
    rows = t // GRID_W
    return a.reshape(b, rows, GRID_W // COL_BLOCK, COL_BLOCK, n)


def _tok_spec(n, tm, grid_tile):
    if grid_tile is None:
        return pl.BlockSpec((None, tm, n), lambda b, j: (b, j, 0))
    rows = grid_tile
    return pl.BlockSpec((None, rows, None, COL_BLOCK, n), lambda b, j: (b, 0, j, 0, 0))


def _tok_shape(b, t, n, grid_tile):
    if grid_tile is None:
        return (b, t, n)
    return (b, grid_tile, GRID_W // COL_BLOCK, COL_BLOCK, n)


def _const_spec(shape):
    nd = len(shape)
    return pl.BlockSpec(shape, lambda b, j: (0,) * nd)


def _mod_spec(per_batch):
    if per_batch:
        return pl.BlockSpec((None, N_MOD, D_MODEL), lambda b, j: (b + 1, 0, 0))
    return pl.BlockSpec((None, N_MOD, D_MODEL), lambda b, j: (0, 0, 0))


def _perm_matrix(rows):
    n = rows * COL_BLOCK
    p = np.zeros((n, n), np.float32)
    for r in range(rows):
        for c in range(COL_BLOCK):
            p[c * rows + r, r * COL_BLOCK + c] = 1.0
    return p


def _inproj_kernel(*refs, n_out, permute):
    x_ref, mod_ref, ng_ref = refs[:3]
    k = 3
    if permute:
        perm_ref = refs[3]
        k = 4
    w_refs = refs[k:k + n_out]
    o_refs = refs[k + n_out:]
    x = x_ref[...].reshape(-1, D_MODEL)
    h = _rms(x, ng_ref[0:1, :]) * (1.0 + mod_ref[1:2, :]) + mod_ref[0:1, :]
    hb = h.astype(BF16)
    if permute:
        hb = _bdot(perm_ref[...], hb).astype(BF16)
    for w_ref, o_ref in zip(w_refs, o_refs):
        z = _bdot(hb, w_ref[...])
        o_ref[...] = z.astype(o_ref.dtype).reshape(o_ref.shape)


def _inproj(x, mod, norm_g, weights, out_dtypes, *, per_batch_mod, tm, col_major, name):
    b, t, _ = x.shape
    if col_major:
        rows = t // GRID_W
        tm = rows * COL_BLOCK
        x_in = _tok_view(x, rows)
        x_spec = _tok_spec(D_MODEL, tm, rows)
        extra = [jnp.asarray(_perm_matrix(rows), BF16)]
        extra_specs = [_const_spec((tm, tm))]
    else:
        x_in = x
        x_spec = _tok_spec(D_MODEL, tm, None)
        extra, extra_specs = [], []
    n_tiles = t // tm
    return pl.pallas_call(
        functools.partial(_inproj_kernel, n_out=len(weights), permute=col_major),
        grid=(b, n_tiles),
        in_specs=[x_spec, _mod_spec(per_batch_mod), _const_spec((4, D_MODEL))]
        + extra_specs + [_const_spec(w.shape) for w in weights],
        out_specs=[_tok_spec(w.shape[1], tm, None) for w in weights],
        out_shape=[jax.ShapeDtypeStruct((b, t, w.shape[1]), dt) for w, dt in zip(weights, out_dtypes)],
        compiler_params=_cparams(("parallel", "parallel")),
        name=name,
    )(x_in, mod, norm_g, *extra, *weights)


def _lru_kernel(zf_ref, zb_ref, h0_ref, cw_ref, cb_ref, wa_ref, wx_ref, ba_ref, bx_ref, l_ref,
                hf_ref, hb_ref, fin_ref, a_s, u_s, carry_s, *, tc, period):
    j = pl.program_id(1)

    @pl.when(j == 0)
    def _():
        carry_s[...] = h0_ref[...]

    t = lax.broadcasted_iota(jnp.int32, (tc, LRU_COLB), 0)
    tp = jnp.bitwise_and(t, period - 1)
    for d, z_ref in ((0, zf_ref), (1, zb_ref)):
        for cb in range(LRU_NCOLB):
            cs = slice(cb * LRU_COLB, (cb + 1) * LRU_COLB)
            x = z_ref[:, cs]
            xm1 = jnp.where(tp >= 1, pltpu.roll(x, 1, 0), 0.0)
            xp1 = jnp.where(tp <= period - 2, pltpu.roll(x, tc - 1, 0), 0.0)
            xp2 = jnp.where(tp <= period - 3, pltpu.roll(x, tc - 2, 0), 0.0)
            xc = (cb_ref[0:1, cs] + xm1 * cw_ref[0:1, cs] + x * cw_ref[1:2, cs]
                  + xp1 * cw_ref[2:3, cs] + xp2 * cw_ref[3:4, cs])
            xcb = xc.astype(BF16)
            r = _sigmoid(_bdot(xcb, wa_ref[d, cb]) + ba_ref[d:d + 1, cs])
            i = _sigmoid(_bdot(xcb, wx_ref[d, cb]) + bx_ref[d:d + 1, cs])
            log_a = (-LRU_C * r) * _softplus(-l_ref[d:d + 1, cs])
            a = jnp.exp(log_a)
            u = jnp.sqrt(1.0 - a * a) * (i * xc)
            a_s[d, :, cs] = a
            u_s[d, :, cs] = u

    sub = lax.broadcasted_iota(jnp.int32, (V7X_SUBLANES, D_MODEL), 0)
    n_tiles = tc // V7X_SUBLANES

    def tile_scan(a, u, forward):
        for s in (1, 2, 4):
            if forward:
                m = sub >= s
                shift = s
            else:
                m = sub < V7X_SUBLANES - s
                shift = V7X_SUBLANES - s
            a_sh = jnp.where(m, pltpu.roll(a, shift, 0), 1.0)
            u_sh = jnp.where(m, pltpu.roll(u, shift, 0), 0.0)
            u = a * u_sh + u
            a = a * a_sh
        return a, u

    def body(i, carry):
        cf, cbk = carry
        rf = pl.multiple_of(i * V7X_SUBLANES, V7X_SUBLANES)
        rb = pl.multiple_of((n_tiles - 1 - i) * V7X_SUBLANES, V7X_SUBLANES)
        af, uf = tile_scan(a_s[0, pl.ds(rf, V7X_SUBLANES), :], u_s[0, pl.ds(rf, V7X_SUBLANES), :], True)
        ab, ub = tile_scan(a_s[1, pl.ds(rb, V7X_SUBLANES), :], u_s[1, pl.ds(rb, V7X_SUBLANES), :], False)
        hf = uf + af * cf
        hb = ub + ab * cbk
        hf_ref[pl.ds(rf, V7X_SUBLANES), :] = hf
        hb_ref[pl.ds(rb, V7X_SUBLANES), :] = hb
        return hf[V7X_SUBLANES - 1:V7X_SUBLANES, :], hb[0:1, :]

    cf, cbk = lax.fori_loop(0, n_tiles, body, (carry_s[0:1, :], carry_s[1:2, :]))
    carry_s[0:1, :] = cf
    carry_s[1:2, :] = cbk
    fin_ref[0:1, :] = cf
    fin_ref[1:2, :] = cbk


def _lru(zx, h0, conv_w, conv_b, wa_bd, wx_bd, ba, bx, lru_l, *, period, name):
    b, t, _ = zx.shape
    tc = min(256, t)
    nc = t // tc
    const = lambda shape: pl.BlockSpec(shape, lambda bb, j: (0,) * len(shape))
    return pl.pallas_call(
        functools.partial(_lru_kernel, tc=tc, period=period),
        grid=(b, nc),
        in_specs=[
            pl.BlockSpec((None, tc, D_MODEL), lambda bb, j: (bb, j, 0)),
            pl.BlockSpec((None, tc, D_MODEL), lambda bb, j: (bb, nc - 1 - j, 0)),
            pl.BlockSpec((None, 2, D_MODEL), lambda bb, j: (bb, 0, 0)),
            const((4, D_MODEL)), const((1, D_MODEL)),
            const((2, LRU_NCOLB, LRU_COLB, LRU_COLB)), const((2, LRU_NCOLB, LRU_COLB, LRU_COLB)),
            const((2, D_MODEL)), const((2, D_MODEL)), const((2, D_MODEL)),
        ],
        out_specs=[
            pl.BlockSpec((None, tc, D_MODEL), lambda bb, j: (bb, j, 0)),
            pl.BlockSpec((None, tc, D_MODEL), lambda bb, j: (bb, nc - 1 - j, 0)),
            pl.BlockSpec((None, 2, D_MODEL), lambda bb, j: (bb, 0, 0)),
        ],
        out_shape=[
            jax.ShapeDtypeStruct((b, t, D_MODEL), F32),
            jax.ShapeDtypeStruct((b, t, D_MODEL), F32),
            jax.ShapeDtypeStruct((b, 2, D_MODEL), F32),
        ],
        scratch_shapes=[
            pltpu.VMEM((2, tc, D_MODEL), F32),
            pltpu.VMEM((2, tc, D_MODEL), F32),
            pltpu.VMEM((2, D_MODEL), F32),
        ],
        compiler_params=_cparams(("parallel", "arbitrary")),
        name=name,
    )(zx, zx, h0, conv_w, conv_b, wa_bd, wx_bd, ba, bx, lru_l)


def _gla_kernel(*refs, t, hps, has_s0, want_final):
    q_ref, k_ref, v_ref, g_ref, lr_ref, w2_ref, b2_ref, gn_ref = refs[:8]
    n = 8
    if has_s0:
        s0_ref = refs[n]
        n += 1
    on_ref = refs[n]
    n += 1
    if want_final:
        fin_ref = refs[n]
        n += 1
    oacc, st, qe_s, kt_s, bt_s = refs[n:]

    c = GLA_CHUNK
    n_chunks = t // c
    blk = min(GLA_BLOCK, t)
    for d in range(2):
        for hh in range(hps):
            if has_s0:
                st[d, hh] = s0_ref[d, hh].T
            else:
                st[d, hh] = jnp.zeros((GLA_DV, GLA_DK), F32)

    ti = lax.broadcasted_iota(jnp.int32, (blk, blk), 0)
    si = lax.broadcasted_iota(jnp.int32, (blk, blk), 1)
    same = (ti // c) == (si // c)
    masks = (same & (si <= ti), same & (si >= ti))
    cums = tuple(m.astype(BF16) for m in masks)
    scale = GLA_DK ** -0.5

    def block(ib, carry):
        r0 = pl.multiple_of(ib * blk, blk)
        rows = pl.ds(r0, blk)
        lrb = lr_ref[rows, :]
        scores = [[None, None] for _ in range(hps)]
        for d in range(2):
            pre = _bdot(lrb, w2_ref[d]) + b2_ref[d:d + 1, :]
            la = -_softplus(-pre) * (1.0 / GLA_TAU)
            hi = la.astype(BF16)
            r1 = la - hi.astype(F32)
            mid = r1.astype(BF16)
            lo = (r1 - mid.astype(F32)).astype(BF16)
            bcum = _bdot(cums[d], hi) + _bdot(cums[d], mid) + _bdot(cums[d], lo)
            edge = c - 1 if d == 0 else 0
            bc3 = bcum.reshape(blk // c, c, hps * GLA_DK)
            btot = jnp.broadcast_to(bc3[:, edge:edge + 1, :], bc3.shape).reshape(bcum.shape)
            bt_s[d, rows, :] = btot
            for hh in range(hps):
                ks = slice(hh * GLA_DK, (hh + 1) * GLA_DK)
                q = q_ref[rows, ks]
                k = k_ref[rows, ks]
                bc = bcum[:, ks]
                qe = ((q * scale) * jnp.exp(bc)).astype(BF16)
                ke = (k * jnp.exp(-bc)).astype(BF16)
                qe_s[d, rows, ks] = qe
                kt_s[d, rows, ks] = (k * jnp.exp(btot[:, ks] - bc)).astype(BF16)
                scores[hh][d] = lax.dot_general(qe, ke, (((1,), (1,)), ((), ())),
                                                preferred_element_type=F32)
        for hh in range(hps):
            vs = slice(hh * GLA_DV, (hh + 1) * GLA_DV)
            sc = jnp.where(masks[0], scores[hh][0], 0.0) + jnp.where(masks[1], scores[hh][1], 0.0)
            oacc[rows, vs] = _bdot(sc.astype(BF16), v_ref[rows, vs])
        return carry

    lax.fori_loop(0, t // blk, block, 0)

    def chunk(c0, d, hh):
        rows = pl.ds(c0, c)
        ks = slice(hh * GLA_DK, (hh + 1) * GLA_DK)
        vs = slice(hh * GLA_DV, (hh + 1) * GLA_DV)
        s_t = st[d, hh]
        o = lax.dot_general(qe_s[d, rows, ks], s_t.astype(BF16), (((1,), (1,)), ((), ())),
                            preferred_element_type=F32)
        upd = lax.dot_general(v_ref[rows, vs], kt_s[d, rows, ks], (((0,), (0,)), ((), ())),
                              preferred_element_type=F32)
        st[d, hh] = s_t * jnp.exp(bt_s[d, pl.ds(c0, 1), ks]) + upd
        return o

    def make_body(finalize):
        def body(jj, carry):
            cf = pl.multiple_of(jj * c, c)
            cbk = pl.multiple_of((n_chunks - 1 - jj) * c, c)
            for hh in range(hps):
                vs = slice(hh * GLA_DV, (hh + 1) * GLA_DV)
                for c0, d in ((cf, 0), (cbk, 1)):
                    rows = pl.ds(c0, c)
                    tot = oacc[rows, vs] + chunk(c0, d, hh)
                    if finalize:
                        gg = g_ref[rows, vs]
                        y = _rms(tot, gn_ref[...])
                        on_ref[rows, vs] = (y * (gg * _sigmoid(gg))).astype(on_ref.dtype)
                    else:
                        oacc[rows, vs] = tot
            return carry
        return body

    half = n_chunks // 2
    lax.fori_loop(0, half, make_body(False), 0, unroll=4)
    lax.fori_loop(half, n_chunks, make_body(True), 0, unroll=4)

    if want_final:
        for d in range(2):
            for hh in range(hps):
                fin_ref[d, hh] = st[d, hh].T


def _gla(q, k, v, g, lr, w2p, b2, gn, s0, *, want_final, hps, name):
    b, t, _ = q.shape
    assert (t // GLA_CHUNK) % 2 == 0
    ng = GLA_HEADS // hps
    has_s0 = s0 is not None
    in_specs = [
        pl.BlockSpec((None, t, hps * GLA_DK), lambda bb, h: (bb, 0, h)),
        pl.BlockSpec((None, t, hps * GLA_DK), lambda bb, h: (bb, 0, h)),
        pl.BlockSpec((None, t, hps * GLA_DV), lambda bb, h: (bb, 0, h)),
        pl.BlockSpec((None, t, hps * GLA_DV), lambda bb, h: (bb, 0, h)),
        pl.BlockSpec((None, t, LR_PAD), lambda bb, h: (bb, 0, 0)),
        pl.BlockSpec((2, LR_PAD, hps * GLA_DK), lambda bb, h: (0, 0, h)),
        pl.BlockSpec((2, hps * GLA_DK), lambda bb, h: (0, h)),
        pl.BlockSpec((1, GLA_DV), lambda bb, h: (0, 0)),
    ]
    args = [q, k, v, g, lr, w2p, b2, gn]
    state_spec = pl.BlockSpec((None, 2, hps, GLA_DK, GLA_DV), lambda bb, h: (bb, 0, h, 0, 0))
    if has_s0:
        in_specs.append(state_spec)
        args.append(s0)
    out_specs = [pl.BlockSpec((None, t, hps * GLA_DV), lambda bb, h: (bb, 0, h))]
    out_shape = [jax.ShapeDtypeStruct((b, t, GLA_HEADS * GLA_DV), BF16)]
    if want_final:
        out_specs.append(state_spec)
        out_shape.append(jax.ShapeDtypeStruct((b, 2, GLA_HEADS, GLA_DK, GLA_DV), F32))
    return pl.pallas_call(
        functools.partial(_gla_kernel, t=t, hps=hps, has_s0=has_s0, want_final=want_final),
        grid=(b, ng),
        in_specs=in_specs,
        out_specs=out_specs,
        out_shape=out_shape,
        scratch_shapes=[
            pltpu.VMEM((t, hps * GLA_DV), F32),
            pltpu.VMEM((2, hps, GLA_DV, GLA_DK), F32),
            pltpu.VMEM((2, t, hps * GLA_DK), BF16),
            pltpu.VMEM((2, t, hps * GLA_DK), BF16),
            pltpu.VMEM((2, t, hps * GLA_DK), F32),
        ],
        compiler_params=_cparams(("parallel", "parallel")),
        name=name,
    )(*args)


def _gelu_tanh(x):
    return x * (0.5 * (1.0 + jnp.tanh(0.7978845608028654 * (x + 0.044715 * (x * x * x)))))


def _merge_kernel(*refs, permute):
    x_ref, hf_ref, hb_ref, zg_ref, ga_ref, gb_ref, on_ref, mod_ref, ng_ref = refs[:9]
    n = 9
    if permute:
        perm_ref = refs[n]
        n += 1
    lup_ref, gup_ref, wo_ref, o_ref = refs[n:]
    ld = lambda r: r[...].reshape(-1, r.shape[-1])
    hsum = ld(hf_ref) + ld(hb_ref)
    y_a = _bdot((hsum * _gelu_tanh(ld(zg_ref))).astype(BF16), lup_ref[...])
    on = on_ref[...]
    if permute:
        on = _bdot(perm_ref[...], on).astype(BF16)
    y_b = _bdot(on, gup_ref[...])
    mm = _sigmoid(ld(ga_ref)) * y_a + _sigmoid(ld(gb_ref)) * y_b
    m = _bdot(mm.astype(BF16), wo_ref[...])
    x1 = ld(x_ref) + mod_ref[2:3, :] * _rms(m, ng_ref[1:2, :])
    o_ref[...] = x1.reshape(o_ref.shape)


def _merge(x, hf, hb, zg, ga, gb, on, mod, norm_g, lru_up, gla_up, w_out, *, per_batch_mod, tm,
           col_major, name):
    b, t, _ = x.shape
    if col_major:
        rows = t // GRID_W
        tm = rows * COL_BLOCK
        gt = rows
        extra = [jnp.asarray(_perm_matrix(rows).T, BF16)]
        extra_specs = [_const_spec((tm, tm))]
    else:
        gt = None
        extra, extra_specs = [], []
    tok = lambda a: _tok_view(a, gt)
    spec = _tok_spec(D_MODEL, tm, gt)
    wspec = _const_spec((D_MODEL, D_MODEL))
    out = pl.pallas_call(
        functools.partial(_merge_kernel, permute=col_major),
        grid=(b, t // tm),
        in_specs=[spec] * 6 + [_tok_spec(D_MODEL, tm, None), _mod_spec(per_batch_mod),
                               _const_spec((4, D_MODEL))] + extra_specs + [wspec] * 3,
        out_specs=spec,
        out_shape=jax.ShapeDtypeStruct(_tok_shape(b, t, D_MODEL, gt), F32),
        compiler_params=_cparams(("parallel", "parallel")),
        name=name,
    )(tok(x), tok(hf), tok(hb), tok(zg), tok(ga), tok(gb), on, mod, norm_g, *extra,
      lru_up, gla_up, w_out)
    return out.reshape(b, t, D_MODEL)


def _ffn_kernel(x_ref, mod_ref, ng_ref, w1_ref, w2_ref, o_ref, *, n_split):
    x = x_ref[...]
    h = (_rms(x, ng_ref[2:3, :]) * (1.0 + mod_ref[4:5, :]) + mod_ref[3:4, :]).astype(BF16)
    fs = D_FF // n_split
    acc = jnp.zeros(x.shape, F32)
    for kf in range(n_split):
        hid = jnp.maximum(_bdot(h, w1_ref[:, kf * fs:(kf + 1) * fs]), 0.0)
        acc = acc + _bdot((hid * hid).astype(BF16), w2_ref[kf * fs:(kf + 1) * fs, :])
    o_ref[...] = x + mod_ref[5:6, :] * _rms(acc, ng_ref[3:4, :])


def _ffn(x, mod, norm_g, w1, w2, *, per_batch_mod, tm, name):
    b, t, _ = x.shape
    spec = _tok_spec(D_MODEL, tm, None)
    return pl.pallas_call(
        functools.partial(_ffn_kernel, n_split=4),
        grid=(b, t // tm),
        in_specs=[spec, _mod_spec(per_batch_mod), _const_spec((4, D_MODEL)),
                  _const_spec((D_MODEL, D_FF)), _const_spec((D_FF, D_MODEL))],
        out_specs=spec,
        out_shape=jax.ShapeDtypeStruct((b, t, D_MODEL), F32),
        compiler_params=_cparams(("parallel", "parallel")),
        name=name,
    )(x, mod, norm_g, w1, w2)


def _blockdiag_slabs(w):
    per = LRU_COLB // LRU_BW
    w = w.reshape(2, LRU_NCOLB, per, LRU_BW, LRU_BW)
    eye = jnp.eye(per, dtype=w.dtype)
    slab = jnp.einsum('dspij,pq->dspiqj', w, eye)
    return slab.reshape(2, LRU_NCOLB, LRU_COLB, LRU_COLB).astype(BF16)


def _layer_group(x, mod, p, lru_h0, gla_s0, *, latent, want_state, tag):
    b, t, _ = x.shape
    per_batch = latent
    tm = 256 if t % 256 == 0 else t
    zx, zg, ga, gb = _inproj(x, mod, p['norm_g'], p['w_rm'], (F32,) * 4, per_batch_mod=per_batch,
                             tm=tm, col_major=False, name=f"inproj_rm_{tag}")
    q, k, v, g, lr = _inproj(x, mod, p['norm_g'], p['w_cm'], (F32, F32, BF16, F32, BF16),
                             per_batch_mod=per_batch, tm=tm, col_major=latent,
                             name=f"inproj_cm_{tag}")
    period = GRID_W if latent else t
    hf, hb, s_lru = _lru(zx, lru_h0, p['conv_w'], p['conv_b'], p['wa_bd'], p['wx_bd'], p['lru_ba'],
                         p['lru_bx'], p['lru_L'], period=period, name=f"lru_{tag}")
    res = _gla(q, k, v, g, lr, p['w2p'], p['gla_b2'], p['gla_norm_g'], gla_s0,
               want_final=want_state, hps=2, name=f"gla_{tag}")
    on = res[0]
    s_gla = res[1] if want_state else None
    x1 = _merge(x, hf, hb, zg, ga, gb, on, mod, p['norm_g'], p['lru_up'], p['gla_up'], p['w_out'],
                per_batch_mod=per_batch, tm=tm, col_major=latent, name=f"merge_{tag}")
    y = _ffn(x1, mod, p['norm_g'], p['mlp_w1'], p['mlp_w2'], per_batch_mod=per_batch, tm=tm,
             name=f"ffn_{tag}")
    return y, s_lru, s_gla


def kernel(x_prompt, x_sample, state_lru, state_gla, c, c_ctx, w_mod, b_mod, norm_g, w_in, conv_w, conv_b,
           lru_wa, lru_ba, lru_wx, lru_bx, lru_L, lru_up, gla_w2, gla_b2, gla_norm_g, gla_up, w_out,
           mlp_w1, mlp_w2):
    depth = w_in.shape[0]
    assert depth == 1
    dec_b = x_sample.shape[0]
    b0 = x_prompt.shape[0]
    xp, xs = x_prompt, x_sample
    lru_states, gla_states = [], []
    mod_rows = -(-(1 + dec_b) // V7X_SUBLANES) * V7X_SUBLANES
    cc = jnp.zeros((mod_rows, D_MODEL), F32).at[0].set(c_ctx).at[1:1 + dec_b].set(c)
    for l in range(depth):
        wl = w_in[l]
        o = np.cumsum((0, D_MODEL, D_MODEL, 512, 512, 1024, 1024, 2 * GLA_RANK, D_MODEL, D_MODEL))
        col = lambda i: wl[:, o[i]:o[i + 1]].astype(BF16)
        lr_w = jnp.pad(col(6), ((0, 0), (0, LR_PAD - 2 * GLA_RANK)))
        w2 = gla_w2[l].astype(BF16)
        w2p = jnp.zeros((2, LR_PAD, GLA_HEADS * GLA_DK), BF16)
        w2p = w2p.at[0, 0:GLA_RANK].set(w2[0]).at[1, GLA_RANK:2 * GLA_RANK].set(w2[1])
        p = {
            'norm_g': norm_g[l],
            'w_rm': [col(0), col(1), col(7), col(8)],
            'w_cm': [col(2), col(3), col(4), col(5), lr_w],
            'conv_w': conv_w[l], 'conv_b': conv_b[l].reshape(1, D_MODEL),
            'wa_bd': _blockdiag_slabs(lru_wa[l]), 'wx_bd': _blockdiag_slabs(lru_wx[l]),
            'lru_ba': lru_ba[l], 'lru_bx': lru_bx[l], 'lru_L': lru_L[l],
            'lru_up': lru_up[l].astype(BF16),
            'w2p': w2p, 'gla_b2': gla_b2[l], 'gla_norm_g': gla_norm_g[l].reshape(1, GLA_DV),
            'gla_up': gla_up[l].astype(BF16), 'w_out': w_out[l].astype(BF16),
            'mlp_w1': mlp_w1[l].astype(BF16), 'mlp_w2': mlp_w2[l].astype(BF16),
        }
        mod = _modulation(cc, w_mod[l], b_mod[l])
        h0_ctx = jnp.zeros((b0, 2, D_MODEL), F32)
        xp, s_lru, s_gla = _layer_group(xp, mod, p, h0_ctx, None, latent=False, want_state=True,
                                        tag="ctx")
        lru_states.append(s_lru)
        gla_states.append(s_gla)
        xs, _, _ = _layer_group(xs, mod, p, state_lru[:, l], state_gla[:, l], latent=True,
                                want_state=False, tag="lat")
    return (xp, xs, jnp.stack(lru_states, axis=1), jnp.stack(gla_states, axis=1))
```

```python
import functools

import numpy as np
import jax
import jax.numpy as jnp
from jax import lax
from jax.experimental import pallas as pl
from jax.experimental.pallas import tpu as pltpu

F32 = jnp.float32
BF16 = jnp.bfloat16

D_MODEL = 1024
GRID_W = 64
LRU_BLOCKS = 16
LRU_BW = D_MODEL // LRU_BLOCKS
LRU_C = 8.0
GLA_HEADS = 4
GLA_DK = 128
GLA_DV = 256
GLA_RANK = 16
GLA_TAU = 16.0
GLA_CHUNK = 64
D_FF = 4 * D_MODEL
N_MOD = 6
EPS = 1e-6

V7X_LANES = 128
V7X_SUBLANES = 8
V7X_MXU_DIM = 256
V7X_VMEM_BYTES = 64 * 1024 * 1024
VMEM_LIMIT = V7X_VMEM_BYTES - 8 * 1024 * 1024

LRU_COLB = V7X_MXU_DIM
LRU_NCOLB = D_MODEL // LRU_COLB
LR_PAD = V7X_LANES
COL_BLOCK = 16
GLA_BLOCK = V7X_MXU_DIM
GLA_OUT_COLS = 16


def _cparams(sem):
    return pltpu.CompilerParams(dimension_semantics=sem, vmem_limit_bytes=VMEM_LIMIT)


def _rms(x, g):
    ms = jnp.mean(x * x, axis=-1, keepdims=True)
    return x * lax.rsqrt(ms + EPS) * g


def _sigmoid(x):
    return 1.0 / (1.0 + jnp.exp(-x))


def _softplus(x):
    return jnp.maximum(x, 0.0) + jnp.log1p(jnp.exp(-jnp.abs(x)))


def _bdot(a, b):
    return jnp.dot(a, b, preferred_element_type=F32)


def _mod_kernel(c_ref, w_ref, b_ref, o_ref):
    c = c_ref[...]
    s = (c * _sigmoid(c)).astype(BF16)
    o_ref[...] = _bdot(s, w_ref[...].astype(BF16)) + b_ref[...]


def _modulation(cc, w_mod, b_mod):
    rows = cc.shape[0]
    return pl.pallas_call(
        _mod_kernel,
        grid=(N_MOD,),
        in_specs=[
            pl.BlockSpec((rows, D_MODEL), lambda n: (0, 0)),
            pl.BlockSpec((D_MODEL, D_MODEL), lambda n: (0, n)),
            pl.BlockSpec((1, D_MODEL), lambda n: (0, n)),
        ],
        out_specs=pl.BlockSpec((rows, D_MODEL), lambda n: (0, n)),
        out_shape=jax.ShapeDtypeStruct((rows, N_MOD * D_MODEL), F32),
        compiler_params=_cparams(("arbitrary",)),
        name="modulation",
    )(cc, w_mod, b_mod.reshape(1, N_MOD * D_MODEL)).reshape(rows, N_MOD, D_MODEL)


def _tok_view(a, grid_tile):
    if grid_tile is None:
        return a
    b, t, n = a.shape
    rows = t // GRID_W
    return a.reshape(b, rows, GRID_W // COL_BLOCK, COL_BLOCK, n)


def _tok_spec(n, tm, grid_tile):
    if grid_tile is None:
        return pl.BlockSpec((None, tm, n), lambda b, j: (b, j, 0))
    rows = grid_tile
    return pl.BlockSpec((None, rows, None, COL_BLOCK, n), lambda b, j: (b, 0, j, 0, 0))


def _tok_shape(b, t, n, grid_tile):
    if grid_tile is None:
        return (b, t, n)
    return (b, grid_tile, GRID_W // COL_BLOCK, COL_BLOCK, n)


def _const_spec(shape):
    nd = len(shape)
    return pl.BlockSpec(shape, lambda b, j: (0,) * nd)


def _mod_spec(per_batch):
    if per_batch:
        return pl.BlockSpec((None, N_MOD, D_MODEL), lambda b, j: (b + 1, 0, 0))
    return pl.BlockSpec((None, N_MOD, D_MODEL), lambda b, j: (0, 0, 0))


def _perm_matrix(rows, cols):
    n = rows * cols
    p = np.zeros((n, n), np.float32)
    for r in range(rows):
        for c in range(cols):
            p[c * rows + r, r * cols + c] = 1.0
    return p


def _inproj_kernel(*refs, n_out, permute):
    x_ref, mod_ref, ng_ref = refs[:3]
    k = 3
    if permute:
        perm_ref = refs[3]
        k = 4
    w_refs = refs[k:k + n_out]
    o_refs = refs[k + n_out:]
    x = x_ref[...].reshape(-1, D_MODEL)
    h = _rms(x, ng_ref[0:1, :]) * (1.0 + mod_ref[1:2, :]) + mod_ref[0:1, :]
    hb = h.astype(BF16)
    if permute:
        hb = _bdot(perm_ref[...], hb).astype(BF16)
    for w_ref, o_ref in zip(w_refs, o_refs):
        z = _bdot(hb, w_ref[...])
        o_ref[...] = z.astype(o_ref.dtype).reshape(o_ref.shape)


def _inproj(x, mod, norm_g, weights, out_dtypes, *, per_batch_mod, tm, col_major, name):
    b, t, _ = x.shape
    if col_major:
        rows = t // GRID_W
        tm = rows * COL_BLOCK
        x_in = _tok_view(x, rows)
        x_spec = _tok_spec(D_MODEL, tm, rows)
        extra = [jnp.asarray(_perm_matrix(rows, COL_BLOCK), BF16)]
        extra_specs = [_const_spec((tm, tm))]
    else:
        x_in = x
        x_spec = _tok_spec(D_MODEL, tm, None)
        extra, extra_specs = [], []
    n_tiles = t // tm
    return pl.pallas_call(
        functools.partial(_inproj_kernel, n_out=len(weights), permute=col_major),
        grid=(b, n_tiles),
        in_specs=[x_spec, _mod_spec(per_batch_mod), _const_spec((4, D_MODEL))]
        + extra_specs + [_const_spec(w.shape) for w in weights],
        out_specs=[_tok_spec(w.shape[1], tm, None) for w in weights],
        out_shape=[jax.ShapeDtypeStruct((b, t, w.shape[1]), dt) for w, dt in zip(weights, out_dtypes)],
        compiler_params=_cparams(("parallel", "parallel")),
        name=name,
    )(x_in, mod, norm_g, *extra, *weights)


def _lru_kernel(zf_ref, zb_ref, h0_ref, cw_ref, cb_ref, wa_ref, wx_ref, ba_ref, bx_ref, l_ref,
                hf_ref, hb_ref, fin_ref, a_s, u_s, carry_s, *, tc, period):
    j = pl.program_id(1)

    @pl.when(j == 0)
    def _():
        carry_s[...] = h0_ref[...]

    t = lax.broadcasted_iota(jnp.int32, (tc, LRU_COLB), 0)
    tp = jnp.bitwise_and(t, period - 1)
    for d, z_ref in ((0, zf_ref), (1, zb_ref)):
        for cb in range(LRU_NCOLB):
            cs = slice(cb * LRU_COLB, (cb + 1) * LRU_COLB)
            x = z_ref[:, cs].astype(F32)
            xm1 = jnp.where(tp >= 1, pltpu.roll(x, 1, 0), 0.0)
            xp1 = jnp.where(tp <= period - 2, pltpu.roll(x, tc - 1, 0), 0.0)
            xp2 = jnp.where(tp <= period - 3, pltpu.roll(x, tc - 2, 0), 0.0)
            xc = (cb_ref[0:1, cs] + xm1 * cw_ref[0:1, cs] + x * cw_ref[1:2, cs]
                  + xp1 * cw_ref[2:3, cs] + xp2 * cw_ref[3:4, cs])
            xcb = xc.astype(BF16)
            r = _sigmoid(_bdot(xcb, wa_ref[d, cb]) + ba_ref[d:d + 1, cs])
            i = _sigmoid(_bdot(xcb, wx_ref[d, cb]) + bx_ref[d:d + 1, cs])
            log_a = (-LRU_C * r) * _softplus(-l_ref[d:d + 1, cs])
            a = jnp.exp(log_a)
            u = jnp.sqrt(1.0 - a * a) * (i * xc)
            a_s[d, :, cs] = a
            u_s[d, :, cs] = u

    sub = lax.broadcasted_iota(jnp.int32, (V7X_SUBLANES, D_MODEL), 0)
    n_tiles = tc // V7X_SUBLANES

    def tile_scan(a, u, forward):
        for s in (1, 2, 4):
            if forward:
                m = sub >= s
                shift = s
            else:
                m = sub < V7X_SUBLANES - s
                shift = V7X_SUBLANES - s
            a_sh = jnp.where(m, pltpu.roll(a, shift, 0), 1.0)
            u_sh = jnp.where(m, pltpu.roll(u, shift, 0), 0.0)
            u = a * u_sh + u
            a = a * a_sh
        return a, u

    def body(i, carry):
        cf, cbk = carry
        rf = pl.multiple_of(i * V7X_SUBLANES, V7X_SUBLANES)
        rb = pl.multiple_of((n_tiles - 1 - i) * V7X_SUBLANES, V7X_SUBLANES)
        af, uf = tile_scan(a_s[0, pl.ds(rf, V7X_SUBLANES), :], u_s[0, pl.ds(rf, V7X_SUBLANES), :], True)
        ab, ub = tile_scan(a_s[1, pl.ds(rb, V7X_SUBLANES), :], u_s[1, pl.ds(rb, V7X_SUBLANES), :], False)
        hf = uf + af * cf
        hb = ub + ab * cbk
        hf_ref[pl.ds(rf, V7X_SUBLANES), :] = hf
        hb_ref[pl.ds(rb, V7X_SUBLANES), :] = hb
        return hf[V7X_SUBLANES - 1:V7X_SUBLANES, :], hb[0:1, :]

    cf, cbk = lax.fori_loop(0, n_tiles, body, (carry_s[0:1, :], carry_s[1:2, :]))
    carry_s[0:1, :] = cf
    carry_s[1:2, :] = cbk
    fin_ref[0:1, :] = cf
    fin_ref[1:2, :] = cbk


def _lru(zx, h0, conv_w, conv_b, wa_bd, wx_bd, ba, bx, lru_l, *, period, name):
    b, t, _ = zx.shape
    tc = min(256, t)
    nc = t // tc
    const = lambda shape: pl.BlockSpec(shape, lambda bb, j: (0,) * len(shape))
    return pl.pallas_call(
        functools.partial(_lru_kernel, tc=tc, period=period),
        grid=(b, nc),
        in_specs=[
            pl.BlockSpec((None, tc, D_MODEL), lambda bb, j: (bb, j, 0)),
            pl.BlockSpec((None, tc, D_MODEL), lambda bb, j: (bb, nc - 1 - j, 0)),
            pl.BlockSpec((None, 2, D_MODEL), lambda bb, j: (bb, 0, 0)),
            const((4, D_MODEL)), const((1, D_MODEL)),
            const((2, LRU_NCOLB, LRU_COLB, LRU_COLB)), const((2, LRU_NCOLB, LRU_COLB, LRU_COLB)),
            const((2, D_MODEL)), const((2, D_MODEL)), const((2, D_MODEL)),
        ],
        out_specs=[
            pl.BlockSpec((None, tc, D_MODEL), lambda bb, j: (bb, j, 0)),
            pl.BlockSpec((None, tc, D_MODEL), lambda bb, j: (bb, nc - 1 - j, 0)),
            pl.BlockSpec((None, 2, D_MODEL), lambda bb, j: (bb, 0, 0)),
        ],
        out_shape=[
            jax.ShapeDtypeStruct((b, t, D_MODEL), F32),
            jax.ShapeDtypeStruct((b, t, D_MODEL), F32),
            jax.ShapeDtypeStruct((b, 2, D_MODEL), F32),
        ],
        scratch_shapes=[
            pltpu.VMEM((2, tc, D_MODEL), F32),
            pltpu.VMEM((2, tc, D_MODEL), F32),
            pltpu.VMEM((2, D_MODEL), F32),
        ],
        compiler_params=_cparams(("parallel", "arbitrary")),
        name=name,
    )(zx, zx, h0, conv_w, conv_b, wa_bd, wx_bd, ba, bx, lru_l)


def _gla_kernel(*refs, t, hps, has_s0, want_final, grid_rows):
    q_ref, k_ref, v_ref, g_ref, lr_ref, w2_ref, b2_ref, gn_ref = refs[:8]
    n = 8
    if grid_rows is not None:
        perm_ref = refs[n]
        n += 1
    if has_s0:
        s0_ref = refs[n]
        n += 1
    on_ref = refs[n]
    n += 1
    if want_final:
        fin_ref = refs[n]
        n += 1
    oacc, st, qe_s, kt_s, bt_s = refs[n:]

    c = GLA_CHUNK
    n_chunks = t // c
    blk = min(GLA_BLOCK, t)
    for d in range(2):
        for hh in range(hps):
            if has_s0:
                st[d, hh] = s0_ref[d, hh].T
            else:
                st[d, hh] = jnp.zeros((GLA_DV, GLA_DK), F32)

    ti = lax.broadcasted_iota(jnp.int32, (blk, blk), 0)
    si = lax.broadcasted_iota(jnp.int32, (blk, blk), 1)
    same = (ti // c) == (si // c)
    masks = (same & (si <= ti), same & (si >= ti))
    cums = tuple(m.astype(BF16) for m in masks)
    scale = GLA_DK ** -0.5

    def block(ib, carry):
        r0 = pl.multiple_of(ib * blk, blk)
        rows = pl.ds(r0, blk)
        lrb = lr_ref[rows, :]
        scores = [[None, None] for _ in range(hps)]
        for d in range(2):
            pre = _bdot(lrb, w2_ref[d]) + b2_ref[d:d + 1, :]
            la = -_softplus(-pre) * (1.0 / GLA_TAU)
            hi = la.astype(BF16)
            r1 = la - hi.astype(F32)
            mid = r1.astype(BF16)
            lo = (r1 - mid.astype(F32)).astype(BF16)
            bcum = _bdot(cums[d], hi) + _bdot(cums[d], mid) + _bdot(cums[d], lo)
            edge = c - 1 if d == 0 else 0
            bc3 = bcum.reshape(blk // c, c, hps * GLA_DK)
            btot = jnp.broadcast_to(bc3[:, edge:edge + 1, :], bc3.shape).reshape(bcum.shape)
            bt_s[d, rows, :] = btot
            for hh in range(hps):
                ks = slice(hh * GLA_DK, (hh + 1) * GLA_DK)
                q = q_ref[rows, ks].astype(F32)
                k = k_ref[rows, ks].astype(F32)
                bc = bcum[:, ks]
                qe = ((q * scale) * jnp.exp(bc)).astype(BF16)
                ke = (k * jnp.exp(-bc)).astype(BF16)
                qe_s[d, rows, ks] = qe
                kt_s[d, rows, ks] = (k * jnp.exp(btot[:, ks] - bc)).astype(BF16)
                scores[hh][d] = lax.dot_general(qe, ke, (((1,), (1,)), ((), ())),
                                                preferred_element_type=F32)
        for hh in range(hps):
            vs = slice(hh * GLA_DV, (hh + 1) * GLA_DV)
            sc = jnp.where(masks[0], scores[hh][0], 0.0) + jnp.where(masks[1], scores[hh][1], 0.0)
            oacc[rows, vs] = _bdot(sc.astype(BF16), v_ref[rows, vs])
        return carry

    lax.fori_loop(0, t // blk, block, 0)

    def chunk(c0, d, hh):
        rows = pl.ds(c0, c)
        ks = slice(hh * GLA_DK, (hh + 1) * GLA_DK)
        vs = slice(hh * GLA_DV, (hh + 1) * GLA_DV)
        s_t = st[d, hh]
        o = lax.dot_general(qe_s[d, rows, ks], s_t.astype(BF16), (((1,), (1,)), ((), ())),
                            preferred_element_type=F32)
        upd = lax.dot_general(v_ref[rows, vs], kt_s[d, rows, ks], (((0,), (0,)), ((), ())),
                              preferred_element_type=F32)
        st[d, hh] = s_t * jnp.exp(bt_s[d, pl.ds(c0, 1), ks]) + upd
        return o

    def body(jj, carry):
        cf = pl.multiple_of(jj * c, c)
        cbk = pl.multiple_of((n_chunks - 1 - jj) * c, c)
        for hh in range(hps):
            vs = slice(hh * GLA_DV, (hh + 1) * GLA_DV)
            for c0, d in ((cf, 0), (cbk, 1)):
                oacc[pl.ds(c0, c), vs] += chunk(c0, d, hh)
        return carry

    lax.fori_loop(0, n_chunks, body, 0, unroll=4)

    if grid_rows is None:
        ob = min(GLA_BLOCK, t)
    else:
        ob = GLA_OUT_COLS * grid_rows

    def out_block(ib, carry):
        r0 = pl.multiple_of(ib * ob, ob)
        rows = pl.ds(r0, ob)
        parts = []
        for hh in range(hps):
            vs = slice(hh * GLA_DV, (hh + 1) * GLA_DV)
            gg = g_ref[rows, vs].astype(F32)
            parts.append((_rms(oacc[rows, vs], gn_ref[...]) * (gg * _sigmoid(gg))).astype(BF16))
        on = jnp.concatenate(parts, axis=1)
        if grid_rows is None:
            on_ref[rows, :] = on
        else:
            on = _bdot(perm_ref[...], on).astype(BF16)
            for r in range(grid_rows):
                dst = pl.multiple_of(r * GRID_W + ib * GLA_OUT_COLS, GLA_OUT_COLS)
                on_ref[pl.ds(dst, GLA_OUT_COLS), :] = on[r * GLA_OUT_COLS:(r + 1) * GLA_OUT_COLS]
        return carry

    lax.fori_loop(0, t // ob, out_block, 0)

    if want_final:
        for d in range(2):
            for hh in range(hps):
                fin_ref[d, hh] = st[d, hh].T


def _gla(q, k, v, g, lr, w2p, b2, gn, s0, *, want_final, hps, col_major, name):
    b, t, _ = q.shape
    ng = GLA_HEADS // hps
    has_s0 = s0 is not None
    grid_rows = t // GRID_W if col_major else None
    in_specs = [
        pl.BlockSpec((None, t, hps * GLA_DK), lambda bb, h: (bb, 0, h)),
        pl.BlockSpec((None, t, hps * GLA_DK), lambda bb, h: (bb, 0, h)),
        pl.BlockSpec((None, t, hps * GLA_DV), lambda bb, h: (bb, 0, h)),
        pl.BlockSpec((None, t, hps * GLA_DV), lambda bb, h: (bb, 0, h)),
        pl.BlockSpec((None, t, LR_PAD), lambda bb, h: (bb, 0, 0)),
        pl.BlockSpec((2, LR_PAD, hps * GLA_DK), lambda bb, h: (0, 0, h)),
        pl.BlockSpec((2, hps * GLA_DK), lambda bb, h: (0, h)),
        pl.BlockSpec((1, GLA_DV), lambda bb, h: (0, 0)),
    ]
    args = [q, k, v, g, lr, w2p, b2, gn]
    if col_major:
        ob = GLA_OUT_COLS * grid_rows
        in_specs.append(pl.BlockSpec((ob, ob), lambda bb, h: (0, 0)))
        args.append(jnp.asarray(_perm_matrix(grid_rows, GLA_OUT_COLS).T, BF16))
    state_spec = pl.BlockSpec((None, 2, hps, GLA_DK, GLA_DV), lambda bb, h: (bb, 0, h, 0, 0))
    if has_s0:
        in_specs.append(state_spec)
        args.append(s0)
    out_specs = [pl.BlockSpec((None, t, hps * GLA_DV), lambda bb, h: (bb, 0, h))]
    out_shape = [jax.ShapeDtypeStruct((b, t, GLA_HEADS * GLA_DV), BF16)]
    if want_final:
        out_specs.append(state_spec)
        out_shape.append(jax.ShapeDtypeStruct((b, 2, GLA_HEADS, GLA_DK, GLA_DV), F32))
    return pl.pallas_call(
        functools.partial(_gla_kernel, t=t, hps=hps, has_s0=has_s0, want_final=want_final,
                          grid_rows=grid_rows),
        grid=(b, ng),
        in_specs=in_specs,
        out_specs=out_specs,
        out_shape=out_shape,
        scratch_shapes=[
            pltpu.VMEM((t, hps * GLA_DV), F32),
            pltpu.VMEM((2, hps, GLA_DV, GLA_DK), F32),
            pltpu.VMEM((2, t, hps * GLA_DK), BF16),
            pltpu.VMEM((2, t, hps * GLA_DK), BF16),
            pltpu.VMEM((2, t, hps * GLA_DK), F32),
        ],
        compiler_params=_cparams(("parallel", "parallel")),
        name=name,
    )(*args)


def _gelu_tanh(x):
    return x * (0.5 * (1.0 + jnp.tanh(0.7978845608028654 * (x + 0.044715 * (x * x * x)))))


def _merge_kernel(x_ref, hf_ref, hb_ref, zg_ref, ga_ref, gb_ref, on_ref, mod_ref, ng_ref,
                  lup_ref, gup_ref, wo_ref, o_ref):
    ld = lambda r: r[...].astype(F32)
    hsum = ld(hf_ref) + ld(hb_ref)
    y_a = _bdot((hsum * _gelu_tanh(ld(zg_ref))).astype(BF16), lup_ref[...])
    y_b = _bdot(on_ref[...], gup_ref[...])
    mm = _sigmoid(ld(ga_ref)) * y_a + _sigmoid(ld(gb_ref)) * y_b
    m = _bdot(mm.astype(BF16), wo_ref[...])
    o_ref[...] = x_ref[...] + mod_ref[2:3, :] * _rms(m, ng_ref[1:2, :])


def _merge(x, hf, hb, zg, ga, gb, on, mod, norm_g, lru_up, gla_up, w_out, *, per_batch_mod, tm,
           name):
    b, t, _ = x.shape
    spec = _tok_spec(D_MODEL, tm, None)
    wspec = _const_spec((D_MODEL, D_MODEL))
    return pl.pallas_call(
        _merge_kernel,
        grid=(b, t // tm),
        in_specs=[spec] * 7 + [_mod_spec(per_batch_mod), _const_spec((4, D_MODEL))] + [wspec] * 3,
        out_specs=spec,
        out_shape=jax.ShapeDtypeStruct((b, t, D_MODEL), F32),
        compiler_params=_cparams(("parallel", "parallel")),
        name=name,
    )(x, hf, hb, zg, ga, gb, on, mod, norm_g, lru_up, gla_up, w_out)


def _ffn_kernel(x_ref, mod_ref, ng_ref, w1_ref, w2_ref, o_ref, *, n_split):
    x = x_ref[...]
    h = (_rms(x, ng_ref[2:3, :]) * (1.0 + mod_ref[4:5, :]) + mod_ref[3:4, :]).astype(BF16)
    fs = D_FF // n_split
    acc = jnp.zeros(x.shape, F32)
    for kf in range(n_split):
        hid = jnp.maximum(_bdot(h, w1_ref[:, kf * fs:(kf + 1) * fs]), 0.0)
        acc = acc + _bdot((hid * hid).astype(BF16), w2_ref[kf * fs:(kf + 1) * fs, :])
    o_ref[...] = x + mod_ref[5:6, :] * _rms(acc, ng_ref[3:4, :])


def _ffn(x, mod, norm_g, w1, w2, *, per_batch_mod, tm, name):
    b, t, _ = x.shape
    spec = _tok_spec(D_MODEL, tm, None)
    return pl.pallas_call(
        functools.partial(_ffn_kernel, n_split=4),
        grid=(b, t // tm),
        in_specs=[spec, _mod_spec(per_batch_mod), _const_spec((4, D_MODEL)),
                  _const_spec((D_MODEL, D_FF)), _const_spec((D_FF, D_MODEL))],
        out_specs=spec,
        out_shape=jax.ShapeDtypeStruct((b, t, D_MODEL), F32),
        compiler_params=_cparams(("parallel", "parallel")),
        name=name,
    )(x, mod, norm_g, w1, w2)


def _blockdiag_slabs(w):
    per = LRU_COLB // LRU_BW
    w = w.reshape(2, LRU_NCOLB, per, LRU_BW, LRU_BW)
    eye = jnp.eye(per, dtype=w.dtype)
    slab = jnp.einsum('dspij,pq->dspiqj', w, eye)
    return slab.reshape(2, LRU_NCOLB, LRU_COLB, LRU_COLB).astype(BF16)


def _layer_group(x, mod, p, lru_h0, gla_s0, *, latent, want_state, tag):
    b, t, _ = x.shape
    per_batch = latent
    tm = 256 if t % 256 == 0 else t
    zx, zg, ga, gb = _inproj(x, mod, p['norm_g'], p['w_rm'], (BF16,) * 4, per_batch_mod=per_batch,
                             tm=tm, col_major=False, name=f"inproj_rm_{tag}")
    q, k, v, g, lr = _inproj(x, mod, p['norm_g'], p['w_cm'], (BF16,) * 5,
                             per_batch_mod=per_batch, tm=tm, col_major=latent,
                             name=f"inproj_cm_{tag}")
    period = GRID_W if latent else t
    hf, hb, s_lru = _lru(zx, lru_h0, p['conv_w'], p['conv_b'], p['wa_bd'], p['wx_bd'], p['lru_ba'],
                         p['lru_bx'], p['lru_L'], period=period, name=f"lru_{tag}")
    res = _gla(q, k, v, g, lr, p['w2p'], p['gla_b2'], p['gla_norm_g'], gla_s0,
               want_final=want_state, hps=2, col_major=latent, name=f"gla_{tag}")
    on = res[0]
    s_gla = res[1] if want_state else None
    x1 = _merge(x, hf, hb, zg, ga, gb, on, mod, p['norm_g'], p['lru_up'], p['gla_up'], p['w_out'],
                per_batch_mod=per_batch, tm=tm, name=f"merge_{tag}")
    y = _ffn(x1, mod, p['norm_g'], p['mlp_w1'], p['mlp_w2'], per_batch_mod=per_batch, tm=tm,
             name=f"ffn_{tag}")
    return y, s_lru, s_gla


def kernel(x_prompt, x_sample, state_lru, state_gla, c, c_ctx, w_mod, b_mod, norm_g, w_in, conv_w, conv_b,
           lru_wa, lru_ba, lru_wx, lru_bx, lru_L, lru_up, gla_w2, gla_b2, gla_norm_g, gla_up, w_out,
           mlp_w1, mlp_w2):
    depth = w_in.shape[0]
    assert depth == 1
    dec_b = x_sample.shape[0]
    b0 = x_prompt.shape[0]
    xp, xs = x_prompt, x_sample
    lru_states, gla_states = [], []
    mod_rows = -(-(1 + dec_b) // V7X_SUBLANES) * V7X_SUBLANES
    cc = jnp.zeros((mod_rows, D_MODEL), F32).at[0].set(c_ctx).at[1:1 + dec_b].set(c)
    for l in range(depth):
        wl = w_in[l]
        o = np.cumsum((0, D_MODEL, D_MODEL, 512, 512, 1024, 1024, 2 * GLA_RANK, D_MODEL, D_MODEL))
        col = lambda i: wl[:, o[i]:o[i + 1]].astype(BF16)
        lr_w = jnp.pad(col(6), ((0, 0), (0, LR_PAD - 2 * GLA_RANK)))
        w2 = gla_w2[l].astype(BF16)
        w2p = jnp.zeros((2, LR_PAD, GLA_HEADS * GLA_DK), BF16)
        w2p = w2p.at[0, 0:GLA_RANK].set(w2[0]).at[1, GLA_RANK:2 * GLA_RANK].set(w2[1])
        p = {
            'norm_g': norm_g[l],
            'w_rm': [col(0), col(1), col(7), col(8)],
            'w_cm': [col(2), col(3), col(4), col(5), lr_w],
            'conv_w': conv_w[l], 'conv_b': conv_b[l].reshape(1, D_MODEL),
            'wa_bd': _blockdiag_slabs(lru_wa[l]), 'wx_bd': _blockdiag_slabs(lru_wx[l]),
            'lru_ba': lru_ba[l], 'lru_bx': lru_bx[l], 'lru_L': lru_L[l],
            'lru_up': lru_up[l].astype(BF16),
            'w2p': w2p, 'gla_b2': gla_b2[l], 'gla_norm_g': gla_norm_g[l].reshape(1, GLA_DV),
            'gla_up': gla_up[l].astype(BF16), 'w_out': w_out[l].astype(BF16),
            'mlp_w1': mlp_w1[l].astype(BF16), 'mlp_w2': mlp_w2[l].astype(BF16),
        }
        mod = _modulation(cc, w_mod[l], b_mod[l])
        h0_ctx = jnp.zeros((b0, 2, D_MODEL), F32)
        xp, s_lru, s_gla = _layer_group(xp, mod, p, h0_ctx, None, latent=False, want_state=True,
                                        tag="ctx")
        lru_states.append(s_lru)
        gla_states.append(s_gla)
        xs, _, _ = _layer_group(xs, mod, p, state_lru[:, l], state_gla[:, l], latent=True,
                                want_state=False, tag="lat")
    return (xp, xs, jnp.stack(lru_states, axis=1), jnp.stack(gla_states, axis=1))
```

```python
import functools

import numpy as np
import jax
import jax.numpy as jnp
from jax import lax
from jax.experimental import pallas as pl
from jax.experimental.pallas import tpu as pltpu

F32 = jnp.float32
BF16 = jnp.bfloat16

D_MODEL = 1024
GRID_W = 64
LRU_BLOCKS = 16
LRU_BW = D_MODEL // LRU_BLOCKS
LRU_C = 8.0
GLA_HEADS = 4
GLA_DK = 128
GLA_DV = 256
GLA_RANK = 16
GLA_TAU = 16.0
GLA_CHUNK = 64
D_FF = 4 * D_MODEL
N_MOD = 6
EPS = 1e-6
LOG2_E = 1.4426950408889634

V7X_LANES = 128
V7X_SUBLANES = 8
V7X_MXU_DIM = 256
V7X_VMEM_BYTES = 64 * 1024 * 1024
VMEM_LIMIT = V7X_VMEM_BYTES - 8 * 1024 * 1024

LRU_COLB = V7X_MXU_DIM
LRU_NCOLB = D_MODEL // LRU_COLB
LR_PAD = V7X_LANES
COL_BLOCK = 16
GLA_BLOCK = V7X_MXU_DIM
GLA_OUT_COLS = 16
LRU_CHUNK = 256


def _cparams(sem):
    return pltpu.CompilerParams(dimension_semantics=sem, vmem_limit_bytes=VMEM_LIMIT)


def _rms(x, g):
    ms = jnp.mean(x * x, axis=-1, keepdims=True)
    return x * lax.rsqrt(ms + EPS) * g


def _sigmoid(x):
    return 1.0 / (1.0 + jnp.exp(-x))


def _softplus(x):
    return jnp.maximum(x, 0.0) + jnp.log1p(jnp.exp(-jnp.abs(x)))


def _bdot(a, b):
    return jnp.dot(a, b, preferred_element_type=F32)


def _mod_kernel(c_ref, w_ref, b_ref, o_ref):
    c = c_ref[...]
    s = (c * _sigmoid(c)).astype(BF16)
    o_ref[...] = _bdot(s, w_ref[...].astype(BF16)) + b_ref[...]


def _modulation(cc, w_mod, b_mod):
    rows = cc.shape[0]
    return pl.pallas_call(
        _mod_kernel,
        grid=(N_MOD,),
        in_specs=[
            pl.BlockSpec((rows, D_MODEL), lambda n: (0, 0)),
            pl.BlockSpec((D_MODEL, D_MODEL), lambda n: (0, n)),
            pl.BlockSpec((1, D_MODEL), lambda n: (0, n)),
        ],
        out_specs=pl.BlockSpec((rows, D_MODEL), lambda n: (0, n)),
        out_shape=jax.ShapeDtypeStruct((rows, N_MOD * D_MODEL), F32),
        compiler_params=_cparams(("arbitrary",)),
        name="modulation",
    )(cc, w_mod, b_mod.reshape(1, N_MOD * D_MODEL)).reshape(rows, N_MOD, D_MODEL)


def _tok_view(a, grid_tile):
    if grid_tile is None:
        return a
    b, t, n = a.shape
    rows = t // GRID_W
    return a.reshape(b, rows, GRID_W // COL_BLOCK, COL_BLOCK, n)


def _tok_spec(n, tm, grid_tile):
    if grid_tile is None:
        return pl.BlockSpec((None, tm, n), lambda b, j: (b, j, 0))
    rows = grid_tile
    return pl.BlockSpec((None, rows, None, COL_BLOCK, n), lambda b, j: (b, 0, j, 0, 0))


def _const_spec(shape):
    nd = len(shape)
    return pl.BlockSpec(shape, lambda b, j: (0,) * nd)


def _mod_spec(per_batch):
    if per_batch:
        return pl.BlockSpec((None, N_MOD, D_MODEL), lambda b, j: (b + 1, 0, 0))
    return pl.BlockSpec((None, N_MOD, D_MODEL), lambda b, j: (0, 0, 0))


def _perm_matrix(rows, cols):
    n = rows * cols
    p = np.zeros((n, n), np.float32)
    for r in range(rows):
        for c in range(cols):
            p[c * rows + r, r * cols + c] = 1.0
    return p


def _inproj_kernel(*refs, n_out, permute):
    x_ref, mod_ref, ng_ref = refs[:3]
    k = 3
    if permute:
        perm_ref = refs[3]
        k = 4
    w_refs = refs[k:k + n_out]
    o_refs = refs[k + n_out:]
    x = x_ref[...].reshape(-1, D_MODEL)
    h = _rms(x, ng_ref[0:1, :]) * (1.0 + mod_ref[1:2, :]) + mod_ref[0:1, :]
    hb = h.astype(BF16)
    if permute:
        hb = _bdot(perm_ref[...], hb).astype(BF16)
    for w_ref, o_ref in zip(w_refs, o_refs):
        z = _bdot(hb, w_ref[...])
        o_ref[...] = z.astype(o_ref.dtype).reshape(o_ref.shape)


def _inproj(x, mod, norm_g, weights, *, per_batch_mod, order, name):
    b, t, _ = x.shape
    if order == 'col_major':
        rows = t // GRID_W
        tm = rows * COL_BLOCK
        x_in = _tok_view(x, rows)
        x_spec = _tok_spec(D_MODEL, tm, rows)
        perm = _perm_matrix(rows, COL_BLOCK)
    else:
        tm = _lru_chunk(t)
        x_in = x
        x_spec = _tok_spec(D_MODEL, tm, None)
        perm = _interleave_matrix(tm) if order == 'interleaved' else None
    extra = [] if perm is None else [jnp.asarray(perm, BF16)]
    return pl.pallas_call(
        functools.partial(_inproj_kernel, n_out=len(weights), permute=perm is not None),
        grid=(b, t // tm),
        in_specs=[x_spec, _mod_spec(per_batch_mod), _const_spec((4, D_MODEL))]
        + [_const_spec(e.shape) for e in extra] + [_const_spec(w.shape) for w in weights],
        out_specs=[_tok_spec(w.shape[1], tm, None) for w in weights],
        out_shape=[jax.ShapeDtypeStruct((b, t, w.shape[1]), BF16) for w in weights],
        compiler_params=_cparams(("parallel", "parallel")),
        name=name,
    )(x_in, mod, norm_g, *extra, *weights)


def _lru_chunk(t):
    return min(LRU_CHUNK, t)


def _interleave_matrix(tc):
    return _perm_matrix(V7X_SUBLANES, tc // V7X_SUBLANES)


def _lru_kernel(zf_ref, zb_ref, h0_ref, cw_ref, cb_ref, wa_ref, wx_ref, ba_ref, bx_ref, l_ref,
                hf_ref, hb_ref, fin_ref, a_s, u_s, carry_s, *, tc, period):
    j = pl.program_id(1)
    ns = V7X_SUBLANES
    sl = tc // ns

    @pl.when(j == 0)
    def _():
        carry_s[...] = h0_ref[...]

    sub_c = lax.broadcasted_iota(jnp.int32, (ns, LRU_COLB), 0)
    prev_ok = jnp.bitwise_and(sub_c * sl, period - 1) != 0
    next_ok = jnp.bitwise_and((sub_c + 1) * sl, period - 1) != 0
    for d, z_ref in ((0, zf_ref), (1, zb_ref)):
        for cb in range(LRU_NCOLB):
            cs = slice(cb * LRU_COLB, (cb + 1) * LRU_COLB)
            x = z_ref[:, cs].astype(F32)
            e_prev = jnp.where(prev_ok, pltpu.roll(x[tc - ns:], 1, 0), 0.0)
            e_next0 = jnp.where(next_ok, pltpu.roll(x[:ns], ns - 1, 0), 0.0)
            e_next1 = jnp.where(next_ok, pltpu.roll(x[ns:2 * ns], ns - 1, 0), 0.0)
            xm1 = jnp.concatenate([e_prev, x[:tc - ns]], axis=0)
            xp1 = jnp.concatenate([x[ns:], e_next0], axis=0)
            xp2 = jnp.concatenate([x[2 * ns:], e_next0, e_next1], axis=0)
            xc = (cb_ref[0:1, cs] + xm1 * cw_ref[0:1, cs] + x * cw_ref[1:2, cs]
                  + xp1 * cw_ref[2:3, cs] + xp2 * cw_ref[3:4, cs])
            xcb = xc.astype(BF16)
            tr = jnp.tanh(_bdot(xcb, wa_ref[d, cb]) + 0.5 * ba_ref[d:d + 1, cs])
            ti = jnp.tanh(_bdot(xcb, wx_ref[d, cb]) + 0.5 * bx_ref[d:d + 1, cs])
            k2 = (-0.5 * LRU_C * LOG2_E) * _softplus(-l_ref[d:d + 1, cs])
            a = jnp.exp2(tr * k2 + k2)
            u = jnp.sqrt(1.0 - a * a) * ((0.5 * ti + 0.5) * xc)
            a_s[d, :, cs] = a
            u_s[d, :, cs] = u

    unroll = 4
    row = lambda i: pl.ds(pl.multiple_of(i * ns, ns), ns)

    def pass1(g, carry):
        hf, af, hb, ab = carry
        for k in range(unroll):
            rf = row(g * unroll + k)
            rb = row(sl - 1 - (g * unroll + k))
            a = a_s[0, rf, :]
            hf = a * hf + u_s[0, rf, :]
            af = a * af
            u_s[0, rf, :] = hf
            a_s[0, rf, :] = af
            a = a_s[1, rb, :]
            hb = a * hb + u_s[1, rb, :]
            ab = a * ab
            u_s[1, rb, :] = hb
            a_s[1, rb, :] = ab
        return hf, af, hb, ab

    zeros = jnp.zeros((ns, D_MODEL), F32)
    ones = jnp.ones((ns, D_MODEL), F32)
    hf, af, hb, ab = lax.fori_loop(0, sl // unroll, pass1, (zeros, ones, zeros, ones))

    sub = lax.broadcasted_iota(jnp.int32, (ns, D_MODEL), 0)

    def sublane_scan(a, u, forward):
        for s in (1, 2, 4):
            if forward:
                m = sub >= s
                shift = s
            else:
                m = sub < ns - s
                shift = ns - s
            a_sh = jnp.where(m, pltpu.roll(a, shift, 0), 1.0)
            u_sh = jnp.where(m, pltpu.roll(u, shift, 0), 0.0)
            u = a * u_sh + u
            a = a * a_sh
        return a, u

    cin_f = carry_s[0:1, :]
    cin_b = carry_s[1:2, :]
    af, hf = sublane_scan(af, hf, True)
    ab, hb = sublane_scan(ab, hb, False)
    fin_f = hf + af * cin_f
    fin_b = hb + ab * cin_b
    enter_f = jnp.where(sub == 0, cin_f, pltpu.roll(fin_f, 1, 0))
    enter_b = jnp.where(sub == ns - 1, cin_b, pltpu.roll(fin_b, ns - 1, 0))
    out_f = fin_f[ns - 1:ns, :]
    out_b = fin_b[0:1, :]
    carry_s[0:1, :] = out_f
    carry_s[1:2, :] = out_b
    fin_ref[0:1, :] = out_f
    fin_ref[1:2, :] = out_b

    pk = 2 * ns
    enter_f2 = jnp.concatenate([enter_f, enter_f], axis=0)
    enter_b2 = jnp.concatenate([enter_b, enter_b], axis=0)

    def pass2(g, carry):
        for k in range(unroll // 2):
            rr = pl.ds(pl.multiple_of((g * (unroll // 2) + k) * pk, pk), pk)
            hf_ref[rr, :] = (u_s[0, rr, :] + a_s[0, rr, :] * enter_f2).astype(hf_ref.dtype)
            hb_ref[rr, :] = (u_s[1, rr, :] + a_s[1, rr, :] * enter_b2).astype(hb_ref.dtype)
        return carry

    lax.fori_loop(0, sl // unroll, pass2, 0)


def _lru(zx, h0, conv_w, conv_b, wa_bd, wx_bd, ba, bx, lru_l, *, period, name):
    b, t, _ = zx.shape
    tc = _lru_chunk(t)
    nc = t // tc
    assert period % (tc // V7X_SUBLANES) == 0 and tc % period == 0
    const = lambda shape: pl.BlockSpec(shape, lambda bb, j: (0,) * len(shape))
    return pl.pallas_call(
        functools.partial(_lru_kernel, tc=tc, period=period),
        grid=(b, nc),
        in_specs=[
            pl.BlockSpec((None, tc, D_MODEL), lambda bb, j: (bb, j, 0)),
            pl.BlockSpec((None, tc, D_MODEL), lambda bb, j: (bb, nc - 1 - j, 0)),
            pl.BlockSpec((None, 2, D_MODEL), lambda bb, j: (bb, 0, 0)),
            const((4, D_MODEL)), const((1, D_MODEL)),
            const((2, LRU_NCOLB, LRU_COLB, LRU_COLB)), const((2, LRU_NCOLB, LRU_COLB, LRU_COLB)),
            const((2, D_MODEL)), const((2, D_MODEL)), const((2, D_MODEL)),
        ],
        out_specs=[
            pl.BlockSpec((None, tc, D_MODEL), lambda bb, j: (bb, j, 0)),
            pl.BlockSpec((None, tc, D_MODEL), lambda bb, j: (bb, nc - 1 - j, 0)),
            pl.BlockSpec((None, 2, D_MODEL), lambda bb, j: (bb, 0, 0)),
        ],
        out_shape=[
            jax.ShapeDtypeStruct((b, t, D_MODEL), BF16),
            jax.ShapeDtypeStruct((b, t, D_MODEL), BF16),
            jax.ShapeDtypeStruct((b, 2, D_MODEL), F32),
        ],
        scratch_shapes=[
            pltpu.VMEM((2, tc, D_MODEL), F32),
            pltpu.VMEM((2, tc, D_MODEL), F32),
            pltpu.VMEM((2, D_MODEL), F32),
        ],
        compiler_params=_cparams(("parallel", "arbitrary")),
        name=name,
    )(zx, zx, h0, conv_w, conv_b, wa_bd, wx_bd, ba, bx, lru_l)


def _gla_kernel(*refs, t, hps, has_s0, want_final, grid_rows):
    q_ref, k_ref, v_ref, g_ref, lr_ref, w2_ref, b2_ref, gn_ref = refs[:8]
    n = 8
    if grid_rows is not None:
        perm_ref = refs[n]
        n += 1
    if has_s0:
        s0_ref = refs[n]
        n += 1
    on_ref = refs[n]
    n += 1
    if want_final:
        fin_ref = refs[n]
        n += 1
    oacc, st, qe_s, kt_s, bt_s = refs[n:]

    c = GLA_CHUNK
    n_chunks = t // c
    blk = min(GLA_BLOCK, t)
    for d in range(2):
        for hh in range(hps):
            if has_s0:
                st[d, hh] = s0_ref[d, hh].T
            else:
                st[d, hh] = jnp.zeros((GLA_DV, GLA_DK), F32)

    ti = lax.broadcasted_iota(jnp.int32, (blk, blk), 0)
    si = lax.broadcasted_iota(jnp.int32, (blk, blk), 1)
    same = (ti // c) == (si // c)
    masks = (same & (si <= ti), same & (si >= ti))
    cums = tuple(m.astype(BF16) for m in masks)
    scale = GLA_DK ** -0.5

    def block(ib, carry):
        r0 = pl.multiple_of(ib * blk, blk)
        rows = pl.ds(r0, blk)
        lrb = lr_ref[rows, :]
        scores = [[None, None] for _ in range(hps)]
        for d in range(2):
            pre = _bdot(lrb, w2_ref[d]) + b2_ref[d:d + 1, :]
            la = -_softplus(-pre) * (1.0 / GLA_TAU)
            hi = la.astype(BF16)
            r1 = la - hi.astype(F32)
            mid = r1.astype(BF16)
            lo = (r1 - mid.astype(F32)).astype(BF16)
            bcum = _bdot(cums[d], hi) + _bdot(cums[d], mid) + _bdot(cums[d], lo)
            edge = c - 1 if d == 0 else 0
            bc3 = bcum.reshape(blk // c, c, hps * GLA_DK)
            btot = jnp.broadcast_to(bc3[:, edge:edge + 1, :], bc3.shape).reshape(bcum.shape)
            bt_s[d, rows, :] = btot
            for hh in range(hps):
                ks = slice(hh * GLA_DK, (hh + 1) * GLA_DK)
                q = q_ref[rows, ks].astype(F32)
                k = k_ref[rows, ks].astype(F32)
                bc = bcum[:, ks]
                qe = ((q * scale) * jnp.exp(bc)).astype(BF16)
                ke = (k * jnp.exp(-bc)).astype(BF16)
                qe_s[d, rows, ks] = qe
                kt_s[d, rows, ks] = (k * jnp.exp(btot[:, ks] - bc)).astype(BF16)
                scores[hh][d] = lax.dot_general(qe, ke, (((1,), (1,)), ((), ())),
                                                preferred_element_type=F32)
        for hh in range(hps):
            vs = slice(hh * GLA_DV, (hh + 1) * GLA_DV)
            sc = jnp.where(masks[0], scores[hh][0], 0.0) + jnp.where(masks[1], scores[hh][1], 0.0)
            oacc[rows, vs] = _bdot(sc.astype(BF16), v_ref[rows, vs])
        return carry

    lax.fori_loop(0, t // blk, block, 0)

    def chunk(c0, d, hh):
        rows = pl.ds(c0, c)
        ks = slice(hh * GLA_DK, (hh + 1) * GLA_DK)
        vs = slice(hh * GLA_DV, (hh + 1) * GLA_DV)
        s_t = st[d, hh]
        o = lax.dot_general(qe_s[d, rows, ks], s_t.astype(BF16), (((1,), (1,)), ((), ())),
                            preferred_element_type=F32)
        upd = lax.dot_general(v_ref[rows, vs], kt_s[d, rows, ks], (((0,), (0,)), ((), ())),
                              preferred_element_type=F32)
        st[d, hh] = s_t * jnp.exp(bt_s[d, pl.ds(c0, 1), ks]) + upd
        return o

    def body(jj, carry):
        cf = pl.multiple_of(jj * c, c)
        cbk = pl.multiple_of((n_chunks - 1 - jj) * c, c)
        for hh in range(hps):
            vs = slice(hh * GLA_DV, (hh + 1) * GLA_DV)
            for c0, d in ((cf, 0), (cbk, 1)):
                oacc[pl.ds(c0, c), vs] += chunk(c0, d, hh)
        return carry

    lax.fori_loop(0, n_chunks, body, 0, unroll=4)

    if grid_rows is None:
        ob = min(GLA_BLOCK, t)
    else:
        ob = GLA_OUT_COLS * grid_rows

    def out_block(ib, carry):
        r0 = pl.multiple_of(ib * ob, ob)
        rows = pl.ds(r0, ob)
        parts = []
        for hh in range(hps):
            vs = slice(hh * GLA_DV, (hh + 1) * GLA_DV)
            gg = g_ref[rows, vs].astype(F32)
            parts.append((_rms(oacc[rows, vs], gn_ref[...]) * (gg * _sigmoid(gg))).astype(BF16))
        on = jnp.concatenate(parts, axis=1)
        if grid_rows is None:
            on_ref[rows, :] = on
        else:
            on = _bdot(perm_ref[...], on).astype(BF16)
            for r in range(grid_rows):
                dst = pl.multiple_of(r * GRID_W + ib * GLA_OUT_COLS, GLA_OUT_COLS)
                on_ref[pl.ds(dst, GLA_OUT_COLS), :] = on[r * GLA_OUT_COLS:(r + 1) * GLA_OUT_COLS]
        return carry

    lax.fori_loop(0, t // ob, out_block, 0)

    if want_final:
        for d in range(2):
            for hh in range(hps):
                fin_ref[d, hh] = st[d, hh].T


def _gla(q, k, v, g, lr, w2p, b2, gn, s0, *, want_final, hps, col_major, name):
    b, t, _ = q.shape
    ng = GLA_HEADS // hps
    has_s0 = s0 is not None
    grid_rows = t // GRID_W if col_major else None
    in_specs = [
        pl.BlockSpec((None, t, hps * GLA_DK), lambda bb, h: (bb, 0, h)),
        pl.BlockSpec((None, t, hps * GLA_DK), lambda bb, h: (bb, 0, h)),
        pl.BlockSpec((None, t, hps * GLA_DV), lambda bb, h: (bb, 0, h)),
        pl.BlockSpec((None, t, hps * GLA_DV), lambda bb, h: (bb, 0, h)),
        pl.BlockSpec((None, t, LR_PAD), lambda bb, h: (bb, 0, 0)),
        pl.BlockSpec((2, LR_PAD, hps * GLA_DK), lambda bb, h: (0, 0, h)),
        pl.BlockSpec((2, hps * GLA_DK), lambda bb, h: (0, h)),
        pl.BlockSpec((1, GLA_DV), lambda bb, h: (0, 0)),
    ]
    args = [q, k, v, g, lr, w2p, b2, gn]
    if col_major:
        ob = GLA_OUT_COLS * grid_rows
        in_specs.append(pl.BlockSpec((ob, ob), lambda bb, h: (0, 0)))
        args.append(jnp.asarray(_perm_matrix(grid_rows, GLA_OUT_COLS).T, BF16))
    state_spec = pl.BlockSpec((None, 2, hps, GLA_DK, GLA_DV), lambda bb, h: (bb, 0, h, 0, 0))
    if has_s0:
        in_specs.append(state_spec)
        args.append(s0)
    out_specs = [pl.BlockSpec((None, t, hps * GLA_DV), lambda bb, h: (bb, 0, h))]
    out_shape = [jax.ShapeDtypeStruct((b, t, GLA_HEADS * GLA_DV), BF16)]
    if want_final:
        out_specs.append(state_spec)
        out_shape.append(jax.ShapeDtypeStruct((b, 2, GLA_HEADS, GLA_DK, GLA_DV), F32))
    return pl.pallas_call(
        functools.partial(_gla_kernel, t=t, hps=hps, has_s0=has_s0, want_final=want_final,
                          grid_rows=grid_rows),
        grid=(b, ng),
        in_specs=in_specs,
        out_specs=out_specs,
        out_shape=out_shape,
        scratch_shapes=[
            pltpu.VMEM((t, hps * GLA_DV), F32),
            pltpu.VMEM((2, hps, GLA_DV, GLA_DK), F32),
            pltpu.VMEM((2, t, hps * GLA_DK), BF16),
            pltpu.VMEM((2, t, hps * GLA_DK), BF16),
            pltpu.VMEM((2, t, hps * GLA_DK), F32),
        ],
        compiler_params=_cparams(("parallel", "parallel")),
        name=name,
    )(*args)


def _gelu_tanh(x):
    return x * (0.5 * (1.0 + jnp.tanh(0.7978845608028654 * (x + 0.044715 * (x * x * x)))))


def _merge_kernel(x_ref, hf_ref, hb_ref, zg_ref, ga_ref, gb_ref, on_ref, mod_ref, ng_ref,
                  perm_ref, unperm_ref, lup_ref, gup_ref, wo_ref, o_ref):
    ld = lambda r: r[...].astype(F32)
    hsum = ld(hf_ref) + ld(hb_ref)
    y_a = _bdot((hsum * _gelu_tanh(ld(zg_ref))).astype(BF16), lup_ref[...])
    on = _bdot(perm_ref[...], on_ref[...]).astype(BF16)
    y_b = _bdot(on, gup_ref[...])
    mm = (_sigmoid(ld(ga_ref)) * y_a + _sigmoid(ld(gb_ref)) * y_b).astype(BF16)
    mm = _bdot(unperm_ref[...], mm).astype(BF16)
    m = _bdot(mm, wo_ref[...])
    o_ref[...] = x_ref[...] + mod_ref[2:3, :] * _rms(m, ng_ref[1:2, :])


def _merge(x, hf, hb, zg, ga, gb, on, mod, norm_g, lru_up, gla_up, w_out, *, per_batch_mod, name):
    b, t, _ = x.shape
    tm = _lru_chunk(t)
    perm = _interleave_matrix(tm)
    spec = _tok_spec(D_MODEL, tm, None)
    wspec = _const_spec((D_MODEL, D_MODEL))
    pspec = _const_spec((tm, tm))
    return pl.pallas_call(
        _merge_kernel,
        grid=(b, t // tm),
        in_specs=[spec] * 7 + [_mod_spec(per_batch_mod), _const_spec((4, D_MODEL)), pspec, pspec]
        + [wspec] * 3,
        out_specs=spec,
        out_shape=jax.ShapeDtypeStruct((b, t, D_MODEL), F32),
        compiler_params=_cparams(("parallel", "parallel")),
        name=name,
    )(x, hf, hb, zg, ga, gb, on, mod, norm_g, jnp.asarray(perm, BF16), jnp.asarray(perm.T, BF16),
      lru_up, gla_up, w_out)


def _ffn_kernel(x_ref, mod_ref, ng_ref, w1_ref, w2_ref, o_ref, *, n_split):
    x = x_ref[...]
    h = (_rms(x, ng_ref[2:3, :]) * (1.0 + mod_ref[4:5, :]) + mod_ref[3:4, :]).astype(BF16)
    fs = D_FF // n_split
    acc = jnp.zeros(x.shape, F32)
    for kf in range(n_split):
        hid = jnp.maximum(_bdot(h, w1_ref[:, kf * fs:(kf + 1) * fs]), 0.0)
        acc = acc + _bdot((hid * hid).astype(BF16), w2_ref[kf * fs:(kf + 1) * fs, :])
    o_ref[...] = x + mod_ref[5:6, :] * _rms(acc, ng_ref[3:4, :])


def _ffn(x, mod, norm_g, w1, w2, *, per_batch_mod, tm, name):
    b, t, _ = x.shape
    spec = _tok_spec(D_MODEL, tm, None)
    return pl.pallas_call(
        functools.partial(_ffn_kernel, n_split=4),
        grid=(b, t // tm),
        in_specs=[spec, _mod_spec(per_batch_mod), _const_spec((4, D_MODEL)),
                  _const_spec((D_MODEL, D_FF)), _const_spec((D_FF, D_MODEL))],
        out_specs=spec,
        out_shape=jax.ShapeDtypeStruct((b, t, D_MODEL), F32),
        compiler_params=_cparams(("parallel", "parallel")),
        name=name,
    )(x, mod, norm_g, w1, w2)


def _blockdiag_slabs(w):
    per = LRU_COLB // LRU_BW
    w = w.reshape(2, LRU_NCOLB, per, LRU_BW, LRU_BW)
    eye = jnp.eye(per, dtype=w.dtype)
    slab = jnp.einsum('dspij,pq->dspiqj', w, eye)
    return slab.reshape(2, LRU_NCOLB, LRU_COLB, LRU_COLB).astype(BF16)


def _layer_group(x, mod, p, lru_h0, gla_s0, *, latent, want_state, tag):
    b, t, _ = x.shape
    per_batch = latent
    zx, zg, ga, gb = _inproj(x, mod, p['norm_g'], p['w_rm'], per_batch_mod=per_batch,
                             order='interleaved', name=f"inproj_rm_{tag}")
    q, k, v, g, lr = _inproj(x, mod, p['norm_g'], p['w_cm'], per_batch_mod=per_batch,
                             order='col_major' if latent else 'natural', name=f"inproj_cm_{tag}")
    period = GRID_W if latent else t
    hf, hb, s_lru = _lru(zx, lru_h0, p['conv_w'], p['conv_b'], p['wa_bd'], p['wx_bd'], p['lru_ba'],
                         p['lru_bx'], p['lru_L'], period=period, name=f"lru_{tag}")
    res = _gla(q, k, v, g, lr, p['w2p'], p['gla_b2'], p['gla_norm_g'], gla_s0,
               want_final=want_state, hps=2, col_major=latent, name=f"gla_{tag}")
    on = res[0]
    s_gla = res[1] if want_state else None
    x1 = _merge(x, hf, hb, zg, ga, gb, on, mod, p['norm_g'], p['lru_up'], p['gla_up'], p['w_out'],
                per_batch_mod=per_batch, name=f"merge_{tag}")
    tm = 256 if t % 256 == 0 else t
    y = _ffn(x1, mod, p['norm_g'], p['mlp_w1'], p['mlp_w2'], per_batch_mod=per_batch, tm=tm,
             name=f"ffn_{tag}")
    return y, s_lru, s_gla


def kernel(x_prompt, x_sample, state_lru, state_gla, c, c_ctx, w_mod, b_mod, norm_g, w_in, conv_w, conv_b,
           lru_wa, lru_ba, lru_wx, lru_bx, lru_L, lru_up, gla_w2, gla_b2, gla_norm_g, gla_up, w_out,
           mlp_w1, mlp_w2):
    depth = w_in.shape[0]
    assert depth == 1
    dec_b = x_sample.shape[0]
    b0 = x_prompt.shape[0]
    xp, xs = x_prompt, x_sample
    lru_states, gla_states = [], []
    mod_rows = -(-(1 + dec_b) // V7X_SUBLANES) * V7X_SUBLANES
    cc = jnp.zeros((mod_rows, D_MODEL), F32).at[0].set(c_ctx).at[1:1 + dec_b].set(c)
    for l in range(depth):
        wl = w_in[l]
        o = np.cumsum((0, D_MODEL, D_MODEL, 512, 512, 1024, 1024, 2 * GLA_RANK, D_MODEL, D_MODEL))
        col = lambda i: wl[:, o[i]:o[i + 1]].astype(BF16)
        lr_w = jnp.pad(col(6), ((0, 0), (0, LR_PAD - 2 * GLA_RANK)))
        w2 = gla_w2[l].astype(BF16)
        w2p = jnp.zeros((2, LR_PAD, GLA_HEADS * GLA_DK), BF16)
        w2p = w2p.at[0, 0:GLA_RANK].set(w2[0]).at[1, GLA_RANK:2 * GLA_RANK].set(w2[1])
        p = {
            'norm_g': norm_g[l],
            'w_rm': [col(0), col(1), col(7), col(8)],
            'w_cm': [col(2), col(3), col(4), col(5), lr_w],
            'conv_w': conv_w[l], 'conv_b': conv_b[l].reshape(1, D_MODEL),
            'wa_bd': _blockdiag_slabs(0.5 * lru_wa[l]), 'wx_bd': _blockdiag_slabs(0.5 * lru_wx[l]),
            'lru_ba': lru_ba[l], 'lru_bx': lru_bx[l], 'lru_L': lru_L[l],
            'lru_up': lru_up[l].astype(BF16),
            'w2p': w2p, 'gla_b2': gla_b2[l], 'gla_norm_g': gla_norm_g[l].reshape(1, GLA_DV),
            'gla_up': gla_up[l].astype(BF16), 'w_out': w_out[l].astype(BF16),
            'mlp_w1': mlp_w1[l].astype(BF16), 'mlp_w2': mlp_w2[l].astype(BF16),
        }
        mod = _modulation(cc, w_mod[l], b_mod[l])
        h0_ctx = jnp.zeros((b0, 2, D_MODEL), F32)
        xp, s_lru, s_gla = _layer_group(xp, mod, p, h0_ctx, None, latent=False, want_state=True,
                                        tag="ctx")
        lru_states.append(s_lru)
        gla_states.append(s_gla)
        xs, _, _ = _layer_group(xs, mod, p, state_lru[:, l], state_gla[:, l], latent=True,
                                want_state=False, tag="lat")
    return (xp, xs, jnp.stack(lru_states, axis=1), jnp.stack(gla_states, axis=1))
```

```python
import functools

import numpy as np
import jax
import jax.numpy as jnp
from jax import lax
from jax.experimental import pallas as pl
from jax.experimental.pallas import tpu as pltpu

F32 = jnp.float32
BF16 = jnp.bfloat16

D_MODEL = 1024
GRID_W = 64
LRU_BLOCKS = 16
LRU_BW = D_MODEL // LRU_BLOCKS
LRU_C = 8.0
GLA_HEADS = 4
GLA_DK = 128
GLA_DV = 256
GLA_RANK = 16
GLA_TAU = 16.0
GLA_CHUNK = 64
D_FF = 4 * D_MODEL
N_MOD = 6
EPS = 1e-6
LOG2_E = 1.4426950408889634

V7X_LANES = 128
V7X_SUBLANES = 8
V7X_MXU_DIM = 256
V7X_VMEM_BYTES = 64 * 1024 * 1024
VMEM_LIMIT = V7X_VMEM_BYTES - 8 * 1024 * 1024

LRU_COLB = V7X_MXU_DIM
LRU_NCOLB = D_MODEL // LRU_COLB
LR_PAD = V7X_LANES
COL_BLOCK = 16
GLA_BLOCK = V7X_MXU_DIM
GLA_SCORE_BLOCK = 2 * GLA_CHUNK
GLA_OUT_COLS = 16
LRU_CHUNK = 256


def _cparams(sem):
    return pltpu.CompilerParams(dimension_semantics=sem, vmem_limit_bytes=VMEM_LIMIT)


def _rms(x, g):
    ms = jnp.mean(x * x, axis=-1, keepdims=True)
    return x * lax.rsqrt(ms + EPS) * g


def _sigmoid(x):
    return 0.5 * jnp.tanh(0.5 * x) + 0.5


def _softplus(x):
    return jnp.maximum(x, 0.0) + jnp.log1p(jnp.exp(-jnp.abs(x)))


def _bdot(a, b):
    return jnp.dot(a, b, preferred_element_type=F32)


def _mod_kernel(c_ref, w_ref, b_ref, o_ref):
    c = c_ref[...]
    s = (c * _sigmoid(c)).astype(BF16)
    o_ref[...] = _bdot(s, w_ref[...].astype(BF16)) + b_ref[...]


def _modulation(cc, w_mod, b_mod):
    rows = cc.shape[0]
    return pl.pallas_call(
        _mod_kernel,
        grid=(N_MOD,),
        in_specs=[
            pl.BlockSpec((rows, D_MODEL), lambda n: (0, 0)),
            pl.BlockSpec((D_MODEL, D_MODEL), lambda n: (0, n)),
            pl.BlockSpec((1, D_MODEL), lambda n: (0, n)),
        ],
        out_specs=pl.BlockSpec((rows, D_MODEL), lambda n: (0, n)),
        out_shape=jax.ShapeDtypeStruct((rows, N_MOD * D_MODEL), F32),
        compiler_params=_cparams(("arbitrary",)),
        name="modulation",
    )(cc, w_mod, b_mod.reshape(1, N_MOD * D_MODEL)).reshape(rows, N_MOD, D_MODEL)


def _tok_view(a, grid_tile):
    if grid_tile is None:
        return a
    b, t, n = a.shape
    rows = t // GRID_W
    return a.reshape(b, rows, GRID_W // COL_BLOCK, COL_BLOCK, n)


def _tok_spec(n, tm, grid_tile):
    if grid_tile is None:
        return pl.BlockSpec((None, tm, n), lambda b, j: (b, j, 0))
    rows = grid_tile
    return pl.BlockSpec((None, rows, None, COL_BLOCK, n), lambda b, j: (b, 0, j, 0, 0))


def _const_spec(shape):
    nd = len(shape)
    return pl.BlockSpec(shape, lambda b, j: (0,) * nd)


def _mod_spec(per_batch):
    if per_batch:
        return pl.BlockSpec((None, N_MOD, D_MODEL), lambda b, j: (b + 1, 0, 0))
    return pl.BlockSpec((None, N_MOD, D_MODEL), lambda b, j: (0, 0, 0))


def _perm_matrix(rows, cols):
    n = rows * cols
    p = np.zeros((n, n), np.float32)
    for r in range(rows):
        for c in range(cols):
            p[c * rows + r, r * cols + c] = 1.0
    return p


def _inproj_kernel(*refs, n_out, permute, n_natural):
    x_ref, mod_ref, ng_ref = refs[:3]
    k = 3
    if permute:
        perm_ref = refs[3]
        k = 4
    w_refs = refs[k:k + n_out]
    o_refs = refs[k + n_out:]
    x = x_ref[...].reshape(-1, D_MODEL)
    h = _rms(x, ng_ref[0:1, :]) * (1.0 + mod_ref[1:2, :]) + mod_ref[0:1, :]
    hb = h.astype(BF16)
    hp = hb
    if permute:
        hp = _bdot(perm_ref[...], hb).astype(BF16)
    for i, (w_ref, o_ref) in enumerate(zip(w_refs, o_refs)):
        z = _bdot(hp if i < n_out - n_natural else hb, w_ref[...])
        o_ref[...] = z.astype(o_ref.dtype).reshape(o_ref.shape)


def _inproj(x, mod, norm_g, weights, *, per_batch_mod, order, name, n_natural=0):
    b, t, _ = x.shape
    if order == 'col_major':
        rows = t // GRID_W
        tm = rows * COL_BLOCK
        x_in = _tok_view(x, rows)
        x_spec = _tok_spec(D_MODEL, tm, rows)
        perm = _perm_matrix(rows, COL_BLOCK)
    else:
        tm = _lru_chunk(t)
        x_in = x
        x_spec = _tok_spec(D_MODEL, tm, None)
        perm = _interleave_matrix(tm) if order == 'interleaved' else None
    extra = [] if perm is None else [jnp.asarray(perm, BF16)]
    return pl.pallas_call(
        functools.partial(_inproj_kernel, n_out=len(weights), permute=perm is not None,
                          n_natural=n_natural),
        grid=(b, t // tm),
        in_specs=[x_spec, _mod_spec(per_batch_mod), _const_spec((4, D_MODEL))]
        + [_const_spec(e.shape) for e in extra] + [_const_spec(w.shape) for w in weights],
        out_specs=[_tok_spec(w.shape[1], tm, None) for w in weights],
        out_shape=[jax.ShapeDtypeStruct((b, t, w.shape[1]), BF16) for w in weights],
        compiler_params=_cparams(("parallel", "parallel")),
        name=name,
    )(x_in, mod, norm_g, *extra, *weights)


def _lru_chunk(t):
    return min(LRU_CHUNK, t)


def _interleave_matrix(tc):
    return _perm_matrix(V7X_SUBLANES, tc // V7X_SUBLANES)


def _lru_kernel(zf_ref, zb_ref, h0_ref, cw_ref, cb_ref, wa_ref, wx_ref, ba_ref, bx_ref, l_ref,
                hf_ref, hb_ref, fin_ref, a_s, u_s, carry_s, *, tc, period):
    j = pl.program_id(1)
    ns = V7X_SUBLANES
    sl = tc // ns

    @pl.when(j == 0)
    def _():
        carry_s[...] = h0_ref[...]

    sub_c = lax.broadcasted_iota(jnp.int32, (ns, LRU_COLB), 0)
    prev_ok = jnp.bitwise_and(sub_c * sl, period - 1) != 0
    next_ok = jnp.bitwise_and((sub_c + 1) * sl, period - 1) != 0
    for d, z_ref in ((0, zf_ref), (1, zb_ref)):
        for cb in range(LRU_NCOLB):
            cs = slice(cb * LRU_COLB, (cb + 1) * LRU_COLB)
            x = z_ref[:, cs].astype(F32)
            e_prev = jnp.where(prev_ok, pltpu.roll(x[tc - ns:], 1, 0), 0.0)
            e_next0 = jnp.where(next_ok, pltpu.roll(x[:ns], ns - 1, 0), 0.0)
            e_next1 = jnp.where(next_ok, pltpu.roll(x[ns:2 * ns], ns - 1, 0), 0.0)
            xm1 = jnp.concatenate([e_prev, x[:tc - ns]], axis=0)
            xp1 = jnp.concatenate([x[ns:], e_next0], axis=0)
            xp2 = jnp.concatenate([x[2 * ns:], e_next0, e_next1], axis=0)
            xc = (cb_ref[0:1, cs] + xm1 * cw_ref[0:1, cs] + x * cw_ref[1:2, cs]
                  + xp1 * cw_ref[2:3, cs] + xp2 * cw_ref[3:4, cs])
            xcb = xc.astype(BF16)
            tr = jnp.tanh(_bdot(xcb, wa_ref[d, cb]) + 0.5 * ba_ref[d:d + 1, cs])
            ti = jnp.tanh(_bdot(xcb, wx_ref[d, cb]) + 0.5 * bx_ref[d:d + 1, cs])
            k2 = (-0.5 * LRU_C * LOG2_E) * _softplus(-l_ref[d:d + 1, cs])
            a = jnp.exp2(tr * k2 + k2)
            u = jnp.sqrt(1.0 - a * a) * ((0.5 * ti + 0.5) * xc)
            a_s[d, :, cs] = a
            u_s[d, :, cs] = u

    unroll = 4
    row = lambda i: pl.ds(pl.multiple_of(i * ns, ns), ns)

    def pass1(g, carry):
        hf, af, hb, ab = carry
        for k in range(unroll):
            rf = row(g * unroll + k)
            rb = row(sl - 1 - (g * unroll + k))
            a = a_s[0, rf, :]
            hf = a * hf + u_s[0, rf, :]
            af = a * af
            u_s[0, rf, :] = hf
            a_s[0, rf, :] = af
            a = a_s[1, rb, :]
            hb = a * hb + u_s[1, rb, :]
            ab = a * ab
            u_s[1, rb, :] = hb
            a_s[1, rb, :] = ab
        return hf, af, hb, ab

    zeros = jnp.zeros((ns, D_MODEL), F32)
    ones = jnp.ones((ns, D_MODEL), F32)
    hf, af, hb, ab = lax.fori_loop(0, sl // unroll, pass1, (zeros, ones, zeros, ones))

    sub = lax.broadcasted_iota(jnp.int32, (ns, D_MODEL), 0)

    def sublane_scan(a, u, forward):
        for s in (1, 2, 4):
            if forward:
                m = sub >= s
                shift = s
            else:
                m = sub < ns - s
                shift = ns - s
            a_sh = jnp.where(m, pltpu.roll(a, shift, 0), 1.0)
            u_sh = jnp.where(m, pltpu.roll(u, shift, 0), 0.0)
            u = a * u_sh + u
            a = a * a_sh
        return a, u

    cin_f = carry_s[0:1, :]
    cin_b = carry_s[1:2, :]
    af, hf = sublane_scan(af, hf, True)
    ab, hb = sublane_scan(ab, hb, False)
    fin_f = hf + af * cin_f
    fin_b = hb + ab * cin_b
    enter_f = jnp.where(sub == 0, cin_f, pltpu.roll(fin_f, 1, 0))
    enter_b = jnp.where(sub == ns - 1, cin_b, pltpu.roll(fin_b, ns - 1, 0))
    out_f = fin_f[ns - 1:ns, :]
    out_b = fin_b[0:1, :]
    carry_s[0:1, :] = out_f
    carry_s[1:2, :] = out_b
    fin_ref[0:1, :] = out_f
    fin_ref[1:2, :] = out_b

    pk = 2 * ns
    enter_f2 = jnp.concatenate([enter_f, enter_f], axis=0)
    enter_b2 = jnp.concatenate([enter_b, enter_b], axis=0)

    def pass2(g, carry):
        for k in range(unroll // 2):
            rr = pl.ds(pl.multiple_of((g * (unroll // 2) + k) * pk, pk), pk)
            hf_ref[rr, :] = (u_s[0, rr, :] + a_s[0, rr, :] * enter_f2).astype(hf_ref.dtype)
            hb_ref[rr, :] = (u_s[1, rr, :] + a_s[1, rr, :] * enter_b2).astype(hb_ref.dtype)
        return carry

    lax.fori_loop(0, sl // unroll, pass2, 0)


def _lru(zx, h0, conv_w, conv_b, wa_bd, wx_bd, ba, bx, lru_l, *, period, name):
    b, t, _ = zx.shape
    tc = _lru_chunk(t)
    nc = t // tc
    assert period % (tc // V7X_SUBLANES) == 0 and tc % period == 0
    const = lambda shape: pl.BlockSpec(shape, lambda bb, j: (0,) * len(shape))
    return pl.pallas_call(
        functools.partial(_lru_kernel, tc=tc, period=period),
        grid=(b, nc),
        in_specs=[
            pl.BlockSpec((None, tc, D_MODEL), lambda bb, j: (bb, j, 0)),
            pl.BlockSpec((None, tc, D_MODEL), lambda bb, j: (bb, nc - 1 - j, 0)),
            pl.BlockSpec((None, 2, D_MODEL), lambda bb, j: (bb, 0, 0)),
            const((4, D_MODEL)), const((1, D_MODEL)),
            const((2, LRU_NCOLB, LRU_COLB, LRU_COLB)), const((2, LRU_NCOLB, LRU_COLB, LRU_COLB)),
            const((2, D_MODEL)), const((2, D_MODEL)), const((2, D_MODEL)),
        ],
        out_specs=[
            pl.BlockSpec((None, tc, D_MODEL), lambda bb, j: (bb, j, 0)),
            pl.BlockSpec((None, tc, D_MODEL), lambda bb, j: (bb, nc - 1 - j, 0)),
            pl.BlockSpec((None, 2, D_MODEL), lambda bb, j: (bb, 0, 0)),
        ],
        out_shape=[
            jax.ShapeDtypeStruct((b, t, D_MODEL), BF16),
            jax.ShapeDtypeStruct((b, t, D_MODEL), BF16),
            jax.ShapeDtypeStruct((b, 2, D_MODEL), F32),
        ],
        scratch_shapes=[
            pltpu.VMEM((2, tc, D_MODEL), F32),
            pltpu.VMEM((2, tc, D_MODEL), F32),
            pltpu.VMEM((2, D_MODEL), F32),
        ],
        compiler_params=_cparams(("parallel", "arbitrary")),
        name=name,
    )(zx, zx, h0, conv_w, conv_b, wa_bd, wx_bd, ba, bx, lru_l)


def _gla_kernel(*refs, t, hps, has_s0, want_final, grid_rows):
    q_ref, k_ref, v_ref, g_ref, lr_ref, w2_ref, b2_ref, gn_ref = refs[:8]
    n = 8
    if grid_rows is not None:
        perm_ref = refs[n]
        n += 1
    if has_s0:
        s0_ref = refs[n]
        n += 1
    on_ref = refs[n]
    n += 1
    if want_final:
        fin_ref = refs[n]
        n += 1
    oacc, st, qe_s, kt_s, bt_s, pp = refs[n:]

    c = GLA_CHUNK
    n_chunks = t // c
    blk = min(GLA_BLOCK, t)
    for d in range(2):
        for hh in range(hps):
            if has_s0:
                st[d, hh] = s0_ref[d, hh].T
            else:
                st[d, hh] = jnp.zeros((GLA_DV, GLA_DK), F32)

    ti = lax.broadcasted_iota(jnp.int32, (blk, blk), 0)
    si = lax.broadcasted_iota(jnp.int32, (blk, blk), 1)
    same = (ti // c) == (si // c)
    cums = ((same & (si <= ti)).astype(BF16), (same & (si >= ti)).astype(BF16))
    sb = min(GLA_SCORE_BLOCK, blk)
    tj = lax.broadcasted_iota(jnp.int32, (sb, sb), 0)
    sj = lax.broadcasted_iota(jnp.int32, (sb, sb), 1)
    same_j = (tj // c) == (sj // c)
    masks = (same_j & (sj <= tj), same_j & (sj >= tj))
    scale = GLA_DK ** -0.5

    def step(ia, slot_a, ib, slot_b):
        if ia is not None:
            rows_a = pl.ds(pl.multiple_of(ia * blk, blk), blk)
            lrb = lr_ref[rows_a, :]
            pre = [_bdot(lrb, w2_ref[d]) + b2_ref[d:d + 1, :] for d in range(2)]
        if ib is not None:
            rows_b = [pl.ds(pl.multiple_of(ib * blk + g * sb, sb), sb) for g in range(blk // sb)]
            loc_b = [slice(g * sb, (g + 1) * sb) for g in range(blk // sb)]
            raw = [[[lax.dot_general(pp[slot_b, 0, d, loc, hh * GLA_DK:(hh + 1) * GLA_DK],
                                     pp[slot_b, 1, d, loc, hh * GLA_DK:(hh + 1) * GLA_DK],
                                     (((1,), (1,)), ((), ())), preferred_element_type=F32)
                     for d in range(2)] for hh in range(hps)] for loc in loc_b]
        if ia is not None:
            bcum = []
            for d in range(2):
                soft = jnp.log(1.0 + jnp.exp2(jnp.abs(pre[d]) * (-LOG2_E)))
                la = jnp.minimum(pre[d], 0.0) * (LOG2_E / GLA_TAU) - soft * (LOG2_E / GLA_TAU)
                hi = la.astype(BF16)
                mid = (la - hi.astype(F32)).astype(BF16)
                bcum.append(_bdot(cums[d], hi) + _bdot(cums[d], mid))
        if ib is not None:
            for g, rows in enumerate(rows_b):
                for hh in range(hps):
                    vs = slice(hh * GLA_DV, (hh + 1) * GLA_DV)
                    sc = jnp.where(masks[0], raw[g][hh][0], 0.0) + jnp.where(masks[1], raw[g][hh][1], 0.0)
                    oacc[rows, vs] = _bdot(sc.astype(BF16), v_ref[rows, vs])
        if ia is not None:
            q = q_ref[rows_a, :].astype(F32)
            k = k_ref[rows_a, :].astype(F32)
            for d in range(2):
                edge = c - 1 if d == 0 else 0
                bc3 = bcum[d].reshape(blk // c, c, hps * GLA_DK)
                btot = jnp.broadcast_to(bc3[:, edge:edge + 1, :], bc3.shape).reshape(bcum[d].shape)
                bt_s[d, rows_a, :] = btot
                qe = ((q * scale) * jnp.exp2(bcum[d])).astype(BF16)
                qe_s[d, rows_a, :] = qe
                pp[slot_a, 0, d] = qe
                pp[slot_a, 1, d] = (k * jnp.exp2(-bcum[d])).astype(BF16)
                kt_s[d, rows_a, :] = (k * jnp.exp2(btot - bcum[d])).astype(BF16)

    def pair(j, carry):
        step(2 * j + 1, 1, 2 * j, 0)
        step(2 * j + 2, 0, 2 * j + 1, 1)
        return carry

    n_blocks = t // blk
    n_pairs = (n_blocks - 1) // 2
    step(0, 0, None, None)
    lax.fori_loop(0, n_pairs, pair, 0)
    done = 2 * n_pairs
    if done < n_blocks - 1:
        step(done + 1, 1, done, 0)
        done += 1
    step(None, None, done, done % 2)

    def chunk(c0, d, hh):
        rows = pl.ds(c0, c)
        ks = slice(hh * GLA_DK, (hh + 1) * GLA_DK)
        vs = slice(hh * GLA_DV, (hh + 1) * GLA_DV)
        s_t = st[d, hh]
        o = lax.dot_general(qe_s[d, rows, ks], s_t.astype(BF16), (((1,), (1,)), ((), ())),
                            preferred_element_type=F32)
        upd = lax.dot_general(v_ref[rows, vs], kt_s[d, rows, ks], (((0,), (0,)), ((), ())),
                              preferred_element_type=F32)
        st[d, hh] = s_t * jnp.exp2(bt_s[d, pl.ds(c0, 1), ks]) + upd
        return o

    def body(jj, carry):
        cf = pl.multiple_of(jj * c, c)
        cbk = pl.multiple_of((n_chunks - 1 - jj) * c, c)
        for hh in range(hps):
            vs = slice(hh * GLA_DV, (hh + 1) * GLA_DV)
            for c0, d in ((cf, 0), (cbk, 1)):
                oacc[pl.ds(c0, c), vs] += chunk(c0, d, hh)
        return carry

    lax.fori_loop(0, n_chunks, body, 0, unroll=4)

    if grid_rows is None:
        ob = min(GLA_BLOCK, t)
    else:
        ob = GLA_OUT_COLS * grid_rows

    def out_block(ib, carry):
        r0 = pl.multiple_of(ib * ob, ob)
        rows = pl.ds(r0, ob)
        parts = []
        for hh in range(hps):
            vs = slice(hh * GLA_DV, (hh + 1) * GLA_DV)
            gg = g_ref[rows, vs].astype(F32)
            parts.append((_rms(oacc[rows, vs], gn_ref[...]) * (gg * _sigmoid(gg))).astype(BF16))
        on = jnp.concatenate(parts, axis=1)
        if grid_rows is None:
            on_ref[rows, :] = on
        else:
            on = _bdot(perm_ref[...], on).astype(BF16)
            for r in range(grid_rows):
                dst = pl.multiple_of(r * GRID_W + ib * GLA_OUT_COLS, GLA_OUT_COLS)
                on_ref[pl.ds(dst, GLA_OUT_COLS), :] = on[r * GLA_OUT_COLS:(r + 1) * GLA_OUT_COLS]
        return carry

    lax.fori_loop(0, t // ob, out_block, 0)

    if want_final:
        for d in range(2):
            for hh in range(hps):
                fin_ref[d, hh] = st[d, hh].T


def _gla(q, k, v, g, lr, w2p, b2, gn, s0, *, want_final, hps, col_major, name):
    b, t, _ = q.shape
    ng = GLA_HEADS // hps
    has_s0 = s0 is not None
    grid_rows = t // GRID_W if col_major else None
    in_specs = [
        pl.BlockSpec((None, t, hps * GLA_DK), lambda bb, h: (bb, 0, h)),
        pl.BlockSpec((None, t, hps * GLA_DK), lambda bb, h: (bb, 0, h)),
        pl.BlockSpec((None, t, hps * GLA_DV), lambda bb, h: (bb, 0, h)),
        pl.BlockSpec((None, t, hps * GLA_DV), lambda bb, h: (bb, 0, h)),
        pl.BlockSpec((None, t, LR_PAD), lambda bb, h: (bb, 0, 0)),
        pl.BlockSpec((2, LR_PAD, hps * GLA_DK), lambda bb, h: (0, 0, h)),
        pl.BlockSpec((2, hps * GLA_DK), lambda bb, h: (0, h)),
        pl.BlockSpec((1, GLA_DV), lambda bb, h: (0, 0)),
    ]
    args = [q, k, v, g, lr, w2p, b2, gn]
    if col_major:
        ob = GLA_OUT_COLS * grid_rows
        in_specs.append(pl.BlockSpec((ob, ob), lambda bb, h: (0, 0)))
        args.append(jnp.asarray(_perm_matrix(grid_rows, GLA_OUT_COLS).T, BF16))
    state_spec = pl.BlockSpec((None, 2, hps, GLA_DK, GLA_DV), lambda bb, h: (bb, 0, h, 0, 0))
    if has_s0:
        in_specs.append(state_spec)
        args.append(s0)
    out_specs = [pl.BlockSpec((None, t, hps * GLA_DV), lambda bb, h: (bb, 0, h))]
    out_shape = [jax.ShapeDtypeStruct((b, t, GLA_HEADS * GLA_DV), BF16)]
    if want_final:
        out_specs.append(state_spec)
        out_shape.append(jax.ShapeDtypeStruct((b, 2, GLA_HEADS, GLA_DK, GLA_DV), F32))
    return pl.pallas_call(
        functools.partial(_gla_kernel, t=t, hps=hps, has_s0=has_s0, want_final=want_final,
                          grid_rows=grid_rows),
        grid=(b, ng),
        in_specs=in_specs,
        out_specs=out_specs,
        out_shape=out_shape,
        scratch_shapes=[
            pltpu.VMEM((t, hps * GLA_DV), F32),
            pltpu.VMEM((2, hps, GLA_DV, GLA_DK), F32),
            pltpu.VMEM((2, t, hps * GLA_DK), BF16),
            pltpu.VMEM((2, t, hps * GLA_DK), BF16),
            pltpu.VMEM((2, t, hps * GLA_DK), F32),
            pltpu.VMEM((2, 2, 2, min(GLA_BLOCK, t), hps * GLA_DK), BF16),
        ],
        compiler_params=_cparams(("parallel", "parallel")),
        name=name,
    )(*args)


def _gelu_tanh(x):
    return x * (0.5 * (1.0 + jnp.tanh(0.7978845608028654 * (x + 0.044715 * (x * x * x)))))


def _merge_kernel(x_ref, hf_ref, hb_ref, zg_ref, ga_ref, gb_ref, on_ref, mod_ref, ng_ref,
                  unperm_ref, lup_ref, gup_ref, wo_ref, o_ref):
    ld = lambda r: r[...].astype(F32)
    ya_in = ((ld(hf_ref) + ld(hb_ref)) * _gelu_tanh(ld(zg_ref))).astype(BF16)
    ya_in = _bdot(unperm_ref[...], ya_in).astype(BF16)
    y_a = _bdot(ya_in, lup_ref[...])
    y_b = _bdot(on_ref[...], gup_ref[...])
    mm = _sigmoid(ld(ga_ref)) * y_a + _sigmoid(ld(gb_ref)) * y_b
    m = _bdot(mm.astype(BF16), wo_ref[...])
    o_ref[...] = x_ref[...] + mod_ref[2:3, :] * _rms(m, ng_ref[1:2, :])


def _merge(x, hf, hb, zg, ga, gb, on, mod, norm_g, lru_up, gla_up, w_out, *, per_batch_mod, name):
    b, t, _ = x.shape
    tm = _lru_chunk(t)
    perm = _interleave_matrix(tm)
    spec = _tok_spec(D_MODEL, tm, None)
    wspec = _const_spec((D_MODEL, D_MODEL))
    pspec = _const_spec((tm, tm))
    return pl.pallas_call(
        _merge_kernel,
        grid=(b, t // tm),
        in_specs=[spec] * 7 + [_mod_spec(per_batch_mod), _const_spec((4, D_MODEL)), pspec]
        + [wspec] * 3,
        out_specs=spec,
        out_shape=jax.ShapeDtypeStruct((b, t, D_MODEL), F32),
        compiler_params=_cparams(("parallel", "parallel")),
        name=name,
    )(x, hf, hb, zg, ga, gb, on, mod, norm_g, jnp.asarray(perm.T, BF16), lru_up, gla_up, w_out)


def _ffn_kernel(x_ref, mod_ref, ng_ref, w1_ref, w2_ref, o_ref, *, n_split):
    x = x_ref[...]
    h = (_rms(x, ng_ref[2:3, :]) * (1.0 + mod_ref[4:5, :]) + mod_ref[3:4, :]).astype(BF16)
    fs = D_FF // n_split
    acc = jnp.zeros(x.shape, F32)
    for kf in range(n_split):
        hid = jnp.maximum(_bdot(h, w1_ref[:, kf * fs:(kf + 1) * fs]), 0.0)
        acc = acc + _bdot((hid * hid).astype(BF16), w2_ref[kf * fs:(kf + 1) * fs, :])
    o_ref[...] = x + mod_ref[5:6, :] * _rms(acc, ng_ref[3:4, :])


def _ffn(x, mod, norm_g, w1, w2, *, per_batch_mod, tm, name):
    b, t, _ = x.shape
    spec = _tok_spec(D_MODEL, tm, None)
    return pl.pallas_call(
        functools.partial(_ffn_kernel, n_split=4),
        grid=(b, t // tm),
        in_specs=[spec, _mod_spec(per_batch_mod), _const_spec((4, D_MODEL)),
                  _const_spec((D_MODEL, D_FF)), _const_spec((D_FF, D_MODEL))],
        out_specs=spec,
        out_shape=jax.ShapeDtypeStruct((b, t, D_MODEL), F32),
        compiler_params=_cparams(("parallel", "parallel")),
        name=name,
    )(x, mod, norm_g, w1, w2)


def _blockdiag_slabs(w):
    per = LRU_COLB // LRU_BW
    w = w.reshape(2, LRU_NCOLB, per, LRU_BW, LRU_BW)
    eye = jnp.eye(per, dtype=w.dtype)
    slab = jnp.einsum('dspij,pq->dspiqj', w, eye)
    return slab.reshape(2, LRU_NCOLB, LRU_COLB, LRU_COLB).astype(BF16)


def _layer_group(x, mod, p, lru_h0, gla_s0, *, latent, want_state, tag):
    b, t, _ = x.shape
    per_batch = latent
    zx, zg, ga, gb = _inproj(x, mod, p['norm_g'], p['w_rm'], per_batch_mod=per_batch,
                             order='interleaved', n_natural=2, name=f"inproj_rm_{tag}")
    q, k, v, g, lr = _inproj(x, mod, p['norm_g'], p['w_cm'], per_batch_mod=per_batch,
                             order='col_major' if latent else 'natural', name=f"inproj_cm_{tag}")
    period = GRID_W if latent else t
    hf, hb, s_lru = _lru(zx, lru_h0, p['conv_w'], p['conv_b'], p['wa_bd'], p['wx_bd'], p['lru_ba'],
                         p['lru_bx'], p['lru_L'], period=period, name=f"lru_{tag}")
    res = _gla(q, k, v, g, lr, p['w2p'], p['gla_b2'], p['gla_norm_g'], gla_s0,
               want_final=want_state, hps=2, col_major=latent, name=f"gla_{tag}")
    on = res[0]
    s_gla = res[1] if want_state else None
    x1 = _merge(x, hf, hb, zg, ga, gb, on, mod, p['norm_g'], p['lru_up'], p['gla_up'], p['w_out'],
                per_batch_mod=per_batch, name=f"merge_{tag}")
    tm = 256 if t % 256 == 0 else t
    y = _ffn(x1, mod, p['norm_g'], p['mlp_w1'], p['mlp_w2'], per_batch_mod=per_batch, tm=tm,
             name=f"ffn_{tag}")
    return y, s_lru, s_gla


def kernel(x_prompt, x_sample, state_lru, state_gla, c, c_ctx, w_mod, b_mod, norm_g, w_in, conv_w, conv_b,
           lru_wa, lru_ba, lru_wx, lru_bx, lru_L, lru_up, gla_w2, gla_b2, gla_norm_g, gla_up, w_out,
           mlp_w1, mlp_w2):
    depth = w_in.shape[0]
    assert depth == 1
    dec_b = x_sample.shape[0]
    b0 = x_prompt.shape[0]
    xp, xs = x_prompt, x_sample
    lru_states, gla_states = [], []
    mod_rows = -(-(1 + dec_b) // V7X_SUBLANES) * V7X_SUBLANES
    cc = jnp.zeros((mod_rows, D_MODEL), F32).at[0].set(c_ctx).at[1:1 + dec_b].set(c)
    for l in range(depth):
        wl = w_in[l]
        o = np.cumsum((0, D_MODEL, D_MODEL, 512, 512, 1024, 1024, 2 * GLA_RANK, D_MODEL, D_MODEL))
        col = lambda i: wl[:, o[i]:o[i + 1]].astype(BF16)
        lr_w = jnp.pad(col(6), ((0, 0), (0, LR_PAD - 2 * GLA_RANK)))
        w2 = gla_w2[l].astype(BF16)
        w2p = jnp.zeros((2, LR_PAD, GLA_HEADS * GLA_DK), BF16)
        w2p = w2p.at[0, 0:GLA_RANK].set(w2[0]).at[1, GLA_RANK:2 * GLA_RANK].set(w2[1])
        p = {
            'norm_g': norm_g[l],
            'w_rm': [col(0), col(1), col(7), col(8)],
            'w_cm': [col(2), col(3), col(4), col(5), lr_w],
            'conv_w': conv_w[l], 'conv_b': conv_b[l].reshape(1, D_MODEL),
            'wa_bd': _blockdiag_slabs(0.5 * lru_wa[l]), 'wx_bd': _blockdiag_slabs(0.5 * lru_wx[l]),
            'lru_ba': lru_ba[l], 'lru_bx': lru_bx[l], 'lru_L': lru_L[l],
            'lru_up': lru_up[l].astype(BF16),
            'w2p': w2p, 'gla_b2': gla_b2[l], 'gla_norm_g': gla_norm_g[l].reshape(1, GLA_DV),
            'gla_up': gla_up[l].astype(BF16), 'w_out': w_out[l].astype(BF16),
            'mlp_w1': mlp_w1[l].astype(BF16), 'mlp_w2': mlp_w2[l].astype(BF16),
        }
        mod = _modulation(cc, w_mod[l], b_mod[l])
        h0_ctx = jnp.zeros((b0, 2, D_MODEL), F32)
        xp, s_lru, s_gla = _layer_group(xp, mod, p, h0_ctx, None, latent=False, want_state=True,
                                        tag="ctx")
        lru_states.append(s_lru)
        gla_states.append(s_gla)
        xs, _, _ = _layer_group(xs, mod, p, state_lru[:, l], state_gla[:, l], latent=True,
                                want_state=False, tag="lat")
    return (xp, xs, jnp.stack(lru_states, axis=1), jnp.stack(gla_states, axis=1))
```

```python
import functools

import numpy as np
import jax
import jax.numpy as jnp
from jax import lax
from jax.experimental import pallas as pl
from jax.experimental.pallas import tpu as pltpu

F32 = jnp.float32
BF16 = jnp.bfloat16

D_MODEL = 1024
GRID_W = 64
LRU_BLOCKS = 16
LRU_BW = D_MODEL // LRU_BLOCKS
LRU_C = 8.0
GLA_HEADS = 4
GLA_DK = 128
GLA_DV = 256
GLA_RANK = 16
GLA_TAU = 16.0
GLA_CHUNK = 64
D_FF = 4 * D_MODEL
N_MOD = 6
EPS = 1e-6
LOG2_E = 1.4426950408889634

V7X_LANES = 128
V7X_SUBLANES = 8
V7X_MXU_DIM = 256
V7X_VMEM_BYTES = 64 * 1024 * 1024
VMEM_LIMIT = V7X_VMEM_BYTES - 8 * 1024 * 1024

LRU_COLB = V7X_MXU_DIM
LRU_NCOLB = D_MODEL // LRU_COLB
LR_PAD = V7X_LANES
COL_BLOCK = 16
GLA_BLOCK = V7X_MXU_DIM
GLA_SCORE_BLOCK = 2 * GLA_CHUNK
GLA_OUT_COLS = 16
LRU_CHUNK = 256
MERGE_KBLOCK = V7X_MXU_DIM
MERGE_ROW_SPLIT = 2


def _cparams(sem):
    return pltpu.CompilerParams(dimension_semantics=sem, vmem_limit_bytes=VMEM_LIMIT)


def _rms(x, g):
    ms = jnp.mean(x * x, axis=-1, keepdims=True)
    return x * lax.rsqrt(ms + EPS) * g


def _sigmoid(x):
    return 0.5 * jnp.tanh(0.5 * x) + 0.5


def _softplus(x):
    return jnp.maximum(x, 0.0) + jnp.log1p(jnp.exp(-jnp.abs(x)))


def _bdot(a, b):
    return jnp.dot(a, b, preferred_element_type=F32)


def _mod_kernel(c_ref, w_ref, b_ref, o_ref):
    c = c_ref[...]
    s = (c * _sigmoid(c)).astype(BF16)
    o_ref[...] = _bdot(s, w_ref[...].astype(BF16)) + b_ref[...]


def _modulation(cc, w_mod, b_mod):
    rows = cc.shape[0]
    return pl.pallas_call(
        _mod_kernel,
        grid=(N_MOD,),
        in_specs=[
            pl.BlockSpec((rows, D_MODEL), lambda n: (0, 0)),
            pl.BlockSpec((D_MODEL, D_MODEL), lambda n: (0, n)),
            pl.BlockSpec((1, D_MODEL), lambda n: (0, n)),
        ],
        out_specs=pl.BlockSpec((rows, D_MODEL), lambda n: (0, n)),
        out_shape=jax.ShapeDtypeStruct((rows, N_MOD * D_MODEL), F32),
        compiler_params=_cparams(("arbitrary",)),
        name="modulation",
    )(cc, w_mod, b_mod.reshape(1, N_MOD * D_MODEL)).reshape(rows, N_MOD, D_MODEL)


def _tok_view(a, grid_tile):
    if grid_tile is None:
        return a
    b, t, n = a.shape
    rows = t // GRID_W
    return a.reshape(b, rows, GRID_W // COL_BLOCK, COL_BLOCK, n)


def _tok_spec(n, tm, grid_tile):
    if grid_tile is None:
        return pl.BlockSpec((None, tm, n), lambda b, j: (b, j, 0))
    rows = grid_tile
    return pl.BlockSpec((None, rows, None, COL_BLOCK, n), lambda b, j: (b, 0, j, 0, 0))


def _const_spec(shape):
    nd = len(shape)
    return pl.BlockSpec(shape, lambda b, j: (0,) * nd)


def _mod_spec(per_batch):
    if per_batch:
        return pl.BlockSpec((None, N_MOD, D_MODEL), lambda b, j: (b + 1, 0, 0))
    return pl.BlockSpec((None, N_MOD, D_MODEL), lambda b, j: (0, 0, 0))


def _perm_matrix(rows, cols):
    n = rows * cols
    p = np.zeros((n, n), np.float32)
    for r in range(rows):
        for c in range(cols):
            p[c * rows + r, r * cols + c] = 1.0
    return p


def _silu(x):
    return x * _sigmoid(x)


def _inproj_kernel(*refs, n_out, permute, n_natural, acts):
    x_ref, mod_ref, ng_ref = refs[:3]
    k = 3
    if permute:
        perm_ref = refs[3]
        k = 4
    w_refs = refs[k:k + n_out]
    o_refs = refs[k + n_out:]
    x = x_ref[...].reshape(-1, D_MODEL)
    h = _rms(x, ng_ref[0:1, :]) * (1.0 + mod_ref[1:2, :]) + mod_ref[0:1, :]
    hb = h.astype(BF16)
    hp = hb
    if permute:
        hp = _bdot(perm_ref[...], hb).astype(BF16)
    for i, (w_ref, o_ref) in enumerate(zip(w_refs, o_refs)):
        z = _bdot(hb if i < n_natural else hp, w_ref[...])
        if acts[i] is not None:
            z = acts[i](z)
        o_ref[...] = z.astype(o_ref.dtype).reshape(o_ref.shape)


def _inproj(x, mod, norm_g, weights, acts, *, per_batch_mod, order, name, n_natural=0):
    b, t, _ = x.shape
    if order == 'col_major':
        rows = t // GRID_W
        tm = rows * COL_BLOCK
        x_in = _tok_view(x, rows)
        x_spec = _tok_spec(D_MODEL, tm, rows)
        perm = _perm_matrix(rows, COL_BLOCK)
    else:
        tm = _lru_chunk(t)
        x_in = x
        x_spec = _tok_spec(D_MODEL, tm, None)
        perm = _interleave_matrix(tm) if order == 'interleaved' else None
    extra = [] if perm is None else [jnp.asarray(perm, BF16)]
    return pl.pallas_call(
        functools.partial(_inproj_kernel, n_out=len(weights), permute=perm is not None,
                          n_natural=n_natural, acts=tuple(acts)),
        grid=(b, t // tm),
        in_specs=[x_spec, _mod_spec(per_batch_mod), _const_spec((4, D_MODEL))]
        + [_const_spec(e.shape) for e in extra] + [_const_spec(w.shape) for w in weights],
        out_specs=[_tok_spec(w.shape[1], tm, None) for w in weights],
        out_shape=[jax.ShapeDtypeStruct((b, t, w.shape[1]), BF16) for w in weights],
        compiler_params=_cparams(("parallel", "parallel")),
        name=name,
    )(x_in, mod, norm_g, *extra, *weights)


def _lru_chunk(t):
    return min(LRU_CHUNK, t)


def _interleave_matrix(tc):
    return _perm_matrix(V7X_SUBLANES, tc // V7X_SUBLANES)


def _lru_kernel(zf_ref, zb_ref, h0_ref, cw_ref, cb_ref, wa_ref, wx_ref, ba_ref, bx_ref, l_ref,
                hf_ref, hb_ref, fin_ref, a_s, u_s, carry_s, *, tc, period):
    j = pl.program_id(1)
    ns = V7X_SUBLANES
    sl = tc // ns

    @pl.when(j == 0)
    def _():
        carry_s[...] = h0_ref[...]

    sub_c = lax.broadcasted_iota(jnp.int32, (ns, LRU_COLB), 0)
    prev_ok = jnp.bitwise_and(sub_c * sl, period - 1) != 0
    next_ok = jnp.bitwise_and((sub_c + 1) * sl, period - 1) != 0
    for d, z_ref in ((0, zf_ref), (1, zb_ref)):
        for cb in range(LRU_NCOLB):
            cs = slice(cb * LRU_COLB, (cb + 1) * LRU_COLB)
            x = z_ref[:, cs].astype(F32)
            e_prev = jnp.where(prev_ok, pltpu.roll(x[tc - ns:], 1, 0), 0.0)
            e_next0 = jnp.where(next_ok, pltpu.roll(x[:ns], ns - 1, 0), 0.0)
            e_next1 = jnp.where(next_ok, pltpu.roll(x[ns:2 * ns], ns - 1, 0), 0.0)
            xm1 = jnp.concatenate([e_prev, x[:tc - ns]], axis=0)
            xp1 = jnp.concatenate([x[ns:], e_next0], axis=0)
            xp2 = jnp.concatenate([x[2 * ns:], e_next0, e_next1], axis=0)
            xc = (cb_ref[0:1, cs] + xm1 * cw_ref[0:1, cs] + x * cw_ref[1:2, cs]
                  + xp1 * cw_ref[2:3, cs] + xp2 * cw_ref[3:4, cs])
            xcb = xc.astype(BF16)
            tr = jnp.tanh(_bdot(xcb, wa_ref[d, cb]) + 0.5 * ba_ref[d:d + 1, cs])
            ti = jnp.tanh(_bdot(xcb, wx_ref[d, cb]) + 0.5 * bx_ref[d:d + 1, cs])
            k2 = (-0.5 * LRU_C * LOG2_E) * _softplus(-l_ref[d:d + 1, cs])
            a = jnp.exp2(tr * k2 + k2)
            y = 1.0 - a * a
            u = jnp.where(y > 0.0, y * lax.rsqrt(y), 0.0) * ((0.5 * ti + 0.5) * xc)
            a_s[d, :, cs] = a
            u_s[d, :, cs] = u

    unroll = 4
    row = lambda i: pl.ds(pl.multiple_of(i * ns, ns), ns)

    def pass1(g, carry):
        hf, af, hb, ab = carry
        for k in range(unroll):
            rf = row(g * unroll + k)
            rb = row(sl - 1 - (g * unroll + k))
            a = a_s[0, rf, :]
            hf = a * hf + u_s[0, rf, :]
            af = a * af
            u_s[0, rf, :] = hf
            a_s[0, rf, :] = af
            a = a_s[1, rb, :]
            hb = a * hb + u_s[1, rb, :]
            ab = a * ab
            u_s[1, rb, :] = hb
            a_s[1, rb, :] = ab
        return hf, af, hb, ab

    zeros = jnp.zeros((ns, D_MODEL), F32)
    ones = jnp.ones((ns, D_MODEL), F32)
    hf, af, hb, ab = lax.fori_loop(0, sl // unroll, pass1, (zeros, ones, zeros, ones))

    sub = lax.broadcasted_iota(jnp.int32, (ns, D_MODEL), 0)

    def sublane_scan(a, u, forward):
        for s in (1, 2, 4):
            if forward:
                m = sub >= s
                shift = s
            else:
                m = sub < ns - s
                shift = ns - s
            a_sh = jnp.where(m, pltpu.roll(a, shift, 0), 1.0)
            u_sh = jnp.where(m, pltpu.roll(u, shift, 0), 0.0)
            u = a * u_sh + u
            a = a * a_sh
        return a, u

    cin_f = carry_s[0:1, :]
    cin_b = carry_s[1:2, :]
    af, hf = sublane_scan(af, hf, True)
    ab, hb = sublane_scan(ab, hb, False)
    fin_f = hf + af * cin_f
    fin_b = hb + ab * cin_b
    enter_f = jnp.where(sub == 0, cin_f, pltpu.roll(fin_f, 1, 0))
    enter_b = jnp.where(sub == ns - 1, cin_b, pltpu.roll(fin_b, ns - 1, 0))
    out_f = fin_f[ns - 1:ns, :]
    out_b = fin_b[0:1, :]
    carry_s[0:1, :] = out_f
    carry_s[1:2, :] = out_b
    fin_ref[0:1, :] = out_f
    fin_ref[1:2, :] = out_b

    pk = 2 * ns
    enter_f2 = jnp.concatenate([enter_f, enter_f], axis=0)
    enter_b2 = jnp.concatenate([enter_b, enter_b], axis=0)

    def pass2(g, carry):
        for k in range(unroll // 2):
            rr = pl.ds(pl.multiple_of((g * (unroll // 2) + k) * pk, pk), pk)
            hf_ref[rr, :] = (u_s[0, rr, :] + a_s[0, rr, :] * enter_f2).astype(hf_ref.dtype)
            hb_ref[rr, :] = (u_s[1, rr, :] + a_s[1, rr, :] * enter_b2).astype(hb_ref.dtype)
        return carry

    lax.fori_loop(0, sl // unroll, pass2, 0)


def _lru(zx, h0, conv_w, conv_b, wa_bd, wx_bd, ba, bx, lru_l, *, period, name):
    b, t, _ = zx.shape
    tc = _lru_chunk(t)
    nc = t // tc
    assert period % (tc // V7X_SUBLANES) == 0 and tc % period == 0
    const = lambda shape: pl.BlockSpec(shape, lambda bb, j: (0,) * len(shape))
    return pl.pallas_call(
        functools.partial(_lru_kernel, tc=tc, period=period),
        grid=(b, nc),
        in_specs=[
            pl.BlockSpec((None, tc, D_MODEL), lambda bb, j: (bb, j, 0)),
            pl.BlockSpec((None, tc, D_MODEL), lambda bb, j: (bb, nc - 1 - j, 0)),
            pl.BlockSpec((None, 2, D_MODEL), lambda bb, j: (bb, 0, 0)),
            const((4, D_MODEL)), const((1, D_MODEL)),
            const((2, LRU_NCOLB, LRU_COLB, LRU_COLB)), const((2, LRU_NCOLB, LRU_COLB, LRU_COLB)),
            const((2, D_MODEL)), const((2, D_MODEL)), const((2, D_MODEL)),
        ],
        out_specs=[
            pl.BlockSpec((None, tc, D_MODEL), lambda bb, j: (bb, j, 0)),
            pl.BlockSpec((None, tc, D_MODEL), lambda bb, j: (bb, nc - 1 - j, 0)),
            pl.BlockSpec((None, 2, D_MODEL), lambda bb, j: (bb, 0, 0)),
        ],
        out_shape=[
            jax.ShapeDtypeStruct((b, t, D_MODEL), BF16),
            jax.ShapeDtypeStruct((b, t, D_MODEL), BF16),
            jax.ShapeDtypeStruct((b, 2, D_MODEL), F32),
        ],
        scratch_shapes=[
            pltpu.VMEM((2, tc, D_MODEL), F32),
            pltpu.VMEM((2, tc, D_MODEL), F32),
            pltpu.VMEM((2, D_MODEL), F32),
        ],
        compiler_params=_cparams(("parallel", "arbitrary")),
        name=name,
    )(zx, zx, h0, conv_w, conv_b, wa_bd, wx_bd, ba, bx, lru_l)


def _gla_kernel(*refs, t, hps, has_s0, want_final, grid_rows):
    q_ref, k_ref, v_ref, g_ref, lr_ref, w2_ref, b2_ref, gn_ref = refs[:8]
    n = 8
    if grid_rows is not None:
        perm_ref = refs[n]
        n += 1
    if has_s0:
        s0_ref = refs[n]
        n += 1
    on_ref = refs[n]
    n += 1
    if want_final:
        fin_ref = refs[n]
        n += 1
    oacc, st, qe_s, kt_s, bt_s, pp = refs[n:]

    c = GLA_CHUNK
    n_chunks = t // c
    blk = min(GLA_BLOCK, t)
    for d in range(2):
        for hh in range(hps):
            if has_s0:
                st[d, hh] = s0_ref[d, hh].T
            else:
                st[d, hh] = jnp.zeros((GLA_DV, GLA_DK), F32)

    ti = lax.broadcasted_iota(jnp.int32, (blk, blk), 0)
    si = lax.broadcasted_iota(jnp.int32, (blk, blk), 1)
    same = (ti // c) == (si // c)
    cums = ((same & (si <= ti)).astype(BF16), (same & (si >= ti)).astype(BF16))
    sb = min(GLA_SCORE_BLOCK, blk)
    tj = lax.broadcasted_iota(jnp.int32, (sb, sb), 0)
    sj = lax.broadcasted_iota(jnp.int32, (sb, sb), 1)
    same_j = (tj // c) == (sj // c)
    masks = (same_j & (sj <= tj), same_j & (sj >= tj))
    scale = GLA_DK ** -0.5

    def step(ia, slot_a, ib, slot_b):
        if ia is not None:
            rows_a = pl.ds(pl.multiple_of(ia * blk, blk), blk)
            lrb = lr_ref[rows_a, :]
            pre = [_bdot(lrb, w2_ref[d]) + b2_ref[d:d + 1, :] for d in range(2)]
        if ib is not None:
            rows_b = [pl.ds(pl.multiple_of(ib * blk + g * sb, sb), sb) for g in range(blk // sb)]
            loc_b = [slice(g * sb, (g + 1) * sb) for g in range(blk // sb)]
            raw = [[[lax.dot_general(pp[slot_b, 0, d, loc, hh * GLA_DK:(hh + 1) * GLA_DK],
                                     pp[slot_b, 1, d, loc, hh * GLA_DK:(hh + 1) * GLA_DK],
                                     (((1,), (1,)), ((), ())), preferred_element_type=F32)
                     for d in range(2)] for hh in range(hps)] for loc in loc_b]
        if ia is not None:
            bcum = []
            for d in range(2):
                soft = jnp.log(1.0 + jnp.exp2(jnp.abs(pre[d]) * (-LOG2_E)))
                la = jnp.minimum(pre[d], 0.0) * (LOG2_E / GLA_TAU) - soft * (LOG2_E / GLA_TAU)
                hi = la.astype(BF16)
                mid = (la - hi.astype(F32)).astype(BF16)
                bcum.append(_bdot(cums[d], hi) + _bdot(cums[d], mid))
        if ib is not None:
            for g, rows in enumerate(rows_b):
                for hh in range(hps):
                    vs = slice(hh * GLA_DV, (hh + 1) * GLA_DV)
                    sc = jnp.where(masks[0], raw[g][hh][0], 0.0) + jnp.where(masks[1], raw[g][hh][1], 0.0)
                    oacc[rows, vs] = _bdot(sc.astype(BF16), v_ref[rows, vs])
        if ia is not None:
            q = q_ref[rows_a, :].astype(F32)
            k = k_ref[rows_a, :].astype(F32)
            for d in range(2):
                edge = c - 1 if d == 0 else 0
                bc3 = bcum[d].reshape(blk // c, c, hps * GLA_DK)
                btot = jnp.broadcast_to(bc3[:, edge:edge + 1, :], bc3.shape).reshape(bcum[d].shape)
                bt_s[d, rows_a, :] = btot
                qe = ((q * scale) * jnp.exp2(bcum[d])).astype(BF16)
                qe_s[d, rows_a, :] = qe
                pp[slot_a, 0, d] = qe
                pp[slot_a, 1, d] = (k * jnp.exp2(-bcum[d])).astype(BF16)
                kt_s[d, rows_a, :] = (k * jnp.exp2(btot - bcum[d])).astype(BF16)

    def pair(j, carry):
        step(2 * j + 1, 1, 2 * j, 0)
        step(2 * j + 2, 0, 2 * j + 1, 1)
        return carry

    n_blocks = t // blk
    n_pairs = (n_blocks - 1) // 2
    step(0, 0, None, None)
    lax.fori_loop(0, n_pairs, pair, 0)
    done = 2 * n_pairs
    if done < n_blocks - 1:
        step(done + 1, 1, done, 0)
        done += 1
    step(None, None, done, done % 2)

    def chunk(c0, d, hh):
        rows = pl.ds(c0, c)
        ks = slice(hh * GLA_DK, (hh + 1) * GLA_DK)
        vs = slice(hh * GLA_DV, (hh + 1) * GLA_DV)
        s_t = st[d, hh]
        o = lax.dot_general(qe_s[d, rows, ks], s_t.astype(BF16), (((1,), (1,)), ((), ())),
                            preferred_element_type=F32)
        upd = lax.dot_general(v_ref[rows, vs], kt_s[d, rows, ks], (((0,), (0,)), ((), ())),
                              preferred_element_type=F32)
        st[d, hh] = s_t * jnp.exp2(bt_s[d, pl.ds(c0, 1), ks]) + upd
        return o

    def body(jj, carry):
        cf = pl.multiple_of(jj * c, c)
        cbk = pl.multiple_of((n_chunks - 1 - jj) * c, c)
        for hh in range(hps):
            vs = slice(hh * GLA_DV, (hh + 1) * GLA_DV)
            for c0, d in ((cf, 0), (cbk, 1)):
                oacc[pl.ds(c0, c), vs] += chunk(c0, d, hh)
        return carry

    lax.fori_loop(0, n_chunks, body, 0, unroll=4)

    if grid_rows is None:
        ob = min(GLA_BLOCK, t)
    else:
        ob = GLA_OUT_COLS * grid_rows

    def out_block(ib, carry):
        r0 = pl.multiple_of(ib * ob, ob)
        rows = pl.ds(r0, ob)
        parts = []
        for hh in range(hps):
            vs = slice(hh * GLA_DV, (hh + 1) * GLA_DV)
            parts.append((_rms(oacc[rows, vs], gn_ref[...]) * g_ref[rows, vs].astype(F32)).astype(BF16))
        on = jnp.concatenate(parts, axis=1)
        if grid_rows is None:
            on_ref[rows, :] = on
        else:
            on = _bdot(perm_ref[...], on).astype(BF16)
            for r in range(grid_rows):
                dst = pl.multiple_of(r * GRID_W + ib * GLA_OUT_COLS, GLA_OUT_COLS)
                on_ref[pl.ds(dst, GLA_OUT_COLS), :] = on[r * GLA_OUT_COLS:(r + 1) * GLA_OUT_COLS]
        return carry

    lax.fori_loop(0, t // ob, out_block, 0)

    if want_final:
        for d in range(2):
            for hh in range(hps):
                fin_ref[d, hh] = st[d, hh].T


def _gla(q, k, v, g, lr, w2p, b2, gn, s0, *, want_final, hps, col_major, name):
    b, t, _ = q.shape
    ng = GLA_HEADS // hps
    has_s0 = s0 is not None
    grid_rows = t // GRID_W if col_major else None
    in_specs = [
        pl.BlockSpec((None, t, hps * GLA_DK), lambda bb, h: (bb, 0, h)),
        pl.BlockSpec((None, t, hps * GLA_DK), lambda bb, h: (bb, 0, h)),
        pl.BlockSpec((None, t, hps * GLA_DV), lambda bb, h: (bb, 0, h)),
        pl.BlockSpec((None, t, hps * GLA_DV), lambda bb, h: (bb, 0, h)),
        pl.BlockSpec((None, t, LR_PAD), lambda bb, h: (bb, 0, 0)),
        pl.BlockSpec((2, LR_PAD, hps * GLA_DK), lambda bb, h: (0, 0, h)),
        pl.BlockSpec((2, hps * GLA_DK), lambda bb, h: (0, h)),
        pl.BlockSpec((1, GLA_DV), lambda bb, h: (0, 0)),
    ]
    args = [q, k, v, g, lr, w2p, b2, gn]
    if col_major:
        ob = GLA_OUT_COLS * grid_rows
        in_specs.append(pl.BlockSpec((ob, ob), lambda bb, h: (0, 0)))
        args.append(jnp.asarray(_perm_matrix(grid_rows, GLA_OUT_COLS).T, BF16))
    state_spec = pl.BlockSpec((None, 2, hps, GLA_DK, GLA_DV), lambda bb, h: (bb, 0, h, 0, 0))
    if has_s0:
        in_specs.append(state_spec)
        args.append(s0)
    out_specs = [pl.BlockSpec((None, t, hps * GLA_DV), lambda bb, h: (bb, 0, h))]
    out_shape = [jax.ShapeDtypeStruct((b, t, GLA_HEADS * GLA_DV), BF16)]
    if want_final:
        out_specs.append(state_spec)
        out_shape.append(jax.ShapeDtypeStruct((b, 2, GLA_HEADS, GLA_DK, GLA_DV), F32))
    return pl.pallas_call(
        functools.partial(_gla_kernel, t=t, hps=hps, has_s0=has_s0, want_final=want_final,
                          grid_rows=grid_rows),
        grid=(b, ng),
        in_specs=in_specs,
        out_specs=out_specs,
        out_shape=out_shape,
        scratch_shapes=[
            pltpu.VMEM((t, hps * GLA_DV), F32),
            pltpu.VMEM((2, hps, GLA_DV, GLA_DK), F32),
            pltpu.VMEM((2, t, hps * GLA_DK), BF16),
            pltpu.VMEM((2, t, hps * GLA_DK), BF16),
            pltpu.VMEM((2, t, hps * GLA_DK), F32),
            pltpu.VMEM((2, 2, 2, min(GLA_BLOCK, t), hps * GLA_DK), BF16),
        ],
        compiler_params=_cparams(("parallel", "parallel")),
        name=name,
    )(*args)


def _gelu_tanh(x):
    return x * (0.5 * (1.0 + jnp.tanh(0.7978845608028654 * (x + 0.044715 * (x * x * x)))))


def _merge_kernel(x_ref, hf_ref, hb_ref, gz_ref, sga_ref, sgb_ref, on_ref, mod_ref, ng_ref,
                  unperm_ref, lup_ref, gup_ref, wo_ref, o_ref):
    tm = x_ref.shape[0]
    nkb = D_MODEL // MERGE_KBLOCK

    def lru_branch_in(kb):
        ks = slice(kb * MERGE_KBLOCK, (kb + 1) * MERGE_KBLOCK)
        hsum = hf_ref[:, ks].astype(F32) + hb_ref[:, ks].astype(F32)
        return (hsum * gz_ref[:, ks].astype(F32)).astype(BF16)

    ya_in = lru_branch_in(0)
    y_a = None
    y_b = []
    for kb in range(nkb):
        ks = slice(kb * MERGE_KBLOCK, (kb + 1) * MERGE_KBLOCK)
        y_b.append(_bdot(on_ref[...], gup_ref[:, ks]))
        ya_nat = _bdot(unperm_ref[...], ya_in).astype(BF16)
        part = _bdot(ya_nat, lup_ref[ks, :])
        y_a = part if y_a is None else y_a + part
        if kb + 1 < nkb:
            ya_in = lru_branch_in(kb + 1)
    y_b = jnp.concatenate(y_b, axis=1)
    rb = tm // MERGE_ROW_SPLIT
    for r in range(MERGE_ROW_SPLIT):
        rs = slice(r * rb, (r + 1) * rb)
        mm = sga_ref[rs, :].astype(F32) * y_a[rs] + sgb_ref[rs, :].astype(F32) * y_b[rs]
        m = _bdot(mm.astype(BF16), wo_ref[...])
        o_ref[rs, :] = x_ref[rs, :] + mod_ref[2:3, :] * _rms(m, ng_ref[1:2, :])


def _merge(x, hf, hb, zg, ga, gb, on, mod, norm_g, lru_up, gla_up, w_out, *, per_batch_mod, name):
    b, t, _ = x.shape
    tm = _lru_chunk(t)
    perm = _interleave_matrix(tm)
    spec = _tok_spec(D_MODEL, tm, None)
    wspec = _const_spec((D_MODEL, D_MODEL))
    pspec = _const_spec((tm, tm))
    return pl.pallas_call(
        _merge_kernel,
        grid=(b, t // tm),
        in_specs=[spec] * 7 + [_mod_spec(per_batch_mod), _const_spec((4, D_MODEL)), pspec]
        + [wspec] * 3,
        out_specs=spec,
        out_shape=jax.ShapeDtypeStruct((b, t, D_MODEL), F32),
        compiler_params=_cparams(("parallel", "parallel")),
        name=name,
    )(x, hf, hb, zg, ga, gb, on, mod, norm_g, jnp.asarray(perm.T, BF16), lru_up, gla_up, w_out)


def _ffn_kernel(x_ref, mod_ref, ng_ref, w1_ref, w2_ref, o_ref, *, n_split):
    x = x_ref[...]
    h = (_rms(x, ng_ref[2:3, :]) * (1.0 + mod_ref[4:5, :]) + mod_ref[3:4, :]).astype(BF16)
    fs = D_FF // n_split
    acc = jnp.zeros(x.shape, F32)
    for kf in range(n_split):
        hid = jnp.maximum(_bdot(h, w1_ref[:, kf * fs:(kf + 1) * fs]), 0.0)
        acc = acc + _bdot((hid * hid).astype(BF16), w2_ref[kf * fs:(kf + 1) * fs, :])
    o_ref[...] = x + mod_ref[5:6, :] * _rms(acc, ng_ref[3:4, :])


def _ffn(x, mod, norm_g, w1, w2, *, per_batch_mod, tm, name):
    b, t, _ = x.shape
    spec = _tok_spec(D_MODEL, tm, None)
    return pl.pallas_call(
        functools.partial(_ffn_kernel, n_split=4),
        grid=(b, t // tm),
        in_specs=[spec, _mod_spec(per_batch_mod), _const_spec((4, D_MODEL)),
                  _const_spec((D_MODEL, D_FF)), _const_spec((D_FF, D_MODEL))],
        out_specs=spec,
        out_shape=jax.ShapeDtypeStruct((b, t, D_MODEL), F32),
        compiler_params=_cparams(("parallel", "parallel")),
        name=name,
    )(x, mod, norm_g, w1, w2)


def _blockdiag_slabs(w):
    per = LRU_COLB // LRU_BW
    w = w.reshape(2, LRU_NCOLB, per, LRU_BW, LRU_BW)
    eye = jnp.eye(per, dtype=w.dtype)
    slab = jnp.einsum('dspij,pq->dspiqj', w, eye)
    return slab.reshape(2, LRU_NCOLB, LRU_COLB, LRU_COLB).astype(BF16)


def _layer_group(x, mod, p, lru_h0, gla_s0, *, latent, want_state, tag):
    b, t, _ = x.shape
    per_batch = latent
    ga, gb, zg, zx = _inproj(x, mod, p['norm_g'], p['w_rm'], (_sigmoid, _sigmoid, _gelu_tanh, None),
                             per_batch_mod=per_batch, order='interleaved', n_natural=2,
                             name=f"inproj_rm_{tag}")
    q, k, v, g, lr = _inproj(x, mod, p['norm_g'], p['w_cm'], (None, None, None, _silu, None),
                             per_batch_mod=per_batch, order='col_major' if latent else 'natural',
                             name=f"inproj_cm_{tag}")
    period = GRID_W if latent else t
    hf, hb, s_lru = _lru(zx, lru_h0, p['conv_w'], p['conv_b'], p['wa_bd'], p['wx_bd'], p['lru_ba'],
                         p['lru_bx'], p['lru_L'], period=period, name=f"lru_{tag}")
    res = _gla(q, k, v, g, lr, p['w2p'], p['gla_b2'], p['gla_norm_g'], gla_s0,
               want_final=want_state, hps=2, col_major=latent, name=f"gla_{tag}")
    on = res[0]
    s_gla = res[1] if want_state else None
    x1 = _merge(x, hf, hb, zg, ga, gb, on, mod, p['norm_g'], p['lru_up'], p['gla_up'], p['w_out'],
                per_batch_mod=per_batch, name=f"merge_{tag}")
    tm = 256 if t % 256 == 0 else t
    y = _ffn(x1, mod, p['norm_g'], p['mlp_w1'], p['mlp_w2'], per_batch_mod=per_batch, tm=tm,
             name=f"ffn_{tag}")
    return y, s_lru, s_gla


def kernel(x_prompt, x_sample, state_lru, state_gla, c, c_ctx, w_mod, b_mod, norm_g, w_in, conv_w, conv_b,
           lru_wa, lru_ba, lru_wx, lru_bx, lru_L, lru_up, gla_w2, gla_b2, gla_norm_g, gla_up, w_out,
           mlp_w1, mlp_w2):
    depth = w_in.shape[0]
    assert depth == 1
    dec_b = x_sample.shape[0]
    b0 = x_prompt.shape[0]
    xp, xs = x_prompt, x_sample
    lru_states, gla_states = [], []
    mod_rows = -(-(1 + dec_b) // V7X_SUBLANES) * V7X_SUBLANES
    cc = jnp.zeros((mod_rows, D_MODEL), F32).at[0].set(c_ctx).at[1:1 + dec_b].set(c)
    for l in range(depth):
        wl = w_in[l]
        o = np.cumsum((0, D_MODEL, D_MODEL, 512, 512, 1024, 1024, 2 * GLA_RANK, D_MODEL, D_MODEL))
        col = lambda i: wl[:, o[i]:o[i + 1]].astype(BF16)
        lr_w = jnp.pad(col(6), ((0, 0), (0, LR_PAD - 2 * GLA_RANK)))
        w2 = gla_w2[l].astype(BF16)
        w2p = jnp.zeros((2, LR_PAD, GLA_HEADS * GLA_DK), BF16)
        w2p = w2p.at[0, 0:GLA_RANK].set(w2[0]).at[1, GLA_RANK:2 * GLA_RANK].set(w2[1])
        p = {
            'norm_g': norm_g[l],
            'w_rm': [col(7), col(8), col(1), col(0)],
            'w_cm': [col(2), col(3), col(4), col(5), lr_w],
            'conv_w': conv_w[l], 'conv_b': conv_b[l].reshape(1, D_MODEL),
            'wa_bd': _blockdiag_slabs(0.5 * lru_wa[l]), 'wx_bd': _blockdiag_slabs(0.5 * lru_wx[l]),
            'lru_ba': lru_ba[l], 'lru_bx': lru_bx[l], 'lru_L': lru_L[l],
            'lru_up': lru_up[l].astype(BF16),
            'w2p': w2p, 'gla_b2': gla_b2[l], 'gla_norm_g': gla_norm_g[l].reshape(1, GLA_DV),
            'gla_up': gla_up[l].astype(BF16), 'w_out': w_out[l].astype(BF16),
            'mlp_w1': mlp_w1[l].astype(BF16), 'mlp_w2': mlp_w2[l].astype(BF16),
        }
        mod = _modulation(cc, w_mod[l], b_mod[l])
        h0_ctx = jnp.zeros((b0, 2, D_MODEL), F32)
        xp, s_lru, s_gla = _layer_group(xp, mod, p, h0_ctx, None, latent=False, want_state=True,
                                        tag="ctx")
        lru_states.append(s_lru)
        gla_states.append(s_gla)
        xs, _, _ = _layer_group(xs, mod, p, state_lru[:, l], state_gla[:, l], latent=True,
                                want_state=False, tag="lat")
    return (xp, xs, jnp.stack(lru_states, axis=1), jnp.stack(gla_states, axis=1))
```

```python
import functools

import numpy as np
import jax
import jax.numpy as jnp
from jax import lax
from jax.experimental import pallas as pl
from jax.experimental.pallas import tpu as pltpu

F32 = jnp.float32
BF16 = jnp.bfloat16

D_MODEL = 1024
GRID_W = 64
LRU_BLOCKS = 16
LRU_BW = D_MODEL // LRU_BLOCKS
LRU_C = 8.0
GLA_HEADS = 4
GLA_DK = 128
GLA_DV = 256
GLA_RANK = 16
GLA_TAU = 16.0
GLA_CHUNK = 64
D_FF = 4 * D_MODEL
N_MOD = 6
EPS = 1e-6
LOG2_E = 1.4426950408889634

V7X_LANES = 128
V7X_SUBLANES = 8
V7X_MXU_DIM = 256
V7X_VMEM_BYTES = 64 * 1024 * 1024
VMEM_LIMIT = V7X_VMEM_BYTES - 8 * 1024 * 1024

LRU_COLB = V7X_MXU_DIM
LRU_NCOLB = D_MODEL // LRU_COLB
LR_PAD = V7X_LANES
COL_BLOCK = 16
GLA_BLOCK = V7X_MXU_DIM
GLA_SCORE_BLOCK = 2 * GLA_CHUNK
GLA_OUT_COLS = 16
LRU_CHUNK = 256
MERGE_KBLOCK = V7X_MXU_DIM
MERGE_ROW_SPLIT = 2


def _cparams(sem):
    return pltpu.CompilerParams(dimension_semantics=sem, vmem_limit_bytes=VMEM_LIMIT)


def _rms(x, g):
    ms = jnp.mean(x * x, axis=-1, keepdims=True)
    return x * lax.rsqrt(ms + EPS) * g


def _sigmoid(x):
    return 0.5 * jnp.tanh(0.5 * x) + 0.5


def _softplus(x):
    return jnp.maximum(x, 0.0) + jnp.log1p(jnp.exp(-jnp.abs(x)))


def _bdot(a, b):
    return jnp.dot(a, b, preferred_element_type=F32)


def _mod_kernel(c_ref, w_ref, b_ref, o_ref):
    c = c_ref[...]
    s = (c * _sigmoid(c)).astype(BF16)
    o_ref[...] = _bdot(s, w_ref[...].astype(BF16)) + b_ref[...]


def _modulation(cc, w_mod, b_mod):
    rows = cc.shape[0]
    return pl.pallas_call(
        _mod_kernel,
        grid=(N_MOD,),
        in_specs=[
            pl.BlockSpec((rows, D_MODEL), lambda n: (0, 0)),
            pl.BlockSpec((D_MODEL, D_MODEL), lambda n: (0, n)),
            pl.BlockSpec((1, D_MODEL), lambda n: (0, n)),
        ],
        out_specs=pl.BlockSpec((rows, D_MODEL), lambda n: (0, n)),
        out_shape=jax.ShapeDtypeStruct((rows, N_MOD * D_MODEL), F32),
        compiler_params=_cparams(("arbitrary",)),
        name="modulation",
    )(cc, w_mod, b_mod.reshape(1, N_MOD * D_MODEL)).reshape(rows, N_MOD, D_MODEL)


def _tok_view(a, grid_tile):
    if grid_tile is None:
        return a
    b, t, n = a.shape
    rows = t // GRID_W
    return a.reshape(b, rows, GRID_W // COL_BLOCK, COL_BLOCK, n)


def _tok_spec(n, tm, grid_tile):
    if grid_tile is None:
        return pl.BlockSpec((None, tm, n), lambda b, j: (b, j, 0))
    rows = grid_tile
    return pl.BlockSpec((None, rows, None, COL_BLOCK, n), lambda b, j: (b, 0, j, 0, 0))


def _const_spec(shape):
    nd = len(shape)
    return pl.BlockSpec(shape, lambda b, j: (0,) * nd)


def _mod_spec(per_batch):
    if per_batch:
        return pl.BlockSpec((None, N_MOD, D_MODEL), lambda b, j: (b + 1, 0, 0))
    return pl.BlockSpec((None, N_MOD, D_MODEL), lambda b, j: (0, 0, 0))


def _perm_matrix(rows, cols):
    n = rows * cols
    p = np.zeros((n, n), np.float32)
    for r in range(rows):
        for c in range(cols):
            p[c * rows + r, r * cols + c] = 1.0
    return p


def _silu(x):
    return x * _sigmoid(x)


def _inproj_kernel(*refs, n_out, permute, n_natural, acts):
    x_ref, mod_ref, ng_ref = refs[:3]
    k = 3
    if permute:
        perm_ref = refs[3]
        k = 4
    w_refs = refs[k:k + n_out]
    o_refs = refs[k + n_out:]
    x = x_ref[...].reshape(-1, D_MODEL)
    h = _rms(x, ng_ref[0:1, :]) * (1.0 + mod_ref[1:2, :]) + mod_ref[0:1, :]
    hb = h.astype(BF16)
    hp = hb
    if permute:
        hp = _bdot(perm_ref[...], hb).astype(BF16)
    for i, (w_ref, o_ref) in enumerate(zip(w_refs, o_refs)):
        z = _bdot(hb if i < n_natural else hp, w_ref[...])
        if acts[i] is not None:
            z = acts[i](z)
        o_ref[...] = z.astype(o_ref.dtype).reshape(o_ref.shape)


def _inproj(x, mod, norm_g, weights, acts, *, per_batch_mod, order, name, n_natural=0):
    b, t, _ = x.shape
    if order == 'col_major':
        rows = t // GRID_W
        tm = rows * COL_BLOCK
        x_in = _tok_view(x, rows)
        x_spec = _tok_spec(D_MODEL, tm, rows)
        perm = _perm_matrix(rows, COL_BLOCK)
    else:
        tm = _lru_chunk(t)
        x_in = x
        x_spec = _tok_spec(D_MODEL, tm, None)
        perm = _interleave_matrix(tm) if order == 'interleaved' else None
    extra = [] if perm is None else [jnp.asarray(perm, BF16)]
    return pl.pallas_call(
        functools.partial(_inproj_kernel, n_out=len(weights), permute=perm is not None,
                          n_natural=n_natural, acts=tuple(acts)),
        grid=(b, t // tm),
        in_specs=[x_spec, _mod_spec(per_batch_mod), _const_spec((4, D_MODEL))]
        + [_const_spec(e.shape) for e in extra] + [_const_spec(w.shape) for w in weights],
        out_specs=[_tok_spec(w.shape[1], tm, None) for w in weights],
        out_shape=[jax.ShapeDtypeStruct((b, t, w.shape[1]), BF16) for w in weights],
        compiler_params=_cparams(("parallel", "parallel")),
        name=name,
    )(x_in, mod, norm_g, *extra, *weights)


def _lru_chunk(t):
    return min(LRU_CHUNK, t)


def _interleave_matrix(tc):
    return _perm_matrix(V7X_SUBLANES, tc // V7X_SUBLANES)


def _lru_kernel(zf_ref, zb_ref, h0_ref, cw_ref, cb_ref, wa_ref, wx_ref, ba_ref, bx_ref, l_ref,
                hf_ref, hb_ref, fin_ref, a_s, u_s, carry_s, *, tc, period):
    j = pl.program_id(1)
    ns = V7X_SUBLANES
    sl = tc // ns

    @pl.when(j == 0)
    def _():
        carry_s[...] = h0_ref[...]

    sub_c = lax.broadcasted_iota(jnp.int32, (ns, LRU_COLB), 0)
    prev_ok = jnp.bitwise_and(sub_c * sl, period - 1) != 0
    next_ok = jnp.bitwise_and((sub_c + 1) * sl, period - 1) != 0
    for d, z_ref in ((0, zf_ref), (1, zb_ref)):
        for cb in range(LRU_NCOLB):
            cs = slice(cb * LRU_COLB, (cb + 1) * LRU_COLB)
            x = z_ref[:, cs].astype(F32)
            e_prev = jnp.where(prev_ok, pltpu.roll(x[tc - ns:], 1, 0), 0.0)
            e_next0 = jnp.where(next_ok, pltpu.roll(x[:ns], ns - 1, 0), 0.0)
            e_next1 = jnp.where(next_ok, pltpu.roll(x[ns:2 * ns], ns - 1, 0), 0.0)
            xm1 = jnp.concatenate([e_prev, x[:tc - ns]], axis=0)
            xp1 = jnp.concatenate([x[ns:], e_next0], axis=0)
            xp2 = jnp.concatenate([x[2 * ns:], e_next0, e_next1], axis=0)
            xc = (cb_ref[0:1, cs] + xm1 * cw_ref[0:1, cs] + x * cw_ref[1:2, cs]
                  + xp1 * cw_ref[2:3, cs] + xp2 * cw_ref[3:4, cs])
            xcb = xc.astype(BF16)
            tr = jnp.tanh(_bdot(xcb, wa_ref[d, cb]) + 0.5 * ba_ref[d:d + 1, cs])
            ti = jnp.tanh(_bdot(xcb, wx_ref[d, cb]) + 0.5 * bx_ref[d:d + 1, cs])
            k2 = (-0.5 * LRU_C * LOG2_E) * _softplus(-l_ref[d:d + 1, cs])
            a = jnp.exp2(tr * k2 + k2)
            y = 1.0 - a * a
            u = jnp.where(y > 0.0, y * lax.rsqrt(y), 0.0) * ((0.5 * ti + 0.5) * xc)
            a_s[d, :, cs] = a
            u_s[d, :, cs] = u

    unroll = 4
    row = lambda i: pl.ds(pl.multiple_of(i * ns, ns), ns)

    def pass1(g, carry):
        hf, af, hb, ab = carry
        for k in range(unroll):
            rf = row(g * unroll + k)
            rb = row(sl - 1 - (g * unroll + k))
            a = a_s[0, rf, :]
            hf = a * hf + u_s[0, rf, :]
            af = a * af
            u_s[0, rf, :] = hf
            a_s[0, rf, :] = af
            a = a_s[1, rb, :]
            hb = a * hb + u_s[1, rb, :]
            ab = a * ab
            u_s[1, rb, :] = hb
            a_s[1, rb, :] = ab
        return hf, af, hb, ab

    zeros = jnp.zeros((ns, D_MODEL), F32)
    ones = jnp.ones((ns, D_MODEL), F32)
    hf, af, hb, ab = lax.fori_loop(0, sl // unroll, pass1, (zeros, ones, zeros, ones))

    sub = lax.broadcasted_iota(jnp.int32, (ns, D_MODEL), 0)

    def sublane_scan(a, u, forward):
        for s in (1, 2, 4):
            if forward:
                m = sub >= s
                shift = s
            else:
                m = sub < ns - s
                shift = ns - s
            a_sh = jnp.where(m, pltpu.roll(a, shift, 0), 1.0)
            u_sh = jnp.where(m, pltpu.roll(u, shift, 0), 0.0)
            u = a * u_sh + u
            a = a * a_sh
        return a, u

    cin_f = carry_s[0:1, :]
    cin_b = carry_s[1:2, :]
    af, hf = sublane_scan(af, hf, True)
    ab, hb = sublane_scan(ab, hb, False)
    fin_f = hf + af * cin_f
    fin_b = hb + ab * cin_b
    enter_f = jnp.where(sub == 0, cin_f, pltpu.roll(fin_f, 1, 0))
    enter_b = jnp.where(sub == ns - 1, cin_b, pltpu.roll(fin_b, ns - 1, 0))
    out_f = fin_f[ns - 1:ns, :]
    out_b = fin_b[0:1, :]
    carry_s[0:1, :] = out_f
    carry_s[1:2, :] = out_b
    fin_ref[0:1, :] = out_f
    fin_ref[1:2, :] = out_b

    pk = 2 * ns
    enter_f2 = jnp.concatenate([enter_f, enter_f], axis=0)
    enter_b2 = jnp.concatenate([enter_b, enter_b], axis=0)

    def pass2(g, carry):
        for k in range(unroll // 2):
            rr = pl.ds(pl.multiple_of((g * (unroll // 2) + k) * pk, pk), pk)
            hf_ref[rr, :] = (u_s[0, rr, :] + a_s[0, rr, :] * enter_f2).astype(hf_ref.dtype)
            hb_ref[rr, :] = (u_s[1, rr, :] + a_s[1, rr, :] * enter_b2).astype(hb_ref.dtype)
        return carry

    lax.fori_loop(0, sl // unroll, pass2, 0)


def _lru(zx, h0, conv_w, conv_b, wa_bd, wx_bd, ba, bx, lru_l, *, period, name):
    b, t, _ = zx.shape
    tc = _lru_chunk(t)
    nc = t // tc
    assert period % (tc // V7X_SUBLANES) == 0 and tc % period == 0
    const = lambda shape: pl.BlockSpec(shape, lambda bb, j: (0,) * len(shape))
    return pl.pallas_call(
        functools.partial(_lru_kernel, tc=tc, period=period),
        grid=(b, nc),
        in_specs=[
            pl.BlockSpec((None, tc, D_MODEL), lambda bb, j: (bb, j, 0)),
            pl.BlockSpec((None, tc, D_MODEL), lambda bb, j: (bb, nc - 1 - j, 0)),
            pl.BlockSpec((None, 2, D_MODEL), lambda bb, j: (bb, 0, 0)),
            const((4, D_MODEL)), const((1, D_MODEL)),
            const((2, LRU_NCOLB, LRU_COLB, LRU_COLB)), const((2, LRU_NCOLB, LRU_COLB, LRU_COLB)),
            const((2, D_MODEL)), const((2, D_MODEL)), const((2, D_MODEL)),
        ],
        out_specs=[
            pl.BlockSpec((None, tc, D_MODEL), lambda bb, j: (bb, j, 0)),
            pl.BlockSpec((None, tc, D_MODEL), lambda bb, j: (bb, nc - 1 - j, 0)),
            pl.BlockSpec((None, 2, D_MODEL), lambda bb, j: (bb, 0, 0)),
        ],
        out_shape=[
            jax.ShapeDtypeStruct((b, t, D_MODEL), BF16),
            jax.ShapeDtypeStruct((b, t, D_MODEL), BF16),
            jax.ShapeDtypeStruct((b, 2, D_MODEL), F32),
        ],
        scratch_shapes=[
            pltpu.VMEM((2, tc, D_MODEL), F32),
            pltpu.VMEM((2, tc, D_MODEL), F32),
            pltpu.VMEM((2, D_MODEL), F32),
        ],
        compiler_params=_cparams(("parallel", "arbitrary")),
        name=name,
    )(zx, zx, h0, conv_w, conv_b, wa_bd, wx_bd, ba, bx, lru_l)


def _gla_kernel(*refs, t, hps, has_s0, want_final, grid_rows):
    q_ref, k_ref, v_ref, g_ref, lr_ref, w2_ref, b2_ref, gn_ref = refs[:8]
    n = 8
    if grid_rows is not None:
        perm_ref = refs[n]
        n += 1
    if has_s0:
        s0_ref = refs[n]
        n += 1
    on_ref = refs[n]
    n += 1
    if want_final:
        fin_ref = refs[n]
        n += 1
    oacc, st, q2_s, k2_s, bt_s, pp_q, pp_k, pp_qx, pp_kx = refs[n:]

    c = GLA_CHUNK
    n_chunks = t // c
    blk = min(GLA_BLOCK, t)
    for d in range(2):
        for hh in range(hps):
            if has_s0:
                st[d, hh] = s0_ref[d, hh].T
            else:
                st[d, hh] = jnp.zeros((GLA_DV, GLA_DK), F32)

    ti = lax.broadcasted_iota(jnp.int32, (blk, blk), 0)
    si = lax.broadcasted_iota(jnp.int32, (blk, blk), 1)
    same = (ti // c) == (si // c)
    cums = ((same & (si <= ti)).astype(BF16), (same & (si >= ti)).astype(BF16))
    sb = min(GLA_SCORE_BLOCK, blk)
    tj = lax.broadcasted_iota(jnp.int32, (sb, sb), 0)
    sj = lax.broadcasted_iota(jnp.int32, (sb, sb), 1)
    same_j = (tj // c) == (sj // c)
    masks = (same_j & (sj <= tj), same_j & (sj >= tj))
    scale = GLA_DK ** -0.5

    def step(ia, slot_a, ib, slot_b):
        nd = hps * GLA_DK
        if ia is not None:
            rows_a = pl.ds(pl.multiple_of(ia * blk, blk), blk)
            lrb = lr_ref[rows_a, :]
            pre = [_bdot(lrb, w2_ref[d]) + b2_ref[d:d + 1, :] for d in range(2)]
        if ib is not None:
            rows_b = [pl.ds(pl.multiple_of(ib * blk + g * sb, sb), sb) for g in range(blk // sb)]
            loc_b = [slice(g * sb, (g + 1) * sb) for g in range(blk // sb)]
            nt = (((1,), (1,)), ((), ()))
            raw = [[[lax.dot_general(pp_q[slot_b, d, loc, hh * GLA_DK:(hh + 1) * GLA_DK],
                                     pp_k[slot_b, d, loc, hh * GLA_DK:(hh + 1) * GLA_DK], nt,
                                     preferred_element_type=F32) for d in range(2)]
                    + [lax.dot_general(pp_qx[slot_b, loc, hh * 2 * GLA_DK:(hh + 1) * 2 * GLA_DK],
                                       pp_kx[slot_b, loc, hh * 2 * GLA_DK:(hh + 1) * 2 * GLA_DK], nt,
                                       preferred_element_type=F32)]
                    for hh in range(hps)] for loc in loc_b]
        if ia is not None:
            bcum = []
            for d in range(2):
                soft = jnp.log(1.0 + jnp.exp2(jnp.abs(pre[d]) * (-LOG2_E)))
                la = jnp.minimum(pre[d], 0.0) * (LOG2_E / GLA_TAU) - soft * (LOG2_E / GLA_TAU)
                hi = la.astype(BF16)
                mid = (la - hi.astype(F32)).astype(BF16)
                bcum.append(_bdot(cums[d], hi) + _bdot(cums[d], mid))
        if ib is not None:
            for g, rows in enumerate(rows_b):
                for hh in range(hps):
                    vs = slice(hh * GLA_DV, (hh + 1) * GLA_DV)
                    sd0, sd1, sx = raw[g][hh]
                    sc = jnp.where(masks[0], sd0, 0.0) + jnp.where(masks[1], sd1, 0.0) + sx
                    oacc[rows, vs] = _bdot(sc.astype(BF16), v_ref[rows, vs])
        if ia is not None:
            q = q_ref[rows_a, :].astype(F32)
            k = k_ref[rows_a, :].astype(F32)
            nc = blk // c
            crow = lambda a, n: a[n * c:(n + 1) * c]
            cat = lambda parts: jnp.concatenate(parts, axis=0)
            zero = jnp.zeros((c, nd), BF16)
            qx, kx = [], []
            for d in range(2):
                edge = c - 1 if d == 0 else 0
                tot = [bcum[d][n * c + edge:n * c + edge + 1, :] for n in range(nc)]
                dec = [jnp.exp2(x) for x in tot]
                btot = cat([jnp.broadcast_to(x, (c, nd)) for x in tot])
                qe = (q * scale) * jnp.exp2(bcum[d])
                kt = k * jnp.exp2(btot - bcum[d])
                pp_q[slot_a, d] = qe.astype(BF16)
                pp_k[slot_a, d] = (k * jnp.exp2(-bcum[d])).astype(BF16)
                first = [(n % 2 == 0) == (d == 0) for n in range(nc)]
                q2 = cat([crow(qe, n) if first[n] else crow(qe, n) * dec[n ^ 1] for n in range(nc)])
                k2 = cat([crow(kt, n) * dec[n ^ 1] if first[n] else crow(kt, n) for n in range(nc)])
                q2_s[d, rows_a, :] = q2.astype(BF16)
                k2_s[d, rows_a, :] = k2.astype(BF16)
                bt_s[d, rows_a, :] = cat([jnp.broadcast_to(tot[n - n % 2] + tot[n - n % 2 + 1], (c, nd))
                                          for n in range(nc)])
                qeb, ktb = qe.astype(BF16), kt.astype(BF16)
                qx.append(cat([zero if first[n] else crow(qeb, n) for n in range(nc)]))
                kx.append(cat([crow(ktb, n) if first[n] else zero for n in range(nc)]))
            for hh in range(hps):
                ks = slice(hh * GLA_DK, (hh + 1) * GLA_DK)
                xs = slice(hh * 2 * GLA_DK, (hh + 1) * 2 * GLA_DK)
                pp_qx[slot_a, :, xs] = jnp.concatenate([qx[0][:, ks], qx[1][:, ks]], axis=1)
                pp_kx[slot_a, :, xs] = jnp.concatenate([kx[0][:, ks], kx[1][:, ks]], axis=1)

    def pair(j, carry):
        step(2 * j + 1, 1, 2 * j, 0)
        step(2 * j + 2, 0, 2 * j + 1, 1)
        return carry

    n_blocks = t // blk
    n_pairs = (n_blocks - 1) // 2
    step(0, 0, None, None)
    lax.fori_loop(0, n_pairs, pair, 0)
    done = 2 * n_pairs
    if done < n_blocks - 1:
        step(done + 1, 1, done, 0)
        done += 1
    step(None, None, done, done % 2)

    pr = 2 * c
    n_prs = t // pr

    def pair_step(p0, d, hh):
        rows = pl.ds(p0, pr)
        ks = slice(hh * GLA_DK, (hh + 1) * GLA_DK)
        vs = slice(hh * GLA_DV, (hh + 1) * GLA_DV)
        s_t = st[d, hh]
        o = lax.dot_general(q2_s[d, rows, ks], s_t.astype(BF16), (((1,), (1,)), ((), ())),
                            preferred_element_type=F32)
        upd = lax.dot_general(v_ref[rows, vs], k2_s[d, rows, ks], (((0,), (0,)), ((), ())),
                              preferred_element_type=F32)
        st[d, hh] = s_t * jnp.exp2(bt_s[d, pl.ds(p0, 1), ks]) + upd
        return o

    def body(jj, carry):
        pf = pl.multiple_of(jj * pr, pr)
        pb = pl.multiple_of((n_prs - 1 - jj) * pr, pr)
        for hh in range(hps):
            vs = slice(hh * GLA_DV, (hh + 1) * GLA_DV)
            for p0, d in ((pf, 0), (pb, 1)):
                oacc[pl.ds(p0, pr), vs] += pair_step(p0, d, hh)
        return carry

    lax.fori_loop(0, n_prs, body, 0, unroll=4)

    if grid_rows is None:
        ob = min(GLA_BLOCK, t)
    else:
        ob = GLA_OUT_COLS * grid_rows

    def out_block(ib, carry):
        r0 = pl.multiple_of(ib * ob, ob)
        rows = pl.ds(r0, ob)
        parts = []
        for hh in range(hps):
            vs = slice(hh * GLA_DV, (hh + 1) * GLA_DV)
            parts.append((_rms(oacc[rows, vs], gn_ref[...]) * g_ref[rows, vs].astype(F32)).astype(BF16))
        on = jnp.concatenate(parts, axis=1)
        if grid_rows is None:
            on_ref[rows, :] = on
        else:
            on = _bdot(perm_ref[...], on).astype(BF16)
            for r in range(grid_rows):
                dst = pl.multiple_of(r * GRID_W + ib * GLA_OUT_COLS, GLA_OUT_COLS)
                on_ref[pl.ds(dst, GLA_OUT_COLS), :] = on[r * GLA_OUT_COLS:(r + 1) * GLA_OUT_COLS]
        return carry

    lax.fori_loop(0, t // ob, out_block, 0)

    if want_final:
        for d in range(2):
            for hh in range(hps):
                fin_ref[d, hh] = st[d, hh].T


def _gla(q, k, v, g, lr, w2p, b2, gn, s0, *, want_final, hps, col_major, name):
    b, t, _ = q.shape
    ng = GLA_HEADS // hps
    has_s0 = s0 is not None
    grid_rows = t // GRID_W if col_major else None
    in_specs = [
        pl.BlockSpec((None, t, hps * GLA_DK), lambda bb, h: (bb, 0, h)),
        pl.BlockSpec((None, t, hps * GLA_DK), lambda bb, h: (bb, 0, h)),
        pl.BlockSpec((None, t, hps * GLA_DV), lambda bb, h: (bb, 0, h)),
        pl.BlockSpec((None, t, hps * GLA_DV), lambda bb, h: (bb, 0, h)),
        pl.BlockSpec((None, t, LR_PAD), lambda bb, h: (bb, 0, 0)),
        pl.BlockSpec((2, LR_PAD, hps * GLA_DK), lambda bb, h: (0, 0, h)),
        pl.BlockSpec((2, hps * GLA_DK), lambda bb, h: (0, h)),
        pl.BlockSpec((1, GLA_DV), lambda bb, h: (0, 0)),
    ]
    args = [q, k, v, g, lr, w2p, b2, gn]
    if col_major:
        ob = GLA_OUT_COLS * grid_rows
        in_specs.append(pl.BlockSpec((ob, ob), lambda bb, h: (0, 0)))
        args.append(jnp.asarray(_perm_matrix(grid_rows, GLA_OUT_COLS).T, BF16))
    state_spec = pl.BlockSpec((None, 2, hps, GLA_DK, GLA_DV), lambda bb, h: (bb, 0, h, 0, 0))
    if has_s0:
        in_specs.append(state_spec)
        args.append(s0)
    out_specs = [pl.BlockSpec((None, t, hps * GLA_DV), lambda bb, h: (bb, 0, h))]
    out_shape = [jax.ShapeDtypeStruct((b, t, GLA_HEADS * GLA_DV), BF16)]
    if want_final:
        out_specs.append(state_spec)
        out_shape.append(jax.ShapeDtypeStruct((b, 2, GLA_HEADS, GLA_DK, GLA_DV), F32))
    return pl.pallas_call(
        functools.partial(_gla_kernel, t=t, hps=hps, has_s0=has_s0, want_final=want_final,
                          grid_rows=grid_rows),
        grid=(b, ng),
        in_specs=in_specs,
        out_specs=out_specs,
        out_shape=out_shape,
        scratch_shapes=[
            pltpu.VMEM((t, hps * GLA_DV), F32),
            pltpu.VMEM((2, hps, GLA_DV, GLA_DK), F32),
            pltpu.VMEM((2, t, hps * GLA_DK), BF16),
            pltpu.VMEM((2, t, hps * GLA_DK), BF16),
            pltpu.VMEM((2, t, hps * GLA_DK), F32),
            pltpu.VMEM((2, 2, min(GLA_BLOCK, t), hps * GLA_DK), BF16),
            pltpu.VMEM((2, 2, min(GLA_BLOCK, t), hps * GLA_DK), BF16),
            pltpu.VMEM((2, min(GLA_BLOCK, t), 2 * hps * GLA_DK), BF16),
            pltpu.VMEM((2, min(GLA_BLOCK, t), 2 * hps * GLA_DK), BF16),
        ],
        compiler_params=_cparams(("parallel", "parallel")),
        name=name,
    )(*args)


def _gelu_tanh(x):
    return x * (0.5 * (1.0 + jnp.tanh(0.7978845608028654 * (x + 0.044715 * (x * x * x)))))


def _merge_kernel(x_ref, hf_ref, hb_ref, zg_ref, ga_ref, gb_ref, on_ref, mod_ref, ng_ref,
                  unperm_ref, lup_ref, gup_ref, wo_ref, o_ref):
    tm = x_ref.shape[0]
    nkb = D_MODEL // MERGE_KBLOCK

    def lru_branch_in(kb):
        ks = slice(kb * MERGE_KBLOCK, (kb + 1) * MERGE_KBLOCK)
        hsum = hf_ref[:, ks].astype(F32) + hb_ref[:, ks].astype(F32)
        return (hsum * _gelu_tanh(zg_ref[:, ks].astype(F32))).astype(BF16)

    ya_in = lru_branch_in(0)
    y_a = None
    y_b = []
    for kb in range(nkb):
        ks = slice(kb * MERGE_KBLOCK, (kb + 1) * MERGE_KBLOCK)
        y_b.append(_bdot(on_ref[...], gup_ref[:, ks]))
        ya_nat = _bdot(unperm_ref[...], ya_in).astype(BF16)
        part = _bdot(ya_nat, lup_ref[ks, :])
        y_a = part if y_a is None else y_a + part
        if kb + 1 < nkb:
            ya_in = lru_branch_in(kb + 1)
    y_b = jnp.concatenate(y_b, axis=1)
    rb = tm // MERGE_ROW_SPLIT
    for r in range(MERGE_ROW_SPLIT):
        rs = slice(r * rb, (r + 1) * rb)
        mm = (_sigmoid(ga_ref[rs, :].astype(F32)) * y_a[rs]
              + _sigmoid(gb_ref[rs, :].astype(F32)) * y_b[rs])
        m = _bdot(mm.astype(BF16), wo_ref[...])
        o_ref[rs, :] = x_ref[rs, :] + mod_ref[2:3, :] * _rms(m, ng_ref[1:2, :])


def _merge(x, hf, hb, zg, ga, gb, on, mod, norm_g, lru_up, gla_up, w_out, *, per_batch_mod, name):
    b, t, _ = x.shape
    tm = _lru_chunk(t)
    perm = _interleave_matrix(tm)
    spec = _tok_spec(D_MODEL, tm, None)
    wspec = _const_spec((D_MODEL, D_MODEL))
    pspec = _const_spec((tm, tm))
    return pl.pallas_call(
        _merge_kernel,
        grid=(b, t // tm),
        in_specs=[spec] * 7 + [_mod_spec(per_batch_mod), _const_spec((4, D_MODEL)), pspec]
        + [wspec] * 3,
        out_specs=spec,
        out_shape=jax.ShapeDtypeStruct((b, t, D_MODEL), F32),
        compiler_params=_cparams(("parallel", "parallel")),
        name=name,
    )(x, hf, hb, zg, ga, gb, on, mod, norm_g, jnp.asarray(perm.T, BF16), lru_up, gla_up, w_out)


def _ffn_kernel(x_ref, mod_ref, ng_ref, w1_ref, w2_ref, o_ref, *, n_split):
    x = x_ref[...]
    h = (_rms(x, ng_ref[2:3, :]) * (1.0 + mod_ref[4:5, :]) + mod_ref[3:4, :]).astype(BF16)
    fs = D_FF // n_split
    acc = jnp.zeros(x.shape, F32)
    for kf in range(n_split):
        hid = jnp.maximum(_bdot(h, w1_ref[:, kf * fs:(kf + 1) * fs]), 0.0)
        acc = acc + _bdot((hid * hid).astype(BF16), w2_ref[kf * fs:(kf + 1) * fs, :])
    o_ref[...] = x + mod_ref[5:6, :] * _rms(acc, ng_ref[3:4, :])


def _ffn(x, mod, norm_g, w1, w2, *, per_batch_mod, tm, name):
    b, t, _ = x.shape
    spec = _tok_spec(D_MODEL, tm, None)
    return pl.pallas_call(
        functools.partial(_ffn_kernel, n_split=4),
        grid=(b, t // tm),
        in_specs=[spec, _mod_spec(per_batch_mod), _const_spec((4, D_MODEL)),
                  _const_spec((D_MODEL, D_FF)), _const_spec((D_FF, D_MODEL))],
        out_specs=spec,
        out_shape=jax.ShapeDtypeStruct((b, t, D_MODEL), F32),
        compiler_params=_cparams(("parallel", "parallel")),
        name=name,
    )(x, mod, norm_g, w1, w2)


def _blockdiag_slabs(w):
    per = LRU_COLB // LRU_BW
    w = w.reshape(2, LRU_NCOLB, per, LRU_BW, LRU_BW)
    eye = jnp.eye(per, dtype=w.dtype)
    slab = jnp.einsum('dspij,pq->dspiqj', w, eye)
    return slab.reshape(2, LRU_NCOLB, LRU_COLB, LRU_COLB).astype(BF16)


def _layer_group(x, mod, p, lru_h0, gla_s0, *, latent, want_state, tag):
    b, t, _ = x.shape
    per_batch = latent
    ga, gb, zg, zx = _inproj(x, mod, p['norm_g'], p['w_rm'], (None,) * 4,
                             per_batch_mod=per_batch, order='interleaved', n_natural=2,
                             name=f"inproj_rm_{tag}")
    q, k, v, g, lr = _inproj(x, mod, p['norm_g'], p['w_cm'], (None, None, None, _silu, None),
                             per_batch_mod=per_batch, order='col_major' if latent else 'natural',
                             name=f"inproj_cm_{tag}")
    period = GRID_W if latent else t
    hf, hb, s_lru = _lru(zx, lru_h0, p['conv_w'], p['conv_b'], p['wa_bd'], p['wx_bd'], p['lru_ba'],
                         p['lru_bx'], p['lru_L'], period=period, name=f"lru_{tag}")
    res = _gla(q, k, v, g, lr, p['w2p'], p['gla_b2'], p['gla_norm_g'], gla_s0,
               want_final=want_state, hps=2, col_major=latent, name=f"gla_{tag}")
    on = res[0]
    s_gla = res[1] if want_state else None
    x1 = _merge(x, hf, hb, zg, ga, gb, on, mod, p['norm_g'], p['lru_up'], p['gla_up'], p['w_out'],
                per_batch_mod=per_batch, name=f"merge_{tag}")
    tm = 256 if t % 256 == 0 else t
    y = _ffn(x1, mod, p['norm_g'], p['mlp_w1'], p['mlp_w2'], per_batch_mod=per_batch, tm=tm,
             name=f"ffn_{tag}")
    return y, s_lru, s_gla


def kernel(x_prompt, x_sample, state_lru, state_gla, c, c_ctx, w_mod, b_mod, norm_g, w_in, conv_w, conv_b,
           lru_wa, lru_ba, lru_wx, lru_bx, lru_L, lru_up, gla_w2, gla_b2, gla_norm_g, gla_up, w_out,
           mlp_w1, mlp_w2):
    depth = w_in.shape[0]
    assert depth == 1
    dec_b = x_sample.shape[0]
    b0 = x_prompt.shape[0]
    xp, xs = x_prompt, x_sample
    lru_states, gla_states = [], []
    mod_rows = -(-(1 + dec_b) // V7X_SUBLANES) * V7X_SUBLANES
    cc = jnp.zeros((mod_rows, D_MODEL), F32).at[0].set(c_ctx).at[1:1 + dec_b].set(c)
    for l in range(depth):
        wl = w_in[l]
        o = np.cumsum((0, D_MODEL, D_MODEL, 512, 512, 1024, 1024, 2 * GLA_RANK, D_MODEL, D_MODEL))
        col = lambda i: wl[:, o[i]:o[i + 1]].astype(BF16)
        lr_w = jnp.pad(col(6), ((0, 0), (0, LR_PAD - 2 * GLA_RANK)))
        w2 = gla_w2[l].astype(BF16)
        w2p = jnp.zeros((2, LR_PAD, GLA_HEADS * GLA_DK), BF16)
        w2p = w2p.at[0, 0:GLA_RANK].set(w2[0]).at[1, GLA_RANK:2 * GLA_RANK].set(w2[1])
        p = {
            'norm_g': norm_g[l],
            'w_rm': [col(7), col(8), col(1), col(0)],
            'w_cm': [col(2), col(3), col(4), col(5), lr_w],
            'conv_w': conv_w[l], 'conv_b': conv_b[l].reshape(1, D_MODEL),
            'wa_bd': _blockdiag_slabs(0.5 * lru_wa[l]), 'wx_bd': _blockdiag_slabs(0.5 * lru_wx[l]),
            'lru_ba': lru_ba[l], 'lru_bx': lru_bx[l], 'lru_L': lru_L[l],
            'lru_up': lru_up[l].astype(BF16),
            'w2p': w2p, 'gla_b2': gla_b2[l], 'gla_norm_g': gla_norm_g[l].reshape(1, GLA_DV),
            'gla_up': gla_up[l].astype(BF16), 'w_out': w_out[l].astype(BF16),
            'mlp_w1': mlp_w1[l].astype(BF16), 'mlp_w2': mlp_w2[l].astype(BF16),
        }
        mod = _modulation(cc, w_mod[l], b_mod[l])
        h0_ctx = jnp.zeros((b0, 2, D_MODEL), F32)
        xp, s_lru, s_gla = _layer_group(xp, mod, p, h0_ctx, None, latent=False, want_state=True,
                                        tag="ctx")
        lru_states.append(s_lru)
        gla_states.append(s_gla)
        xs, _, _ = _layer_group(xs, mod, p, state_lru[:, l], state_gla[:, l], latent=True,
                                want_state=False, tag="lat")
    return (xp, xs, jnp.stack(lru_states, axis=1), jnp.stack(gla_states, axis=1))
```

```python
import functools

import numpy as np
import jax
import jax.numpy as jnp
from jax import lax
from jax.experimental import pallas as pl
from jax.experimental.pallas import tpu as pltpu

F32 = jnp.float32
BF16 = jnp.bfloat16

D_MODEL = 1024
GRID_W = 64
LRU_BLOCKS = 16
LRU_BW = D_MODEL // LRU_BLOCKS
LRU_C = 8.0
GLA_HEADS = 4
GLA_DK = 128
GLA_DV = 256
GLA_RANK = 16
GLA_TAU = 16.0
GLA_CHUNK = 64
D_FF = 4 * D_MODEL
N_MOD = 6
EPS = 1e-6
LOG2_E = 1.4426950408889634

V7X_LANES = 128
V7X_SUBLANES = 8
V7X_MXU_DIM = 256
V7X_VMEM_BYTES = 64 * 1024 * 1024
VMEM_LIMIT = V7X_VMEM_BYTES - 8 * 1024 * 1024

LRU_COLB = V7X_MXU_DIM
LRU_NCOLB = D_MODEL // LRU_COLB
LR_PAD = V7X_LANES
COL_BLOCK = 16
GLA_BLOCK = V7X_MXU_DIM
GLA_SCORE_BLOCK = 2 * GLA_CHUNK
GLA_OUT_COLS = 16
LRU_CHUNK = 256
LRU_SCAN_CUTS = 2
CM_PIECE = 256


def _cparams(sem):
    return pltpu.CompilerParams(dimension_semantics=sem, vmem_limit_bytes=VMEM_LIMIT)


def _rms(x, g):
    ms = jnp.mean(x * x, axis=-1, keepdims=True)
    return x * lax.rsqrt(ms + EPS) * g


def _sigmoid(x):
    return 0.5 * jnp.tanh(0.5 * x) + 0.5


def _softplus(x):
    return jnp.maximum(x, 0.0) + jnp.log1p(jnp.exp(-jnp.abs(x)))


def _bdot(a, b):
    return jnp.dot(a, b, preferred_element_type=F32)


def _mod_kernel(c_ref, w_ref, b_ref, o_ref):
    c = c_ref[...]
    s = (c * _sigmoid(c)).astype(BF16)
    o_ref[...] = _bdot(s, w_ref[...].astype(BF16)) + b_ref[...]


def _modulation(cc, w_mod, b_mod):
    rows = cc.shape[0]
    return pl.pallas_call(
        _mod_kernel,
        grid=(N_MOD,),
        in_specs=[
            pl.BlockSpec((rows, D_MODEL), lambda n: (0, 0)),
            pl.BlockSpec((D_MODEL, D_MODEL), lambda n: (0, n)),
            pl.BlockSpec((1, D_MODEL), lambda n: (0, n)),
        ],
        out_specs=pl.BlockSpec((rows, D_MODEL), lambda n: (0, n)),
        out_shape=jax.ShapeDtypeStruct((rows, N_MOD * D_MODEL), F32),
        compiler_params=_cparams(("arbitrary",)),
        name="modulation",
    )(cc, w_mod, b_mod.reshape(1, N_MOD * D_MODEL)).reshape(rows, N_MOD, D_MODEL)


def _tok_view(a, grid_tile):
    if grid_tile is None:
        return a
    b, t, n = a.shape
    rows = t // GRID_W
    return a.reshape(b, rows, GRID_W // COL_BLOCK, COL_BLOCK, n)


def _tok_spec(n, tm, grid_tile):
    if grid_tile is None:
        return pl.BlockSpec((None, tm, n), lambda b, j: (b, j, 0))
    rows = grid_tile
    return pl.BlockSpec((None, rows, None, COL_BLOCK, n), lambda b, j: (b, 0, j, 0, 0))


def _const_spec(shape):
    nd = len(shape)
    return pl.BlockSpec(shape, lambda b, j: (0,) * nd)


def _mod_spec(per_batch):
    if per_batch:
        return pl.BlockSpec((None, N_MOD, D_MODEL), lambda b, j: (b + 1, 0, 0))
    return pl.BlockSpec((None, N_MOD, D_MODEL), lambda b, j: (0, 0, 0))


def _perm_matrix(rows, cols):
    n = rows * cols
    p = np.zeros((n, n), np.float32)
    for r in range(rows):
        for c in range(cols):
            p[c * rows + r, r * cols + c] = 1.0
    return p


def _silu(x):
    return x * _sigmoid(x)


def _inproj_pieces(x_ref, mod_ref, ng_ref, perm_ref, w_refs, o_refs, *, n_natural, acts, n_piece):
    x = x_ref[...].reshape(-1, D_MODEL)
    h = _rms(x, ng_ref[0:1, :]) * (1.0 + mod_ref[1:2, :]) + mod_ref[0:1, :]
    hb = h.astype(BF16)
    hp = hb
    if perm_ref is not None:
        hp = _bdot(perm_ref[...], hb).astype(BF16)
    pieces = []
    for i, (w_ref, o_ref) in enumerate(zip(w_refs, o_refs)):
        n = w_ref.shape[1]
        width = min(n, n_piece)
        for c0 in range(0, n, width):
            def piece(i=i, w_ref=w_ref, o_ref=o_ref, cs=slice(c0, c0 + width)):
                z = _bdot(hb if i < n_natural else hp, w_ref[:, cs])
                if acts[i] is not None:
                    z = acts[i](z)
                o_ref[:, cs] = z.astype(o_ref.dtype)
            pieces.append(piece)
    return pieces


def _inproj_kernel(*refs, n_out, permute, n_natural, acts):
    x_ref, mod_ref, ng_ref = refs[:3]
    k = 4 if permute else 3
    perm_ref = refs[3] if permute else None
    for piece in _inproj_pieces(x_ref, mod_ref, ng_ref, perm_ref, refs[k:k + n_out], refs[k + n_out:],
                                n_natural=n_natural, acts=acts, n_piece=D_MODEL):
        piece()


def _inproj_setup(x, weights, order):
    b, t, _ = x.shape
    if order == 'col_major':
        rows = t // GRID_W
        tm = rows * COL_BLOCK
        return _tok_view(x, rows), _tok_spec(D_MODEL, tm, rows), tm, _perm_matrix(rows, COL_BLOCK)
    tm = _lru_chunk(t)
    perm = _interleave_matrix(tm) if order == 'interleaved' else None
    return x, _tok_spec(D_MODEL, tm, None), tm, perm


def _inproj(x, mod, norm_g, weights, acts, *, per_batch_mod, order, name, n_natural=0):
    b, t, _ = x.shape
    x_in, x_spec, tm, perm = _inproj_setup(x, weights, order)
    extra = [] if perm is None else [jnp.asarray(perm, BF16)]
    return pl.pallas_call(
        functools.partial(_inproj_kernel, n_out=len(weights), permute=perm is not None,
                          n_natural=n_natural, acts=tuple(acts)),
        grid=(b, t // tm),
        in_specs=[x_spec, _mod_spec(per_batch_mod), _const_spec((4, D_MODEL))]
        + [_const_spec(e.shape) for e in extra] + [_const_spec(w.shape) for w in weights],
        out_specs=[_tok_spec(w.shape[1], tm, None) for w in weights],
        out_shape=[jax.ShapeDtypeStruct((b, t, w.shape[1]), BF16) for w in weights],
        compiler_params=_cparams(("parallel", "parallel")),
        name=name,
    )(x_in, mod, norm_g, *extra, *weights)


def _lru_chunk(t):
    return min(LRU_CHUNK, t)


def _interleave_matrix(tc):
    return _perm_matrix(V7X_SUBLANES, tc // V7X_SUBLANES)


def _lru_pieces(zf_ref, zb_ref, rf0, rb0, cw_ref, cb_ref, wa_ref, wx_ref, ba_ref, bx_ref, l_ref,
                hf_ref, hb_ref, fin_ref, a_s, u_s, carry_s, *, tc, period):
    ns = V7X_SUBLANES
    sl = tc // ns
    sub_c = lax.broadcasted_iota(jnp.int32, (ns, LRU_COLB), 0)
    prev_ok = jnp.bitwise_and(sub_c * sl, period - 1) != 0
    next_ok = jnp.bitwise_and((sub_c + 1) * sl, period - 1) != 0

    def gates(d, z_ref, r0, cb):
        cs = slice(cb * LRU_COLB, (cb + 1) * LRU_COLB)
        x = z_ref[r0:r0 + tc, cs].astype(F32)
        e_prev = jnp.where(prev_ok, pltpu.roll(x[tc - ns:], 1, 0), 0.0)
        e_next0 = jnp.where(next_ok, pltpu.roll(x[:ns], ns - 1, 0), 0.0)
        e_next1 = jnp.where(next_ok, pltpu.roll(x[ns:2 * ns], ns - 1, 0), 0.0)
        xm1 = jnp.concatenate([e_prev, x[:tc - ns]], axis=0)
        xp1 = jnp.concatenate([x[ns:], e_next0], axis=0)
        xp2 = jnp.concatenate([x[2 * ns:], e_next0, e_next1], axis=0)
        xc = (cb_ref[0:1, cs] + xm1 * cw_ref[0:1, cs] + x * cw_ref[1:2, cs]
              + xp1 * cw_ref[2:3, cs] + xp2 * cw_ref[3:4, cs])
        xcb = xc.astype(BF16)
        tr = jnp.tanh(_bdot(xcb, wa_ref[d, cb]) + 0.5 * ba_ref[d:d + 1, cs])
        ti = jnp.tanh(_bdot(xcb, wx_ref[d, cb]) + 0.5 * bx_ref[d:d + 1, cs])
        k2 = (-0.5 * LRU_C * LOG2_E) * _softplus(-l_ref[d:d + 1, cs])
        a = jnp.exp2(tr * k2 + k2)
        y = 1.0 - a * a
        u = jnp.where(y > 0.0, y * lax.rsqrt(y), 0.0) * ((0.5 * ti + 0.5) * xc)
        a_s[d, :, cs] = a
        u_s[d, :, cs] = u

    state = {}
    n_cut = LRU_SCAN_CUTS

    def pass1(part):
        if part == 0:
            zeros = jnp.zeros((ns, D_MODEL), F32)
            ones = jnp.ones((ns, D_MODEL), F32)
            state['c'] = (zeros, ones, zeros, ones)
        hf, af, hb, ab = state['c']
        for i in range(part * sl // n_cut, (part + 1) * sl // n_cut):
            rf = slice(i * ns, (i + 1) * ns)
            rb = slice((sl - 1 - i) * ns, (sl - i) * ns)
            a = a_s[0, rf, :]
            hf = a * hf + u_s[0, rf, :]
            af = a * af
            u_s[0, rf, :] = hf
            a_s[0, rf, :] = af
            a = a_s[1, rb, :]
            hb = a * hb + u_s[1, rb, :]
            ab = a * ab
            u_s[1, rb, :] = hb
            a_s[1, rb, :] = ab
        state['c'] = (hf, af, hb, ab)

    def chain():
        hf, af, hb, ab = state['c']
        sub = lax.broadcasted_iota(jnp.int32, (ns, D_MODEL), 0)

        def sublane_scan(a, u, forward):
            for s in (1, 2, 4):
                if forward:
                    m = sub >= s
                    shift = s
                else:
                    m = sub < ns - s
                    shift = ns - s
                a_sh = jnp.where(m, pltpu.roll(a, shift, 0), 1.0)
                u_sh = jnp.where(m, pltpu.roll(u, shift, 0), 0.0)
                u = a * u_sh + u
                a = a * a_sh
            return a, u

        cin_f = carry_s[0:1, :]
        cin_b = carry_s[1:2, :]
        af, hf = sublane_scan(af, hf, True)
        ab, hb = sublane_scan(ab, hb, False)
        fin_f = hf + af * cin_f
        fin_b = hb + ab * cin_b
        enter_f = jnp.where(sub == 0, cin_f, pltpu.roll(fin_f, 1, 0))
        enter_b = jnp.where(sub == ns - 1, cin_b, pltpu.roll(fin_b, ns - 1, 0))
        out_f = fin_f[ns - 1:ns, :]
        out_b = fin_b[0:1, :]
        carry_s[0:1, :] = out_f
        carry_s[1:2, :] = out_b
        fin_ref[0:1, :] = out_f
        fin_ref[1:2, :] = out_b
        state['e'] = (jnp.concatenate([enter_f, enter_f], axis=0), jnp.concatenate([enter_b, enter_b], axis=0))

    def pass2(part):
        pk = 2 * ns
        enter_f2, enter_b2 = state['e']
        for i in range(part * (sl // 2) // n_cut, (part + 1) * (sl // 2) // n_cut):
            rr = slice(i * pk, (i + 1) * pk)
            hf_ref[rf0 + i * pk:rf0 + (i + 1) * pk, :] = (
                u_s[0, rr, :] + a_s[0, rr, :] * enter_f2).astype(hf_ref.dtype)
            hb_ref[rb0 + i * pk:rb0 + (i + 1) * pk, :] = (
                u_s[1, rr, :] + a_s[1, rr, :] * enter_b2).astype(hb_ref.dtype)

    scan = ([functools.partial(pass1, part) for part in range(n_cut)] + [chain]
            + [functools.partial(pass2, part) for part in range(n_cut)])

    pieces = [functools.partial(gates, d, z_ref, r0, cb)
              for d, z_ref, r0 in ((0, zf_ref, rf0), (1, zb_ref, rb0)) for cb in range(LRU_NCOLB)]
    return pieces + scan


def _interleave(major, minor):
    out, done = [], 0
    for i, p in enumerate(major):
        want = -(-(i + 1) * len(minor) // len(major))
        out.extend(minor[done:want])
        done = want
        out.append(p)
    return out


def _cm_lru_kernel(*refs, n_cm, permute, acts, tc, period, cps):
    x_ref, mod_ref, ng_ref = refs[:3]
    k = 4 if permute else 3
    perm_ref = refs[3] if permute else None
    w_refs = refs[k:k + n_cm]
    k += n_cm
    zf_ref, zb_ref, h0_ref, cw_ref, cb_ref, wa_ref, wx_ref, ba_ref, bx_ref, l_ref = refs[k:k + 10]
    k += 10
    o_refs = refs[k:k + n_cm]
    k += n_cm
    hf_ref, hb_ref, fin_ref, a_s, u_s, carry_s = refs[k:]

    @pl.when(pl.program_id(1) == 0)
    def _():
        carry_s[...] = h0_ref[...]

    cm = _inproj_pieces(x_ref, mod_ref, ng_ref, perm_ref, w_refs, o_refs, n_natural=0, acts=acts,
                        n_piece=CM_PIECE)
    lru = []
    for s in range(cps):
        lru += _lru_pieces(zf_ref, zb_ref, s * tc, (cps - 1 - s) * tc, cw_ref, cb_ref, wa_ref, wx_ref,
                           ba_ref, bx_ref, l_ref, hf_ref, hb_ref, fin_ref, a_s, u_s, carry_s,
                           tc=tc, period=period)
    for piece in _interleave(lru, cm):
        piece()


def _cm_lru(x, mod, norm_g, weights, acts, zx, h0, conv_w, conv_b, wa_bd, wx_bd, ba, bx, lru_l, *,
            per_batch_mod, order, period, name):
    b, t, _ = x.shape
    x_in, x_spec, tm, perm = _inproj_setup(x, weights, order)
    extra = [] if perm is None else [jnp.asarray(perm, BF16)]
    n_steps = t // tm
    tc = _lru_chunk(t)
    cps = t // tc // n_steps
    assert cps >= 1 and cps * tc * n_steps == t
    assert period % (tc // V7X_SUBLANES) == 0 and tc % period == 0
    blk = cps * tc
    res = pl.pallas_call(
        functools.partial(_cm_lru_kernel, n_cm=len(weights), permute=perm is not None, acts=tuple(acts),
                          tc=tc, period=period, cps=cps),
        grid=(b, n_steps),
        in_specs=[x_spec, _mod_spec(per_batch_mod), _const_spec((4, D_MODEL))]
        + [_const_spec(e.shape) for e in extra] + [_const_spec(w.shape) for w in weights]
        + [
            pl.BlockSpec((None, blk, D_MODEL), lambda bb, j: (bb, j, 0)),
            pl.BlockSpec((None, blk, D_MODEL), lambda bb, j: (bb, n_steps - 1 - j, 0)),
            pl.BlockSpec((None, 2, D_MODEL), lambda bb, j: (bb, 0, 0)),
            _const_spec((4, D_MODEL)), _const_spec((1, D_MODEL)),
            _const_spec((2, LRU_NCOLB, LRU_COLB, LRU_COLB)), _const_spec((2, LRU_NCOLB, LRU_COLB, LRU_COLB)),
            _const_spec((2, D_MODEL)), _const_spec((2, D_MODEL)), _const_spec((2, D_MODEL)),
        ],
        out_specs=[_tok_spec(w.shape[1], tm, None) for w in weights] + [
            pl.BlockSpec((None, blk, D_MODEL), lambda bb, j: (bb, j, 0)),
            pl.BlockSpec((None, blk, D_MODEL), lambda bb, j: (bb, n_steps - 1 - j, 0)),
            pl.BlockSpec((None, 2, D_MODEL), lambda bb, j: (bb, 0, 0)),
        ],
        out_shape=[jax.ShapeDtypeStruct((b, t, w.shape[1]), BF16) for w in weights] + [
            jax.ShapeDtypeStruct((b, t, D_MODEL), BF16),
            jax.ShapeDtypeStruct((b, t, D_MODEL), BF16),
            jax.ShapeDtypeStruct((b, 2, D_MODEL), F32),
        ],
        scratch_shapes=[
            pltpu.VMEM((2, tc, D_MODEL), F32),
            pltpu.VMEM((2, tc, D_MODEL), F32),
            pltpu.VMEM((2, D_MODEL), F32),
        ],
        compiler_params=_cparams(("parallel", "arbitrary")),
        name=name,
    )(x_in, mod, norm_g, *extra, *weights, zx, zx, h0, conv_w, conv_b, wa_bd, wx_bd, ba, bx, lru_l)
    return res[:len(weights)], res[len(weights)], res[len(weights) + 1], res[len(weights) + 2]


def _gla_kernel(*refs, t, hps, has_s0, want_final, grid_rows):
    q_ref, k_ref, v_ref, g_ref, lr_ref, w2_ref, b2_ref, gn_ref = refs[:8]
    n = 8
    if grid_rows is not None:
        perm_ref = refs[n]
        n += 1
    if has_s0:
        s0_ref = refs[n]
        n += 1
    on_ref = refs[n]
    n += 1
    if want_final:
        fin_ref = refs[n]
        n += 1
    oacc, st, q2_s, k2_s, bt_s, pp_q, pp_k, pp_qx, pp_kx = refs[n:]

    c = GLA_CHUNK
    n_chunks = t // c
    blk = min(GLA_BLOCK, t)
    for d in range(2):
        for hh in range(hps):
            if has_s0:
                st[d, hh] = s0_ref[d, hh].T
            else:
                st[d, hh] = jnp.zeros((GLA_DV, GLA_DK), F32)

    ti = lax.broadcasted_iota(jnp.int32, (blk, blk), 0)
    si = lax.broadcasted_iota(jnp.int32, (blk, blk), 1)
    same = (ti // c) == (si // c)
    cums = ((same & (si <= ti)).astype(BF16), (same & (si >= ti)).astype(BF16))
    sb = min(GLA_SCORE_BLOCK, blk)
    tj = lax.broadcasted_iota(jnp.int32, (sb, sb), 0)
    sj = lax.broadcasted_iota(jnp.int32, (sb, sb), 1)
    same_j = (tj // c) == (sj // c)
    masks = (same_j & (sj <= tj), same_j & (sj >= tj))
    scale = GLA_DK ** -0.5

    def step(ia, slot_a, ib, slot_b):
        nd = hps * GLA_DK
        if ia is not None:
            rows_a = pl.ds(pl.multiple_of(ia * blk, blk), blk)
            lrb = lr_ref[rows_a, :]
            pre = [_bdot(lrb, w2_ref[d]) + b2_ref[d:d + 1, :] for d in range(2)]
        if ib is not None:
            rows_b = [pl.ds(pl.multiple_of(ib * blk + g * sb, sb), sb) for g in range(blk // sb)]
            loc_b = [slice(g * sb, (g + 1) * sb) for g in range(blk // sb)]
            nt = (((1,), (1,)), ((), ()))
            raw = [[[lax.dot_general(pp_q[slot_b, d, loc, hh * GLA_DK:(hh + 1) * GLA_DK],
                                     pp_k[slot_b, d, loc, hh * GLA_DK:(hh + 1) * GLA_DK], nt,
                                     preferred_element_type=F32) for d in range(2)]
                    + [lax.dot_general(pp_qx[slot_b, loc, hh * 2 * GLA_DK:(hh + 1) * 2 * GLA_DK],
                                       pp_kx[slot_b, loc, hh * 2 * GLA_DK:(hh + 1) * 2 * GLA_DK], nt,
                                       preferred_element_type=F32)]
                    for hh in range(hps)] for loc in loc_b]
        if ia is not None:
            bcum = []
            for d in range(2):
                soft = jnp.log(1.0 + jnp.exp2(jnp.abs(pre[d]) * (-LOG2_E)))
                la = jnp.minimum(pre[d], 0.0) * (LOG2_E / GLA_TAU) - soft * (LOG2_E / GLA_TAU)
                hi = la.astype(BF16)
                mid = (la - hi.astype(F32)).astype(BF16)
                bcum.append(_bdot(cums[d], hi) + _bdot(cums[d], mid))
        if ib is not None:
            for g, rows in enumerate(rows_b):
                for hh in range(hps):
                    vs = slice(hh * GLA_DV, (hh + 1) * GLA_DV)
                    sd0, sd1, sx = raw[g][hh]
                    sc = jnp.where(masks[0], sd0, 0.0) + jnp.where(masks[1], sd1, 0.0) + sx
                    oacc[rows, vs] = _bdot(sc.astype(BF16), v_ref[rows, vs])
        if ia is not None:
            q = q_ref[rows_a, :].astype(F32)
            k = k_ref[rows_a, :].astype(F32)
            nc = blk // c
            crow = lambda a, n: a[n * c:(n + 1) * c]
            cat = lambda parts: jnp.concatenate(parts, axis=0)
            zero = jnp.zeros((c, nd), BF16)
            qx, kx = [], []
            for d in range(2):
                edge = c - 1 if d == 0 else 0
                tot = [bcum[d][n * c + edge:n * c + edge + 1, :] for n in range(nc)]
                dec = [jnp.exp2(x) for x in tot]
                btot = cat([jnp.broadcast_to(x, (c, nd)) for x in tot])
                qe = (q * scale) * jnp.exp2(bcum[d])
                kt = k * jnp.exp2(btot - bcum[d])
                pp_q[slot_a, d] = qe.astype(BF16)
                pp_k[slot_a, d] = (k * jnp.exp2(-bcum[d])).astype(BF16)
                first = [(n % 2 == 0) == (d == 0) for n in range(nc)]
                q2 = cat([crow(qe, n) if first[n] else crow(qe, n) * dec[n ^ 1] for n in range(nc)])
                k2 = cat([crow(kt, n) * dec[n ^ 1] if first[n] else crow(kt, n) for n in range(nc)])
                q2_s[d, rows_a, :] = q2.astype(BF16)
                k2_s[d, rows_a, :] = k2.astype(BF16)
                bt_s[d, rows_a, :] = cat([jnp.broadcast_to(tot[n - n % 2] + tot[n - n % 2 + 1], (c, nd))
                                          for n in range(nc)])
                qeb, ktb = qe.astype(BF16), kt.astype(BF16)
                qx.append(cat([zero if first[n] else crow(qeb, n) for n in range(nc)]))
                kx.append(cat([crow(ktb, n) if first[n] else zero for n in range(nc)]))
            for hh in range(hps):
                ks = slice(hh * GLA_DK, (hh + 1) * GLA_DK)
                xs = slice(hh * 2 * GLA_DK, (hh + 1) * 2 * GLA_DK)
                pp_qx[slot_a, :, xs] = jnp.concatenate([qx[0][:, ks], qx[1][:, ks]], axis=1)
                pp_kx[slot_a, :, xs] = jnp.concatenate([kx[0][:, ks], kx[1][:, ks]], axis=1)

    def pair(j, carry):
        step(2 * j + 1, 1, 2 * j, 0)
        step(2 * j + 2, 0, 2 * j + 1, 1)
        return carry

    n_blocks = t // blk
    n_pairs = (n_blocks - 1) // 2
    step(0, 0, None, None)
    lax.fori_loop(0, n_pairs, pair, 0)
    done = 2 * n_pairs
    if done < n_blocks - 1:
        step(done + 1, 1, done, 0)
        done += 1
    step(None, None, done, done % 2)

    pr = 2 * c
    n_prs = t // pr

    def pair_step(p0, d, hh):
        rows = pl.ds(p0, pr)
        ks = slice(hh * GLA_DK, (hh + 1) * GLA_DK)
        vs = slice(hh * GLA_DV, (hh + 1) * GLA_DV)
        s_t = st[d, hh]
        o = lax.dot_general(q2_s[d, rows, ks], s_t.astype(BF16), (((1,), (1,)), ((), ())),
                            preferred_element_type=F32)
        upd = lax.dot_general(v_ref[rows, vs], k2_s[d, rows, ks], (((0,), (0,)), ((), ())),
                              preferred_element_type=F32)
        st[d, hh] = s_t * jnp.exp2(bt_s[d, pl.ds(p0, 1), ks]) + upd
        return o

    def body(jj, carry):
        pf = pl.multiple_of(jj * pr, pr)
        pb = pl.multiple_of((n_prs - 1 - jj) * pr, pr)
        for hh in range(hps):
            vs = slice(hh * GLA_DV, (hh + 1) * GLA_DV)
            for p0, d in ((pf, 0), (pb, 1)):
                oacc[pl.ds(p0, pr), vs] += pair_step(p0, d, hh)
        return carry

    lax.fori_loop(0, n_prs, body, 0, unroll=4)

    if grid_rows is None:
        ob = min(GLA_BLOCK, t)
    else:
        ob = GLA_OUT_COLS * grid_rows

    def out_block(ib, carry):
        r0 = pl.multiple_of(ib * ob, ob)
        rows = pl.ds(r0, ob)
        parts = []
        for hh in range(hps):
            vs = slice(hh * GLA_DV, (hh + 1) * GLA_DV)
            parts.append((_rms(oacc[rows, vs], gn_ref[...]) * g_ref[rows, vs].astype(F32)).astype(BF16))
        on = jnp.concatenate(parts, axis=1)
        if grid_rows is None:
            on_ref[rows, :] = on
        else:
            on = _bdot(perm_ref[...], on).astype(BF16)
            for r in range(grid_rows):
                dst = pl.multiple_of(r * GRID_W + ib * GLA_OUT_COLS, GLA_OUT_COLS)
                on_ref[pl.ds(dst, GLA_OUT_COLS), :] = on[r * GLA_OUT_COLS:(r + 1) * GLA_OUT_COLS]
        return carry

    lax.fori_loop(0, t // ob, out_block, 0)

    if want_final:
        for d in range(2):
            for hh in range(hps):
                fin_ref[d, hh] = st[d, hh].T


def _gla(q, k, v, g, lr, w2p, b2, gn, s0, *, want_final, hps, col_major, name):
    b, t, _ = q.shape
    ng = GLA_HEADS // hps
    has_s0 = s0 is not None
    grid_rows = t // GRID_W if col_major else None
    in_specs = [
        pl.BlockSpec((None, t, hps * GLA_DK), lambda bb, h: (bb, 0, h)),
        pl.BlockSpec((None, t, hps * GLA_DK), lambda bb, h: (bb, 0, h)),
        pl.BlockSpec((None, t, hps * GLA_DV), lambda bb, h: (bb, 0, h)),
        pl.BlockSpec((None, t, hps * GLA_DV), lambda bb, h: (bb, 0, h)),
        pl.BlockSpec((None, t, LR_PAD), lambda bb, h: (bb, 0, 0)),
        pl.BlockSpec((2, LR_PAD, hps * GLA_DK), lambda bb, h: (0, 0, h)),
        pl.BlockSpec((2, hps * GLA_DK), lambda bb, h: (0, h)),
        pl.BlockSpec((1, GLA_DV), lambda bb, h: (0, 0)),
    ]
    args = [q, k, v, g, lr, w2p, b2, gn]
    if col_major:
        ob = GLA_OUT_COLS * grid_rows
        in_specs.append(pl.BlockSpec((ob, ob), lambda bb, h: (0, 0)))
        args.append(jnp.asarray(_perm_matrix(grid_rows, GLA_OUT_COLS).T, BF16))
    state_spec = pl.BlockSpec((None, 2, hps, GLA_DK, GLA_DV), lambda bb, h: (bb, 0, h, 0, 0))
    if has_s0:
        in_specs.append(state_spec)
        args.append(s0)
    out_specs = [pl.BlockSpec((None, t, hps * GLA_DV), lambda bb, h: (bb, 0, h))]
    out_shape = [jax.ShapeDtypeStruct((b, t, GLA_HEADS * GLA_DV), BF16)]
    if want_final:
        out_specs.append(state_spec)
        out_shape.append(jax.ShapeDtypeStruct((b, 2, GLA_HEADS, GLA_DK, GLA_DV), F32))
    return pl.pallas_call(
        functools.partial(_gla_kernel, t=t, hps=hps, has_s0=has_s0, want_final=want_final,
                          grid_rows=grid_rows),
        grid=(b, ng),
        in_specs=in_specs,
        out_specs=out_specs,
        out_shape=out_shape,
        scratch_shapes=[
            pltpu.VMEM((t, hps * GLA_DV), F32),
            pltpu.VMEM((2, hps, GLA_DV, GLA_DK), F32),
            pltpu.VMEM((2, t, hps * GLA_DK), BF16),
            pltpu.VMEM((2, t, hps * GLA_DK), BF16),
            pltpu.VMEM((2, t, hps * GLA_DK), F32),
            pltpu.VMEM((2, 2, min(GLA_BLOCK, t), hps * GLA_DK), BF16),
            pltpu.VMEM((2, 2, min(GLA_BLOCK, t), hps * GLA_DK), BF16),
            pltpu.VMEM((2, min(GLA_BLOCK, t), 2 * hps * GLA_DK), BF16),
            pltpu.VMEM((2, min(GLA_BLOCK, t), 2 * hps * GLA_DK), BF16),
        ],
        compiler_params=_cparams(("parallel", "parallel")),
        name=name,
    )(*args)


def _gelu_tanh(x):
    return x * (0.5 * (1.0 + jnp.tanh(0.7978845608028654 * (x + 0.044715 * (x * x * x)))))


def _merge_kernel(x_ref, hf_ref, hb_ref, zg_ref, ga_ref, gb_ref, on_ref, mod_ref, ng_ref,
                  unperm_ref, lup_ref, gup_ref, wo_ref, o_ref):
    ld = lambda r: r[...].astype(F32)
    ya_in = ((ld(hf_ref) + ld(hb_ref)) * _gelu_tanh(ld(zg_ref))).astype(BF16)
    ya_in = _bdot(unperm_ref[...], ya_in).astype(BF16)
    y_a = _bdot(ya_in, lup_ref[...])
    y_b = _bdot(on_ref[...], gup_ref[...])
    mm = _sigmoid(ld(ga_ref)) * y_a + _sigmoid(ld(gb_ref)) * y_b
    m = _bdot(mm.astype(BF16), wo_ref[...])
    o_ref[...] = x_ref[...] + mod_ref[2:3, :] * _rms(m, ng_ref[1:2, :])


def _merge(x, hf, hb, zg, ga, gb, on, mod, norm_g, lru_up, gla_up, w_out, *, per_batch_mod, name):
    b, t, _ = x.shape
    tm = _lru_chunk(t)
    perm = _interleave_matrix(tm)
    spec = _tok_spec(D_MODEL, tm, None)
    wspec = _const_spec((D_MODEL, D_MODEL))
    pspec = _const_spec((tm, tm))
    return pl.pallas_call(
        _merge_kernel,
        grid=(b, t // tm),
        in_specs=[spec] * 7 + [_mod_spec(per_batch_mod), _const_spec((4, D_MODEL)), pspec]
        + [wspec] * 3,
        out_specs=spec,
        out_shape=jax.ShapeDtypeStruct((b, t, D_MODEL), F32),
        compiler_params=_cparams(("parallel", "parallel")),
        name=name,
    )(x, hf, hb, zg, ga, gb, on, mod, norm_g, jnp.asarray(perm.T, BF16), lru_up, gla_up, w_out)


def _ffn_kernel(x_ref, mod_ref, ng_ref, w1_ref, w2_ref, o_ref, *, n_split):
    x = x_ref[...]
    h = (_rms(x, ng_ref[2:3, :]) * (1.0 + mod_ref[4:5, :]) + mod_ref[3:4, :]).astype(BF16)
    fs = D_FF // n_split
    acc = jnp.zeros(x.shape, F32)
    for kf in range(n_split):
        hid = jnp.maximum(_bdot(h, w1_ref[:, kf * fs:(kf + 1) * fs]), 0.0)
        acc = acc + _bdot((hid * hid).astype(BF16), w2_ref[kf * fs:(kf + 1) * fs, :])
    o_ref[...] = x + mod_ref[5:6, :] * _rms(acc, ng_ref[3:4, :])


def _ffn(x, mod, norm_g, w1, w2, *, per_batch_mod, tm, name):
    b, t, _ = x.shape
    spec = _tok_spec(D_MODEL, tm, None)
    return pl.pallas_call(
        functools.partial(_ffn_kernel, n_split=4),
        grid=(b, t // tm),
        in_specs=[spec, _mod_spec(per_batch_mod), _const_spec((4, D_MODEL)),
                  _const_spec((D_MODEL, D_FF)), _const_spec((D_FF, D_MODEL))],
        out_specs=spec,
        out_shape=jax.ShapeDtypeStruct((b, t, D_MODEL), F32),
        compiler_params=_cparams(("parallel", "parallel")),
        name=name,
    )(x, mod, norm_g, w1, w2)


def _blockdiag_slabs(w):
    per = LRU_COLB // LRU_BW
    w = w.reshape(2, LRU_NCOLB, per, LRU_BW, LRU_BW)
    eye = jnp.eye(per, dtype=w.dtype)
    slab = jnp.einsum('dspij,pq->dspiqj', w, eye)
    return slab.reshape(2, LRU_NCOLB, LRU_COLB, LRU_COLB).astype(BF16)


def _layer_group(x, mod, p, lru_h0, gla_s0, *, latent, want_state, tag):
    b, t, _ = x.shape
    per_batch = latent
    ga, gb, zg, zx = _inproj(x, mod, p['norm_g'], p['w_rm'], (None,) * 4,
                             per_batch_mod=per_batch, order='interleaved', n_natural=2,
                             name=f"inproj_rm_{tag}")
    period = GRID_W if latent else t
    (q, k, v, g, lr), hf, hb, s_lru = _cm_lru(
        x, mod, p['norm_g'], p['w_cm'], (None, None, None, _silu, None), zx, lru_h0, p['conv_w'],
        p['conv_b'], p['wa_bd'], p['wx_bd'], p['lru_ba'], p['lru_bx'], p['lru_L'], per_batch_mod=per_batch,
        order='col_major' if latent else 'natural', period=period, name=f"cm_lru_{tag}")
    res = _gla(q, k, v, g, lr, p['w2p'], p['gla_b2'], p['gla_norm_g'], gla_s0,
               want_final=want_state, hps=2, col_major=latent, name=f"gla_{tag}")
    on = res[0]
    s_gla = res[1] if want_state else None
    x1 = _merge(x, hf, hb, zg, ga, gb, on, mod, p['norm_g'], p['lru_up'], p['gla_up'], p['w_out'],
                per_batch_mod=per_batch, name=f"merge_{tag}")
    tm = 256 if t % 256 == 0 else t
    y = _ffn(x1, mod, p['norm_g'], p['mlp_w1'], p['mlp_w2'], per_batch_mod=per_batch, tm=tm,
             name=f"ffn_{tag}")
    return y, s_lru, s_gla


def kernel(x_prompt, x_sample, state_lru, state_gla, c, c_ctx, w_mod, b_mod, norm_g, w_in, conv_w, conv_b,
           lru_wa, lru_ba, lru_wx, lru_bx, lru_L, lru_up, gla_w2, gla_b2, gla_norm_g, gla_up, w_out,
           mlp_w1, mlp_w2):
    depth = w_in.shape[0]
    assert depth == 1
    dec_b = x_sample.shape[0]
    b0 = x_prompt.shape[0]
    xp, xs = x_prompt, x_sample
    lru_states, gla_states = [], []
    mod_rows = -(-(1 + dec_b) // V7X_SUBLANES) * V7X_SUBLANES
    cc = jnp.zeros((mod_rows, D_MODEL), F32).at[0].set(c_ctx).at[1:1 + dec_b].set(c)
    for l in range(depth):
        wl = w_in[l]
        o = np.cumsum((0, D_MODEL, D_MODEL, 512, 512, 1024, 1024, 2 * GLA_RANK, D_MODEL, D_MODEL))
        col = lambda i: wl[:, o[i]:o[i + 1]].astype(BF16)
        lr_w = jnp.pad(col(6), ((0, 0), (0, LR_PAD - 2 * GLA_RANK)))
        w2 = gla_w2[l].astype(BF16)
        w2p = jnp.zeros((2, LR_PAD, GLA_HEADS * GLA_DK), BF16)
        w2p = w2p.at[0, 0:GLA_RANK].set(w2[0]).at[1, GLA_RANK:2 * GLA_RANK].set(w2[1])
        p = {
            'norm_g': norm_g[l],
            'w_rm': [col(7), col(8), col(1), col(0)],
            'w_cm': [col(2), col(3), col(4), col(5), lr_w],
            'conv_w': conv_w[l], 'conv_b': conv_b[l].reshape(1, D_MODEL),
            'wa_bd': _blockdiag_slabs(0.5 * lru_wa[l]), 'wx_bd': _blockdiag_slabs(0.5 * lru_wx[l]),
            'lru_ba': lru_ba[l], 'lru_bx': lru_bx[l], 'lru_L': lru_L[l],
            'lru_up': lru_up[l].astype(BF16),
            'w2p': w2p, 'gla_b2': gla_b2[l], 'gla_norm_g': gla_norm_g[l].reshape(1, GLA_DV),
            'gla_up': gla_up[l].astype(BF16), 'w_out': w_out[l].astype(BF16),
            'mlp_w1': mlp_w1[l].astype(BF16), 'mlp_w2': mlp_w2[l].astype(BF16),
        }
        mod = _modulation(cc, w_mod[l], b_mod[l])
        h0_ctx = jnp.zeros((b0, 2, D_MODEL), F32)
        xp, s_lru, s_gla = _layer_group(xp, mod, p, h0_ctx, None, latent=False, want_state=True,
                                        tag="ctx")
        lru_states.append(s_lru)
        gla_states.append(s_gla)
        xs, _, _ = _layer_group(xs, mod, p, state_lru[:, l], state_gla[:, l], latent=True,
                                want_state=False, tag="lat")
    return (xp, xs, jnp.stack(lru_states, axis=1), jnp.stack(gla_states, axis=1))
```

```python
import functools

import numpy as np
import jax
import jax.numpy as jnp
from jax import lax
from jax.experimental import pallas as pl
from jax.experimental.pallas import tpu as pltpu

F32 = jnp.float32
BF16 = jnp.bfloat16

D_MODEL = 1024
GRID_W = 64
LRU_BLOCKS = 16
LRU_BW = D_MODEL // LRU_BLOCKS
LRU_C = 8.0
GLA_HEADS = 4
GLA_DK = 128
GLA_DV = 256
GLA_RANK = 16
GLA_TAU = 16.0
GLA_CHUNK = 64
D_FF = 4 * D_MODEL
N_MOD = 6
EPS = 1e-6
LOG2_E = 1.4426950408889634

V7X_LANES = 128
V7X_SUBLANES = 8
V7X_MXU_DIM = 256
V7X_VMEM_BYTES = 64 * 1024 * 1024
VMEM_LIMIT = V7X_VMEM_BYTES - 8 * 1024 * 1024

LRU_COLB = V7X_MXU_DIM
LRU_NCOLB = D_MODEL // LRU_COLB
LR_PAD = V7X_LANES
COL_BLOCK = 16
GLA_BLOCK = V7X_MXU_DIM
GLA_SCORE_BLOCK = 2 * GLA_CHUNK
GLA_OUT_COLS = 16
LRU_CHUNK = 256
TILES_PER_STEP = 2
MERGE_STAGGER = 2
INPROJ_STAGGER = 2
FFN_SPLIT = 4
FFN_STAGGER = 3
LRU_SCAN_CUTS = 1
CM_PIECE = 256


def _cparams(sem):
    return pltpu.CompilerParams(dimension_semantics=sem, vmem_limit_bytes=VMEM_LIMIT)


def _rms(x, g):
    ms = jnp.mean(x * x, axis=-1, keepdims=True)
    return x * lax.rsqrt(ms + EPS) * g


def _sigmoid(x):
    return 0.5 * jnp.tanh(0.5 * x) + 0.5


def _softplus(x):
    return jnp.maximum(x, 0.0) + jnp.log1p(jnp.exp(-jnp.abs(x)))


def _bdot(a, b):
    return jnp.dot(a, b, preferred_element_type=F32)


def _mod_kernel(c_ref, w_ref, b_ref, o_ref):
    c = c_ref[...]
    s = (c * _sigmoid(c)).astype(BF16)
    o_ref[...] = _bdot(s, w_ref[...].astype(BF16)) + b_ref[...]


def _modulation(cc, w_mod, b_mod):
    rows = cc.shape[0]
    return pl.pallas_call(
        _mod_kernel,
        grid=(N_MOD,),
        in_specs=[
            pl.BlockSpec((rows, D_MODEL), lambda n: (0, 0)),
            pl.BlockSpec((D_MODEL, D_MODEL), lambda n: (0, n)),
            pl.BlockSpec((1, D_MODEL), lambda n: (0, n)),
        ],
        out_specs=pl.BlockSpec((rows, D_MODEL), lambda n: (0, n)),
        out_shape=jax.ShapeDtypeStruct((rows, N_MOD * D_MODEL), F32),
        compiler_params=_cparams(("arbitrary",)),
        name="modulation",
    )(cc, w_mod, b_mod.reshape(1, N_MOD * D_MODEL)).reshape(rows, N_MOD, D_MODEL)


def _tok_view(a, grid_tile):
    if grid_tile is None:
        return a
    b, t, n = a.shape
    rows = t // GRID_W
    return a.reshape(b, rows, GRID_W // COL_BLOCK, COL_BLOCK, n)


def _tok_spec(n, tm, grid_tile):
    if grid_tile is None:
        return pl.BlockSpec((None, tm, n), lambda b, j: (b, j, 0))
    rows = grid_tile
    return pl.BlockSpec((None, rows, None, COL_BLOCK, n), lambda b, j: (b, 0, j, 0, 0))


def _const_spec(shape):
    nd = len(shape)
    return pl.BlockSpec(shape, lambda b, j: (0,) * nd)


def _mod_spec(per_batch):
    if per_batch:
        return pl.BlockSpec((None, N_MOD, D_MODEL), lambda b, j: (b + 1, 0, 0))
    return pl.BlockSpec((None, N_MOD, D_MODEL), lambda b, j: (0, 0, 0))


def _perm_matrix(rows, cols):
    n = rows * cols
    p = np.zeros((n, n), np.float32)
    for r in range(rows):
        for c in range(cols):
            p[c * rows + r, r * cols + c] = 1.0
    return p


def _silu(x):
    return x * _sigmoid(x)


def _inproj_pieces(x_ref, mod_ref, ng_ref, perm_ref, w_refs, o_refs, *, n_natural, acts, n_piece, tile=None):
    v = {}

    def norm():
        x = (x_ref[...] if tile is None else x_ref[tile]).reshape(-1, D_MODEL)
        h = _rms(x, ng_ref[0:1, :]) * (1.0 + mod_ref[1:2, :]) + mod_ref[0:1, :]
        v['hb'] = v['hp'] = h.astype(BF16)
        if perm_ref is not None:
            v['hp'] = _bdot(perm_ref[...], v['hb']).astype(BF16)

    pieces = [norm]
    for i, (w_ref, o_ref) in enumerate(zip(w_refs, o_refs)):
        n = w_ref.shape[1]
        width = min(n, n_piece)
        for c0 in range(0, n, width):
            def piece(i=i, w_ref=w_ref, o_ref=o_ref, cs=slice(c0, c0 + width)):
                z = _bdot(v['hb'] if i < n_natural else v['hp'], w_ref[:, cs])
                if acts[i] is not None:
                    z = acts[i](z)
                if tile is None:
                    o_ref[:, cs] = z.astype(o_ref.dtype)
                else:
                    o_ref[tile, :, cs] = z.astype(o_ref.dtype)
            pieces.append(piece)
    return pieces


def _inproj_kernel(*refs, n_out, permute, n_natural, acts):
    x_ref, mod_ref, ng_ref = refs[:3]
    k = 4 if permute else 3
    perm_ref = refs[3] if permute else None
    _run_staggered([_inproj_pieces(x_ref, mod_ref, ng_ref, perm_ref, refs[k:k + n_out], refs[k + n_out:],
                                   n_natural=n_natural, acts=acts, n_piece=D_MODEL, tile=i)
                    for i in range(x_ref.shape[0])], INPROJ_STAGGER)


def _inproj_setup(x, weights, order):
    b, t, _ = x.shape
    if order == 'col_major':
        rows = t // GRID_W
        tm = rows * COL_BLOCK
        return _tok_view(x, rows), _tok_spec(D_MODEL, tm, rows), tm, _perm_matrix(rows, COL_BLOCK)
    tm = _lru_chunk(t)
    perm = _interleave_matrix(tm) if order == 'interleaved' else None
    return x, _tok_spec(D_MODEL, tm, None), tm, perm


def _inproj(x, mod, norm_g, weights, acts, *, per_batch_mod, name, n_natural=0):
    b, t, _ = x.shape
    tm = _lru_chunk(t)
    n_tiles = b * t // tm
    spec, mod_spec, const = _tile_specs(n_tiles, tm, t, per_batch_mod)
    outs = pl.pallas_call(
        functools.partial(_inproj_kernel, n_out=len(weights), permute=True,
                          n_natural=n_natural, acts=tuple(acts)),
        grid=(n_tiles // TILES_PER_STEP,),
        in_specs=[spec, mod_spec, const((4, D_MODEL)), const((tm, tm))] + [const(w.shape) for w in weights],
        out_specs=[pl.BlockSpec((TILES_PER_STEP, tm, w.shape[1]), lambda i: (i, 0, 0)) for w in weights],
        out_shape=[jax.ShapeDtypeStruct((n_tiles, tm, w.shape[1]), BF16) for w in weights],
        compiler_params=_cparams(("parallel",)),
        name=name,
    )(x.reshape(n_tiles, tm, D_MODEL), mod, norm_g, jnp.asarray(_interleave_matrix(tm), BF16), *weights)
    return [o.reshape(b, t, o.shape[-1]) for o in outs]


def _lru_chunk(t):
    return min(LRU_CHUNK, t)


def _interleave_matrix(tc):
    return _perm_matrix(V7X_SUBLANES, tc // V7X_SUBLANES)


def _lru_pieces(zf_ref, zb_ref, rf0, rb0, cw_ref, cb_ref, wa_ref, wx_ref, ba_ref, bx_ref, l_ref,
                hf_ref, hb_ref, fin_ref, a_s, u_s, carry_s, *, tc, period):
    ns = V7X_SUBLANES
    sl = tc // ns
    sub_c = lax.broadcasted_iota(jnp.int32, (ns, LRU_COLB), 0)
    prev_ok = jnp.bitwise_and(sub_c * sl, period - 1) != 0
    next_ok = jnp.bitwise_and((sub_c + 1) * sl, period - 1) != 0

    def gates(d, z_ref, r0, cb):
        cs = slice(cb * LRU_COLB, (cb + 1) * LRU_COLB)
        x = z_ref[r0:r0 + tc, cs].astype(F32)
        e_prev = jnp.where(prev_ok, pltpu.roll(x[tc - ns:], 1, 0), 0.0)
        e_next0 = jnp.where(next_ok, pltpu.roll(x[:ns], ns - 1, 0), 0.0)
        e_next1 = jnp.where(next_ok, pltpu.roll(x[ns:2 * ns], ns - 1, 0), 0.0)
        xm1 = jnp.concatenate([e_prev, x[:tc - ns]], axis=0)
        xp1 = jnp.concatenate([x[ns:], e_next0], axis=0)
        xp2 = jnp.concatenate([x[2 * ns:], e_next0, e_next1], axis=0)
        xc = (cb_ref[0:1, cs] + xm1 * cw_ref[0:1, cs] + x * cw_ref[1:2, cs]
              + xp1 * cw_ref[2:3, cs] + xp2 * cw_ref[3:4, cs])
        xcb = xc.astype(BF16)
        tr = jnp.tanh(_bdot(xcb, wa_ref[d, cb]) + 0.5 * ba_ref[d:d + 1, cs])
        ti = jnp.tanh(_bdot(xcb, wx_ref[d, cb]) + 0.5 * bx_ref[d:d + 1, cs])
        k2 = (-0.5 * LRU_C * LOG2_E) * _softplus(-l_ref[d:d + 1, cs])
        a = jnp.exp2(tr * k2 + k2)
        y = 1.0 - a * a
        u = jnp.where(y > 0.0, y * lax.rsqrt(y), 0.0) * ((0.5 * ti + 0.5) * xc)
        a_s[d, :, cs] = a
        u_s[d, :, cs] = u

    state = {}
    n_cut = LRU_SCAN_CUTS

    def pass1(part):
        if part == 0:
            zeros = jnp.zeros((ns, D_MODEL), F32)
            ones = jnp.ones((ns, D_MODEL), F32)
            state['c'] = (zeros, ones, zeros, ones)
        hf, af, hb, ab = state['c']
        for i in range(part * sl // n_cut, (part + 1) * sl // n_cut):
            rf = slice(i * ns, (i + 1) * ns)
            rb = slice((sl - 1 - i) * ns, (sl - i) * ns)
            a = a_s[0, rf, :]
            hf = a * hf + u_s[0, rf, :]
            af = a * af
            u_s[0, rf, :] = hf
            a_s[0, rf, :] = af
            a = a_s[1, rb, :]
            hb = a * hb + u_s[1, rb, :]
            ab = a * ab
            u_s[1, rb, :] = hb
            a_s[1, rb, :] = ab
        state['c'] = (hf, af, hb, ab)

    def chain():
        hf, af, hb, ab = state['c']
        sub = lax.broadcasted_iota(jnp.int32, (ns, D_MODEL), 0)

        def sublane_scan(a, u, forward):
            for s in (1, 2, 4):
                if forward:
                    m = sub >= s
                    shift = s
                else:
                    m = sub < ns - s
                    shift = ns - s
                a_sh = jnp.where(m, pltpu.roll(a, shift, 0), 1.0)
                u_sh = jnp.where(m, pltpu.roll(u, shift, 0), 0.0)
                u = a * u_sh + u
                a = a * a_sh
            return a, u

        cin_f = carry_s[0:1, :]
        cin_b = carry_s[1:2, :]
        af, hf = sublane_scan(af, hf, True)
        ab, hb = sublane_scan(ab, hb, False)
        fin_f = hf + af * cin_f
        fin_b = hb + ab * cin_b
        enter_f = jnp.where(sub == 0, cin_f, pltpu.roll(fin_f, 1, 0))
        enter_b = jnp.where(sub == ns - 1, cin_b, pltpu.roll(fin_b, ns - 1, 0))
        out_f = fin_f[ns - 1:ns, :]
        out_b = fin_b[0:1, :]
        carry_s[0:1, :] = out_f
        carry_s[1:2, :] = out_b
        fin_ref[0:1, :] = out_f
        fin_ref[1:2, :] = out_b
        state['e'] = (jnp.concatenate([enter_f, enter_f], axis=0), jnp.concatenate([enter_b, enter_b], axis=0))

    def pass2(part):
        pk = 2 * ns
        enter_f2, enter_b2 = state['e']
        for i in range(part * (sl // 2) // n_cut, (part + 1) * (sl // 2) // n_cut):
            rr = slice(i * pk, (i + 1) * pk)
            hf_ref[rf0 + i * pk:rf0 + (i + 1) * pk, :] = (
                u_s[0, rr, :] + a_s[0, rr, :] * enter_f2).astype(hf_ref.dtype)
            hb_ref[rb0 + i * pk:rb0 + (i + 1) * pk, :] = (
                u_s[1, rr, :] + a_s[1, rr, :] * enter_b2).astype(hb_ref.dtype)

    scan = ([functools.partial(pass1, part) for part in range(n_cut)] + [chain]
            + [functools.partial(pass2, part) for part in range(n_cut)])

    pieces = [functools.partial(gates, d, z_ref, r0, cb)
              for d, z_ref, r0 in ((0, zf_ref, rf0), (1, zb_ref, rb0)) for cb in range(LRU_NCOLB)]
    return pieces + scan


def _interleave(major, minor):
    out, done = [], 0
    for i, p in enumerate(major):
        want = -(-(i + 1) * len(minor) // len(major))
        out.extend(minor[done:want])
        done = want
        out.append(p)
    return out


def _cm_lru_kernel(*refs, n_cm, permute, acts, tc, period, cps):
    x_ref, mod_ref, ng_ref = refs[:3]
    k = 4 if permute else 3
    perm_ref = refs[3] if permute else None
    w_refs = refs[k:k + n_cm]
    k += n_cm
    zf_ref, zb_ref, h0_ref, cw_ref, cb_ref, wa_ref, wx_ref, ba_ref, bx_ref, l_ref = refs[k:k + 10]
    k += 10
    o_refs = refs[k:k + n_cm]
    k += n_cm
    hf_ref, hb_ref, fin_ref, a_s, u_s, carry_s = refs[k:]

    @pl.when(pl.program_id(1) == 0)
    def _():
        carry_s[...] = h0_ref[...]

    cm = _inproj_pieces(x_ref, mod_ref, ng_ref, perm_ref, w_refs, o_refs, n_natural=0, acts=acts,
                        n_piece=CM_PIECE)
    lru = []
    for s in range(cps):
        lru += _lru_pieces(zf_ref, zb_ref, s * tc, (cps - 1 - s) * tc, cw_ref, cb_ref, wa_ref, wx_ref,
                           ba_ref, bx_ref, l_ref, hf_ref, hb_ref, fin_ref, a_s, u_s, carry_s,
                           tc=tc, period=period)
    cm[0]()
    for piece in _interleave(lru, cm[1:]):
        piece()


def _cm_lru(x, mod, norm_g, weights, acts, zx, h0, conv_w, conv_b, wa_bd, wx_bd, ba, bx, lru_l, *,
            per_batch_mod, order, period, name):
    b, t, _ = x.shape
    x_in, x_spec, tm, perm = _inproj_setup(x, weights, order)
    extra = [] if perm is None else [jnp.asarray(perm, BF16)]
    n_steps = t // tm
    tc = _lru_chunk(t)
    cps = t // tc // n_steps
    assert cps >= 1 and cps * tc * n_steps == t
    assert period % (tc // V7X_SUBLANES) == 0 and tc % period == 0
    blk = cps * tc
    res = pl.pallas_call(
        functools.partial(_cm_lru_kernel, n_cm=len(weights), permute=perm is not None, acts=tuple(acts),
                          tc=tc, period=period, cps=cps),
        grid=(b, n_steps),
        in_specs=[x_spec, _mod_spec(per_batch_mod), _const_spec((4, D_MODEL))]
        + [_const_spec(e.shape) for e in extra] + [_const_spec(w.shape) for w in weights]
        + [
            pl.BlockSpec((None, blk, D_MODEL), lambda bb, j: (bb, j, 0)),
            pl.BlockSpec((None, blk, D_MODEL), lambda bb, j: (bb, n_steps - 1 - j, 0)),
            pl.BlockSpec((None, 2, D_MODEL), lambda bb, j: (bb, 0, 0)),
            _const_spec((4, D_MODEL)), _const_spec((1, D_MODEL)),
            _const_spec((2, LRU_NCOLB, LRU_COLB, LRU_COLB)), _const_spec((2, LRU_NCOLB, LRU_COLB, LRU_COLB)),
            _const_spec((2, D_MODEL)), _const_spec((2, D_MODEL)), _const_spec((2, D_MODEL)),
        ],
        out_specs=[_tok_spec(w.shape[1], tm, None) for w in weights] + [
            pl.BlockSpec((None, blk, D_MODEL), lambda bb, j: (bb, j, 0)),
            pl.BlockSpec((None, blk, D_MODEL), lambda bb, j: (bb, n_steps - 1 - j, 0)),
            pl.BlockSpec((None, 2, D_MODEL), lambda bb, j: (bb, 0, 0)),
        ],
        out_shape=[jax.ShapeDtypeStruct((b, t, w.shape[1]), BF16) for w in weights] + [
            jax.ShapeDtypeStruct((b, t, D_MODEL), BF16),
            jax.ShapeDtypeStruct((b, t, D_MODEL), BF16),
            jax.ShapeDtypeStruct((b, 2, D_MODEL), F32),
        ],
        scratch_shapes=[
            pltpu.VMEM((2, tc, D_MODEL), F32),
            pltpu.VMEM((2, tc, D_MODEL), F32),
            pltpu.VMEM((2, D_MODEL), F32),
        ],
        compiler_params=_cparams(("parallel", "arbitrary")),
        name=name,
    )(x_in, mod, norm_g, *extra, *weights, zx, zx, h0, conv_w, conv_b, wa_bd, wx_bd, ba, bx, lru_l)
    return res[:len(weights)], res[len(weights)], res[len(weights) + 1], res[len(weights) + 2]


def _gla_kernel(*refs, t, hps, has_s0, want_final, grid_rows):
    q_ref, k_ref, v_ref, g_ref, lr_ref, w2_ref, b2_ref, gn_ref = refs[:8]
    n = 8
    if grid_rows is not None:
        perm_ref = refs[n]
        n += 1
    if has_s0:
        s0_ref = refs[n]
        n += 1
    on_ref = refs[n]
    n += 1
    if want_final:
        fin_ref = refs[n]
        n += 1
    oacc, st, q2_s, k2_s, bt_s, pp_q, pp_k, pp_qx, pp_kx = refs[n:]

    c = GLA_CHUNK
    n_chunks = t // c
    blk = min(GLA_BLOCK, t)
    for d in range(2):
        for hh in range(hps):
            if has_s0:
                st[d, hh] = s0_ref[d, hh].T
            else:
                st[d, hh] = jnp.zeros((GLA_DV, GLA_DK), F32)

    ti = lax.broadcasted_iota(jnp.int32, (blk, blk), 0)
    si = lax.broadcasted_iota(jnp.int32, (blk, blk), 1)
    same = (ti // c) == (si // c)
    cums = ((same & (si <= ti)).astype(BF16), (same & (si >= ti)).astype(BF16))
    sb = min(GLA_SCORE_BLOCK, blk)
    tj = lax.broadcasted_iota(jnp.int32, (sb, sb), 0)
    sj = lax.broadcasted_iota(jnp.int32, (sb, sb), 1)
    same_j = (tj // c) == (sj // c)
    masks = (same_j & (sj <= tj), same_j & (sj >= tj))
    scale = GLA_DK ** -0.5

    def step(ia, slot_a, ib, slot_b):
        nd = hps * GLA_DK
        if ia is not None:
            rows_a = pl.ds(pl.multiple_of(ia * blk, blk), blk)
            lrb = lr_ref[rows_a, :]
            pre = [_bdot(lrb, w2_ref[d]) + b2_ref[d:d + 1, :] for d in range(2)]
        if ib is not None:
            rows_b = [pl.ds(pl.multiple_of(ib * blk + g * sb, sb), sb) for g in range(blk // sb)]
            loc_b = [slice(g * sb, (g + 1) * sb) for g in range(blk // sb)]
            nt = (((1,), (1,)), ((), ()))
            raw = [[[lax.dot_general(pp_q[slot_b, d, loc, hh * GLA_DK:(hh + 1) * GLA_DK],
                                     pp_k[slot_b, d, loc, hh * GLA_DK:(hh + 1) * GLA_DK], nt,
                                     preferred_element_type=F32) for d in range(2)]
                    + [lax.dot_general(pp_qx[slot_b, loc, hh * 2 * GLA_DK:(hh + 1) * 2 * GLA_DK],
                                       pp_kx[slot_b, loc, hh * 2 * GLA_DK:(hh + 1) * 2 * GLA_DK], nt,
                                       preferred_element_type=F32)]
                    for hh in range(hps)] for loc in loc_b]
        if ia is not None:
            bcum = []
            for d in range(2):
                soft = jnp.log(1.0 + jnp.exp2(jnp.abs(pre[d]) * (-LOG2_E)))
                la = jnp.minimum(pre[d], 0.0) * (LOG2_E / GLA_TAU) - soft * (LOG2_E / GLA_TAU)
                hi = la.astype(BF16)
                mid = (la - hi.astype(F32)).astype(BF16)
                bcum.append(_bdot(cums[d], hi) + _bdot(cums[d], mid))
        if ib is not None:
            for g, rows in enumerate(rows_b):
                for hh in range(hps):
                    vs = slice(hh * GLA_DV, (hh + 1) * GLA_DV)
                    sd0, sd1, sx = raw[g][hh]
                    sc = jnp.where(masks[0], sd0, 0.0) + jnp.where(masks[1], sd1, 0.0) + sx
                    oacc[rows, vs] = _bdot(sc.astype(BF16), v_ref[rows, vs])
        if ia is not None:
            q = q_ref[rows_a, :].astype(F32)
            k = k_ref[rows_a, :].astype(F32)
            nc = blk // c
            crow = lambda a, n: a[n * c:(n + 1) * c]
            cat = lambda parts: jnp.concatenate(parts, axis=0)
            zero = jnp.zeros((c, nd), BF16)
            qx, kx = [], []
            for d in range(2):
                edge = c - 1 if d == 0 else 0
                tot = [bcum[d][n * c + edge:n * c + edge + 1, :] for n in range(nc)]
                dec = [jnp.exp2(x) for x in tot]
                btot = cat([jnp.broadcast_to(x, (c, nd)) for x in tot])
                qe = (q * scale) * jnp.exp2(bcum[d])
                kt = k * jnp.exp2(btot - bcum[d])
                pp_q[slot_a, d] = qe.astype(BF16)
                pp_k[slot_a, d] = (k * jnp.exp2(-bcum[d])).astype(BF16)
                first = [(n % 2 == 0) == (d == 0) for n in range(nc)]
                q2 = cat([crow(qe, n) if first[n] else crow(qe, n) * dec[n ^ 1] for n in range(nc)])
                k2 = cat([crow(kt, n) * dec[n ^ 1] if first[n] else crow(kt, n) for n in range(nc)])
                q2_s[d, rows_a, :] = q2.astype(BF16)
                k2_s[d, rows_a, :] = k2.astype(BF16)
                bt_s[d, rows_a, :] = cat([jnp.broadcast_to(tot[n - n % 2] + tot[n - n % 2 + 1], (c, nd))
                                          for n in range(nc)])
                qeb, ktb = qe.astype(BF16), kt.astype(BF16)
                qx.append(cat([zero if first[n] else crow(qeb, n) for n in range(nc)]))
                kx.append(cat([crow(ktb, n) if first[n] else zero for n in range(nc)]))
            for hh in range(hps):
                ks = slice(hh * GLA_DK, (hh + 1) * GLA_DK)
                xs = slice(hh * 2 * GLA_DK, (hh + 1) * 2 * GLA_DK)
                pp_qx[slot_a, :, xs] = jnp.concatenate([qx[0][:, ks], qx[1][:, ks]], axis=1)
                pp_kx[slot_a, :, xs] = jnp.concatenate([kx[0][:, ks], kx[1][:, ks]], axis=1)

    def pair(j, carry):
        step(2 * j + 1, 1, 2 * j, 0)
        step(2 * j + 2, 0, 2 * j + 1, 1)
        return carry

    n_blocks = t // blk
    n_pairs = (n_blocks - 1) // 2
    step(0, 0, None, None)
    lax.fori_loop(0, n_pairs, pair, 0)
    done = 2 * n_pairs
    if done < n_blocks - 1:
        step(done + 1, 1, done, 0)
        done += 1
    step(None, None, done, done % 2)

    pr = 2 * c
    n_prs = t // pr

    def pair_step(p0, d, hh):
        rows = pl.ds(p0, pr)
        ks = slice(hh * GLA_DK, (hh + 1) * GLA_DK)
        vs = slice(hh * GLA_DV, (hh + 1) * GLA_DV)
        s_t = st[d, hh]
        o = lax.dot_general(q2_s[d, rows, ks], s_t.astype(BF16), (((1,), (1,)), ((), ())),
                            preferred_element_type=F32)
        upd = lax.dot_general(v_ref[rows, vs], k2_s[d, rows, ks], (((0,), (0,)), ((), ())),
                              preferred_element_type=F32)
        st[d, hh] = s_t * jnp.exp2(bt_s[d, pl.ds(p0, 1), ks]) + upd
        return o

    def body(jj, carry):
        pf = pl.multiple_of(jj * pr, pr)
        pb = pl.multiple_of((n_prs - 1 - jj) * pr, pr)
        for hh in range(hps):
            vs = slice(hh * GLA_DV, (hh + 1) * GLA_DV)
            for p0, d in ((pf, 0), (pb, 1)):
                oacc[pl.ds(p0, pr), vs] += pair_step(p0, d, hh)
        return carry

    lax.fori_loop(0, n_prs, body, 0, unroll=4)

    if grid_rows is None:
        ob = min(GLA_BLOCK, t)
    else:
        ob = GLA_OUT_COLS * grid_rows

    def out_block(ib, carry):
        r0 = pl.multiple_of(ib * ob, ob)
        rows = pl.ds(r0, ob)
        parts = []
        for hh in range(hps):
            vs = slice(hh * GLA_DV, (hh + 1) * GLA_DV)
            parts.append((_rms(oacc[rows, vs], gn_ref[...]) * g_ref[rows, vs].astype(F32)).astype(BF16))
        on = jnp.concatenate(parts, axis=1)
        if grid_rows is None:
            on_ref[rows, :] = on
        else:
            on = _bdot(perm_ref[...], on).astype(BF16)
            for r in range(grid_rows):
                dst = pl.multiple_of(r * GRID_W + ib * GLA_OUT_COLS, GLA_OUT_COLS)
                on_ref[pl.ds(dst, GLA_OUT_COLS), :] = on[r * GLA_OUT_COLS:(r + 1) * GLA_OUT_COLS]
        return carry

    lax.fori_loop(0, t // ob, out_block, 0)

    if want_final:
        for d in range(2):
            for hh in range(hps):
                fin_ref[d, hh] = st[d, hh].T


def _gla(q, k, v, g, lr, w2p, b2, gn, s0, *, want_final, hps, col_major, name):
    b, t, _ = q.shape
    ng = GLA_HEADS // hps
    has_s0 = s0 is not None
    grid_rows = t // GRID_W if col_major else None
    in_specs = [
        pl.BlockSpec((None, t, hps * GLA_DK), lambda bb, h: (bb, 0, h)),
        pl.BlockSpec((None, t, hps * GLA_DK), lambda bb, h: (bb, 0, h)),
        pl.BlockSpec((None, t, hps * GLA_DV), lambda bb, h: (bb, 0, h)),
        pl.BlockSpec((None, t, hps * GLA_DV), lambda bb, h: (bb, 0, h)),
        pl.BlockSpec((None, t, LR_PAD), lambda bb, h: (bb, 0, 0)),
        pl.BlockSpec((2, LR_PAD, hps * GLA_DK), lambda bb, h: (0, 0, h)),
        pl.BlockSpec((2, hps * GLA_DK), lambda bb, h: (0, h)),
        pl.BlockSpec((1, GLA_DV), lambda bb, h: (0, 0)),
    ]
    args = [q, k, v, g, lr, w2p, b2, gn]
    if col_major:
        ob = GLA_OUT_COLS * grid_rows
        in_specs.append(pl.BlockSpec((ob, ob), lambda bb, h: (0, 0)))
        args.append(jnp.asarray(_perm_matrix(grid_rows, GLA_OUT_COLS).T, BF16))
    state_spec = pl.BlockSpec((None, 2, hps, GLA_DK, GLA_DV), lambda bb, h: (bb, 0, h, 0, 0))
    if has_s0:
        in_specs.append(state_spec)
        args.append(s0)
    out_specs = [pl.BlockSpec((None, t, hps * GLA_DV), lambda bb, h: (bb, 0, h))]
    out_shape = [jax.ShapeDtypeStruct((b, t, GLA_HEADS * GLA_DV), BF16)]
    if want_final:
        out_specs.append(state_spec)
        out_shape.append(jax.ShapeDtypeStruct((b, 2, GLA_HEADS, GLA_DK, GLA_DV), F32))
    return pl.pallas_call(
        functools.partial(_gla_kernel, t=t, hps=hps, has_s0=has_s0, want_final=want_final,
                          grid_rows=grid_rows),
        grid=(b, ng),
        in_specs=in_specs,
        out_specs=out_specs,
        out_shape=out_shape,
        scratch_shapes=[
            pltpu.VMEM((t, hps * GLA_DV), F32),
            pltpu.VMEM((2, hps, GLA_DV, GLA_DK), F32),
            pltpu.VMEM((2, t, hps * GLA_DK), BF16),
            pltpu.VMEM((2, t, hps * GLA_DK), BF16),
            pltpu.VMEM((2, t, hps * GLA_DK), F32),
            pltpu.VMEM((2, 2, min(GLA_BLOCK, t), hps * GLA_DK), BF16),
            pltpu.VMEM((2, 2, min(GLA_BLOCK, t), hps * GLA_DK), BF16),
            pltpu.VMEM((2, min(GLA_BLOCK, t), 2 * hps * GLA_DK), BF16),
            pltpu.VMEM((2, min(GLA_BLOCK, t), 2 * hps * GLA_DK), BF16),
        ],
        compiler_params=_cparams(("parallel", "parallel")),
        name=name,
    )(*args)


def _gelu_tanh(x):
    return x * (0.5 * (1.0 + jnp.tanh(0.7978845608028654 * (x + 0.044715 * (x * x * x)))))


def _run_staggered(tiles, stagger):
    n_st = len(tiles[0])
    for clock in range(n_st + stagger * (len(tiles) - 1)):
        for i, st in enumerate(tiles):
            k = clock - stagger * i
            if 0 <= k < n_st:
                st[k]()


def _tile_specs(n_tiles, tm, t, per_batch_mod):
    tps = TILES_PER_STEP
    assert n_tiles % tps == 0 and (t % (tps * tm) == 0 or not per_batch_mod)
    spec = pl.BlockSpec((tps, tm, D_MODEL), lambda i: (i, 0, 0))
    const = lambda shape: pl.BlockSpec(shape, lambda i: (0,) * len(shape))
    if per_batch_mod:
        mod_spec = pl.BlockSpec((None, N_MOD, D_MODEL), lambda i: ((i * tps * tm) // t + 1, 0, 0))
    else:
        mod_spec = pl.BlockSpec((None, N_MOD, D_MODEL), lambda i: (0, 0, 0))
    return spec, mod_spec, const


def _merge_kernel(x_ref, hf_ref, hb_ref, zg_ref, ga_ref, gb_ref, on_ref, mod_ref, ng_ref,
                  unperm_ref, lup_ref, gup_ref, wo_ref, o_ref):
    def stages(i):
        ld = lambda r: r[i].astype(F32)
        v = {}

        def s0():
            v['ya'] = ((ld(hf_ref) + ld(hb_ref)) * _gelu_tanh(ld(zg_ref))).astype(BF16)

        def s1():
            ya = _bdot(unperm_ref[...], v['ya']).astype(BF16)
            v['y_a'] = _bdot(ya, lup_ref[...])

        def s2():
            v['y_b'] = _bdot(on_ref[i], gup_ref[...])

        def s3():
            v['mm'] = (_sigmoid(ld(ga_ref)) * v['y_a'] + _sigmoid(ld(gb_ref)) * v['y_b']).astype(BF16)

        def s4():
            v['m'] = _bdot(v['mm'], wo_ref[...])

        def s5():
            o_ref[i] = x_ref[i] + mod_ref[2:3, :] * _rms(v['m'], ng_ref[1:2, :])

        return [s0, s1, s2, s3, s4, s5]

    _run_staggered([stages(i) for i in range(x_ref.shape[0])], MERGE_STAGGER)


def _merge(x, hf, hb, zg, ga, gb, on, mod, norm_g, lru_up, gla_up, w_out, *, per_batch_mod, name):
    b, t, _ = x.shape
    tm = _lru_chunk(t)
    n_tiles = b * t // tm
    spec, mod_spec, const = _tile_specs(n_tiles, tm, t, per_batch_mod)
    tiled = lambda a: a.reshape(n_tiles, tm, a.shape[-1])
    out = pl.pallas_call(
        _merge_kernel,
        grid=(n_tiles // TILES_PER_STEP,),
        in_specs=[spec] * 7 + [mod_spec, const((4, D_MODEL)), const((tm, tm))]
        + [const((D_MODEL, D_MODEL))] * 3,
        out_specs=spec,
        out_shape=jax.ShapeDtypeStruct((n_tiles, tm, D_MODEL), F32),
        compiler_params=_cparams(("parallel",)),
        name=name,
    )(*(tiled(a) for a in (x, hf, hb, zg, ga, gb, on)), mod, norm_g,
      jnp.asarray(_interleave_matrix(tm).T, BF16), lru_up, gla_up, w_out)
    return out.reshape(b, t, D_MODEL)


def _ffn_kernel(x_ref, mod_ref, ng_ref, w1_ref, w2_ref, o_ref):
    fs = D_FF // FFN_SPLIT

    def stages(i):
        v = {}

        def norm():
            x = x_ref[i]
            v['h'] = (_rms(x, ng_ref[2:3, :]) * (1.0 + mod_ref[4:5, :]) + mod_ref[3:4, :]).astype(BF16)

        def hidden(kf):
            hid = jnp.maximum(_bdot(v['h'], w1_ref[:, kf * fs:(kf + 1) * fs]), 0.0)
            part = _bdot((hid * hid).astype(BF16), w2_ref[kf * fs:(kf + 1) * fs, :])
            v['acc'] = part if kf == 0 else v['acc'] + part

        def out():
            o_ref[i] = x_ref[i] + mod_ref[5:6, :] * _rms(v['acc'], ng_ref[3:4, :])

        return [norm] + [functools.partial(hidden, kf) for kf in range(FFN_SPLIT)] + [out]

    _run_staggered([stages(i) for i in range(x_ref.shape[0])], FFN_STAGGER)


def _ffn(x, mod, norm_g, w1, w2, *, per_batch_mod, name):
    b, t, _ = x.shape
    tm = _lru_chunk(t)
    n_tiles = b * t // tm
    spec, mod_spec, const = _tile_specs(n_tiles, tm, t, per_batch_mod)
    out = pl.pallas_call(
        _ffn_kernel,
        grid=(n_tiles // TILES_PER_STEP,),
        in_specs=[spec, mod_spec, const((4, D_MODEL)), const((D_MODEL, D_FF)), const((D_FF, D_MODEL))],
        out_specs=spec,
        out_shape=jax.ShapeDtypeStruct((n_tiles, tm, D_MODEL), F32),
        compiler_params=_cparams(("parallel",)),
        name=name,
    )(x.reshape(n_tiles, tm, D_MODEL), mod, norm_g, w1, w2)
    return out.reshape(b, t, D_MODEL)


def _blockdiag_slabs(w):
    per = LRU_COLB // LRU_BW
    w = w.reshape(2, LRU_NCOLB, per, LRU_BW, LRU_BW)
    eye = jnp.eye(per, dtype=w.dtype)
    slab = jnp.einsum('dspij,pq->dspiqj', w, eye)
    return slab.reshape(2, LRU_NCOLB, LRU_COLB, LRU_COLB).astype(BF16)


def _layer_group(x, mod, p, lru_h0, gla_s0, *, latent, want_state, tag):
    b, t, _ = x.shape
    per_batch = latent
    ga, gb, zg, zx = _inproj(x, mod, p['norm_g'], p['w_rm'], (None,) * 4,
                             per_batch_mod=per_batch, n_natural=2, name=f"inproj_rm_{tag}")
    period = GRID_W if latent else t
    (q, k, v, g, lr), hf, hb, s_lru = _cm_lru(
        x, mod, p['norm_g'], p['w_cm'], (None, None, None, _silu, None), zx, lru_h0, p['conv_w'],
        p['conv_b'], p['wa_bd'], p['wx_bd'], p['lru_ba'], p['lru_bx'], p['lru_L'], per_batch_mod=per_batch,
        order='col_major' if latent else 'natural', period=period, name=f"cm_lru_{tag}")
    res = _gla(q, k, v, g, lr, p['w2p'], p['gla_b2'], p['gla_norm_g'], gla_s0,
               want_final=want_state, hps=2, col_major=latent, name=f"gla_{tag}")
    on = res[0]
    s_gla = res[1] if want_state else None
    x1 = _merge(x, hf, hb, zg, ga, gb, on, mod, p['norm_g'], p['lru_up'], p['gla_up'], p['w_out'],
                per_batch_mod=per_batch, name=f"merge_{tag}")
    y = _ffn(x1, mod, p['norm_g'], p['mlp_w1'], p['mlp_w2'], per_batch_mod=per_batch, name=f"ffn_{tag}")
    return y, s_lru, s_gla


def kernel(x_prompt, x_sample, state_lru, state_gla, c, c_ctx, w_mod, b_mod, norm_g, w_in, conv_w, conv_b,
           lru_wa, lru_ba, lru_wx, lru_bx, lru_L, lru_up, gla_w2, gla_b2, gla_norm_g, gla_up, w_out,
           mlp_w1, mlp_w2):
    depth = w_in.shape[0]
    assert depth == 1
    dec_b = x_sample.shape[0]
    b0 = x_prompt.shape[0]
    xp, xs = x_prompt, x_sample
    lru_states, gla_states = [], []
    mod_rows = -(-(1 + dec_b) // V7X_SUBLANES) * V7X_SUBLANES
    cc = jnp.zeros((mod_rows, D_MODEL), F32).at[0].set(c_ctx).at[1:1 + dec_b].set(c)
    for l in range(depth):
        wl = w_in[l]
        o = np.cumsum((0, D_MODEL, D_MODEL, 512, 512, 1024, 1024, 2 * GLA_RANK, D_MODEL, D_MODEL))
        col = lambda i: wl[:, o[i]:o[i + 1]].astype(BF16)
        lr_w = jnp.pad(col(6), ((0, 0), (0, LR_PAD - 2 * GLA_RANK)))
        w2 = gla_w2[l].astype(BF16)
        w2p = jnp.zeros((2, LR_PAD, GLA_HEADS * GLA_DK), BF16)
        w2p = w2p.at[0, 0:GLA_RANK].set(w2[0]).at[1, GLA_RANK:2 * GLA_RANK].set(w2[1])
        p = {
            'norm_g': norm_g[l],
            'w_rm': [col(7), col(8), col(1), col(0)],
            'w_cm': [col(2), col(3), col(4), col(5), lr_w],
            'conv_w': conv_w[l], 'conv_b': conv_b[l].reshape(1, D_MODEL),
            'wa_bd': _blockdiag_slabs(0.5 * lru_wa[l]), 'wx_bd': _blockdiag_slabs(0.5 * lru_wx[l]),
            'lru_ba': lru_ba[l], 'lru_bx': lru_bx[l], 'lru_L': lru_L[l],
            'lru_up': lru_up[l].astype(BF16),
            'w2p': w2p, 'gla_b2': gla_b2[l], 'gla_norm_g': gla_norm_g[l].reshape(1, GLA_DV),
            'gla_up': gla_up[l].astype(BF16), 'w_out': w_out[l].astype(BF16),
            'mlp_w1': mlp_w1[l].astype(BF16), 'mlp_w2': mlp_w2[l].astype(BF16),
        }
        mod = _modulation(cc, w_mod[l], b_mod[l])
        h0_ctx = jnp.zeros((b0, 2, D_MODEL), F32)
        xp, s_lru, s_gla = _layer_group(xp, mod, p, h0_ctx, None, latent=False, want_state=True,
                                        tag="ctx")
        lru_states.append(s_lru)
        gla_states.append(s_gla)
        xs, _, _ = _layer_group(xs, mod, p, state_lru[:, l], state_gla[:, l], latent=True,
                                want_state=False, tag="lat")
    return (xp, xs, jnp.stack(lru_states, axis=1), jnp.stack(gla_states, axis=1))
```

```python
import functools

import numpy as np
import jax
import jax.numpy as jnp
from jax import lax
from jax.experimental import pallas as pl
from jax.experimental.pallas import tpu as pltpu

F32 = jnp.float32
BF16 = jnp.bfloat16

D_MODEL = 1024
GRID_W = 64
LRU_BLOCKS = 16
LRU_BW = D_MODEL // LRU_BLOCKS
LRU_C = 8.0
GLA_HEADS = 4
GLA_DK = 128
GLA_DV = 256
GLA_RANK = 16
GLA_TAU = 16.0
GLA_CHUNK = 64
D_FF = 4 * D_MODEL
N_MOD = 6
EPS = 1e-6
LOG2_E = 1.4426950408889634

V7X_LANES = 128
V7X_SUBLANES = 8
V7X_MXU_DIM = 256
V7X_VMEM_BYTES = 64 * 1024 * 1024
VMEM_LIMIT = V7X_VMEM_BYTES - 8 * 1024 * 1024

LRU_COLB = V7X_MXU_DIM
LRU_NCOLB = D_MODEL // LRU_COLB
LR_PAD = V7X_LANES
COL_BLOCK = 16
GLA_BLOCK = V7X_MXU_DIM
GLA_SCORE_BLOCK = 2 * GLA_CHUNK
GLA_OUT_COLS = 16
LRU_CHUNK = 256
TILES_PER_STEP = 4
MERGE_STAGGER = 2
INPROJ_STAGGER = 2
FFN_SPLIT = 4
FFN_STAGGER = 3
LRU_SCAN_CUTS = 1
CM_PIECE = 256


def _cparams(sem):
    return pltpu.CompilerParams(dimension_semantics=sem, vmem_limit_bytes=VMEM_LIMIT)


def _rms(x, g):
    ms = jnp.mean(x * x, axis=-1, keepdims=True)
    return x * lax.rsqrt(ms + EPS) * g


def _sigmoid(x):
    return 0.5 * jnp.tanh(0.5 * x) + 0.5


def _softplus(x):
    return jnp.maximum(x, 0.0) + jnp.log1p(jnp.exp(-jnp.abs(x)))


def _bdot(a, b):
    return jnp.dot(a, b, preferred_element_type=F32)


def _mod_kernel(c_ref, w_ref, b_ref, o_ref):
    c = c_ref[...]
    s = (c * _sigmoid(c)).astype(BF16)
    o_ref[...] = _bdot(s, w_ref[...].astype(BF16)) + b_ref[...]


def _modulation(cc, w_mod, b_mod):
    rows = cc.shape[0]
    return pl.pallas_call(
        _mod_kernel,
        grid=(N_MOD,),
        in_specs=[
            pl.BlockSpec((rows, D_MODEL), lambda n: (0, 0)),
            pl.BlockSpec((D_MODEL, D_MODEL), lambda n: (0, n)),
            pl.BlockSpec((1, D_MODEL), lambda n: (0, n)),
        ],
        out_specs=pl.BlockSpec((rows, D_MODEL), lambda n: (0, n)),
        out_shape=jax.ShapeDtypeStruct((rows, N_MOD * D_MODEL), F32),
        compiler_params=_cparams(("arbitrary",)),
        name="modulation",
    )(cc, w_mod, b_mod.reshape(1, N_MOD * D_MODEL)).reshape(rows, N_MOD, D_MODEL)


IN_SPLITS = (D_MODEL, D_MODEL, GLA_HEADS * GLA_DK, GLA_HEADS * GLA_DK, GLA_HEADS * GLA_DV,
             GLA_HEADS * GLA_DV, 2 * GLA_RANK, D_MODEL, D_MODEL)
IN_OFFSETS = tuple(int(v) for v in np.cumsum((0,) + IN_SPLITS))
W_SPLIT_ROWS = 256


def _split_kernel(w_ref, *o_refs):
    for o_ref, off, n in zip(o_refs, IN_OFFSETS, IN_SPLITS):
        lo = off // V7X_LANES * V7X_LANES
        hi = -(-(off + n) // V7X_LANES) * V7X_LANES
        piece = w_ref[:, lo:hi][:, off - lo:off - lo + n].astype(BF16)
        if n == o_ref.shape[1]:
            o_ref[...] = piece
        else:
            o_ref[...] = jnp.zeros(o_ref.shape, BF16)
            o_ref[:, 0:n] = piece


def _split_w_in(w):
    widths = [LR_PAD if n == 2 * GLA_RANK else n for n in IN_SPLITS]
    return pl.pallas_call(
        _split_kernel,
        grid=(D_MODEL // W_SPLIT_ROWS,),
        in_specs=[pl.BlockSpec((W_SPLIT_ROWS, IN_OFFSETS[-1]), lambda i: (i, 0))],
        out_specs=[pl.BlockSpec((W_SPLIT_ROWS, n), lambda i: (i, 0)) for n in widths],
        out_shape=[jax.ShapeDtypeStruct((D_MODEL, n), BF16) for n in widths],
        compiler_params=_cparams(("parallel",)),
        name="split_w_in",
    )(w)


def _tok_view(a, grid_tile):
    if grid_tile is None:
        return a
    b, t, n = a.shape
    rows = t // GRID_W
    return a.reshape(b, rows, GRID_W // COL_BLOCK, COL_BLOCK, n)


def _tok_spec(n, tm, grid_tile):
    if grid_tile is None:
        return pl.BlockSpec((None, tm, n), lambda b, j: (b, j, 0))
    rows = grid_tile
    return pl.BlockSpec((None, rows, None, COL_BLOCK, n), lambda b, j: (b, 0, j, 0, 0))


def _const_spec(shape):
    nd = len(shape)
    return pl.BlockSpec(shape, lambda b, j: (0,) * nd)


def _mod_spec(per_batch):
    if per_batch:
        return pl.BlockSpec((None, N_MOD, D_MODEL), lambda b, j: (b + 1, 0, 0))
    return pl.BlockSpec((None, N_MOD, D_MODEL), lambda b, j: (0, 0, 0))


def _perm_matrix(rows, cols):
    n = rows * cols
    p = np.zeros((n, n), np.float32)
    for r in range(rows):
        for c in range(cols):
            p[c * rows + r, r * cols + c] = 1.0
    return p


def _silu(x):
    return x * _sigmoid(x)


def _inproj_pieces(x_ref, mod_ref, ng_ref, perm_ref, w_refs, o_refs, *, n_natural, acts, n_piece, tile=None):
    v = {}

    def norm():
        x = (x_ref[...] if tile is None else x_ref[tile]).reshape(-1, D_MODEL)
        h = _rms(x, ng_ref[0:1, :]) * (1.0 + mod_ref[1:2, :]) + mod_ref[0:1, :]
        v['hb'] = v['hp'] = h.astype(BF16)
        if perm_ref is not None:
            v['hp'] = _bdot(perm_ref[...], v['hb']).astype(BF16)

    pieces = [norm]
    for i, (w_ref, o_ref) in enumerate(zip(w_refs, o_refs)):
        n = w_ref.shape[1]
        width = min(n, n_piece)
        for c0 in range(0, n, width):
            def piece(i=i, w_ref=w_ref, o_ref=o_ref, cs=slice(c0, c0 + width)):
                z = _bdot(v['hb'] if i < n_natural else v['hp'], w_ref[:, cs])
                if acts[i] is not None:
                    z = acts[i](z)
                if tile is None:
                    o_ref[:, cs] = z.astype(o_ref.dtype)
                else:
                    o_ref[tile, :, cs] = z.astype(o_ref.dtype)
            pieces.append(piece)
    return pieces


def _inproj_kernel(*refs, n_out, permute, n_natural, acts):
    x_ref, mod_ref, ng_ref = refs[:3]
    k = 4 if permute else 3
    perm_ref = refs[3] if permute else None
    _run_staggered([_inproj_pieces(x_ref, mod_ref, ng_ref, perm_ref, refs[k:k + n_out], refs[k + n_out:],
                                   n_natural=n_natural, acts=acts, n_piece=D_MODEL, tile=i)
                    for i in range(x_ref.shape[0])], INPROJ_STAGGER)


def _inproj_setup(x, weights, order):
    b, t, _ = x.shape
    if order == 'col_major':
        rows = t // GRID_W
        tm = rows * COL_BLOCK
        return _tok_view(x, rows), _tok_spec(D_MODEL, tm, rows), tm, _perm_matrix(rows, COL_BLOCK)
    tm = _lru_chunk(t)
    perm = _interleave_matrix(tm) if order == 'interleaved' else None
    return x, _tok_spec(D_MODEL, tm, None), tm, perm


def _inproj(x, mod, norm_g, weights, acts, *, per_batch_mod, name, n_natural=0):
    b, t, _ = x.shape
    tm = _lru_chunk(t)
    n_tiles = b * t // tm
    spec, mod_spec, const = _tile_specs(n_tiles, tm, t, per_batch_mod)
    outs = pl.pallas_call(
        functools.partial(_inproj_kernel, n_out=len(weights), permute=True,
                          n_natural=n_natural, acts=tuple(acts)),
        grid=(n_tiles // TILES_PER_STEP,),
        in_specs=[spec, mod_spec, const((4, D_MODEL)), const((tm, tm))] + [const(w.shape) for w in weights],
        out_specs=[pl.BlockSpec((TILES_PER_STEP, tm, w.shape[1]), lambda i: (i, 0, 0)) for w in weights],
        out_shape=[jax.ShapeDtypeStruct((n_tiles, tm, w.shape[1]), BF16) for w in weights],
        compiler_params=_cparams(("parallel",)),
        name=name,
    )(x.reshape(n_tiles, tm, D_MODEL), mod, norm_g, jnp.asarray(_interleave_matrix(tm), BF16), *weights)
    return [o.reshape(b, t, o.shape[-1]) for o in outs]


def _lru_chunk(t):
    return min(LRU_CHUNK, t)


def _interleave_matrix(tc):
    return _perm_matrix(V7X_SUBLANES, tc // V7X_SUBLANES)


def _lru_pieces(zf_ref, zb_ref, rf0, rb0, cw_ref, cb_ref, wa_ref, wx_ref, ba_ref, bx_ref, l_ref,
                hf_ref, hb_ref, fin_ref, a_s, u_s, carry_s, *, tc, period):
    ns = V7X_SUBLANES
    sl = tc // ns
    sub_c = lax.broadcasted_iota(jnp.int32, (ns, LRU_COLB), 0)
    prev_ok = jnp.bitwise_and(sub_c * sl, period - 1) != 0
    next_ok = jnp.bitwise_and((sub_c + 1) * sl, period - 1) != 0

    def gates(d, z_ref, r0, cb):
        cs = slice(cb * LRU_COLB, (cb + 1) * LRU_COLB)
        x = z_ref[r0:r0 + tc, cs].astype(F32)
        e_prev = jnp.where(prev_ok, pltpu.roll(x[tc - ns:], 1, 0), 0.0)
        e_next0 = jnp.where(next_ok, pltpu.roll(x[:ns], ns - 1, 0), 0.0)
        e_next1 = jnp.where(next_ok, pltpu.roll(x[ns:2 * ns], ns - 1, 0), 0.0)
        xm1 = jnp.concatenate([e_prev, x[:tc - ns]], axis=0)
        xp1 = jnp.concatenate([x[ns:], e_next0], axis=0)
        xp2 = jnp.concatenate([x[2 * ns:], e_next0, e_next1], axis=0)
        xc = (cb_ref[0:1, cs] + xm1 * cw_ref[0:1, cs] + x * cw_ref[1:2, cs]
              + xp1 * cw_ref[2:3, cs] + xp2 * cw_ref[3:4, cs])
        xcb = xc.astype(BF16)
        tr = jnp.tanh(_bdot(xcb, wa_ref[d, cb]) + 0.5 * ba_ref[d:d + 1, cs])
        ti = jnp.tanh(_bdot(xcb, wx_ref[d, cb]) + 0.5 * bx_ref[d:d + 1, cs])
        k2 = (-0.5 * LRU_C * LOG2_E) * _softplus(-l_ref[d:d + 1, cs])
        a = jnp.exp2(tr * k2 + k2)
        y = 1.0 - a * a
        u = jnp.where(y > 0.0, y * lax.rsqrt(y), 0.0) * ((0.5 * ti + 0.5) * xc)
        a_s[d, :, cs] = a
        u_s[d, :, cs] = u

    state = {}
    n_cut = LRU_SCAN_CUTS

    def pass1(part):
        if part == 0:
            zeros = jnp.zeros((ns, D_MODEL), F32)
            ones = jnp.ones((ns, D_MODEL), F32)
            state['c'] = (zeros, ones, zeros, ones)
        hf, af, hb, ab = state['c']
        for i in range(part * sl // n_cut, (part + 1) * sl // n_cut):
            rf = slice(i * ns, (i + 1) * ns)
            rb = slice((sl - 1 - i) * ns, (sl - i) * ns)
            a = a_s[0, rf, :]
            hf = a * hf + u_s[0, rf, :]
            af = a * af
            u_s[0, rf, :] = hf
            a_s[0, rf, :] = af
            a = a_s[1, rb, :]
            hb = a * hb + u_s[1, rb, :]
            ab = a * ab
            u_s[1, rb, :] = hb
            a_s[1, rb, :] = ab
        state['c'] = (hf, af, hb, ab)

    def chain():
        hf, af, hb, ab = state['c']
        sub = lax.broadcasted_iota(jnp.int32, (ns, D_MODEL), 0)

        def sublane_scan(a, u, forward):
            for s in (1, 2, 4):
                if forward:
                    m = sub >= s
                    shift = s
                else:
                    m = sub < ns - s
                    shift = ns - s
                a_sh = jnp.where(m, pltpu.roll(a, shift, 0), 1.0)
                u_sh = jnp.where(m, pltpu.roll(u, shift, 0), 0.0)
                u = a * u_sh + u
                a = a * a_sh
            return a, u

        cin_f = carry_s[0:1, :]
        cin_b = carry_s[1:2, :]
        af, hf = sublane_scan(af, hf, True)
        ab, hb = sublane_scan(ab, hb, False)
        fin_f = hf + af * cin_f
        fin_b = hb + ab * cin_b
        enter_f = jnp.where(sub == 0, cin_f, pltpu.roll(fin_f, 1, 0))
        enter_b = jnp.where(sub == ns - 1, cin_b, pltpu.roll(fin_b, ns - 1, 0))
        out_f = fin_f[ns - 1:ns, :]
        out_b = fin_b[0:1, :]
        carry_s[0:1, :] = out_f
        carry_s[1:2, :] = out_b
        fin_ref[0:1, :] = out_f
        fin_ref[1:2, :] = out_b
        state['e'] = (jnp.concatenate([enter_f, enter_f], axis=0), jnp.concatenate([enter_b, enter_b], axis=0))

    def pass2(part):
        pk = 2 * ns
        enter_f2, enter_b2 = state['e']
        for i in range(part * (sl // 2) // n_cut, (part + 1) * (sl // 2) // n_cut):
            rr = slice(i * pk, (i + 1) * pk)
            hf_ref[rf0 + i * pk:rf0 + (i + 1) * pk, :] = (
                u_s[0, rr, :] + a_s[0, rr, :] * enter_f2).astype(hf_ref.dtype)
            hb_ref[rb0 + i * pk:rb0 + (i + 1) * pk, :] = (
                u_s[1, rr, :] + a_s[1, rr, :] * enter_b2).astype(hb_ref.dtype)

    scan = ([functools.partial(pass1, part) for part in range(n_cut)] + [chain]
            + [functools.partial(pass2, part) for part in range(n_cut)])

    pieces = [functools.partial(gates, d, z_ref, r0, cb)
              for d, z_ref, r0 in ((0, zf_ref, rf0), (1, zb_ref, rb0)) for cb in range(LRU_NCOLB)]
    return pieces + scan


def _interleave(major, minor):
    out, done = [], 0
    for i, p in enumerate(major):
        want = -(-(i + 1) * len(minor) // len(major))
        out.extend(minor[done:want])
        done = want
        out.append(p)
    return out


def _cm_lru_kernel(*refs, n_cm, permute, acts, tc, period, cps):
    x_ref, mod_ref, ng_ref = refs[:3]
    k = 4 if permute else 3
    perm_ref = refs[3] if permute else None
    w_refs = refs[k:k + n_cm]
    k += n_cm
    zf_ref, zb_ref, h0_ref, cw_ref, cb_ref, wa_ref, wx_ref, ba_ref, bx_ref, l_ref = refs[k:k + 10]
    k += 10
    o_refs = refs[k:k + n_cm]
    k += n_cm
    hf_ref, hb_ref, fin_ref, a_s, u_s, carry_s = refs[k:]

    @pl.when(pl.program_id(1) == 0)
    def _():
        carry_s[...] = h0_ref[...]

    cm = _inproj_pieces(x_ref, mod_ref, ng_ref, perm_ref, w_refs, o_refs, n_natural=0, acts=acts,
                        n_piece=CM_PIECE)
    lru = []
    for s in range(cps):
        lru += _lru_pieces(zf_ref, zb_ref, s * tc, (cps - 1 - s) * tc, cw_ref, cb_ref, wa_ref, wx_ref,
                           ba_ref, bx_ref, l_ref, hf_ref, hb_ref, fin_ref, a_s, u_s, carry_s,
                           tc=tc, period=period)
    cm[0]()
    for piece in _interleave(lru, cm[1:]):
        piece()


def _cm_lru(x, mod, norm_g, weights, acts, zx, h0, conv_w, conv_b, wa_bd, wx_bd, ba, bx, lru_l, *,
            per_batch_mod, order, period, name):
    b, t, _ = x.shape
    x_in, x_spec, tm, perm = _inproj_setup(x, weights, order)
    extra = [] if perm is None else [jnp.asarray(perm, BF16)]
    n_steps = t // tm
    tc = _lru_chunk(t)
    cps = t // tc // n_steps
    assert cps >= 1 and cps * tc * n_steps == t
    assert period % (tc // V7X_SUBLANES) == 0 and tc % period == 0
    blk = cps * tc
    res = pl.pallas_call(
        functools.partial(_cm_lru_kernel, n_cm=len(weights), permute=perm is not None, acts=tuple(acts),
                          tc=tc, period=period, cps=cps),
        grid=(b, n_steps),
        in_specs=[x_spec, _mod_spec(per_batch_mod), _const_spec((4, D_MODEL))]
        + [_const_spec(e.shape) for e in extra] + [_const_spec(w.shape) for w in weights]
        + [
            pl.BlockSpec((None, blk, D_MODEL), lambda bb, j: (bb, j, 0)),
            pl.BlockSpec((None, blk, D_MODEL), lambda bb, j: (bb, n_steps - 1 - j, 0)),
            pl.BlockSpec((None, 2, D_MODEL), lambda bb, j: (bb, 0, 0)),
            _const_spec((4, D_MODEL)), _const_spec((1, D_MODEL)),
            _const_spec((2, LRU_NCOLB, LRU_COLB, LRU_COLB)), _const_spec((2, LRU_NCOLB, LRU_COLB, LRU_COLB)),
            _const_spec((2, D_MODEL)), _const_spec((2, D_MODEL)), _const_spec((2, D_MODEL)),
        ],
        out_specs=[_tok_spec(w.shape[1], tm, None) for w in weights] + [
            pl.BlockSpec((None, blk, D_MODEL), lambda bb, j: (bb, j, 0)),
            pl.BlockSpec((None, blk, D_MODEL), lambda bb, j: (bb, n_steps - 1 - j, 0)),
            pl.BlockSpec((None, 2, D_MODEL), lambda bb, j: (bb, 0, 0)),
        ],
        out_shape=[jax.ShapeDtypeStruct((b, t, w.shape[1]), BF16) for w in weights] + [
            jax.ShapeDtypeStruct((b, t, D_MODEL), BF16),
            jax.ShapeDtypeStruct((b, t, D_MODEL), BF16),
            jax.ShapeDtypeStruct((b, 2, D_MODEL), F32),
        ],
        scratch_shapes=[
            pltpu.VMEM((2, tc, D_MODEL), F32),
            pltpu.VMEM((2, tc, D_MODEL), F32),
            pltpu.VMEM((2, D_MODEL), F32),
        ],
        compiler_params=_cparams(("parallel", "arbitrary")),
        name=name,
    )(x_in, mod, norm_g, *extra, *weights, zx, zx, h0, conv_w, conv_b, wa_bd, wx_bd, ba, bx, lru_l)
    return res[:len(weights)], res[len(weights)], res[len(weights) + 1], res[len(weights) + 2]


def _gla_kernel(*refs, t, hps, has_s0, want_final, grid_rows):
    q_ref, k_ref, v_ref, g_ref, lr_ref, w2_ref, b2_ref, gn_ref = refs[:8]
    n = 8
    if grid_rows is not None:
        perm_ref = refs[n]
        n += 1
    if has_s0:
        s0_ref = refs[n]
        n += 1
    on_ref = refs[n]
    n += 1
    if want_final:
        fin_ref = refs[n]
        n += 1
    oacc, st, q2_s, k2_s, bt_s, pp_q, pp_k, pp_qx, pp_kx = refs[n:]

    c = GLA_CHUNK
    n_chunks = t // c
    blk = min(GLA_BLOCK, t)
    for d in range(2):
        for hh in range(hps):
            if has_s0:
                st[d, hh] = s0_ref[d, hh].T
            else:
                st[d, hh] = jnp.zeros((GLA_DV, GLA_DK), F32)

    ti = lax.broadcasted_iota(jnp.int32, (blk, blk), 0)
    si = lax.broadcasted_iota(jnp.int32, (blk, blk), 1)
    same = (ti // c) == (si // c)
    cums = ((same & (si <= ti)).astype(BF16), (same & (si >= ti)).astype(BF16))
    sb = min(GLA_SCORE_BLOCK, blk)
    tj = lax.broadcasted_iota(jnp.int32, (sb, sb), 0)
    sj = lax.broadcasted_iota(jnp.int32, (sb, sb), 1)
    same_j = (tj // c) == (sj // c)
    masks = (same_j & (sj <= tj), same_j & (sj >= tj))
    scale = GLA_DK ** -0.5

    def step(ia, slot_a, ib, slot_b):
        nd = hps * GLA_DK
        if ia is not None:
            rows_a = pl.ds(pl.multiple_of(ia * blk, blk), blk)
            lrb = lr_ref[rows_a, :]
            pre = [_bdot(lrb, w2_ref[d]) + b2_ref[d:d + 1, :] for d in range(2)]
        if ib is not None:
            rows_b = [pl.ds(pl.multiple_of(ib * blk + g * sb, sb), sb) for g in range(blk // sb)]
            loc_b = [slice(g * sb, (g + 1) * sb) for g in range(blk // sb)]
            nt = (((1,), (1,)), ((), ()))
            raw = [[[lax.dot_general(pp_q[slot_b, d, loc, hh * GLA_DK:(hh + 1) * GLA_DK],
                                     pp_k[slot_b, d, loc, hh * GLA_DK:(hh + 1) * GLA_DK], nt,
                                     preferred_element_type=F32) for d in range(2)]
                    + [lax.dot_general(pp_qx[slot_b, loc, hh * 2 * GLA_DK:(hh + 1) * 2 * GLA_DK],
                                       pp_kx[slot_b, loc, hh * 2 * GLA_DK:(hh + 1) * 2 * GLA_DK], nt,
                                       preferred_element_type=F32)]
                    for hh in range(hps)] for loc in loc_b]
        if ia is not None:
            bcum = []
            for d in range(2):
                soft = jnp.log(1.0 + jnp.exp2(jnp.abs(pre[d]) * (-LOG2_E)))
                la = jnp.minimum(pre[d], 0.0) * (LOG2_E / GLA_TAU) - soft * (LOG2_E / GLA_TAU)
                hi = la.astype(BF16)
                mid = (la - hi.astype(F32)).astype(BF16)
                bcum.append(_bdot(cums[d], hi) + _bdot(cums[d], mid))
        if ib is not None:
            for g, rows in enumerate(rows_b):
                for hh in range(hps):
                    vs = slice(hh * GLA_DV, (hh + 1) * GLA_DV)
                    sd0, sd1, sx = raw[g][hh]
                    sc = jnp.where(masks[0], sd0, 0.0) + jnp.where(masks[1], sd1, 0.0) + sx
                    oacc[rows, vs] = _bdot(sc.astype(BF16), v_ref[rows, vs])
        if ia is not None:
            q = q_ref[rows_a, :].astype(F32)
            k = k_ref[rows_a, :].astype(F32)
            nc = blk // c
            crow = lambda a, n: a[n * c:(n + 1) * c]
            cat = lambda parts: jnp.concatenate(parts, axis=0)
            zero = jnp.zeros((c, nd), BF16)
            qx, kx = [], []
            for d in range(2):
                edge = c - 1 if d == 0 else 0
                tot = [bcum[d][n * c + edge:n * c + edge + 1, :] for n in range(nc)]
                dec = [jnp.exp2(x) for x in tot]
                btot = cat([jnp.broadcast_to(x, (c, nd)) for x in tot])
                qe = (q * scale) * jnp.exp2(bcum[d])
                kt = k * jnp.exp2(btot - bcum[d])
                pp_q[slot_a, d] = qe.astype(BF16)
                pp_k[slot_a, d] = (k * jnp.exp2(-bcum[d])).astype(BF16)
                first = [(n % 2 == 0) == (d == 0) for n in range(nc)]
                q2 = cat([crow(qe, n) if first[n] else crow(qe, n) * dec[n ^ 1] for n in range(nc)])
                k2 = cat([crow(kt, n) * dec[n ^ 1] if first[n] else crow(kt, n) for n in range(nc)])
                q2_s[d, rows_a, :] = q2.astype(BF16)
                k2_s[d, rows_a, :] = k2.astype(BF16)
                bt_s[d, rows_a, :] = cat([jnp.broadcast_to(tot[n - n % 2] + tot[n - n % 2 + 1], (c, nd))
                                          for n in range(nc)])
                qeb, ktb = qe.astype(BF16), kt.astype(BF16)
                qx.append(cat([zero if first[n] else crow(qeb, n) for n in range(nc)]))
                kx.append(cat([crow(ktb, n) if first[n] else zero for n in range(nc)]))
            for hh in range(hps):
                ks = slice(hh * GLA_DK, (hh + 1) * GLA_DK)
                xs = slice(hh * 2 * GLA_DK, (hh + 1) * 2 * GLA_DK)
                pp_qx[slot_a, :, xs] = jnp.concatenate([qx[0][:, ks], qx[1][:, ks]], axis=1)
                pp_kx[slot_a, :, xs] = jnp.concatenate([kx[0][:, ks], kx[1][:, ks]], axis=1)

    def pair(j, carry):
        step(2 * j + 1, 1, 2 * j, 0)
        step(2 * j + 2, 0, 2 * j + 1, 1)
        return carry

    n_blocks = t // blk
    n_pairs = (n_blocks - 1) // 2
    step(0, 0, None, None)
    lax.fori_loop(0, n_pairs, pair, 0)
    done = 2 * n_pairs
    if done < n_blocks - 1:
        step(done + 1, 1, done, 0)
        done += 1
    step(None, None, done, done % 2)

    pr = 2 * c
    n_prs = t // pr

    def pair_step(p0, d, hh):
        rows = pl.ds(p0, pr)
        ks = slice(hh * GLA_DK, (hh + 1) * GLA_DK)
        vs = slice(hh * GLA_DV, (hh + 1) * GLA_DV)
        s_t = st[d, hh]
        o = lax.dot_general(q2_s[d, rows, ks], s_t.astype(BF16), (((1,), (1,)), ((), ())),
                            preferred_element_type=F32)
        upd = lax.dot_general(v_ref[rows, vs], k2_s[d, rows, ks], (((0,), (0,)), ((), ())),
                              preferred_element_type=F32)
        st[d, hh] = s_t * jnp.exp2(bt_s[d, pl.ds(p0, 1), ks]) + upd
        return o

    def body(jj, carry):
        pf = pl.multiple_of(jj * pr, pr)
        pb = pl.multiple_of((n_prs - 1 - jj) * pr, pr)
        for hh in range(hps):
            vs = slice(hh * GLA_DV, (hh + 1) * GLA_DV)
            for p0, d in ((pf, 0), (pb, 1)):
                oacc[pl.ds(p0, pr), vs] += pair_step(p0, d, hh)
        return carry

    lax.fori_loop(0, n_prs, body, 0, unroll=4)

    if grid_rows is None:
        ob = min(GLA_BLOCK, t)
    else:
        ob = GLA_OUT_COLS * grid_rows

    def out_block(ib, carry):
        r0 = pl.multiple_of(ib * ob, ob)
        rows = pl.ds(r0, ob)
        parts = []
        for hh in range(hps):
            vs = slice(hh * GLA_DV, (hh + 1) * GLA_DV)
            parts.append((_rms(oacc[rows, vs], gn_ref[...]) * g_ref[rows, vs].astype(F32)).astype(BF16))
        on = jnp.concatenate(parts, axis=1)
        if grid_rows is None:
            on_ref[rows, :] = on
        else:
            on = _bdot(perm_ref[...], on).astype(BF16)
            for r in range(grid_rows):
                dst = pl.multiple_of(r * GRID_W + ib * GLA_OUT_COLS, GLA_OUT_COLS)
                on_ref[pl.ds(dst, GLA_OUT_COLS), :] = on[r * GLA_OUT_COLS:(r + 1) * GLA_OUT_COLS]
        return carry

    lax.fori_loop(0, t // ob, out_block, 0)

    if want_final:
        for d in range(2):
            for hh in range(hps):
                fin_ref[d, hh] = st[d, hh].T


def _gla(q, k, v, g, lr, w2p, b2, gn, s0, *, want_final, hps, col_major, name):
    b, t, _ = q.shape
    ng = GLA_HEADS // hps
    has_s0 = s0 is not None
    grid_rows = t // GRID_W if col_major else None
    in_specs = [
        pl.BlockSpec((None, t, hps * GLA_DK), lambda bb, h: (bb, 0, h)),
        pl.BlockSpec((None, t, hps * GLA_DK), lambda bb, h: (bb, 0, h)),
        pl.BlockSpec((None, t, hps * GLA_DV), lambda bb, h: (bb, 0, h)),
        pl.BlockSpec((None, t, hps * GLA_DV), lambda bb, h: (bb, 0, h)),
        pl.BlockSpec((None, t, LR_PAD), lambda bb, h: (bb, 0, 0)),
        pl.BlockSpec((2, LR_PAD, hps * GLA_DK), lambda bb, h: (0, 0, h)),
        pl.BlockSpec((2, hps * GLA_DK), lambda bb, h: (0, h)),
        pl.BlockSpec((1, GLA_DV), lambda bb, h: (0, 0)),
    ]
    args = [q, k, v, g, lr, w2p, b2, gn]
    if col_major:
        ob = GLA_OUT_COLS * grid_rows
        in_specs.append(pl.BlockSpec((ob, ob), lambda bb, h: (0, 0)))
        args.append(jnp.asarray(_perm_matrix(grid_rows, GLA_OUT_COLS).T, BF16))
    state_spec = pl.BlockSpec((None, 2, hps, GLA_DK, GLA_DV), lambda bb, h: (bb, 0, h, 0, 0))
    if has_s0:
        in_specs.append(state_spec)
        args.append(s0)
    out_specs = [pl.BlockSpec((None, t, hps * GLA_DV), lambda bb, h: (bb, 0, h))]
    out_shape = [jax.ShapeDtypeStruct((b, t, GLA_HEADS * GLA_DV), BF16)]
    if want_final:
        out_specs.append(state_spec)
        out_shape.append(jax.ShapeDtypeStruct((b, 2, GLA_HEADS, GLA_DK, GLA_DV), F32))
    return pl.pallas_call(
        functools.partial(_gla_kernel, t=t, hps=hps, has_s0=has_s0, want_final=want_final,
                          grid_rows=grid_rows),
        grid=(b, ng),
        in_specs=in_specs,
        out_specs=out_specs,
        out_shape=out_shape,
        scratch_shapes=[
            pltpu.VMEM((t, hps * GLA_DV), F32),
            pltpu.VMEM((2, hps, GLA_DV, GLA_DK), F32),
            pltpu.VMEM((2, t, hps * GLA_DK), BF16),
            pltpu.VMEM((2, t, hps * GLA_DK), BF16),
            pltpu.VMEM((2, t, hps * GLA_DK), F32),
            pltpu.VMEM((2, 2, min(GLA_BLOCK, t), hps * GLA_DK), BF16),
            pltpu.VMEM((2, 2, min(GLA_BLOCK, t), hps * GLA_DK), BF16),
            pltpu.VMEM((2, min(GLA_BLOCK, t), 2 * hps * GLA_DK), BF16),
            pltpu.VMEM((2, min(GLA_BLOCK, t), 2 * hps * GLA_DK), BF16),
        ],
        compiler_params=_cparams(("parallel", "parallel")),
        name=name,
    )(*args)


def _gelu_tanh(x):
    return x * (0.5 * (1.0 + jnp.tanh(0.7978845608028654 * (x + 0.044715 * (x * x * x)))))


def _run_staggered(tiles, stagger):
    n_st = len(tiles[0])
    for clock in range(n_st + stagger * (len(tiles) - 1)):
        for i, st in enumerate(tiles):
            k = clock - stagger * i
            if 0 <= k < n_st:
                st[k]()


def _tile_specs(n_tiles, tm, t, per_batch_mod):
    tps = TILES_PER_STEP
    assert n_tiles % tps == 0 and (t % (tps * tm) == 0 or not per_batch_mod)
    spec = pl.BlockSpec((tps, tm, D_MODEL), lambda i: (i, 0, 0))
    const = lambda shape: pl.BlockSpec(shape, lambda i: (0,) * len(shape))
    if per_batch_mod:
        mod_spec = pl.BlockSpec((None, N_MOD, D_MODEL), lambda i: ((i * tps * tm) // t + 1, 0, 0))
    else:
        mod_spec = pl.BlockSpec((None, N_MOD, D_MODEL), lambda i: (0, 0, 0))
    return spec, mod_spec, const


def _merge_kernel(x_ref, hf_ref, hb_ref, zg_ref, ga_ref, gb_ref, on_ref, mod_ref, ng_ref,
                  unperm_ref, lup_ref, gup_ref, wo_ref, o_ref):
    def stages(i):
        ld = lambda r: r[i].astype(F32)
        v = {}

        def s0():
            v['ya'] = ((ld(hf_ref) + ld(hb_ref)) * _gelu_tanh(ld(zg_ref))).astype(BF16)

        def s1():
            ya = _bdot(unperm_ref[...], v['ya']).astype(BF16)
            v['y_a'] = _bdot(ya, lup_ref[...])

        def s2():
            v['y_b'] = _bdot(on_ref[i], gup_ref[...])

        def s3():
            v['mm'] = (_sigmoid(ld(ga_ref)) * v['y_a'] + _sigmoid(ld(gb_ref)) * v['y_b']).astype(BF16)

        def s4():
            v['m'] = _bdot(v['mm'], wo_ref[...])

        def s5():
            o_ref[i] = x_ref[i] + mod_ref[2:3, :] * _rms(v['m'], ng_ref[1:2, :])

        return [s0, s1, s2, s3, s4, s5]

    _run_staggered([stages(i) for i in range(x_ref.shape[0])], MERGE_STAGGER)


def _merge(x, hf, hb, zg, ga, gb, on, mod, norm_g, lru_up, gla_up, w_out, *, per_batch_mod, name):
    b, t, _ = x.shape
    tm = _lru_chunk(t)
    n_tiles = b * t // tm
    spec, mod_spec, const = _tile_specs(n_tiles, tm, t, per_batch_mod)
    tiled = lambda a: a.reshape(n_tiles, tm, a.shape[-1])
    out = pl.pallas_call(
        _merge_kernel,
        grid=(n_tiles // TILES_PER_STEP,),
        in_specs=[spec] * 7 + [mod_spec, const((4, D_MODEL)), const((tm, tm))]
        + [const((D_MODEL, D_MODEL))] * 3,
        out_specs=spec,
        out_shape=jax.ShapeDtypeStruct((n_tiles, tm, D_MODEL), F32),
        compiler_params=_cparams(("parallel",)),
        name=name,
    )(*(tiled(a) for a in (x, hf, hb, zg, ga, gb, on)), mod, norm_g,
      jnp.asarray(_interleave_matrix(tm).T, BF16), lru_up, gla_up, w_out)
    return out.reshape(b, t, D_MODEL)


def _ffn_kernel(x_ref, mod_ref, ng_ref, w1_ref, w2_ref, o_ref):
    fs = D_FF // FFN_SPLIT

    def stages(i):
        v = {}

        def norm():
            x = x_ref[i]
            v['h'] = (_rms(x, ng_ref[2:3, :]) * (1.0 + mod_ref[4:5, :]) + mod_ref[3:4, :]).astype(BF16)

        def hidden(kf):
            hid = jnp.maximum(_bdot(v['h'], w1_ref[:, kf * fs:(kf + 1) * fs]), 0.0)
            part = _bdot((hid * hid).astype(BF16), w2_ref[kf * fs:(kf + 1) * fs, :])
            v['acc'] = part if kf == 0 else v['acc'] + part

        def out():
            o_ref[i] = x_ref[i] + mod_ref[5:6, :] * _rms(v['acc'], ng_ref[3:4, :])

        return [norm] + [functools.partial(hidden, kf) for kf in range(FFN_SPLIT)] + [out]

    _run_staggered([stages(i) for i in range(x_ref.shape[0])], FFN_STAGGER)


def _ffn(x, mod, norm_g, w1, w2, *, per_batch_mod, name):
    b, t, _ = x.shape
    tm = _lru_chunk(t)
    n_tiles = b * t // tm
    spec, mod_spec, const = _tile_specs(n_tiles, tm, t, per_batch_mod)
    out = pl.pallas_call(
        _ffn_kernel,
        grid=(n_tiles // TILES_PER_STEP,),
        in_specs=[spec, mod_spec, const((4, D_MODEL)), const((D_MODEL, D_FF)), const((D_FF, D_MODEL))],
        out_specs=spec,
        out_shape=jax.ShapeDtypeStruct((n_tiles, tm, D_MODEL), F32),
        compiler_params=_cparams(("parallel",)),
        name=name,
    )(x.reshape(n_tiles, tm, D_MODEL), mod, norm_g, w1, w2)
    return out.reshape(b, t, D_MODEL)


def _blockdiag_slabs(w):
    per = LRU_COLB // LRU_BW
    w = w.reshape(2, LRU_NCOLB, per, LRU_BW, LRU_BW)
    eye = jnp.eye(per, dtype=w.dtype)
    slab = jnp.einsum('dspij,pq->dspiqj', w, eye)
    return slab.reshape(2, LRU_NCOLB, LRU_COLB, LRU_COLB).astype(BF16)


def _layer_group(x, mod, p, lru_h0, gla_s0, *, latent, want_state, tag):
    b, t, _ = x.shape
    per_batch = latent
    ga, gb, zg, zx = _inproj(x, mod, p['norm_g'], p['w_rm'], (None,) * 4,
                             per_batch_mod=per_batch, n_natural=2, name=f"inproj_rm_{tag}")
    period = GRID_W if latent else t
    (q, k, v, g, lr), hf, hb, s_lru = _cm_lru(
        x, mod, p['norm_g'], p['w_cm'], (None, None, None, _silu, None), zx, lru_h0, p['conv_w'],
        p['conv_b'], p['wa_bd'], p['wx_bd'], p['lru_ba'], p['lru_bx'], p['lru_L'], per_batch_mod=per_batch,
        order='col_major' if latent else 'natural', period=period, name=f"cm_lru_{tag}")
    res = _gla(q, k, v, g, lr, p['w2p'], p['gla_b2'], p['gla_norm_g'], gla_s0,
               want_final=want_state, hps=2, col_major=latent, name=f"gla_{tag}")
    on = res[0]
    s_gla = res[1] if want_state else None
    x1 = _merge(x, hf, hb, zg, ga, gb, on, mod, p['norm_g'], p['lru_up'], p['gla_up'], p['w_out'],
                per_batch_mod=per_batch, name=f"merge_{tag}")
    y = _ffn(x1, mod, p['norm_g'], p['mlp_w1'], p['mlp_w2'], per_batch_mod=per_batch, name=f"ffn_{tag}")
    return y, s_lru, s_gla


def kernel(x_prompt, x_sample, state_lru, state_gla, c, c_ctx, w_mod, b_mod, norm_g, w_in, conv_w, conv_b,
           lru_wa, lru_ba, lru_wx, lru_bx, lru_L, lru_up, gla_w2, gla_b2, gla_norm_g, gla_up, w_out,
           mlp_w1, mlp_w2):
    depth = w_in.shape[0]
    assert depth == 1
    dec_b = x_sample.shape[0]
    b0 = x_prompt.shape[0]
    xp, xs = x_prompt, x_sample
    lru_states, gla_states = [], []
    mod_rows = -(-(1 + dec_b) // V7X_SUBLANES) * V7X_SUBLANES
    cc = jnp.zeros((mod_rows, D_MODEL), F32).at[0].set(c_ctx).at[1:1 + dec_b].set(c)
    for l in range(depth):
        zx_w, zg_w, q_w, k_w, v_w, g_w, lr_w, ga_w, gb_w = _split_w_in(w_in[l])
        w2 = gla_w2[l].astype(BF16)
        w2p = jnp.zeros((2, LR_PAD, GLA_HEADS * GLA_DK), BF16)
        w2p = w2p.at[0, 0:GLA_RANK].set(w2[0]).at[1, GLA_RANK:2 * GLA_RANK].set(w2[1])
        p = {
            'norm_g': norm_g[l],
            'w_rm': [ga_w, gb_w, zg_w, zx_w],
            'w_cm': [q_w, k_w, v_w, g_w, lr_w],
            'conv_w': conv_w[l], 'conv_b': conv_b[l].reshape(1, D_MODEL),
            'wa_bd': _blockdiag_slabs(0.5 * lru_wa[l]), 'wx_bd': _blockdiag_slabs(0.5 * lru_wx[l]),
            'lru_ba': lru_ba[l], 'lru_bx': lru_bx[l], 'lru_L': lru_L[l],
            'lru_up': lru_up[l].astype(BF16),
            'w2p': w2p, 'gla_b2': gla_b2[l], 'gla_norm_g': gla_norm_g[l].reshape(1, GLA_DV),
            'gla_up': gla_up[l].astype(BF16), 'w_out': w_out[l].astype(BF16),
            'mlp_w1': mlp_w1[l].astype(BF16), 'mlp_w2': mlp_w2[l].astype(BF16),
        }
        mod = _modulation(cc, w_mod[l], b_mod[l])
        h0_ctx = jnp.zeros((b0, 2, D_MODEL), F32)
        xp, s_lru, s_gla = _layer_group(xp, mod, p, h0_ctx, None, latent=False, want_state=True,
                                        tag="ctx")
        lru_states.append(s_lru)
        gla_states.append(s_gla)
        xs, _, _ = _layer_group(xs, mod, p, state_lru[:, l], state_gla[:, l], latent=True,
                                want_state=False, tag="lat")
    return (xp, xs, jnp.stack(lru_states, axis=1), jnp.stack(gla_states, axis=1))
```

```python
import functools

import numpy as np
import jax
import jax.numpy as jnp
from jax import lax
from jax.experimental import pallas as pl
from jax.experimental.pallas import tpu as pltpu

F32 = jnp.float32
BF16 = jnp.bfloat16

D_MODEL = 1024
GRID_W = 64
LRU_BLOCKS = 16
LRU_BW = D_MODEL // LRU_BLOCKS
LRU_C = 8.0
GLA_HEADS = 4
GLA_DK = 128
GLA_DV = 256
GLA_RANK = 16
GLA_TAU = 16.0
GLA_CHUNK = 64
D_FF = 4 * D_MODEL
N_MOD = 6
EPS = 1e-6
LOG2_E = 1.4426950408889634

V7X_LANES = 128
V7X_SUBLANES = 8
V7X_MXU_DIM = 256
V7X_VMEM_BYTES = 64 * 1024 * 1024
VMEM_LIMIT = V7X_VMEM_BYTES - 8 * 1024 * 1024

LRU_COLB = V7X_MXU_DIM
LRU_NCOLB = D_MODEL // LRU_COLB
LR_PAD = V7X_LANES
COL_BLOCK = 16
GLA_BLOCK = V7X_MXU_DIM
GLA_SCORE_BLOCK = 2 * GLA_CHUNK
GLA_OUT_COLS = 16
LRU_CHUNK = 256
TILES_PER_STEP = 2
MERGE_STAGGER = 2
INPROJ_STAGGER = 2
FFN_SPLIT = 4
FFN_STAGGER = 3
LRU_SCAN_CUTS = 1
CM_PIECE = 256


def _cparams(sem):
    return pltpu.CompilerParams(dimension_semantics=sem, vmem_limit_bytes=VMEM_LIMIT)


def _rms(x, g):
    ms = jnp.mean(x * x, axis=-1, keepdims=True)
    return x * lax.rsqrt(ms + EPS) * g


def _sigmoid(x):
    return 0.5 * jnp.tanh(0.5 * x) + 0.5


def _softplus(x):
    return jnp.maximum(x, 0.0) + jnp.log1p(jnp.exp(-jnp.abs(x)))


def _bdot(a, b):
    return jnp.dot(a, b, preferred_element_type=F32)


def _mod_kernel(c_ref, w_ref, b_ref, o_ref):
    c = c_ref[...]
    s = (c * _sigmoid(c)).astype(BF16)
    o_ref[...] = _bdot(s, w_ref[...].astype(BF16)) + b_ref[...]


def _modulation(cc, w_mod, b_mod):
    rows = cc.shape[0]
    return pl.pallas_call(
        _mod_kernel,
        grid=(N_MOD,),
        in_specs=[
            pl.BlockSpec((rows, D_MODEL), lambda n: (0, 0)),
            pl.BlockSpec((D_MODEL, D_MODEL), lambda n: (0, n)),
            pl.BlockSpec((1, D_MODEL), lambda n: (0, n)),
        ],
        out_specs=pl.BlockSpec((rows, D_MODEL), lambda n: (0, n)),
        out_shape=jax.ShapeDtypeStruct((rows, N_MOD * D_MODEL), F32),
        compiler_params=_cparams(("arbitrary",)),
        name="modulation",
    )(cc, w_mod, b_mod.reshape(1, N_MOD * D_MODEL)).reshape(rows, N_MOD, D_MODEL)


def _tok_view(a, grid_tile):
    if grid_tile is None:
        return a
    b, t, n = a.shape
    rows = t // GRID_W
    return a.reshape(b, rows, GRID_W // COL_BLOCK, COL_BLOCK, n)


def _tok_spec(n, tm, grid_tile):
    if grid_tile is None:
        return pl.BlockSpec((None, tm, n), lambda b, j: (b, j, 0))
    rows = grid_tile
    return pl.BlockSpec((None, rows, None, COL_BLOCK, n), lambda b, j: (b, 0, j, 0, 0))


def _const_spec(shape):
    nd = len(shape)
    return pl.BlockSpec(shape, lambda b, j: (0,) * nd)


def _mod_spec(per_batch):
    if per_batch:
        return pl.BlockSpec((None, N_MOD, D_MODEL), lambda b, j: (b + 1, 0, 0))
    return pl.BlockSpec((None, N_MOD, D_MODEL), lambda b, j: (0, 0, 0))


def _perm_matrix(rows, cols):
    n = rows * cols
    p = np.zeros((n, n), np.float32)
    for r in range(rows):
        for c in range(cols):
            p[c * rows + r, r * cols + c] = 1.0
    return p


def _silu(x):
    return x * _sigmoid(x)


def _inproj_pieces(x_ref, mod_ref, ng_ref, perm_ref, w_refs, o_refs, *, n_natural, acts, n_piece, tile=None):
    v = {}

    def norm():
        x = (x_ref[...] if tile is None else x_ref[tile]).reshape(-1, D_MODEL)
        h = _rms(x, ng_ref[0:1, :]) * (1.0 + mod_ref[1:2, :]) + mod_ref[0:1, :]
        v['hb'] = v['hp'] = h.astype(BF16)
        if perm_ref is not None:
            v['hp'] = _bdot(perm_ref[...], v['hb']).astype(BF16)

    pieces = [norm]
    for i, (w_ref, o_ref) in enumerate(zip(w_refs, o_refs)):
        n = w_ref.shape[1]
        width = min(n, n_piece)
        for c0 in range(0, n, width):
            def piece(i=i, w_ref=w_ref, o_ref=o_ref, cs=slice(c0, c0 + width)):
                z = _bdot(v['hb'] if i < n_natural else v['hp'], w_ref[:, cs])
                if acts[i] is not None:
                    z = acts[i](z)
                if tile is None:
                    o_ref[:, cs] = z.astype(o_ref.dtype)
                else:
                    o_ref[tile, :, cs] = z.astype(o_ref.dtype)
            pieces.append(piece)
    return pieces


def _inproj_kernel(*refs, n_out, permute, n_natural, acts):
    x_ref, mod_ref, ng_ref = refs[:3]
    k = 4 if permute else 3
    perm_ref = refs[3] if permute else None
    _run_staggered([_inproj_pieces(x_ref, mod_ref, ng_ref, perm_ref, refs[k:k + n_out], refs[k + n_out:],
                                   n_natural=n_natural, acts=acts, n_piece=D_MODEL, tile=i)
                    for i in range(x_ref.shape[0])], INPROJ_STAGGER)


def _inproj_setup(x, weights, order):
    b, t, _ = x.shape
    if order == 'col_major':
        rows = t // GRID_W
        tm = rows * COL_BLOCK
        return _tok_view(x, rows), _tok_spec(D_MODEL, tm, rows), tm, _perm_matrix(rows, COL_BLOCK)
    tm = _lru_chunk(t)
    perm = _interleave_matrix(tm) if order == 'interleaved' else None
    return x, _tok_spec(D_MODEL, tm, None), tm, perm


def _inproj(x, mod, norm_g, weights, acts, *, per_batch_mod, name, n_natural=0):
    b, t, _ = x.shape
    tm = _lru_chunk(t)
    n_tiles = b * t // tm
    spec, mod_spec, const = _tile_specs(n_tiles, tm, t, per_batch_mod)
    outs = pl.pallas_call(
        functools.partial(_inproj_kernel, n_out=len(weights), permute=True,
                          n_natural=n_natural, acts=tuple(acts)),
        grid=(n_tiles // TILES_PER_STEP,),
        in_specs=[spec, mod_spec, const((4, D_MODEL)), const((tm, tm))] + [const(w.shape) for w in weights],
        out_specs=[pl.BlockSpec((TILES_PER_STEP, tm, w.shape[1]), lambda i: (i, 0, 0)) for w in weights],
        out_shape=[jax.ShapeDtypeStruct((n_tiles, tm, w.shape[1]), BF16) for w in weights],
        compiler_params=_cparams(("parallel",)),
        name=name,
    )(x.reshape(n_tiles, tm, D_MODEL), mod, norm_g, jnp.asarray(_interleave_matrix(tm), BF16), *weights)
    return [o.reshape(b, t, o.shape[-1]) for o in outs]


def _lru_chunk(t):
    return min(LRU_CHUNK, t)


def _interleave_matrix(tc):
    return _perm_matrix(V7X_SUBLANES, tc // V7X_SUBLANES)


def _lru_pieces(zf_ref, zb_ref, rf0, rb0, cw_ref, cb_ref, wa_ref, wx_ref, ba_ref, bx_ref, l_ref,
                hf_ref, hb_ref, fin_ref, a_s, u_s, carry_s, *, tc, period):
    ns = V7X_SUBLANES
    sl = tc // ns
    sub_c = lax.broadcasted_iota(jnp.int32, (ns, LRU_COLB), 0)
    prev_ok = jnp.bitwise_and(sub_c * sl, period - 1) != 0
    next_ok = jnp.bitwise_and((sub_c + 1) * sl, period - 1) != 0

    def gates(d, z_ref, r0, cb):
        cs = slice(cb * LRU_COLB, (cb + 1) * LRU_COLB)
        x = z_ref[r0:r0 + tc, cs].astype(F32)
        e_prev = jnp.where(prev_ok, pltpu.roll(x[tc - ns:], 1, 0), 0.0)
        e_next0 = jnp.where(next_ok, pltpu.roll(x[:ns], ns - 1, 0), 0.0)
        e_next1 = jnp.where(next_ok, pltpu.roll(x[ns:2 * ns], ns - 1, 0), 0.0)
        xm1 = jnp.concatenate([e_prev, x[:tc - ns]], axis=0)
        xp1 = jnp.concatenate([x[ns:], e_next0], axis=0)
        xp2 = jnp.concatenate([x[2 * ns:], e_next0, e_next1], axis=0)
        xc = (cb_ref[0:1, cs] + xm1 * cw_ref[0:1, cs] + x * cw_ref[1:2, cs]
              + xp1 * cw_ref[2:3, cs] + xp2 * cw_ref[3:4, cs])
        xcb = xc.astype(BF16)
        tr = jnp.tanh(_bdot(xcb, wa_ref[d, cb]) + 0.5 * ba_ref[d:d + 1, cs])
        ti = jnp.tanh(_bdot(xcb, wx_ref[d, cb]) + 0.5 * bx_ref[d:d + 1, cs])
        k2 = (-0.5 * LRU_C * LOG2_E) * _softplus(-l_ref[d:d + 1, cs])
        a = jnp.exp2(tr * k2 + k2)
        y = 1.0 - a * a
        u = jnp.where(y > 0.0, y * lax.rsqrt(y), 0.0) * ((0.5 * ti + 0.5) * xc)
        a_s[d, :, cs] = a
        u_s[d, :, cs] = u

    state = {}
    n_cut = LRU_SCAN_CUTS

    def pass1(part):
        if part == 0:
            zeros = jnp.zeros((ns, D_MODEL), F32)
            ones = jnp.ones((ns, D_MODEL), F32)
            state['c'] = (zeros, ones, zeros, ones)
        hf, af, hb, ab = state['c']
        for i in range(part * sl // n_cut, (part + 1) * sl // n_cut):
            rf = slice(i * ns, (i + 1) * ns)
            rb = slice((sl - 1 - i) * ns, (sl - i) * ns)
            a = a_s[0, rf, :]
            hf = a * hf + u_s[0, rf, :]
            af = a * af
            u_s[0, rf, :] = hf
            a_s[0, rf, :] = af
            a = a_s[1, rb, :]
            hb = a * hb + u_s[1, rb, :]
            ab = a * ab
            u_s[1, rb, :] = hb
            a_s[1, rb, :] = ab
        state['c'] = (hf, af, hb, ab)

    def chain():
        hf, af, hb, ab = state['c']
        sub = lax.broadcasted_iota(jnp.int32, (ns, D_MODEL), 0)

        def sublane_scan(a, u, forward):
            for s in (1, 2, 4):
                if forward:
                    m = sub >= s
                    shift = s
                else:
                    m = sub < ns - s
                    shift = ns - s
                a_sh = jnp.where(m, pltpu.roll(a, shift, 0), 1.0)
                u_sh = jnp.where(m, pltpu.roll(u, shift, 0), 0.0)
                u = a * u_sh + u
                a = a * a_sh
            return a, u

        cin_f = carry_s[0:1, :]
        cin_b = carry_s[1:2, :]
        af, hf = sublane_scan(af, hf, True)
        ab, hb = sublane_scan(ab, hb, False)
        fin_f = hf + af * cin_f
        fin_b = hb + ab * cin_b
        enter_f = jnp.where(sub == 0, cin_f, pltpu.roll(fin_f, 1, 0))
        enter_b = jnp.where(sub == ns - 1, cin_b, pltpu.roll(fin_b, ns - 1, 0))
        out_f = fin_f[ns - 1:ns, :]
        out_b = fin_b[0:1, :]
        carry_s[0:1, :] = out_f
        carry_s[1:2, :] = out_b
        fin_ref[0:1, :] = out_f
        fin_ref[1:2, :] = out_b
        state['e'] = (jnp.concatenate([enter_f, enter_f], axis=0), jnp.concatenate([enter_b, enter_b], axis=0))

    def pass2(part):
        pk = 2 * ns
        enter_f2, enter_b2 = state['e']
        for i in range(part * (sl // 2) // n_cut, (part + 1) * (sl // 2) // n_cut):
            rr = slice(i * pk, (i + 1) * pk)
            hf_ref[rf0 + i * pk:rf0 + (i + 1) * pk, :] = (
                u_s[0, rr, :] + a_s[0, rr, :] * enter_f2).astype(hf_ref.dtype)
            hb_ref[rb0 + i * pk:rb0 + (i + 1) * pk, :] = (
                u_s[1, rr, :] + a_s[1, rr, :] * enter_b2).astype(hb_ref.dtype)

    scan = ([functools.partial(pass1, part) for part in range(n_cut)] + [chain]
            + [functools.partial(pass2, part) for part in range(n_cut)])

    pieces = [functools.partial(gates, d, z_ref, r0, cb)
              for d, z_ref, r0 in ((0, zf_ref, rf0), (1, zb_ref, rb0)) for cb in range(LRU_NCOLB)]
    return pieces + scan


def _interleave(major, minor):
    out, done = [], 0
    for i, p in enumerate(major):
        want = -(-(i + 1) * len(minor) // len(major))
        out.extend(minor[done:want])
        done = want
        out.append(p)
    return out


def _cm_lru_kernel(*refs, n_cm, permute, acts, tc, period, cps):
    x_ref, mod_ref, ng_ref = refs[:3]
    k = 4 if permute else 3
    perm_ref = refs[3] if permute else None
    w_refs = refs[k:k + n_cm]
    k += n_cm
    zf_ref, zb_ref, h0_ref, cw_ref, cb_ref, wa_ref, wx_ref, ba_ref, bx_ref, l_ref = refs[k:k + 10]
    k += 10
    o_refs = refs[k:k + n_cm]
    k += n_cm
    hf_ref, hb_ref, fin_ref, a_s, u_s, carry_s = refs[k:]

    @pl.when(pl.program_id(1) == 0)
    def _():
        carry_s[...] = h0_ref[...]

    cm = _inproj_pieces(x_ref, mod_ref, ng_ref, perm_ref, w_refs, o_refs, n_natural=0, acts=acts,
                        n_piece=CM_PIECE)
    lru = []
    for s in range(cps):
        lru += _lru_pieces(zf_ref, zb_ref, s * tc, (cps - 1 - s) * tc, cw_ref, cb_ref, wa_ref, wx_ref,
                           ba_ref, bx_ref, l_ref, hf_ref, hb_ref, fin_ref, a_s, u_s, carry_s,
                           tc=tc, period=period)
    cm[0]()
    for piece in _interleave(lru, cm[1:]):
        piece()


def _cm_lru(x, mod, norm_g, weights, acts, zx, h0, conv_w, conv_b, wa_bd, wx_bd, ba, bx, lru_l, *,
            per_batch_mod, order, period, name):
    b, t, _ = x.shape
    x_in, x_spec, tm, perm = _inproj_setup(x, weights, order)
    extra = [] if perm is None else [jnp.asarray(perm, BF16)]
    n_steps = t // tm
    tc = _lru_chunk(t)
    cps = t // tc // n_steps
    assert cps >= 1 and cps * tc * n_steps == t
    assert period % (tc // V7X_SUBLANES) == 0 and tc % period == 0
    blk = cps * tc
    res = pl.pallas_call(
        functools.partial(_cm_lru_kernel, n_cm=len(weights), permute=perm is not None, acts=tuple(acts),
                          tc=tc, period=period, cps=cps),
        grid=(b, n_steps),
        in_specs=[x_spec, _mod_spec(per_batch_mod), _const_spec((4, D_MODEL))]
        + [_const_spec(e.shape) for e in extra] + [_const_spec(w.shape) for w in weights]
        + [
            pl.BlockSpec((None, blk, D_MODEL), lambda bb, j: (bb, j, 0)),
            pl.BlockSpec((None, blk, D_MODEL), lambda bb, j: (bb, n_steps - 1 - j, 0)),
            pl.BlockSpec((None, 2, D_MODEL), lambda bb, j: (bb, 0, 0)),
            _const_spec((4, D_MODEL)), _const_spec((1, D_MODEL)),
            _const_spec((2, LRU_NCOLB, LRU_COLB, LRU_COLB)), _const_spec((2, LRU_NCOLB, LRU_COLB, LRU_COLB)),
            _const_spec((2, D_MODEL)), _const_spec((2, D_MODEL)), _const_spec((2, D_MODEL)),
        ],
        out_specs=[_tok_spec(w.shape[1], tm, None) for w in weights] + [
            pl.BlockSpec((None, blk, D_MODEL), lambda bb, j: (bb, j, 0)),
            pl.BlockSpec((None, blk, D_MODEL), lambda bb, j: (bb, n_steps - 1 - j, 0)),
            pl.BlockSpec((None, 2, D_MODEL), lambda bb, j: (bb, 0, 0)),
        ],
        out_shape=[jax.ShapeDtypeStruct((b, t, w.shape[1]), BF16) for w in weights] + [
            jax.ShapeDtypeStruct((b, t, D_MODEL), BF16),
            jax.ShapeDtypeStruct((b, t, D_MODEL), BF16),
            jax.ShapeDtypeStruct((b, 2, D_MODEL), F32),
        ],
        scratch_shapes=[
            pltpu.VMEM((2, tc, D_MODEL), F32),
            pltpu.VMEM((2, tc, D_MODEL), F32),
            pltpu.VMEM((2, D_MODEL), F32),
        ],
        compiler_params=_cparams(("parallel", "arbitrary")),
        name=name,
    )(x_in, mod, norm_g, *extra, *weights, zx, zx, h0, conv_w, conv_b, wa_bd, wx_bd, ba, bx, lru_l)
    return res[:len(weights)], res[len(weights)], res[len(weights) + 1], res[len(weights) + 2]


def _gla_kernel(*refs, t, hps, has_s0, want_final, grid_rows):
    q_ref, k_ref, v_ref, g_ref, lr_ref, w2_ref, b2_ref, gn_ref = refs[:8]
    n = 8
    if grid_rows is not None:
        perm_ref = refs[n]
        n += 1
    if has_s0:
        s0_ref = refs[n]
        n += 1
    on_ref = refs[n]
    n += 1
    if want_final:
        fin_ref = refs[n]
        n += 1
    oacc, st, q2_s, k2_s, bt_s, pp_q, pp_k, pp_qx, pp_kx = refs[n:]

    c = GLA_CHUNK
    n_chunks = t // c
    blk = min(GLA_BLOCK, t)
    for d in range(2):
        for hh in range(hps):
            if has_s0:
                st[d, hh] = s0_ref[d, hh].T
            else:
                st[d, hh] = jnp.zeros((GLA_DV, GLA_DK), F32)

    ti = lax.broadcasted_iota(jnp.int32, (blk, blk), 0)
    si = lax.broadcasted_iota(jnp.int32, (blk, blk), 1)
    same = (ti // c) == (si // c)
    cums = ((same & (si <= ti)).astype(BF16), (same & (si >= ti)).astype(BF16))
    sb = min(GLA_SCORE_BLOCK, blk)
    tj = lax.broadcasted_iota(jnp.int32, (sb, sb), 0)
    sj = lax.broadcasted_iota(jnp.int32, (sb, sb), 1)
    same_j = (tj // c) == (sj // c)
    masks = (same_j & (sj <= tj), same_j & (sj >= tj))
    scale = GLA_DK ** -0.5

    def step(ia, slot_a, ib, slot_b):
        nd = hps * GLA_DK
        if ia is not None:
            rows_a = pl.ds(pl.multiple_of(ia * blk, blk), blk)
            lrb = lr_ref[rows_a, :]
            pre = [_bdot(lrb, w2_ref[d]) + b2_ref[d:d + 1, :] for d in range(2)]
        if ib is not None:
            rows_b = [pl.ds(pl.multiple_of(ib * blk + g * sb, sb), sb) for g in range(blk // sb)]
            loc_b = [slice(g * sb, (g + 1) * sb) for g in range(blk // sb)]
            nt = (((1,), (1,)), ((), ()))
            raw = [[[lax.dot_general(pp_q[slot_b, d, loc, hh * GLA_DK:(hh + 1) * GLA_DK],
                                     pp_k[slot_b, d, loc, hh * GLA_DK:(hh + 1) * GLA_DK], nt,
                                     preferred_element_type=F32) for d in range(2)]
                    + [lax.dot_general(pp_qx[slot_b, loc, hh * 2 * GLA_DK:(hh + 1) * 2 * GLA_DK],
                                       pp_kx[slot_b, loc, hh * 2 * GLA_DK:(hh + 1) * 2 * GLA_DK], nt,
                                       preferred_element_type=F32)]
                    for hh in range(hps)] for loc in loc_b]
        if ia is not None:
            bcum = []
            for d in range(2):
                soft = jnp.log(1.0 + jnp.exp2(jnp.abs(pre[d]) * (-LOG2_E)))
                la = jnp.minimum(pre[d], 0.0) * (LOG2_E / GLA_TAU) - soft * (LOG2_E / GLA_TAU)
                hi = la.astype(BF16)
                mid = (la - hi.astype(F32)).astype(BF16)
                bcum.append(_bdot(cums[d], hi) + _bdot(cums[d], mid))
        if ib is not None:
            for g, rows in enumerate(rows_b):
                for hh in range(hps):
                    vs = slice(hh * GLA_DV, (hh + 1) * GLA_DV)
                    sd0, sd1, sx = raw[g][hh]
                    sc = jnp.where(masks[0], sd0, 0.0) + jnp.where(masks[1], sd1, 0.0) + sx
                    oacc[rows, vs] = _bdot(sc.astype(BF16), v_ref[rows, vs])
        if ia is not None:
            q = q_ref[rows_a, :].astype(F32)
            k = k_ref[rows_a, :].astype(F32)
            nc = blk // c
            crow = lambda a, n: a[n * c:(n + 1) * c]
            cat = lambda parts: jnp.concatenate(parts, axis=0)
            zero = jnp.zeros((c, nd), BF16)
            qx, kx = [], []
            for d in range(2):
                edge = c - 1 if d == 0 else 0
                tot = [bcum[d][n * c + edge:n * c + edge + 1, :] for n in range(nc)]
                dec = [jnp.exp2(x) for x in tot]
                btot = cat([jnp.broadcast_to(x, (c, nd)) for x in tot])
                qe = (q * scale) * jnp.exp2(bcum[d])
                kt = k * jnp.exp2(btot - bcum[d])
                pp_q[slot_a, d] = qe.astype(BF16)
                pp_k[slot_a, d] = (k * jnp.exp2(-bcum[d])).astype(BF16)
                first = [(n % 2 == 0) == (d == 0) for n in range(nc)]
                q2 = cat([crow(qe, n) if first[n] else crow(qe, n) * dec[n ^ 1] for n in range(nc)])
                k2 = cat([crow(kt, n) * dec[n ^ 1] if first[n] else crow(kt, n) for n in range(nc)])
                q2_s[d, rows_a, :] = q2.astype(BF16)
                k2_s[d, rows_a, :] = k2.astype(BF16)
                bt_s[d, rows_a, :] = cat([jnp.broadcast_to(tot[n - n % 2] + tot[n - n % 2 + 1], (c, nd))
                                          for n in range(nc)])
                qeb, ktb = qe.astype(BF16), kt.astype(BF16)
                qx.append(cat([zero if first[n] else crow(qeb, n) for n in range(nc)]))
                kx.append(cat([crow(ktb, n) if first[n] else zero for n in range(nc)]))
            for hh in range(hps):
                ks = slice(hh * GLA_DK, (hh + 1) * GLA_DK)
                xs = slice(hh * 2 * GLA_DK, (hh + 1) * 2 * GLA_DK)
                pp_qx[slot_a, :, xs] = jnp.concatenate([qx[0][:, ks], qx[1][:, ks]], axis=1)
                pp_kx[slot_a, :, xs] = jnp.concatenate([kx[0][:, ks], kx[1][:, ks]], axis=1)

    n_blocks = t // blk
    step(0, 0, None, None)
    for ib in range(n_blocks - 1):
        step(ib + 1, (ib + 1) % 2, ib, ib % 2)
    step(None, None, n_blocks - 1, (n_blocks - 1) % 2)

    pr = 2 * c
    n_prs = t // pr

    def pair_step(p0, d, hh):
        rows = pl.ds(p0, pr)
        ks = slice(hh * GLA_DK, (hh + 1) * GLA_DK)
        vs = slice(hh * GLA_DV, (hh + 1) * GLA_DV)
        s_t = st[d, hh]
        o = lax.dot_general(q2_s[d, rows, ks], s_t.astype(BF16), (((1,), (1,)), ((), ())),
                            preferred_element_type=F32)
        upd = lax.dot_general(v_ref[rows, vs], k2_s[d, rows, ks], (((0,), (0,)), ((), ())),
                              preferred_element_type=F32)
        st[d, hh] = s_t * jnp.exp2(bt_s[d, pl.ds(p0, 1), ks]) + upd
        return o

    for jj in range(n_prs):
        for hh in range(hps):
            vs = slice(hh * GLA_DV, (hh + 1) * GLA_DV)
            for p0, d in ((jj * pr, 0), ((n_prs - 1 - jj) * pr, 1)):
                oacc[pl.ds(p0, pr), vs] += pair_step(p0, d, hh)

    if grid_rows is None:
        ob = min(GLA_BLOCK, t)
    else:
        ob = GLA_OUT_COLS * grid_rows

    for ib in range(t // ob):
        rows = slice(ib * ob, (ib + 1) * ob)
        parts = []
        for hh in range(hps):
            vs = slice(hh * GLA_DV, (hh + 1) * GLA_DV)
            parts.append((_rms(oacc[rows, vs], gn_ref[...]) * g_ref[rows, vs].astype(F32)).astype(BF16))
        on = jnp.concatenate(parts, axis=1)
        if grid_rows is None:
            on_ref[rows, :] = on
        else:
            on = _bdot(perm_ref[...], on).astype(BF16)
            for r in range(grid_rows):
                dst = r * GRID_W + ib * GLA_OUT_COLS
                on_ref[dst:dst + GLA_OUT_COLS, :] = on[r * GLA_OUT_COLS:(r + 1) * GLA_OUT_COLS]

    if want_final:
        for d in range(2):
            for hh in range(hps):
                fin_ref[d, hh] = st[d, hh].T


def _gla(q, k, v, g, lr, w2p, b2, gn, s0, *, want_final, hps, col_major, name):
    b, t, _ = q.shape
    ng = GLA_HEADS // hps
    has_s0 = s0 is not None
    grid_rows = t // GRID_W if col_major else None
    in_specs = [
        pl.BlockSpec((None, t, hps * GLA_DK), lambda bb, h: (bb, 0, h)),
        pl.BlockSpec((None, t, hps * GLA_DK), lambda bb, h: (bb, 0, h)),
        pl.BlockSpec((None, t, hps * GLA_DV), lambda bb, h: (bb, 0, h)),
        pl.BlockSpec((None, t, hps * GLA_DV), lambda bb, h: (bb, 0, h)),
        pl.BlockSpec((None, t, LR_PAD), lambda bb, h: (bb, 0, 0)),
        pl.BlockSpec((2, LR_PAD, hps * GLA_DK), lambda bb, h: (0, 0, h)),
        pl.BlockSpec((2, hps * GLA_DK), lambda bb, h: (0, h)),
        pl.BlockSpec((1, GLA_DV), lambda bb, h: (0, 0)),
    ]
    args = [q, k, v, g, lr, w2p, b2, gn]
    if col_major:
        ob = GLA_OUT_COLS * grid_rows
        in_specs.append(pl.BlockSpec((ob, ob), lambda bb, h: (0, 0)))
        args.append(jnp.asarray(_perm_matrix(grid_rows, GLA_OUT_COLS).T, BF16))
    state_spec = pl.BlockSpec((None, 2, hps, GLA_DK, GLA_DV), lambda bb, h: (bb, 0, h, 0, 0))
    if has_s0:
        in_specs.append(state_spec)
        args.append(s0)
    out_specs = [pl.BlockSpec((None, t, hps * GLA_DV), lambda bb, h: (bb, 0, h))]
    out_shape = [jax.ShapeDtypeStruct((b, t, GLA_HEADS * GLA_DV), BF16)]
    if want_final:
        out_specs.append(state_spec)
        out_shape.append(jax.ShapeDtypeStruct((b, 2, GLA_HEADS, GLA_DK, GLA_DV), F32))
    return pl.pallas_call(
        functools.partial(_gla_kernel, t=t, hps=hps, has_s0=has_s0, want_final=want_final,
                          grid_rows=grid_rows),
        grid=(b, ng),
        in_specs=in_specs,
        out_specs=out_specs,
        out_shape=out_shape,
        scratch_shapes=[
            pltpu.VMEM((t, hps * GLA_DV), F32),
            pltpu.VMEM((2, hps, GLA_DV, GLA_DK), F32),
            pltpu.VMEM((2, t, hps * GLA_DK), BF16),
            pltpu.VMEM((2, t, hps * GLA_DK), BF16),
            pltpu.VMEM((2, t, hps * GLA_DK), F32),
            pltpu.VMEM((2, 2, min(GLA_BLOCK, t), hps * GLA_DK), BF16),
            pltpu.VMEM((2, 2, min(GLA_BLOCK, t), hps * GLA_DK), BF16),
            pltpu.VMEM((2, min(GLA_BLOCK, t), 2 * hps * GLA_DK), BF16),
            pltpu.VMEM((2, min(GLA_BLOCK, t), 2 * hps * GLA_DK), BF16),
        ],
        compiler_params=_cparams(("parallel", "parallel")),
        name=name,
    )(*args)


def _gelu_tanh(x):
    return x * (0.5 * (1.0 + jnp.tanh(0.7978845608028654 * (x + 0.044715 * (x * x * x)))))


def _run_staggered(tiles, stagger):
    n_st = len(tiles[0])
    for clock in range(n_st + stagger * (len(tiles) - 1)):
        for i, st in enumerate(tiles):
            k = clock - stagger * i
            if 0 <= k < n_st:
                st[k]()


def _tile_specs(n_tiles, tm, t, per_batch_mod):
    tps = TILES_PER_STEP
    assert n_tiles % tps == 0 and (t % (tps * tm) == 0 or not per_batch_mod)
    spec = pl.BlockSpec((tps, tm, D_MODEL), lambda i: (i, 0, 0))
    const = lambda shape: pl.BlockSpec(shape, lambda i: (0,) * len(shape))
    if per_batch_mod:
        mod_spec = pl.BlockSpec((None, N_MOD, D_MODEL), lambda i: ((i * tps * tm) // t + 1, 0, 0))
    else:
        mod_spec = pl.BlockSpec((None, N_MOD, D_MODEL), lambda i: (0, 0, 0))
    return spec, mod_spec, const


def _merge_kernel(x_ref, hf_ref, hb_ref, zg_ref, ga_ref, gb_ref, on_ref, mod_ref, ng_ref,
                  unperm_ref, lup_ref, gup_ref, wo_ref, o_ref):
    def stages(i):
        ld = lambda r: r[i].astype(F32)
        v = {}

        def s0():
            v['ya'] = ((ld(hf_ref) + ld(hb_ref)) * _gelu_tanh(ld(zg_ref))).astype(BF16)

        def s1():
            ya = _bdot(unperm_ref[...], v['ya']).astype(BF16)
            v['y_a'] = _bdot(ya, lup_ref[...])

        def s2():
            v['y_b'] = _bdot(on_ref[i], gup_ref[...])

        def s3():
            v['mm'] = (_sigmoid(ld(ga_ref)) * v['y_a'] + _sigmoid(ld(gb_ref)) * v['y_b']).astype(BF16)

        def s4():
            v['m'] = _bdot(v['mm'], wo_ref[...])

        def s5():
            o_ref[i] = x_ref[i] + mod_ref[2:3, :] * _rms(v['m'], ng_ref[1:2, :])

        return [s0, s1, s2, s3, s4, s5]

    _run_staggered([stages(i) for i in range(x_ref.shape[0])], MERGE_STAGGER)


def _merge(x, hf, hb, zg, ga, gb, on, mod, norm_g, lru_up, gla_up, w_out, *, per_batch_mod, name):
    b, t, _ = x.shape
    tm = _lru_chunk(t)
    n_tiles = b * t // tm
    spec, mod_spec, const = _tile_specs(n_tiles, tm, t, per_batch_mod)
    tiled = lambda a: a.reshape(n_tiles, tm, a.shape[-1])
    out = pl.pallas_call(
        _merge_kernel,
        grid=(n_tiles // TILES_PER_STEP,),
        in_specs=[spec] * 7 + [mod_spec, const((4, D_MODEL)), const((tm, tm))]
        + [const((D_MODEL, D_MODEL))] * 3,
        out_specs=spec,
        out_shape=jax.ShapeDtypeStruct((n_tiles, tm, D_MODEL), F32),
        compiler_params=_cparams(("parallel",)),
        name=name,
    )(*(tiled(a) for a in (x, hf, hb, zg, ga, gb, on)), mod, norm_g,
      jnp.asarray(_interleave_matrix(tm).T, BF16), lru_up, gla_up, w_out)
    return out.reshape(b, t, D_MODEL)


def _ffn_kernel(x_ref, mod_ref, ng_ref, w1_ref, w2_ref, o_ref):
    fs = D_FF // FFN_SPLIT

    def stages(i):
        v = {}

        def norm():
            x = x_ref[i]
            v['h'] = (_rms(x, ng_ref[2:3, :]) * (1.0 + mod_ref[4:5, :]) + mod_ref[3:4, :]).astype(BF16)

        def hidden(kf):
            hid = jnp.maximum(_bdot(v['h'], w1_ref[:, kf * fs:(kf + 1) * fs]), 0.0)
            part = _bdot((hid * hid).astype(BF16), w2_ref[kf * fs:(kf + 1) * fs, :])
            v['acc'] = part if kf == 0 else v['acc'] + part

        def out():
            o_ref[i] = x_ref[i] + mod_ref[5:6, :] * _rms(v['acc'], ng_ref[3:4, :])

        return [norm] + [functools.partial(hidden, kf) for kf in range(FFN_SPLIT)] + [out]

    _run_staggered([stages(i) for i in range(x_ref.shape[0])], FFN_STAGGER)


def _ffn(x, mod, norm_g, w1, w2, *, per_batch_mod, name):
    b, t, _ = x.shape
    tm = _lru_chunk(t)
    n_tiles = b * t // tm
    spec, mod_spec, const = _tile_specs(n_tiles, tm, t, per_batch_mod)
    out = pl.pallas_call(
        _ffn_kernel,
        grid=(n_tiles // TILES_PER_STEP,),
        in_specs=[spec, mod_spec, const((4, D_MODEL)), const((D_MODEL, D_FF)), const((D_FF, D_MODEL))],
        out_specs=spec,
        out_shape=jax.ShapeDtypeStruct((n_tiles, tm, D_MODEL), F32),
        compiler_params=_cparams(("parallel",)),
        name=name,
    )(x.reshape(n_tiles, tm, D_MODEL), mod, norm_g, w1, w2)
    return out.reshape(b, t, D_MODEL)


def _blockdiag_slabs(w):
    per = LRU_COLB // LRU_BW
    w = w.reshape(2, LRU_NCOLB, per, LRU_BW, LRU_BW)
    eye = jnp.eye(per, dtype=w.dtype)
    slab = jnp.einsum('dspij,pq->dspiqj', w, eye)
    return slab.reshape(2, LRU_NCOLB, LRU_COLB, LRU_COLB).astype(BF16)


def _layer_group(x, mod, p, lru_h0, gla_s0, *, latent, want_state, tag):
    b, t, _ = x.shape
    per_batch = latent
    ga, gb, zg, zx = _inproj(x, mod, p['norm_g'], p['w_rm'], (None,) * 4,
                             per_batch_mod=per_batch, n_natural=2, name=f"inproj_rm_{tag}")
    period = GRID_W if latent else t
    (q, k, v, g, lr), hf, hb, s_lru = _cm_lru(
        x, mod, p['norm_g'], p['w_cm'], (None, None, None, _silu, None), zx, lru_h0, p['conv_w'],
        p['conv_b'], p['wa_bd'], p['wx_bd'], p['lru_ba'], p['lru_bx'], p['lru_L'], per_batch_mod=per_batch,
        order='col_major' if latent else 'natural', period=period, name=f"cm_lru_{tag}")
    res = _gla(q, k, v, g, lr, p['w2p'], p['gla_b2'], p['gla_norm_g'], gla_s0,
               want_final=want_state, hps=2, col_major=latent, name=f"gla_{tag}")
    on = res[0]
    s_gla = res[1] if want_state else None
    x1 = _merge(x, hf, hb, zg, ga, gb, on, mod, p['norm_g'], p['lru_up'], p['gla_up'], p['w_out'],
                per_batch_mod=per_batch, name=f"merge_{tag}")
    y = _ffn(x1, mod, p['norm_g'], p['mlp_w1'], p['mlp_w2'], per_batch_mod=per_batch, name=f"ffn_{tag}")
    return y, s_lru, s_gla


def kernel(x_prompt, x_sample, state_lru, state_gla, c, c_ctx, w_mod, b_mod, norm_g, w_in, conv_w, conv_b,
           lru_wa, lru_ba, lru_wx, lru_bx, lru_L, lru_up, gla_w2, gla_b2, gla_norm_g, gla_up, w_out,
           mlp_w1, mlp_w2):
    depth = w_in.shape[0]
    assert depth == 1
    dec_b = x_sample.shape[0]
    b0 = x_prompt.shape[0]
    xp, xs = x_prompt, x_sample
    lru_states, gla_states = [], []
    mod_rows = -(-(1 + dec_b) // V7X_SUBLANES) * V7X_SUBLANES
    cc = jnp.zeros((mod_rows, D_MODEL), F32).at[0].set(c_ctx).at[1:1 + dec_b].set(c)
    for l in range(depth):
        wl = w_in[l]
        o = np.cumsum((0, D_MODEL, D_MODEL, 512, 512, 1024, 1024, 2 * GLA_RANK, D_MODEL, D_MODEL))
        col = lambda i: wl[:, o[i]:o[i + 1]].astype(BF16)
        lr_w = jnp.pad(col(6), ((0, 0), (0, LR_PAD - 2 * GLA_RANK)))
        w2 = gla_w2[l].astype(BF16)
        w2p = jnp.zeros((2, LR_PAD, GLA_HEADS * GLA_DK), BF16)
        w2p = w2p.at[0, 0:GLA_RANK].set(w2[0]).at[1, GLA_RANK:2 * GLA_RANK].set(w2[1])
        p = {
            'norm_g': norm_g[l],
            'w_rm': [col(7), col(8), col(1), col(0)],
            'w_cm': [col(2), col(3), col(4), col(5), lr_w],
            'conv_w': conv_w[l], 'conv_b': conv_b[l].reshape(1, D_MODEL),
            'wa_bd': _blockdiag_slabs(0.5 * lru_wa[l]), 'wx_bd': _blockdiag_slabs(0.5 * lru_wx[l]),
            'lru_ba': lru_ba[l], 'lru_bx': lru_bx[l], 'lru_L': lru_L[l],
            'lru_up': lru_up[l].astype(BF16),
            'w2p': w2p, 'gla_b2': gla_b2[l], 'gla_norm_g': gla_norm_g[l].reshape(1, GLA_DV),
            'gla_up': gla_up[l].astype(BF16), 'w_out': w_out[l].astype(BF16),
            'mlp_w1': mlp_w1[l].astype(BF16), 'mlp_w2': mlp_w2[l].astype(BF16),
        }
        mod = _modulation(cc, w_mod[l], b_mod[l])
        h0_ctx = jnp.zeros((b0, 2, D_MODEL), F32)
        xp, s_lru, s_gla = _layer_group(xp, mod, p, h0_ctx, None, latent=False, want_state=True,
                                        tag="ctx")
        lru_states.append(s_lru)
        gla_states.append(s_gla)
        xs, _, _ = _layer_group(xs, mod, p, state_lru[:, l], state_gla[:, l], latent=True,
                                want_state=False, tag="lat")
    return (xp, xs, jnp.stack(lru_states, axis=1), jnp.stack(gla_states, axis=1))
```

```python
import functools

import numpy as np
import jax
import jax.numpy as jnp
from jax import lax
from jax.experimental import pallas as pl
from jax.experimental.pallas import tpu as pltpu

F32 = jnp.float32
BF16 = jnp.bfloat16

D_MODEL = 1024
GRID_W = 64
LRU_BLOCKS = 16
LRU_BW = D_MODEL // LRU_BLOCKS
LRU_C = 8.0
GLA_HEADS = 4
GLA_DK = 128
GLA_DV = 256
GLA_RANK = 16
GLA_TAU = 16.0
GLA_CHUNK = 64
D_FF = 4 * D_MODEL
N_MOD = 6
EPS = 1e-6
LOG2_E = 1.4426950408889634

V7X_LANES = 128
V7X_SUBLANES = 8
V7X_MXU_DIM = 256
V7X_VMEM_BYTES = 64 * 1024 * 1024
VMEM_LIMIT = V7X_VMEM_BYTES - 8 * 1024 * 1024

LRU_COLB = V7X_MXU_DIM
LRU_NCOLB = D_MODEL // LRU_COLB
LR_PAD = V7X_LANES
COL_BLOCK = 16
GLA_BLOCK = V7X_MXU_DIM
GLA_SCORE_BLOCK = 2 * GLA_CHUNK
GLA_OUT_COLS = 16
LRU_CHUNK = 256
TILES_PER_STEP = 2
MERGE_STAGGER = 2
INPROJ_STAGGER = 2
FFN_SPLIT = 4
FFN_STAGGER = 3
LRU_SCAN_CUTS = 1
CM_PIECE = 256


def _cparams(sem):
    return pltpu.CompilerParams(dimension_semantics=sem, vmem_limit_bytes=VMEM_LIMIT)


def _rms(x, g):
    ms = jnp.mean(x * x, axis=-1, keepdims=True)
    return x * lax.rsqrt(ms + EPS) * g


def _sigmoid(x):
    return 0.5 * jnp.tanh(0.5 * x) + 0.5


def _softplus(x):
    return jnp.maximum(x, 0.0) + jnp.log1p(jnp.exp(-jnp.abs(x)))


def _bdot(a, b):
    return jnp.dot(a, b, preferred_element_type=F32)


def _mod_kernel(c_ref, w_ref, b_ref, o_ref):
    c = c_ref[...]
    s = (c * _sigmoid(c)).astype(BF16)
    o_ref[...] = _bdot(s, w_ref[...].astype(BF16)) + b_ref[...]


def _modulation(cc, w_mod, b_mod):
    rows = cc.shape[0]
    return pl.pallas_call(
        _mod_kernel,
        grid=(N_MOD,),
        in_specs=[
            pl.BlockSpec((rows, D_MODEL), lambda n: (0, 0)),
            pl.BlockSpec((D_MODEL, D_MODEL), lambda n: (0, n)),
            pl.BlockSpec((1, D_MODEL), lambda n: (0, n)),
        ],
        out_specs=pl.BlockSpec((rows, D_MODEL), lambda n: (0, n)),
        out_shape=jax.ShapeDtypeStruct((rows, N_MOD * D_MODEL), F32),
        compiler_params=_cparams(("arbitrary",)),
        name="modulation",
    )(cc, w_mod, b_mod.reshape(1, N_MOD * D_MODEL)).reshape(rows, N_MOD, D_MODEL)


def _tok_view(a, grid_tile):
    if grid_tile is None:
        return a
    b, t, n = a.shape
    rows = t // GRID_W
    return a.reshape(b, rows, GRID_W // COL_BLOCK, COL_BLOCK, n)


def _tok_spec(n, tm, grid_tile):
    if grid_tile is None:
        return pl.BlockSpec((None, tm, n), lambda b, j: (b, j, 0))
    rows = grid_tile
    return pl.BlockSpec((None, rows, None, COL_BLOCK, n), lambda b, j: (b, 0, j, 0, 0))


def _const_spec(shape):
    nd = len(shape)
    return pl.BlockSpec(shape, lambda b, j: (0,) * nd)


def _mod_spec(per_batch):
    if per_batch:
        return pl.BlockSpec((None, N_MOD, D_MODEL), lambda b, j: (b + 1, 0, 0))
    return pl.BlockSpec((None, N_MOD, D_MODEL), lambda b, j: (0, 0, 0))


def _perm_matrix(rows, cols):
    n = rows * cols
    p = np.zeros((n, n), np.float32)
    for r in range(rows):
        for c in range(cols):
            p[c * rows + r, r * cols + c] = 1.0
    return p


def _silu(x):
    return x * _sigmoid(x)


def _inproj_pieces(x_ref, mod_ref, ng_ref, perm_ref, w_refs, o_refs, *, n_natural, acts, n_piece, tile=None):
    v = {}

    def norm():
        x = (x_ref[...] if tile is None else x_ref[tile]).reshape(-1, D_MODEL)
        h = _rms(x, ng_ref[0:1, :]) * (1.0 + mod_ref[1:2, :]) + mod_ref[0:1, :]
        v['hb'] = v['hp'] = h.astype(BF16)
        if perm_ref is not None:
            v['hp'] = _bdot(perm_ref[...], v['hb']).astype(BF16)

    pieces = [norm]
    for i, (w_ref, o_ref) in enumerate(zip(w_refs, o_refs)):
        n = w_ref.shape[1]
        width = min(n, n_piece)
        for c0 in range(0, n, width):
            def piece(i=i, w_ref=w_ref, o_ref=o_ref, cs=slice(c0, c0 + width)):
                z = _bdot(v['hb'] if i < n_natural else v['hp'], w_ref[:, cs])
                if acts[i] is not None:
                    z = acts[i](z)
                if tile is None:
                    o_ref[:, cs] = z.astype(o_ref.dtype)
                else:
                    o_ref[tile, :, cs] = z.astype(o_ref.dtype)
            pieces.append(piece)
    return pieces


def _inproj_kernel(*refs, n_out, permute, n_natural, acts):
    x_ref, mod_ref, ng_ref = refs[:3]
    k = 4 if permute else 3
    perm_ref = refs[3] if permute else None
    _run_staggered([_inproj_pieces(x_ref, mod_ref, ng_ref, perm_ref, refs[k:k + n_out], refs[k + n_out:],
                                   n_natural=n_natural, acts=acts, n_piece=D_MODEL, tile=i)
                    for i in range(x_ref.shape[0])], INPROJ_STAGGER)


def _inproj_setup(x, weights, order):
    b, t, _ = x.shape
    if order == 'col_major':
        rows = t // GRID_W
        tm = rows * COL_BLOCK
        return _tok_view(x, rows), _tok_spec(D_MODEL, tm, rows), tm, _perm_matrix(rows, COL_BLOCK)
    tm = _lru_chunk(t)
    perm = _interleave_matrix(tm) if order == 'interleaved' else None
    return x, _tok_spec(D_MODEL, tm, None), tm, perm


def _inproj(x, mod, norm_g, weights, acts, *, per_batch_mod, name):
    b, t, _ = x.shape
    tm = _lru_chunk(t)
    n_tiles = b * t // tm
    spec, mod_spec, const = _tile_specs(n_tiles, tm, t, per_batch_mod)
    outs = pl.pallas_call(
        functools.partial(_inproj_kernel, n_out=len(weights), permute=False, n_natural=0,
                          acts=tuple(acts)),
        grid=(n_tiles // TILES_PER_STEP,),
        in_specs=[spec, mod_spec, const((4, D_MODEL))] + [const(w.shape) for w in weights],
        out_specs=[pl.BlockSpec((TILES_PER_STEP, tm, w.shape[1]), lambda i: (i, 0, 0)) for w in weights],
        out_shape=[jax.ShapeDtypeStruct((n_tiles, tm, w.shape[1]), BF16) for w in weights],
        compiler_params=_cparams(("parallel",)),
        name=name,
    )(x.reshape(n_tiles, tm, D_MODEL), mod, norm_g, *weights)
    return [o.reshape(b, t, o.shape[-1]) for o in outs]


def _lru_chunk(t):
    return min(LRU_CHUNK, t)


def _interleave_matrix(tc):
    return _perm_matrix(V7X_SUBLANES, tc // V7X_SUBLANES)


def _lru_pieces(zf_ref, zb_ref, rf0, rb0, il_ref, cw_ref, cb_ref, wa_ref, wx_ref, ba_ref, bx_ref, l_ref,
                hf_ref, hb_ref, fin_ref, a_s, u_s, carry_s, zi_s, *, tc, period):
    ns = V7X_SUBLANES
    sl = tc // ns
    sub_c = lax.broadcasted_iota(jnp.int32, (ns, LRU_COLB), 0)
    prev_ok = jnp.bitwise_and(sub_c * sl, period - 1) != 0
    next_ok = jnp.bitwise_and((sub_c + 1) * sl, period - 1) != 0

    def reorder(d, z_ref, r0):
        zi_s[d] = _bdot(il_ref[...], z_ref[r0:r0 + tc, :]).astype(BF16)

    def gates(d, cb):
        cs = slice(cb * LRU_COLB, (cb + 1) * LRU_COLB)
        x = zi_s[d, :, cs].astype(F32)
        e_prev =jnp.where(prev_ok, pltpu.roll(x[tc - ns:], 1, 0), 0.0)
        e_next0 = jnp.where(next_ok, pltpu.roll(x[:ns], ns - 1, 0), 0.0)
        e_next1 = jnp.where(next_ok, pltpu.roll(x[ns:2 * ns], ns - 1, 0), 0.0)
        xm1 = jnp.concatenate([e_prev, x[:tc - ns]], axis=0)
        xp1 = jnp.concatenate([x[ns:], e_next0], axis=0)
        xp2 = jnp.concatenate([x[2 * ns:], e_next0, e_next1], axis=0)
        xc = (cb_ref[0:1, cs] + xm1 * cw_ref[0:1, cs] + x * cw_ref[1:2, cs]
              + xp1 * cw_ref[2:3, cs] + xp2 * cw_ref[3:4, cs])
        xcb = xc.astype(BF16)
        tr = jnp.tanh(_bdot(xcb, wa_ref[d, cb]) + 0.5 * ba_ref[d:d + 1, cs])
        ti = jnp.tanh(_bdot(xcb, wx_ref[d, cb]) + 0.5 * bx_ref[d:d + 1, cs])
        k2 = (-0.5 * LRU_C * LOG2_E) * _softplus(-l_ref[d:d + 1, cs])
        a = jnp.exp2(tr * k2 + k2)
        y = 1.0 - a * a
        u = jnp.where(y > 0.0, y * lax.rsqrt(y), 0.0) * ((0.5 * ti + 0.5) * xc)
        a_s[d, :, cs] = a
        u_s[d, :, cs] = u

    state = {}
    n_cut = LRU_SCAN_CUTS

    def pass1(part):
        if part == 0:
            zeros = jnp.zeros((ns, D_MODEL), F32)
            ones = jnp.ones((ns, D_MODEL), F32)
            state['c'] = (zeros, ones, zeros, ones)
        hf, af, hb, ab = state['c']
        for i in range(part * sl // n_cut, (part + 1) * sl // n_cut):
            rf = slice(i * ns, (i + 1) * ns)
            rb = slice((sl - 1 - i) * ns, (sl - i) * ns)
            a = a_s[0, rf, :]
            hf = a * hf + u_s[0, rf, :]
            af = a * af
            u_s[0, rf, :] = hf
            a_s[0, rf, :] = af
            a = a_s[1, rb, :]
            hb = a * hb + u_s[1, rb, :]
            ab = a * ab
            u_s[1, rb, :] = hb
            a_s[1, rb, :] = ab
        state['c'] = (hf, af, hb, ab)

    def chain():
        hf, af, hb, ab = state['c']
        sub = lax.broadcasted_iota(jnp.int32, (ns, D_MODEL), 0)

        def sublane_scan(a, u, forward):
            for s in (1, 2, 4):
                if forward:
                    m = sub >= s
                    shift = s
                else:
                    m = sub < ns - s
                    shift = ns - s
                a_sh = jnp.where(m, pltpu.roll(a, shift, 0), 1.0)
                u_sh = jnp.where(m, pltpu.roll(u, shift, 0), 0.0)
                u = a * u_sh + u
                a = a * a_sh
            return a, u

        cin_f = carry_s[0:1, :]
        cin_b = carry_s[1:2, :]
        af, hf = sublane_scan(af, hf, True)
        ab, hb = sublane_scan(ab, hb, False)
        fin_f = hf + af * cin_f
        fin_b = hb + ab * cin_b
        enter_f = jnp.where(sub == 0, cin_f, pltpu.roll(fin_f, 1, 0))
        enter_b = jnp.where(sub == ns - 1, cin_b, pltpu.roll(fin_b, ns - 1, 0))
        out_f = fin_f[ns - 1:ns, :]
        out_b = fin_b[0:1, :]
        carry_s[0:1, :] = out_f
        carry_s[1:2, :] = out_b
        fin_ref[0:1, :] = out_f
        fin_ref[1:2, :] = out_b
        state['e'] = (jnp.concatenate([enter_f, enter_f], axis=0), jnp.concatenate([enter_b, enter_b], axis=0))

    def pass2(part):
        pk = 2 * ns
        enter_f2, enter_b2 = state['e']
        for i in range(part * (sl // 2) // n_cut, (part + 1) * (sl // 2) // n_cut):
            rr = slice(i * pk, (i + 1) * pk)
            hf_ref[rf0 + i * pk:rf0 + (i + 1) * pk, :] = (
                u_s[0, rr, :] + a_s[0, rr, :] * enter_f2).astype(hf_ref.dtype)
            hb_ref[rb0 + i * pk:rb0 + (i + 1) * pk, :] = (
                u_s[1, rr, :] + a_s[1, rr, :] * enter_b2).astype(hb_ref.dtype)

    scan = ([functools.partial(pass1, part) for part in range(n_cut)] + [chain]
            + [functools.partial(pass2, part) for part in range(n_cut)])

    pieces = [functools.partial(reorder, 0, zf_ref, rf0), functools.partial(reorder, 1, zb_ref, rb0)]
    pieces += [functools.partial(gates, d, cb) for d in range(2) for cb in range(LRU_NCOLB)]
    return pieces + scan


def _interleave(major, minor):
    out, done = [], 0
    for i, p in enumerate(major):
        want = -(-(i + 1) * len(minor) // len(major))
        out.extend(minor[done:want])
        done = want
        out.append(p)
    return out


def _cm_lru_kernel(*refs, n_cm, permute, acts, tc, period, cps):
    x_ref, mod_ref, ng_ref = refs[:3]
    k = 4 if permute else 3
    perm_ref = refs[3] if permute else None
    w_refs = refs[k:k + n_cm]
    k += n_cm
    zf_ref, zb_ref, h0_ref, il_ref, cw_ref, cb_ref, wa_ref, wx_ref, ba_ref, bx_ref, l_ref = refs[k:k + 11]
    k += 11
    o_refs = refs[k:k + n_cm]
    k += n_cm
    hf_ref, hb_ref, fin_ref, a_s, u_s, carry_s, zi_s = refs[k:]

    @pl.when(pl.program_id(1) == 0)
    def _():
        carry_s[...] = h0_ref[...]

    cm = _inproj_pieces(x_ref, mod_ref, ng_ref, perm_ref, w_refs, o_refs, n_natural=0, acts=acts,
                        n_piece=CM_PIECE)
    lru = []
    for s in range(cps):
        lru += _lru_pieces(zf_ref, zb_ref, s * tc, (cps - 1 - s) * tc, il_ref, cw_ref, cb_ref, wa_ref, wx_ref,
                           ba_ref, bx_ref, l_ref, hf_ref, hb_ref, fin_ref, a_s, u_s, carry_s, zi_s,
                           tc=tc, period=period)
    cm[0]()
    for piece in _interleave(lru, cm[1:]):
        piece()


def _cm_lru(x, mod, norm_g, weights, acts, zx, h0, conv_w, conv_b, wa_bd, wx_bd, ba, bx, lru_l, *,
            per_batch_mod, order, period, name):
    b, t, _ = x.shape
    x_in, x_spec, tm, perm = _inproj_setup(x, weights, order)
    extra = [] if perm is None else [jnp.asarray(perm, BF16)]
    n_steps = t // tm
    tc = _lru_chunk(t)
    cps = t // tc // n_steps
    assert cps >= 1 and cps * tc * n_steps == t
    assert period % (tc // V7X_SUBLANES) == 0 and tc % period == 0
    blk = cps * tc
    res = pl.pallas_call(
        functools.partial(_cm_lru_kernel, n_cm=len(weights), permute=perm is not None, acts=tuple(acts),
                          tc=tc, period=period, cps=cps),
        grid=(b, n_steps),
        in_specs=[x_spec, _mod_spec(per_batch_mod), _const_spec((4, D_MODEL))]
        + [_const_spec(e.shape) for e in extra] + [_const_spec(w.shape) for w in weights]
        + [
            pl.BlockSpec((None, blk, D_MODEL), lambda bb, j: (bb, j, 0)),
            pl.BlockSpec((None, blk, D_MODEL), lambda bb, j: (bb, n_steps - 1 - j, 0)),
            pl.BlockSpec((None, 2, D_MODEL), lambda bb, j: (bb, 0, 0)),
            _const_spec((tc, tc)), _const_spec((4, D_MODEL)), _const_spec((1, D_MODEL)),
            _const_spec((2, LRU_NCOLB, LRU_COLB, LRU_COLB)), _const_spec((2, LRU_NCOLB, LRU_COLB, LRU_COLB)),
            _const_spec((2, D_MODEL)), _const_spec((2, D_MODEL)), _const_spec((2, D_MODEL)),
        ],
        out_specs=[_tok_spec(w.shape[1], tm, None) for w in weights] + [
            pl.BlockSpec((None, blk, D_MODEL), lambda bb, j: (bb, j, 0)),
            pl.BlockSpec((None, blk, D_MODEL), lambda bb, j: (bb, n_steps - 1 - j, 0)),
            pl.BlockSpec((None, 2, D_MODEL), lambda bb, j: (bb, 0, 0)),
        ],
        out_shape=[jax.ShapeDtypeStruct((b, t, w.shape[1]), BF16) for w in weights] + [
            jax.ShapeDtypeStruct((b, t, D_MODEL), BF16),
            jax.ShapeDtypeStruct((b, t, D_MODEL), BF16),
            jax.ShapeDtypeStruct((b, 2, D_MODEL), F32),
        ],
        scratch_shapes=[
            pltpu.VMEM((2, tc, D_MODEL), F32),
            pltpu.VMEM((2, tc, D_MODEL), F32),
            pltpu.VMEM((2, D_MODEL), F32),
            pltpu.VMEM((2, tc, D_MODEL), BF16),
        ],
        compiler_params=_cparams(("parallel", "arbitrary")),
        name=name,
    )(x_in, mod, norm_g, *extra, *weights, zx, zx, h0, jnp.asarray(_interleave_matrix(tc), BF16),
      conv_w, conv_b, wa_bd, wx_bd, ba, bx, lru_l)
    return res[:len(weights)], res[len(weights)], res[len(weights) + 1], res[len(weights) + 2]


def _gla_kernel(*refs, t, hps, has_s0, want_final, grid_rows):
    q_ref, k_ref, v_ref, g_ref, lr_ref, w2_ref, b2_ref, gn_ref = refs[:8]
    n = 8
    if grid_rows is not None:
        perm_ref = refs[n]
        n += 1
    if has_s0:
        s0_ref = refs[n]
        n += 1
    on_ref = refs[n]
    n += 1
    if want_final:
        fin_ref = refs[n]
        n += 1
    oacc, st, q2_s, k2_s, bt_s, pp_q, pp_k, pp_qx, pp_kx = refs[n:]

    c = GLA_CHUNK
    n_chunks = t // c
    blk = min(GLA_BLOCK, t)
    for d in range(2):
        for hh in range(hps):
            if has_s0:
                st[d, hh] = s0_ref[d, hh].T
            else:
                st[d, hh] = jnp.zeros((GLA_DV, GLA_DK), F32)

    ti = lax.broadcasted_iota(jnp.int32, (blk, blk), 0)
    si = lax.broadcasted_iota(jnp.int32, (blk, blk), 1)
    same = (ti // c) == (si // c)
    cums = ((same & (si <= ti)).astype(BF16), (same & (si >= ti)).astype(BF16))
    sb = min(GLA_SCORE_BLOCK, blk)
    tj = lax.broadcasted_iota(jnp.int32, (sb, sb), 0)
    sj = lax.broadcasted_iota(jnp.int32, (sb, sb), 1)
    same_j = (tj // c) == (sj // c)
    masks = (same_j & (sj <= tj), same_j & (sj >= tj))
    scale = GLA_DK ** -0.5

    def step(ia, slot_a, ib, slot_b):
        nd = hps * GLA_DK
        if ia is not None:
            rows_a = pl.ds(pl.multiple_of(ia * blk, blk), blk)
            lrb = lr_ref[rows_a, :]
            pre = [_bdot(lrb, w2_ref[d]) + b2_ref[d:d + 1, :] for d in range(2)]
        if ib is not None:
            rows_b = [pl.ds(pl.multiple_of(ib * blk + g * sb, sb), sb) for g in range(blk // sb)]
            loc_b = [slice(g * sb, (g + 1) * sb) for g in range(blk // sb)]
            nt = (((1,), (1,)), ((), ()))
            raw = [[[lax.dot_general(pp_q[slot_b, d, loc, hh * GLA_DK:(hh + 1) * GLA_DK],
                                     pp_k[slot_b, d, loc, hh * GLA_DK:(hh + 1) * GLA_DK], nt,
                                     preferred_element_type=F32) for d in range(2)]
                    + [lax.dot_general(pp_qx[slot_b, loc, hh * 2 * GLA_DK:(hh + 1) * 2 * GLA_DK],
                                       pp_kx[slot_b, loc, hh * 2 * GLA_DK:(hh + 1) * 2 * GLA_DK], nt,
                                       preferred_element_type=F32)]
                    for hh in range(hps)] for loc in loc_b]
        if ia is not None:
            bcum = []
            for d in range(2):
                soft = jnp.log(1.0 + jnp.exp2(jnp.abs(pre[d]) * (-LOG2_E)))
                la = jnp.minimum(pre[d], 0.0) * (LOG2_E / GLA_TAU) - soft * (LOG2_E / GLA_TAU)
                hi = la.astype(BF16)
                mid = (la - hi.astype(F32)).astype(BF16)
                bcum.append(_bdot(cums[d], hi) + _bdot(cums[d], mid))
        if ib is not None:
            for g, rows in enumerate(rows_b):
                for hh in range(hps):
                    vs = slice(hh * GLA_DV, (hh + 1) * GLA_DV)
                    sd0, sd1, sx = raw[g][hh]
                    sc = jnp.where(masks[0], sd0, 0.0) + jnp.where(masks[1], sd1, 0.0) + sx
                    oacc[rows, vs] = _bdot(sc.astype(BF16), v_ref[rows, vs])
        if ia is not None:
            q = q_ref[rows_a, :].astype(F32)
            k = k_ref[rows_a, :].astype(F32)
            nc = blk // c
            crow = lambda a, n: a[n * c:(n + 1) * c]
            cat = lambda parts: jnp.concatenate(parts, axis=0)
            zero = jnp.zeros((c, nd), BF16)
            qx, kx = [], []
            for d in range(2):
                edge = c - 1 if d == 0 else 0
                tot = [bcum[d][n * c + edge:n * c + edge + 1, :] for n in range(nc)]
                dec = [jnp.exp2(x) for x in tot]
                btot = cat([jnp.broadcast_to(x, (c, nd)) for x in tot])
                qe = (q * scale) * jnp.exp2(bcum[d])
                kt = k * jnp.exp2(btot - bcum[d])
                pp_q[slot_a, d] = qe.astype(BF16)
                pp_k[slot_a, d] = (k * jnp.exp2(-bcum[d])).astype(BF16)
                first = [(n % 2 == 0) == (d == 0) for n in range(nc)]
                q2 = cat([crow(qe, n) if first[n] else crow(qe, n) * dec[n ^ 1] for n in range(nc)])
                k2 = cat([crow(kt, n) * dec[n ^ 1] if first[n] else crow(kt, n) for n in range(nc)])
                q2_s[d, rows_a, :] = q2.astype(BF16)
                k2_s[d, rows_a, :] = k2.astype(BF16)
                bt_s[d, rows_a, :] = cat([jnp.broadcast_to(tot[n - n % 2] + tot[n - n % 2 + 1], (c, nd))
                                          for n in range(nc)])
                qeb, ktb = qe.astype(BF16), kt.astype(BF16)
                qx.append(cat([zero if first[n] else crow(qeb, n) for n in range(nc)]))
                kx.append(cat([crow(ktb, n) if first[n] else zero for n in range(nc)]))
            for hh in range(hps):
                ks = slice(hh * GLA_DK, (hh + 1) * GLA_DK)
                xs = slice(hh * 2 * GLA_DK, (hh + 1) * 2 * GLA_DK)
                pp_qx[slot_a, :, xs] = jnp.concatenate([qx[0][:, ks], qx[1][:, ks]], axis=1)
                pp_kx[slot_a, :, xs] = jnp.concatenate([kx[0][:, ks], kx[1][:, ks]], axis=1)

    n_blocks = t // blk
    step(0, 0, None, None)
    for ib in range(n_blocks - 1):
        step(ib + 1, (ib + 1) % 2, ib, ib % 2)
    step(None, None, n_blocks - 1, (n_blocks - 1) % 2)

    pr = 2 * c
    n_prs = t // pr

    def pair_step(p0, d, hh):
        rows = pl.ds(p0, pr)
        ks = slice(hh * GLA_DK, (hh + 1) * GLA_DK)
        vs = slice(hh * GLA_DV, (hh + 1) * GLA_DV)
        s_t = st[d, hh]
        o = lax.dot_general(q2_s[d, rows, ks], s_t.astype(BF16), (((1,), (1,)), ((), ())),
                            preferred_element_type=F32)
        upd = lax.dot_general(v_ref[rows, vs], k2_s[d, rows, ks], (((0,), (0,)), ((), ())),
                              preferred_element_type=F32)
        st[d, hh] = s_t * jnp.exp2(bt_s[d, pl.ds(p0, 1), ks]) + upd
        return o

    for jj in range(n_prs):
        for hh in range(hps):
            vs = slice(hh * GLA_DV, (hh + 1) * GLA_DV)
            for p0, d in ((jj * pr, 0), ((n_prs - 1 - jj) * pr, 1)):
                oacc[pl.ds(p0, pr), vs] += pair_step(p0, d, hh)

    if grid_rows is None:
        ob = min(GLA_BLOCK, t)
    else:
        ob = GLA_OUT_COLS * grid_rows

    for ib in range(t // ob):
        rows = slice(ib * ob, (ib + 1) * ob)
        parts = []
        for hh in range(hps):
            vs = slice(hh * GLA_DV, (hh + 1) * GLA_DV)
            parts.append((_rms(oacc[rows, vs], gn_ref[...]) * g_ref[rows, vs].astype(F32)).astype(BF16))
        on = jnp.concatenate(parts, axis=1)
        if grid_rows is None:
            on_ref[rows, :] = on
        else:
            on = _bdot(perm_ref[...], on).astype(BF16)
            for r in range(grid_rows):
                dst = r * GRID_W + ib * GLA_OUT_COLS
                on_ref[dst:dst + GLA_OUT_COLS, :] = on[r * GLA_OUT_COLS:(r + 1) * GLA_OUT_COLS]

    if want_final:
        for d in range(2):
            for hh in range(hps):
                fin_ref[d, hh] = st[d, hh].T


def _gla(q, k, v, g, lr, w2p, b2, gn, s0, *, want_final, hps, col_major, name):
    b, t, _ = q.shape
    ng = GLA_HEADS // hps
    has_s0 = s0 is not None
    grid_rows = t // GRID_W if col_major else None
    in_specs = [
        pl.BlockSpec((None, t, hps * GLA_DK), lambda bb, h: (bb, 0, h)),
        pl.BlockSpec((None, t, hps * GLA_DK), lambda bb, h: (bb, 0, h)),
        pl.BlockSpec((None, t, hps * GLA_DV), lambda bb, h: (bb, 0, h)),
        pl.BlockSpec((None, t, hps * GLA_DV), lambda bb, h: (bb, 0, h)),
        pl.BlockSpec((None, t, LR_PAD), lambda bb, h: (bb, 0, 0)),
        pl.BlockSpec((2, LR_PAD, hps * GLA_DK), lambda bb, h: (0, 0, h)),
        pl.BlockSpec((2, hps * GLA_DK), lambda bb, h: (0, h)),
        pl.BlockSpec((1, GLA_DV), lambda bb, h: (0, 0)),
    ]
    args = [q, k, v, g, lr, w2p, b2, gn]
    if col_major:
        ob = GLA_OUT_COLS * grid_rows
        in_specs.append(pl.BlockSpec((ob, ob), lambda bb, h: (0, 0)))
        args.append(jnp.asarray(_perm_matrix(grid_rows, GLA_OUT_COLS).T, BF16))
    state_spec = pl.BlockSpec((None, 2, hps, GLA_DK, GLA_DV), lambda bb, h: (bb, 0, h, 0, 0))
    if has_s0:
        in_specs.append(state_spec)
        args.append(s0)
    out_specs = [pl.BlockSpec((None, t, hps * GLA_DV), lambda bb, h: (bb, 0, h))]
    out_shape = [jax.ShapeDtypeStruct((b, t, GLA_HEADS * GLA_DV), BF16)]
    if want_final:
        out_specs.append(state_spec)
        out_shape.append(jax.ShapeDtypeStruct((b, 2, GLA_HEADS, GLA_DK, GLA_DV), F32))
    return pl.pallas_call(
        functools.partial(_gla_kernel, t=t, hps=hps, has_s0=has_s0, want_final=want_final,
                          grid_rows=grid_rows),
        grid=(b, ng),
        in_specs=in_specs,
        out_specs=out_specs,
        out_shape=out_shape,
        scratch_shapes=[
            pltpu.VMEM((t, hps * GLA_DV), F32),
            pltpu.VMEM((2, hps, GLA_DV, GLA_DK), F32),
            pltpu.VMEM((2, t, hps * GLA_DK), BF16),
            pltpu.VMEM((2, t, hps * GLA_DK), BF16),
            pltpu.VMEM((2, t, hps * GLA_DK), F32),
            pltpu.VMEM((2, 2, min(GLA_BLOCK, t), hps * GLA_DK), BF16),
            pltpu.VMEM((2, 2, min(GLA_BLOCK, t), hps * GLA_DK), BF16),
            pltpu.VMEM((2, min(GLA_BLOCK, t), 2 * hps * GLA_DK), BF16),
            pltpu.VMEM((2, min(GLA_BLOCK, t), 2 * hps * GLA_DK), BF16),
        ],
        compiler_params=_cparams(("parallel", "parallel")),
        name=name,
    )(*args)


def _gelu_tanh(x):
    return x * (0.5 * (1.0 + jnp.tanh(0.7978845608028654 * (x + 0.044715 * (x * x * x)))))


def _run_staggered(tiles, stagger):
    n_st = len(tiles[0])
    for clock in range(n_st + stagger * (len(tiles) - 1)):
        for i, st in enumerate(tiles):
            k = clock - stagger * i
            if 0 <= k < n_st:
                st[k]()


def _tile_specs(n_tiles, tm, t, per_batch_mod):
    tps = TILES_PER_STEP
    assert n_tiles % tps == 0 and (t % (tps * tm) == 0 or not per_batch_mod)
    spec = pl.BlockSpec((tps, tm, D_MODEL), lambda i: (i, 0, 0))
    const = lambda shape: pl.BlockSpec(shape, lambda i: (0,) * len(shape))
    if per_batch_mod:
        mod_spec = pl.BlockSpec((None, N_MOD, D_MODEL), lambda i: ((i * tps * tm) // t + 1, 0, 0))
    else:
        mod_spec = pl.BlockSpec((None, N_MOD, D_MODEL), lambda i: (0, 0, 0))
    return spec, mod_spec, const


def _merge_kernel(x_ref, hf_ref, hb_ref, zg_ref, ga_ref, gb_ref, on_ref, mod_ref, ng_ref,
                  unperm_ref, lup_ref, gup_ref, wo_ref, o_ref):
    def stages(i):
        ld = lambda r: r[i].astype(F32)
        v = {}

        def s0():
            v['hs'] = (ld(hf_ref) + ld(hb_ref)).astype(BF16)
            v['gz'] = _gelu_tanh(ld(zg_ref))

        def s1():
            hs = _bdot(unperm_ref[...], v['hs'])
            v['y_a'] = _bdot((hs * v['gz']).astype(BF16), lup_ref[...])

        def s2():
            v['y_b'] = _bdot(on_ref[i], gup_ref[...])

        def s3():
            v['mm'] = (_sigmoid(ld(ga_ref)) * v['y_a'] + _sigmoid(ld(gb_ref)) * v['y_b']).astype(BF16)

        def s4():
            v['m'] = _bdot(v['mm'], wo_ref[...])

        def s5():
            o_ref[i] = x_ref[i] + mod_ref[2:3, :] * _rms(v['m'], ng_ref[1:2, :])

        return [s0, s1, s2, s3, s4, s5]

    _run_staggered([stages(i) for i in range(x_ref.shape[0])], MERGE_STAGGER)


def _merge(x, hf, hb, zg, ga, gb, on, mod, norm_g, lru_up, gla_up, w_out, *, per_batch_mod, name):
    b, t, _ = x.shape
    tm = _lru_chunk(t)
    n_tiles = b * t // tm
    spec, mod_spec, const = _tile_specs(n_tiles, tm, t, per_batch_mod)
    tiled = lambda a: a.reshape(n_tiles, tm, a.shape[-1])
    out = pl.pallas_call(
        _merge_kernel,
        grid=(n_tiles // TILES_PER_STEP,),
        in_specs=[spec] * 7 + [mod_spec, const((4, D_MODEL)), const((tm, tm))]
        + [const((D_MODEL, D_MODEL))] * 3,
        out_specs=spec,
        out_shape=jax.ShapeDtypeStruct((n_tiles, tm, D_MODEL), F32),
        compiler_params=_cparams(("parallel",)),
        name=name,
    )(*(tiled(a) for a in (x, hf, hb, zg, ga, gb, on)), mod, norm_g,
      jnp.asarray(_interleave_matrix(tm).T, BF16), lru_up, gla_up, w_out)
    return out.reshape(b, t, D_MODEL)


def _ffn_kernel(x_ref, mod_ref, ng_ref, w1_ref, w2_ref, o_ref):
    fs = D_FF // FFN_SPLIT

    def stages(i):
        v = {}

        def norm():
            x = x_ref[i]
            v['h'] = (_rms(x, ng_ref[2:3, :]) * (1.0 + mod_ref[4:5, :]) + mod_ref[3:4, :]).astype(BF16)

        def hidden(kf):
            hid = jnp.maximum(_bdot(v['h'], w1_ref[:, kf * fs:(kf + 1) * fs]), 0.0)
            part = _bdot((hid * hid).astype(BF16), w2_ref[kf * fs:(kf + 1) * fs, :])
            v['acc'] = part if kf == 0 else v['acc'] + part

        def out():
            o_ref[i] = x_ref[i] + mod_ref[5:6, :] * _rms(v['acc'], ng_ref[3:4, :])

        return [norm] + [functools.partial(hidden, kf) for kf in range(FFN_SPLIT)] + [out]

    _run_staggered([stages(i) for i in range(x_ref.shape[0])], FFN_STAGGER)


def _ffn(x, mod, norm_g, w1, w2, *, per_batch_mod, name):
    b, t, _ = x.shape
    tm = _lru_chunk(t)
    n_tiles = b * t // tm
    spec, mod_spec, const = _tile_specs(n_tiles, tm, t, per_batch_mod)
    out = pl.pallas_call(
        _ffn_kernel,
        grid=(n_tiles // TILES_PER_STEP,),
        in_specs=[spec, mod_spec, const((4, D_MODEL)), const((D_MODEL, D_FF)), const((D_FF, D_MODEL))],
        out_specs=spec,
        out_shape=jax.ShapeDtypeStruct((n_tiles, tm, D_MODEL), F32),
        compiler_params=_cparams(("parallel",)),
        name=name,
    )(x.reshape(n_tiles, tm, D_MODEL), mod, norm_g, w1, w2)
    return out.reshape(b, t, D_MODEL)


def _blockdiag_slabs(w):
    per = LRU_COLB // LRU_BW
    w = w.reshape(2, LRU_NCOLB, per, LRU_BW, LRU_BW)
    eye = jnp.eye(per, dtype=w.dtype)
    slab = jnp.einsum('dspij,pq->dspiqj', w, eye)
    return slab.reshape(2, LRU_NCOLB, LRU_COLB, LRU_COLB).astype(BF16)


def _layer_group(x, mod, p, lru_h0, gla_s0, *, latent, want_state, tag):
    b, t, _ = x.shape
    per_batch = latent
    ga, gb, zg, zx = _inproj(x, mod, p['norm_g'], p['w_rm'], (None,) * 4,
                             per_batch_mod=per_batch, name=f"inproj_rm_{tag}")
    period = GRID_W if latent else t
    (q, k, v, g, lr), hf, hb, s_lru = _cm_lru(
        x, mod, p['norm_g'], p['w_cm'], (None, None, None, _silu, None), zx, lru_h0, p['conv_w'],
        p['conv_b'], p['wa_bd'], p['wx_bd'], p['lru_ba'], p['lru_bx'], p['lru_L'], per_batch_mod=per_batch,
        order='col_major' if latent else 'natural', period=period, name=f"cm_lru_{tag}")
    res = _gla(q, k, v, g, lr, p['w2p'], p['gla_b2'], p['gla_norm_g'], gla_s0,
               want_final=want_state, hps=2, col_major=latent, name=f"gla_{tag}")
    on = res[0]
    s_gla = res[1] if want_state else None
    x1 = _merge(x, hf, hb, zg, ga, gb, on, mod, p['norm_g'], p['lru_up'], p['gla_up'], p['w_out'],
                per_batch_mod=per_batch, name=f"merge_{tag}")
    y = _ffn(x1, mod, p['norm_g'], p['mlp_w1'], p['mlp_w2'], per_batch_mod=per_batch, name=f"ffn_{tag}")
    return y, s_lru, s_gla


def kernel(x_prompt, x_sample, state_lru, state_gla, c, c_ctx, w_mod, b_mod, norm_g, w_in, conv_w, conv_b,
           lru_wa, lru_ba, lru_wx, lru_bx, lru_L, lru_up, gla_w2, gla_b2, gla_norm_g, gla_up, w_out,
           mlp_w1, mlp_w2):
    depth = w_in.shape[0]
    assert depth == 1
    dec_b = x_sample.shape[0]
    b0 = x_prompt.shape[0]
    xp, xs = x_prompt, x_sample
    lru_states, gla_states = [], []
    mod_rows = -(-(1 + dec_b) // V7X_SUBLANES) * V7X_SUBLANES
    cc = jnp.zeros((mod_rows, D_MODEL), F32).at[0].set(c_ctx).at[1:1 + dec_b].set(c)
    for l in range(depth):
        wl = w_in[l]
        o = np.cumsum((0, D_MODEL, D_MODEL, 512, 512, 1024, 1024, 2 * GLA_RANK, D_MODEL, D_MODEL))
        col = lambda i: wl[:, o[i]:o[i + 1]].astype(BF16)
        lr_w = jnp.pad(col(6), ((0, 0), (0, LR_PAD - 2 * GLA_RANK)))
        w2 = gla_w2[l].astype(BF16)
        w2p = jnp.zeros((2, LR_PAD, GLA_HEADS * GLA_DK), BF16)
        w2p = w2p.at[0, 0:GLA_RANK].set(w2[0]).at[1, GLA_RANK:2 * GLA_RANK].set(w2[1])
        p = {
            'norm_g': norm_g[l],
            'w_rm': [col(7), col(8), col(1), col(0)],
            'w_cm': [col(2), col(3), col(4), col(5), lr_w],
            'conv_w': conv_w[l], 'conv_b': conv_b[l].reshape(1, D_MODEL),
            'wa_bd': _blockdiag_slabs(0.5 * lru_wa[l]), 'wx_bd': _blockdiag_slabs(0.5 * lru_wx[l]),
            'lru_ba': lru_ba[l], 'lru_bx': lru_bx[l], 'lru_L': lru_L[l],
            'lru_up': lru_up[l].astype(BF16),
            'w2p': w2p, 'gla_b2': gla_b2[l], 'gla_norm_g': gla_norm_g[l].reshape(1, GLA_DV),
            'gla_up': gla_up[l].astype(BF16), 'w_out': w_out[l].astype(BF16),
            'mlp_w1': mlp_w1[l].astype(BF16), 'mlp_w2': mlp_w2[l].astype(BF16),
        }
        mod = _modulation(cc, w_mod[l], b_mod[l])
        h0_ctx = jnp.zeros((b0, 2, D_MODEL), F32)
        xp, s_lru, s_gla = _layer_group(xp, mod, p, h0_ctx, None, latent=False, want_state=True,
                                        tag="ctx")
        lru_states.append(s_lru)
        gla_states.append(s_gla)
        xs, _, _ = _layer_group(xs, mod, p, state_lru[:, l], state_gla[:, l], latent=True,
                                want_state=False, tag="lat")
    return (xp, xs, jnp.stack(lru_states, axis=1), jnp.stack(gla_states, axis=1))
```

```python
import functools

import numpy as np
import jax
import jax.numpy as jnp
from jax import lax
from jax.experimental import pallas as pl
from jax.experimental.pallas import tpu as pltpu

F32 = jnp.float32
BF16 = jnp.bfloat16

D_MODEL = 1024
GRID_W = 64
LRU_BLOCKS = 16
LRU_BW = D_MODEL // LRU_BLOCKS
LRU_C = 8.0
GLA_HEADS = 4
GLA_DK = 128
GLA_DV = 256
GLA_RANK = 16
GLA_TAU = 16.0
GLA_CHUNK = 64
D_FF = 4 * D_MODEL
N_MOD = 6
EPS = 1e-6
LOG2_E = 1.4426950408889634

V7X_LANES = 128
V7X_SUBLANES = 8
V7X_MXU_DIM = 256
V7X_VMEM_BYTES = 64 * 1024 * 1024
VMEM_LIMIT = V7X_VMEM_BYTES - 8 * 1024 * 1024

LRU_COLB = V7X_MXU_DIM
LRU_NCOLB = D_MODEL // LRU_COLB
LR_PAD = V7X_LANES
COL_BLOCK = 16
GLA_BLOCK = V7X_MXU_DIM
GLA_HEADS_PER_STEP = 2
GLA_SCORE_BLOCK = 2 * GLA_CHUNK
GLA_OUT_COLS = 16
LRU_CHUNK = 256
TILES_PER_STEP = 2
MERGE_STAGGER = 2
INPROJ_STAGGER = 2
FFN_SPLIT = 4
FFN_STAGGER = 3
LRU_SCAN_CUTS = 1
CM_PIECE = 256


def _cparams(sem):
    return pltpu.CompilerParams(dimension_semantics=sem, vmem_limit_bytes=VMEM_LIMIT)


def _rms(x, g):
    ms = jnp.mean(x * x, axis=-1, keepdims=True)
    return x * lax.rsqrt(ms + EPS) * g


def _sigmoid(x):
    return 0.5 * jnp.tanh(0.5 * x) + 0.5


def _softplus(x):
    return jnp.maximum(x, 0.0) + jnp.log1p(jnp.exp(-jnp.abs(x)))


def _bdot(a, b):
    return jnp.dot(a, b, preferred_element_type=F32)


def _mod_kernel(c_ref, w_ref, b_ref, o_ref):
    c = c_ref[...]
    s = (c * _sigmoid(c)).astype(BF16)
    o_ref[...] = _bdot(s, w_ref[...].astype(BF16)) + b_ref[...]


def _modulation(cc, w_mod, b_mod):
    rows = cc.shape[0]
    return pl.pallas_call(
        _mod_kernel,
        grid=(N_MOD,),
        in_specs=[
            pl.BlockSpec((rows, D_MODEL), lambda n: (0, 0)),
            pl.BlockSpec((D_MODEL, D_MODEL), lambda n: (0, n)),
            pl.BlockSpec((1, D_MODEL), lambda n: (0, n)),
        ],
        out_specs=pl.BlockSpec((rows, D_MODEL), lambda n: (0, n)),
        out_shape=jax.ShapeDtypeStruct((rows, N_MOD * D_MODEL), F32),
        compiler_params=_cparams(("arbitrary",)),
        name="modulation",
    )(cc, w_mod, b_mod.reshape(1, N_MOD * D_MODEL)).reshape(rows, N_MOD, D_MODEL)


def _tok_view(a, grid_tile):
    if grid_tile is None:
        return a
    b, t, n = a.shape
    rows = t // GRID_W
    return a.reshape(b, rows, GRID_W // COL_BLOCK, COL_BLOCK, n)


def _tok_spec(n, tm, grid_tile):
    if grid_tile is None:
        return pl.BlockSpec((None, tm, n), lambda b, j: (b, j, 0))
    rows = grid_tile
    return pl.BlockSpec((None, rows, None, COL_BLOCK, n), lambda b, j: (b, 0, j, 0, 0))


def _const_spec(shape):
    nd = len(shape)
    return pl.BlockSpec(shape, lambda b, j: (0,) * nd)


def _mod_spec(per_batch):
    if per_batch:
        return pl.BlockSpec((None, N_MOD, D_MODEL), lambda b, j: (b + 1, 0, 0))
    return pl.BlockSpec((None, N_MOD, D_MODEL), lambda b, j: (0, 0, 0))


def _perm_matrix(rows, cols):
    n = rows * cols
    p = np.zeros((n, n), np.float32)
    for r in range(rows):
        for c in range(cols):
            p[c * rows + r, r * cols + c] = 1.0
    return p


def _silu(x):
    return x * _sigmoid(x)


def _inproj_pieces(x_ref, mod_ref, ng_ref, perm_ref, w_refs, o_refs, *, acts, n_piece, tile=None):
    v = {}

    def norm():
        x = (x_ref[...] if tile is None else x_ref[tile]).reshape(-1, D_MODEL)
        h = _rms(x, ng_ref[0:1, :]) * (1.0 + mod_ref[1:2, :]) + mod_ref[0:1, :]
        v['h'] = h.astype(BF16)
        if perm_ref is not None:
            v['h'] = _bdot(perm_ref[...], v['h']).astype(BF16)

    pieces = [norm]
    for i, (w_ref, o_ref) in enumerate(zip(w_refs, o_refs)):
        n = w_ref.shape[1]
        width = min(n, n_piece)
        for c0 in range(0, n, width):
            def piece(i=i, w_ref=w_ref, o_ref=o_ref, cs=slice(c0, c0 + width)):
                z = _bdot(v['h'], w_ref[:, cs])
                if acts[i] is not None:
                    z = acts[i](z)
                if tile is None:
                    o_ref[:, cs] = z.astype(o_ref.dtype)
                else:
                    o_ref[tile, :, cs] = z.astype(o_ref.dtype)
            pieces.append(piece)
    return pieces


def _inproj_kernel(*refs, n_out, acts):
    x_ref, mod_ref, ng_ref = refs[:3]
    _run_staggered([_inproj_pieces(x_ref, mod_ref, ng_ref, None, refs[3:3 + n_out], refs[3 + n_out:],
                                   acts=acts, n_piece=D_MODEL, tile=i)
                    for i in range(x_ref.shape[0])], INPROJ_STAGGER)


def _inproj_setup(x, weights, order):
    b, t, _ = x.shape
    if order == 'col_major':
        rows = t // GRID_W
        tm = rows * COL_BLOCK
        return _tok_view(x, rows), _tok_spec(D_MODEL, tm, rows), tm, _perm_matrix(rows, COL_BLOCK)
    tm = _lru_chunk(t)
    perm = _interleave_matrix(tm) if order == 'interleaved' else None
    return x, _tok_spec(D_MODEL, tm, None), tm, perm


def _inproj(x, mod, norm_g, weights, acts, *, per_batch_mod, name):
    b, t, _ = x.shape
    tm = _lru_chunk(t)
    n_tiles = b * t // tm
    spec, mod_spec, const = _tile_specs(n_tiles, tm, t, per_batch_mod)
    outs = pl.pallas_call(
        functools.partial(_inproj_kernel, n_out=len(weights), acts=tuple(acts)),
        grid=(n_tiles // TILES_PER_STEP,),
        in_specs=[spec, mod_spec, const((4, D_MODEL))] + [const(w.shape) for w in weights],
        out_specs=[pl.BlockSpec((TILES_PER_STEP, tm, w.shape[1]), lambda i: (i, 0, 0)) for w in weights],
        out_shape=[jax.ShapeDtypeStruct((n_tiles, tm, w.shape[1]), BF16) for w in weights],
        compiler_params=_cparams(("parallel",)),
        name=name,
    )(x.reshape(n_tiles, tm, D_MODEL), mod, norm_g, *weights)
    return [o.reshape(b, t, o.shape[-1]) for o in outs]


def _lru_chunk(t):
    return min(LRU_CHUNK, t)


def _interleave_matrix(tc):
    return _perm_matrix(V7X_SUBLANES, tc // V7X_SUBLANES)


def _lru_pieces(zf_ref, zb_ref, rf0, rb0, il_ref, cw_ref, cb_ref, wa_ref, wx_ref, ba_ref, bx_ref, l_ref,
                hf_ref, hb_ref, fin_ref, a_s, u_s, carry_s, zi_s, *, tc, period):
    ns = V7X_SUBLANES
    sl = tc // ns
    sub_c = lax.broadcasted_iota(jnp.int32, (ns, LRU_COLB), 0)
    prev_ok = jnp.bitwise_and(sub_c * sl, period - 1) != 0
    next_ok = jnp.bitwise_and((sub_c + 1) * sl, period - 1) != 0

    def reorder(d, z_ref, r0):
        zi_s[d] = _bdot(il_ref[...], z_ref[r0:r0 + tc, :]).astype(BF16)

    def gates(d, cb):
        cs = slice(cb * LRU_COLB, (cb + 1) * LRU_COLB)
        x = zi_s[d, :, cs].astype(F32)
        e_prev = jnp.where(prev_ok, pltpu.roll(x[tc - ns:], 1, 0), 0.0)
        e_next0 = jnp.where(next_ok, pltpu.roll(x[:ns], ns - 1, 0), 0.0)
        e_next1 = jnp.where(next_ok, pltpu.roll(x[ns:2 * ns], ns - 1, 0), 0.0)
        xm1 = jnp.concatenate([e_prev, x[:tc - ns]], axis=0)
        xp1 = jnp.concatenate([x[ns:], e_next0], axis=0)
        xp2 = jnp.concatenate([x[2 * ns:], e_next0, e_next1], axis=0)
        xc = (cb_ref[0:1, cs] + xm1 * cw_ref[0:1, cs] + x * cw_ref[1:2, cs]
              + xp1 * cw_ref[2:3, cs] + xp2 * cw_ref[3:4, cs])
        xcb = xc.astype(BF16)
        tr = jnp.tanh(_bdot(xcb, wa_ref[d, cb]) + 0.5 * ba_ref[d:d + 1, cs])
        ti = jnp.tanh(_bdot(xcb, wx_ref[d, cb]) + 0.5 * bx_ref[d:d + 1, cs])
        k2 = (-0.5 * LRU_C * LOG2_E) * _softplus(-l_ref[d:d + 1, cs])
        a = jnp.exp2(tr * k2 + k2)
        y = 1.0 - a * a
        u = jnp.where(y > 0.0, y * lax.rsqrt(y), 0.0) * ((0.5 * ti + 0.5) * xc)
        a_s[d, :, cs] = a
        u_s[d, :, cs] = u

    state = {}
    n_cut = LRU_SCAN_CUTS

    def pass1(part):
        if part == 0:
            zeros = jnp.zeros((ns, D_MODEL), F32)
            ones = jnp.ones((ns, D_MODEL), F32)
            state['c'] = (zeros, ones, zeros, ones)
        hf, af, hb, ab = state['c']
        for i in range(part * sl // n_cut, (part + 1) * sl // n_cut):
            rf = slice(i * ns, (i + 1) * ns)
            rb = slice((sl - 1 - i) * ns, (sl - i) * ns)
            a = a_s[0, rf, :]
            hf = a * hf + u_s[0, rf, :]
            af = a * af
            u_s[0, rf, :] = hf
            a_s[0, rf, :] = af
            a = a_s[1, rb, :]
            hb = a * hb + u_s[1, rb, :]
            ab = a * ab
            u_s[1, rb, :] = hb
            a_s[1, rb, :] = ab
        state['c'] = (hf, af, hb, ab)

    def chain():
        hf, af, hb, ab = state['c']
        sub = lax.broadcasted_iota(jnp.int32, (ns, D_MODEL), 0)

        def sublane_scan(a, u, forward):
            for s in (1, 2, 4):
                if forward:
                    m = sub >= s
                    shift = s
                else:
                    m = sub < ns - s
                    shift = ns - s
                a_sh = jnp.where(m, pltpu.roll(a, shift, 0), 1.0)
                u_sh = jnp.where(m, pltpu.roll(u, shift, 0), 0.0)
                u = a * u_sh + u
                a = a * a_sh
            return a, u

        cin_f = carry_s[0:1, :]
        cin_b = carry_s[1:2, :]
        af, hf = sublane_scan(af, hf, True)
        ab, hb = sublane_scan(ab, hb, False)
        fin_f = hf + af * cin_f
        fin_b = hb + ab * cin_b
        enter_f = jnp.where(sub == 0, cin_f, pltpu.roll(fin_f, 1, 0))
        enter_b = jnp.where(sub == ns - 1, cin_b, pltpu.roll(fin_b, ns - 1, 0))
        out_f = fin_f[ns - 1:ns, :]
        out_b = fin_b[0:1, :]
        carry_s[0:1, :] = out_f
        carry_s[1:2, :] = out_b
        fin_ref[0:1, :] = out_f
        fin_ref[1:2, :] = out_b
        state['e'] = (jnp.concatenate([enter_f, enter_f], axis=0), jnp.concatenate([enter_b, enter_b], axis=0))

    def pass2(part):
        pk = 2 * ns
        enter_f2, enter_b2 = state['e']
        for i in range(part * (sl // 2) // n_cut, (part + 1) * (sl // 2) // n_cut):
            rr = slice(i * pk, (i + 1) * pk)
            hf_ref[rf0 + i * pk:rf0 + (i + 1) * pk, :] = (
                u_s[0, rr, :] + a_s[0, rr, :] * enter_f2).astype(hf_ref.dtype)
            hb_ref[rb0 + i * pk:rb0 + (i + 1) * pk, :] = (
                u_s[1, rr, :] + a_s[1, rr, :] * enter_b2).astype(hb_ref.dtype)

    scan = ([functools.partial(pass1, part) for part in range(n_cut)] + [chain]
            + [functools.partial(pass2, part) for part in range(n_cut)])

    pieces = [functools.partial(reorder, 0, zf_ref, rf0), functools.partial(reorder, 1, zb_ref, rb0)]
    pieces += [functools.partial(gates, d, cb) for d in range(2) for cb in range(LRU_NCOLB)]
    return pieces + scan


def _interleave(major, minor):
    out, done = [], 0
    for i, p in enumerate(major):
        want = -(-(i + 1) * len(minor) // len(major))
        out.extend(minor[done:want])
        done = want
        out.append(p)
    return out


def _cm_lru_kernel(*refs, n_cm, permute, acts, tc, period, cps):
    x_ref, mod_ref, ng_ref = refs[:3]
    k = 4 if permute else 3
    perm_ref = refs[3] if permute else None
    w_refs = refs[k:k + n_cm]
    k += n_cm
    zf_ref, zb_ref, h0_ref, il_ref, cw_ref, cb_ref, wa_ref, wx_ref, ba_ref, bx_ref, l_ref = refs[k:k + 11]
    k += 11
    o_refs = refs[k:k + n_cm]
    k += n_cm
    hf_ref, hb_ref, fin_ref, a_s, u_s, carry_s, zi_s = refs[k:]

    @pl.when(pl.program_id(1) == 0)
    def _():
        carry_s[...] = h0_ref[...]

    cm = _inproj_pieces(x_ref, mod_ref, ng_ref, perm_ref, w_refs, o_refs, acts=acts, n_piece=CM_PIECE)
    lru = []
    for s in range(cps):
        lru += _lru_pieces(zf_ref, zb_ref, s * tc, (cps - 1 - s) * tc, il_ref, cw_ref, cb_ref, wa_ref, wx_ref,
                           ba_ref, bx_ref, l_ref, hf_ref, hb_ref, fin_ref, a_s, u_s, carry_s, zi_s,
                           tc=tc, period=period)
    cm[0]()
    for piece in _interleave(lru, cm[1:]):
        piece()


def _cm_lru(x, mod, norm_g, weights, acts, zx, h0, conv_w, conv_b, wa_bd, wx_bd, ba, bx, lru_l, *,
            per_batch_mod, order, period, name):
    b, t, _ = x.shape
    x_in, x_spec, tm, perm = _inproj_setup(x, weights, order)
    extra = [] if perm is None else [jnp.asarray(perm, BF16)]
    n_steps = t // tm
    tc = _lru_chunk(t)
    cps = t // tc // n_steps
    assert cps >= 1 and cps * tc * n_steps == t
    assert period % (tc // V7X_SUBLANES) == 0 and tc % period == 0
    blk = cps * tc
    res = pl.pallas_call(
        functools.partial(_cm_lru_kernel, n_cm=len(weights), permute=perm is not None, acts=tuple(acts),
                          tc=tc, period=period, cps=cps),
        grid=(b, n_steps),
        in_specs=[x_spec, _mod_spec(per_batch_mod), _const_spec((4, D_MODEL))]
        + [_const_spec(e.shape) for e in extra] + [_const_spec(w.shape) for w in weights]
        + [
            pl.BlockSpec((None, blk, D_MODEL), lambda bb, j: (bb, j, 0)),
            pl.BlockSpec((None, blk, D_MODEL), lambda bb, j: (bb, n_steps - 1 - j, 0)),
            pl.BlockSpec((None, 2, D_MODEL), lambda bb, j: (bb, 0, 0)),
            _const_spec((tc, tc)), _const_spec((4, D_MODEL)), _const_spec((1, D_MODEL)),
            _const_spec((2, LRU_NCOLB, LRU_COLB, LRU_COLB)), _const_spec((2, LRU_NCOLB, LRU_COLB, LRU_COLB)),
            _const_spec((2, D_MODEL)), _const_spec((2, D_MODEL)), _const_spec((2, D_MODEL)),
        ],
        out_specs=[_tok_spec(w.shape[1], tm, None) for w in weights] + [
            pl.BlockSpec((None, blk, D_MODEL), lambda bb, j: (bb, j, 0)),
            pl.BlockSpec((None, blk, D_MODEL), lambda bb, j: (bb, n_steps - 1 - j, 0)),
            pl.BlockSpec((None, 2, D_MODEL), lambda bb, j: (bb, 0, 0)),
        ],
        out_shape=[jax.ShapeDtypeStruct((b, t, w.shape[1]), BF16) for w in weights] + [
            jax.ShapeDtypeStruct((b, t, D_MODEL), BF16),
            jax.ShapeDtypeStruct((b, t, D_MODEL), BF16),
            jax.ShapeDtypeStruct((b, 2, D_MODEL), F32),
        ],
        scratch_shapes=[
            pltpu.VMEM((2, tc, D_MODEL), F32),
            pltpu.VMEM((2, tc, D_MODEL), F32),
            pltpu.VMEM((2, D_MODEL), F32),
            pltpu.VMEM((2, tc, D_MODEL), BF16),
        ],
        compiler_params=_cparams(("parallel", "arbitrary")),
        name=name,
    )(x_in, mod, norm_g, *extra, *weights, zx, zx, h0, jnp.asarray(_interleave_matrix(tc), BF16),
      conv_w, conv_b, wa_bd, wx_bd, ba, bx, lru_l)
    return res[:len(weights)], res[len(weights)], res[len(weights) + 1], res[len(weights) + 2]


def _gla_kernel(*refs, t, hps, has_s0, want_final, grid_rows):
    q_ref, k_ref, v_ref, g_ref, lr_ref, w2_ref, b2_ref, gn_ref = refs[:8]
    n = 8
    if grid_rows is not None:
        perm_ref = refs[n]
        n += 1
    if has_s0:
        s0_ref = refs[n]
        n += 1
    on_ref = refs[n]
    n += 1
    if want_final:
        fin_ref = refs[n]
        n += 1
    oacc, st, q2_s, k2_s, bt_s, pp_q, pp_k, pp_qx, pp_kx = refs[n:]

    c = GLA_CHUNK
    blk = min(GLA_BLOCK, t)
    for d in range(2):
        for hh in range(hps):
            if has_s0:
                st[d, hh] = s0_ref[d, hh].T
            else:
                st[d, hh] = jnp.zeros((GLA_DV, GLA_DK), F32)

    ti = lax.broadcasted_iota(jnp.int32, (blk, blk), 0)
    si = lax.broadcasted_iota(jnp.int32, (blk, blk), 1)
    same = (ti // c) == (si // c)
    cums = ((same & (si <= ti)).astype(BF16), (same & (si >= ti)).astype(BF16))
    sb = min(GLA_SCORE_BLOCK, blk)
    tj = lax.broadcasted_iota(jnp.int32, (sb, sb), 0)
    sj = lax.broadcasted_iota(jnp.int32, (sb, sb), 1)
    same_j = (tj // c) == (sj // c)
    masks = (same_j & (sj <= tj), same_j & (sj >= tj))
    scale = GLA_DK ** -0.5

    def step(ia, slot_a, ib, slot_b):
        nd = hps * GLA_DK
        if ia is not None:
            rows_a = pl.ds(pl.multiple_of(ia * blk, blk), blk)
            lrb = lr_ref[rows_a, :]
            pre = [_bdot(lrb, w2_ref[d]) + b2_ref[d:d + 1, :] for d in range(2)]
        if ib is not None:
            rows_b = [pl.ds(pl.multiple_of(ib * blk + g * sb, sb), sb) for g in range(blk // sb)]
            loc_b = [slice(g * sb, (g + 1) * sb) for g in range(blk // sb)]
            nt = (((1,), (1,)), ((), ()))
            raw = [[[lax.dot_general(pp_q[slot_b, d, loc, hh * GLA_DK:(hh + 1) * GLA_DK],
                                     pp_k[slot_b, d, loc, hh * GLA_DK:(hh + 1) * GLA_DK], nt,
                                     preferred_element_type=F32) for d in range(2)]
                    + [lax.dot_general(pp_qx[slot_b, loc, hh * 2 * GLA_DK:(hh + 1) * 2 * GLA_DK],
                                       pp_kx[slot_b, loc, hh * 2 * GLA_DK:(hh + 1) * 2 * GLA_DK], nt,
                                       preferred_element_type=F32)]
                    for hh in range(hps)] for loc in loc_b]
        if ia is not None:
            bcum = []
            for d in range(2):
                soft = jnp.log(1.0 + jnp.exp2(jnp.abs(pre[d]) * (-LOG2_E)))
                la = jnp.minimum(pre[d], 0.0) * (LOG2_E / GLA_TAU) - soft * (LOG2_E / GLA_TAU)
                hi = la.astype(BF16)
                mid = (la - hi.astype(F32)).astype(BF16)
                bcum.append(_bdot(cums[d], hi) + _bdot(cums[d], mid))
        if ib is not None:
            for g, rows in enumerate(rows_b):
                for hh in range(hps):
                    vs = slice(hh * GLA_DV, (hh + 1) * GLA_DV)
                    sd0, sd1, sx = raw[g][hh]
                    sc = jnp.where(masks[0], sd0, 0.0) + jnp.where(masks[1], sd1, 0.0) + sx
                    oacc[rows, vs] = _bdot(sc.astype(BF16), v_ref[rows, vs])
        if ia is not None:
            q = q_ref[rows_a, :].astype(F32)
            k = k_ref[rows_a, :].astype(F32)
            nc = blk // c
            crow = lambda a, n: a[n * c:(n + 1) * c]
            cat = lambda parts: jnp.concatenate(parts, axis=0)
            zero = jnp.zeros((c, nd), BF16)
            qx, kx = [], []
            for d in range(2):
                edge = c - 1 if d == 0 else 0
                tot = [bcum[d][n * c + edge:n * c + edge + 1, :] for n in range(nc)]
                dec = [jnp.exp2(x) for x in tot]
                btot = cat([jnp.broadcast_to(x, (c, nd)) for x in tot])
                qe = (q * scale) * jnp.exp2(bcum[d])
                kt = k * jnp.exp2(btot - bcum[d])
                pp_q[slot_a, d] = qe.astype(BF16)
                pp_k[slot_a, d] = (k * jnp.exp2(-bcum[d])).astype(BF16)
                first = [(n % 2 == 0) == (d == 0) for n in range(nc)]
                q2 = cat([crow(qe, n) if first[n] else crow(qe, n) * dec[n ^ 1] for n in range(nc)])
                k2 = cat([crow(kt, n) * dec[n ^ 1] if first[n] else crow(kt, n) for n in range(nc)])
                q2_s[d, rows_a, :] = q2.astype(BF16)
                k2_s[d, rows_a, :] = k2.astype(BF16)
                bt_s[d, rows_a, :] = cat([jnp.broadcast_to(tot[n - n % 2] + tot[n - n % 2 + 1], (c, nd))
                                          for n in range(nc)])
                qeb, ktb = qe.astype(BF16), kt.astype(BF16)
                qx.append(cat([zero if first[n] else crow(qeb, n) for n in range(nc)]))
                kx.append(cat([crow(ktb, n) if first[n] else zero for n in range(nc)]))
            for hh in range(hps):
                ks = slice(hh * GLA_DK, (hh + 1) * GLA_DK)
                xs = slice(hh * 2 * GLA_DK, (hh + 1) * 2 * GLA_DK)
                pp_qx[slot_a, :, xs] = jnp.concatenate([qx[0][:, ks], qx[1][:, ks]], axis=1)
                pp_kx[slot_a, :, xs] = jnp.concatenate([kx[0][:, ks], kx[1][:, ks]], axis=1)

    n_blocks = t // blk
    step(0, 0, None, None)
    for ib in range(n_blocks - 1):
        step(ib + 1, (ib + 1) % 2, ib, ib % 2)
    step(None, None, n_blocks - 1, (n_blocks - 1) % 2)

    pr = 2 * c
    n_prs = t // pr

    def pair_step(p0, d, hh):
        rows = pl.ds(p0, pr)
        ks = slice(hh * GLA_DK, (hh + 1) * GLA_DK)
        vs = slice(hh * GLA_DV, (hh + 1) * GLA_DV)
        s_t = st[d, hh]
        o = lax.dot_general(q2_s[d, rows, ks], s_t.astype(BF16), (((1,), (1,)), ((), ())),
                            preferred_element_type=F32)
        upd = lax.dot_general(v_ref[rows, vs], k2_s[d, rows, ks], (((0,), (0,)), ((), ())),
                              preferred_element_type=F32)
        st[d, hh] = s_t * jnp.exp2(bt_s[d, pl.ds(p0, 1), ks]) + upd
        return o

    for jj in range(n_prs):
        for hh in range(hps):
            vs = slice(hh * GLA_DV, (hh + 1) * GLA_DV)
            for p0, d in ((jj * pr, 0), ((n_prs - 1 - jj) * pr, 1)):
                oacc[pl.ds(p0, pr), vs] += pair_step(p0, d, hh)

    if grid_rows is None:
        ob = min(GLA_BLOCK, t)
    else:
        ob = GLA_OUT_COLS * grid_rows

    for ib in range(t // ob):
        rows = slice(ib * ob, (ib + 1) * ob)
        parts = []
        for hh in range(hps):
            vs = slice(hh * GLA_DV, (hh + 1) * GLA_DV)
            parts.append((_rms(oacc[rows, vs], gn_ref[...]) * g_ref[rows, vs].astype(F32)).astype(BF16))
        on = jnp.concatenate(parts, axis=1)
        if grid_rows is None:
            on_ref[rows, :] = on
        else:
            on = _bdot(perm_ref[...], on).astype(BF16)
            for r in range(grid_rows):
                dst = r * GRID_W + ib * GLA_OUT_COLS
                on_ref[dst:dst + GLA_OUT_COLS, :] = on[r * GLA_OUT_COLS:(r + 1) * GLA_OUT_COLS]

    if want_final:
        for d in range(2):
            for hh in range(hps):
                fin_ref[d, hh] = st[d, hh].T


def _gla(q, k, v, g, lr, w2p, b2, gn, s0, *, want_final, hps, col_major, name):
    b, t, _ = q.shape
    ng = GLA_HEADS // hps
    has_s0 = s0 is not None
    grid_rows = t // GRID_W if col_major else None
    in_specs = [
        pl.BlockSpec((None, t, hps * GLA_DK), lambda bb, h: (bb, 0, h)),
        pl.BlockSpec((None, t, hps * GLA_DK), lambda bb, h: (bb, 0, h)),
        pl.BlockSpec((None, t, hps * GLA_DV), lambda bb, h: (bb, 0, h)),
        pl.BlockSpec((None, t, hps * GLA_DV), lambda bb, h: (bb, 0, h)),
        pl.BlockSpec((None, t, LR_PAD), lambda bb, h: (bb, 0, 0)),
        pl.BlockSpec((2, LR_PAD, hps * GLA_DK), lambda bb, h: (0, 0, h)),
        pl.BlockSpec((2, hps * GLA_DK), lambda bb, h: (0, h)),
        pl.BlockSpec((1, GLA_DV), lambda bb, h: (0, 0)),
    ]
    args = [q, k, v, g, lr, w2p, b2, gn]
    if col_major:
        ob = GLA_OUT_COLS * grid_rows
        in_specs.append(pl.BlockSpec((ob, ob), lambda bb, h: (0, 0)))
        args.append(jnp.asarray(_perm_matrix(grid_rows, GLA_OUT_COLS).T, BF16))
    state_spec = pl.BlockSpec((None, 2, hps, GLA_DK, GLA_DV), lambda bb, h: (bb, 0, h, 0, 0))
    if has_s0:
        in_specs.append(state_spec)
        args.append(s0)
    out_specs = [pl.BlockSpec((None, t, hps * GLA_DV), lambda bb, h: (bb, 0, h))]
    out_shape = [jax.ShapeDtypeStruct((b, t, GLA_HEADS * GLA_DV), BF16)]
    if want_final:
        out_specs.append(state_spec)
        out_shape.append(jax.ShapeDtypeStruct((b, 2, GLA_HEADS, GLA_DK, GLA_DV), F32))
    return pl.pallas_call(
        functools.partial(_gla_kernel, t=t, hps=hps, has_s0=has_s0, want_final=want_final,
                          grid_rows=grid_rows),
        grid=(b, ng),
        in_specs=in_specs,
        out_specs=out_specs,
        out_shape=out_shape,
        scratch_shapes=[
            pltpu.VMEM((t, hps * GLA_DV), F32),
            pltpu.VMEM((2, hps, GLA_DV, GLA_DK), F32),
            pltpu.VMEM((2, t, hps * GLA_DK), BF16),
            pltpu.VMEM((2, t, hps * GLA_DK), BF16),
            pltpu.VMEM((2, t, hps * GLA_DK), F32),
            pltpu.VMEM((2, 2, min(GLA_BLOCK, t), hps * GLA_DK), BF16),
            pltpu.VMEM((2, 2, min(GLA_BLOCK, t), hps * GLA_DK), BF16),
            pltpu.VMEM((2, min(GLA_BLOCK, t), 2 * hps * GLA_DK), BF16),
            pltpu.VMEM((2, min(GLA_BLOCK, t), 2 * hps * GLA_DK), BF16),
        ],
        compiler_params=_cparams(("parallel", "parallel")),
        name=name,
    )(*args)


def _gelu_tanh(x):
    return x * (0.5 * (1.0 + jnp.tanh(0.7978845608028654 * (x + 0.044715 * (x * x * x)))))


def _run_staggered(tiles, stagger):
    n_st = len(tiles[0])
    for clock in range(n_st + stagger * (len(tiles) - 1)):
        for i, st in enumerate(tiles):
            k = clock - stagger * i
            if 0 <= k < n_st:
                st[k]()


def _tile_specs(n_tiles, tm, t, per_batch_mod):
    tps = TILES_PER_STEP
    assert n_tiles % tps == 0 and (t % (tps * tm) == 0 or not per_batch_mod)
    spec = pl.BlockSpec((tps, tm, D_MODEL), lambda i: (i, 0, 0))
    const = lambda shape: pl.BlockSpec(shape, lambda i: (0,) * len(shape))
    if per_batch_mod:
        mod_spec = pl.BlockSpec((None, N_MOD, D_MODEL), lambda i: ((i * tps * tm) // t + 1, 0, 0))
    else:
        mod_spec = pl.BlockSpec((None, N_MOD, D_MODEL), lambda i: (0, 0, 0))
    return spec, mod_spec, const


def _merge_kernel(x_ref, hf_ref, hb_ref, zg_ref, ga_ref, gb_ref, on_ref, mod_ref, ng_ref,
                  unperm_ref, lup_ref, gup_ref, wo_ref, o_ref):
    def stages(i):
        ld = lambda r: r[i].astype(F32)
        v = {}

        def s0():
            v['hs'] = (ld(hf_ref) + ld(hb_ref)).astype(BF16)
            v['gz'] = _gelu_tanh(ld(zg_ref))

        def s1():
            hs = _bdot(unperm_ref[...], v['hs'])
            v['y_a'] = _bdot((hs * v['gz']).astype(BF16), lup_ref[...])

        def s2():
            v['y_b'] = _bdot(on_ref[i], gup_ref[...])

        def s3():
            v['mm'] = (_sigmoid(ld(ga_ref)) * v['y_a'] + _sigmoid(ld(gb_ref)) * v['y_b']).astype(BF16)

        def s4():
            v['m'] = _bdot(v['mm'], wo_ref[...])

        def s5():
            o_ref[i] = x_ref[i] + mod_ref[2:3, :] * _rms(v['m'], ng_ref[1:2, :])

        return [s0, s1, s2, s3, s4, s5]

    _run_staggered([stages(i) for i in range(x_ref.shape[0])], MERGE_STAGGER)


def _merge(x, hf, hb, zg, ga, gb, on, mod, norm_g, lru_up, gla_up, w_out, *, per_batch_mod, name):
    b, t, _ = x.shape
    tm = _lru_chunk(t)
    n_tiles = b * t // tm
    spec, mod_spec, const = _tile_specs(n_tiles, tm, t, per_batch_mod)
    tiled = lambda a: a.reshape(n_tiles, tm, a.shape[-1])
    out = pl.pallas_call(
        _merge_kernel,
        grid=(n_tiles // TILES_PER_STEP,),
        in_specs=[spec] * 7 + [mod_spec, const((4, D_MODEL)), const((tm, tm))]
        + [const((D_MODEL, D_MODEL))] * 3,
        out_specs=spec,
        out_shape=jax.ShapeDtypeStruct((n_tiles, tm, D_MODEL), F32),
        compiler_params=_cparams(("parallel",)),
        name=name,
    )(*(tiled(a) for a in (x, hf, hb, zg, ga, gb, on)), mod, norm_g,
      jnp.asarray(_interleave_matrix(tm).T, BF16), lru_up, gla_up, w_out)
    return out.reshape(b, t, D_MODEL)


def _ffn_kernel(x_ref, mod_ref, ng_ref, w1_ref, w2_ref, o_ref):
    fs = D_FF // FFN_SPLIT

    def stages(i):
        v = {}

        def norm():
            x = x_ref[i]
            v['h'] = (_rms(x, ng_ref[2:3, :]) * (1.0 + mod_ref[4:5, :]) + mod_ref[3:4, :]).astype(BF16)

        def hidden(kf):
            hid = jnp.maximum(_bdot(v['h'], w1_ref[:, kf * fs:(kf + 1) * fs]), 0.0)
            part = _bdot((hid * hid).astype(BF16), w2_ref[kf * fs:(kf + 1) * fs, :])
            v['acc'] = part if kf == 0 else v['acc'] + part

        def out():
            o_ref[i] = x_ref[i] + mod_ref[5:6, :] * _rms(v['acc'], ng_ref[3:4, :])

        return [norm] + [functools.partial(hidden, kf) for kf in range(FFN_SPLIT)] + [out]

    _run_staggered([stages(i) for i in range(x_ref.shape[0])], FFN_STAGGER)


def _ffn(x, mod, norm_g, w1, w2, *, per_batch_mod, name):
    b, t, _ = x.shape
    tm = _lru_chunk(t)
    n_tiles = b * t // tm
    spec, mod_spec, const = _tile_specs(n_tiles, tm, t, per_batch_mod)
    out = pl.pallas_call(
        _ffn_kernel,
        grid=(n_tiles // TILES_PER_STEP,),
        in_specs=[spec, mod_spec, const((4, D_MODEL)), const((D_MODEL, D_FF)), const((D_FF, D_MODEL))],
        out_specs=spec,
        out_shape=jax.ShapeDtypeStruct((n_tiles, tm, D_MODEL), F32),
        compiler_params=_cparams(("parallel",)),
        name=name,
    )(x.reshape(n_tiles, tm, D_MODEL), mod, norm_g, w1, w2)
    return out.reshape(b, t, D_MODEL)


def _blockdiag_slabs(w):
    per = LRU_COLB // LRU_BW
    w = w.reshape(2, LRU_NCOLB, per, LRU_BW, LRU_BW)
    eye = jnp.eye(per, dtype=w.dtype)
    slab = jnp.einsum('dspij,pq->dspiqj', w, eye)
    return slab.reshape(2, LRU_NCOLB, LRU_COLB, LRU_COLB).astype(BF16)


def _layer_group(x, mod, p, lru_h0, gla_s0, *, latent, want_state, tag):
    b, t, _ = x.shape
    per_batch = latent
    ga, gb, zg, zx = _inproj(x, mod, p['norm_g'], p['w_rm'], (None,) * 4,
                             per_batch_mod=per_batch, name=f"inproj_rm_{tag}")
    period = GRID_W if latent else t
    (q, k, v, g, lr), hf, hb, s_lru = _cm_lru(
        x, mod, p['norm_g'], p['w_cm'], (None, None, None, _silu, None), zx, lru_h0, p['conv_w'],
        p['conv_b'], p['wa_bd'], p['wx_bd'], p['lru_ba'], p['lru_bx'], p['lru_L'], per_batch_mod=per_batch,
        order='col_major' if latent else 'natural', period=period, name=f"cm_lru_{tag}")
    res = _gla(q, k, v, g, lr, p['w2p'], p['gla_b2'], p['gla_norm_g'], gla_s0,
               want_final=want_state, hps=GLA_HEADS_PER_STEP if latent else GLA_HEADS, col_major=latent,
               name=f"gla_{tag}")
    on = res[0]
    s_gla = res[1] if want_state else None
    x1 = _merge(x, hf, hb, zg, ga, gb, on, mod, p['norm_g'], p['lru_up'], p['gla_up'], p['w_out'],
                per_batch_mod=per_batch, name=f"merge_{tag}")
    y = _ffn(x1, mod, p['norm_g'], p['mlp_w1'], p['mlp_w2'], per_batch_mod=per_batch, name=f"ffn_{tag}")
    return y, s_lru, s_gla


def kernel(x_prompt, x_sample, state_lru, state_gla, c, c_ctx, w_mod, b_mod, norm_g, w_in, conv_w, conv_b,
           lru_wa, lru_ba, lru_wx, lru_bx, lru_L, lru_up, gla_w2, gla_b2, gla_norm_g, gla_up, w_out,
           mlp_w1, mlp_w2):
    depth = w_in.shape[0]
    assert depth == 1
    dec_b = x_sample.shape[0]
    b0 = x_prompt.shape[0]
    xp, xs = x_prompt, x_sample
    lru_states, gla_states = [], []
    mod_rows = -(-(1 + dec_b) // V7X_SUBLANES) * V7X_SUBLANES
    cc = jnp.zeros((mod_rows, D_MODEL), F32).at[0].set(c_ctx).at[1:1 + dec_b].set(c)
    for l in range(depth):
        wl = w_in[l]
        o = np.cumsum((0, D_MODEL, D_MODEL, 512, 512, 1024, 1024, 2 * GLA_RANK, D_MODEL, D_MODEL))
        col = lambda i: wl[:, o[i]:o[i + 1]].astype(BF16)
        lr_w = jnp.pad(col(6), ((0, 0), (0, LR_PAD - 2 * GLA_RANK)))
        w2 = gla_w2[l].astype(BF16)
        w2p = jnp.zeros((2, LR_PAD, GLA_HEADS * GLA_DK), BF16)
        w2p = w2p.at[0, 0:GLA_RANK].set(w2[0]).at[1, GLA_RANK:2 * GLA_RANK].set(w2[1])
        p = {
            'norm_g': norm_g[l],
            'w_rm': [col(7), col(8), col(1), col(0)],
            'w_cm': [col(2), col(3), col(4), col(5), lr_w],
            'conv_w': conv_w[l], 'conv_b': conv_b[l].reshape(1, D_MODEL),
            'wa_bd': _blockdiag_slabs(0.5 * lru_wa[l]), 'wx_bd': _blockdiag_slabs(0.5 * lru_wx[l]),
            'lru_ba': lru_ba[l], 'lru_bx': lru_bx[l], 'lru_L': lru_L[l],
            'lru_up': lru_up[l].astype(BF16),
            'w2p': w2p, 'gla_b2': gla_b2[l], 'gla_norm_g': gla_norm_g[l].reshape(1, GLA_DV),
            'gla_up': gla_up[l].astype(BF16), 'w_out': w_out[l].astype(BF16),
            'mlp_w1': mlp_w1[l].astype(BF16), 'mlp_w2': mlp_w2[l].astype(BF16),
        }
        mod = _modulation(cc, w_mod[l], b_mod[l])
        h0_ctx = jnp.zeros((b0, 2, D_MODEL), F32)
        xp, s_lru, s_gla = _layer_group(xp, mod, p, h0_ctx, None, latent=False, want_state=True,
                                        tag="ctx")
        lru_states.append(s_lru)
        gla_states.append(s_gla)
        xs, _, _ = _layer_group(xs, mod, p, state_lru[:, l], state_gla[:, l], latent=True,
                                want_state=False, tag="lat")
    return (xp, xs, jnp.stack(lru_states, axis=1), jnp.stack(gla_states, axis=1))
```

```python
import functools

import numpy as np
import jax
import jax.numpy as jnp
from jax import lax
from jax.experimental import pallas as pl
from jax.experimental.pallas import tpu as pltpu

F32 = jnp.float32
BF16 = jnp.bfloat16

D_MODEL = 1024
GRID_W = 64
LRU_BLOCKS = 16
LRU_BW = D_MODEL // LRU_BLOCKS
LRU_C = 8.0
GLA_HEADS = 4
GLA_DK = 128
GLA_DV = 256
GLA_RANK = 16
GLA_TAU = 16.0
GLA_CHUNK = 64
D_FF = 4 * D_MODEL
N_MOD = 6
EPS = 1e-6
LOG2_E = 1.4426950408889634

V7X_LANES = 128
V7X_SUBLANES = 8
V7X_MXU_DIM = 256
V7X_VMEM_BYTES = 64 * 1024 * 1024
VMEM_LIMIT = V7X_VMEM_BYTES - 8 * 1024 * 1024

LRU_COLB = V7X_MXU_DIM
LRU_NCOLB = D_MODEL // LRU_COLB
LR_PAD = V7X_LANES
COL_BLOCK = 16
GLA_BLOCK = V7X_MXU_DIM
GLA_HEADS_PER_STEP = 2
GLA_SCORE_BLOCK = 2 * GLA_CHUNK
GLA_OUT_COLS = 16
LRU_CHUNK = 256
TILES_PER_STEP = 2
MERGE_STAGGER = 2
INPROJ_STAGGER = 2
FFN_SPLIT = 4
FFN_STAGGER = 3
LRU_SCAN_CUTS = 1
CM_PIECE = 256


def _cparams(sem):
    return pltpu.CompilerParams(dimension_semantics=sem, vmem_limit_bytes=VMEM_LIMIT)


def _rms(x, g):
    ms = jnp.mean(x * x, axis=-1, keepdims=True)
    return x * lax.rsqrt(ms + EPS) * g


def _sigmoid(x):
    return 0.5 * jnp.tanh(0.5 * x) + 0.5


def _softplus(x):
    return jnp.maximum(x, 0.0) + jnp.log1p(jnp.exp(-jnp.abs(x)))


def _bdot(a, b):
    return jnp.dot(a, b, preferred_element_type=F32)


def _mod_kernel(c_ref, w_ref, b_ref, o_ref):
    c = c_ref[...]
    s = (c * _sigmoid(c)).astype(BF16)
    o_ref[...] = _bdot(s, w_ref[...].astype(BF16)) + b_ref[...]


def _modulation(cc, w_mod, b_mod):
    rows = cc.shape[0]
    return pl.pallas_call(
        _mod_kernel,
        grid=(N_MOD,),
        in_specs=[
            pl.BlockSpec((rows, D_MODEL), lambda n: (0, 0)),
            pl.BlockSpec((D_MODEL, D_MODEL), lambda n: (0, n)),
            pl.BlockSpec((1, D_MODEL), lambda n: (0, n)),
        ],
        out_specs=pl.BlockSpec((rows, D_MODEL), lambda n: (0, n)),
        out_shape=jax.ShapeDtypeStruct((rows, N_MOD * D_MODEL), F32),
        compiler_params=_cparams(("arbitrary",)),
        name="modulation",
    )(cc, w_mod, b_mod.reshape(1, N_MOD * D_MODEL)).reshape(rows, N_MOD, D_MODEL)


def _tok_view(a, grid_tile):
    if grid_tile is None:
        return a
    b, t, n = a.shape
    rows = t // GRID_W
    return a.reshape(b, rows, GRID_W // COL_BLOCK, COL_BLOCK, n)


def _tok_spec(n, tm, grid_tile):
    if grid_tile is None:
        return pl.BlockSpec((None, tm, n), lambda b, j: (b, j, 0))
    rows = grid_tile
    return pl.BlockSpec((None, rows, None, COL_BLOCK, n), lambda b, j: (b, 0, j, 0, 0))


def _const_spec(shape):
    nd = len(shape)
    return pl.BlockSpec(shape, lambda b, j: (0,) * nd)


def _mod_spec(per_batch):
    if per_batch:
        return pl.BlockSpec((None, N_MOD, D_MODEL), lambda b, j: (b + 1, 0, 0))
    return pl.BlockSpec((None, N_MOD, D_MODEL), lambda b, j: (0, 0, 0))


def _perm_matrix(rows, cols):
    n = rows * cols
    p = np.zeros((n, n), np.float32)
    for r in range(rows):
        for c in range(cols):
            p[c * rows + r, r * cols + c] = 1.0
    return p


def _silu(x):
    return x * _sigmoid(x)


def _inproj_pieces(x_ref, mod_ref, ng_ref, perm_ref, w_refs, o_refs, *, acts, n_piece, tile=None):
    v = {}

    def norm():
        x = (x_ref[...] if tile is None else x_ref[tile]).reshape(-1, D_MODEL)
        h = _rms(x, ng_ref[0:1, :]) * (1.0 + mod_ref[1:2, :]) + mod_ref[0:1, :]
        v['h'] = h.astype(BF16)
        if perm_ref is not None:
            v['h'] = _bdot(perm_ref[...], v['h']).astype(BF16)

    pieces = [norm]
    for i, (w_ref, o_ref) in enumerate(zip(w_refs, o_refs)):
        n = w_ref.shape[1]
        width = min(n, n_piece)
        for c0 in range(0, n, width):
            def piece(i=i, w_ref=w_ref, o_ref=o_ref, cs=slice(c0, c0 + width)):
                z = _bdot(v['h'], w_ref[:, cs])
                if acts[i] is not None:
                    z = acts[i](z)
                if tile is None:
                    o_ref[:, cs] = z.astype(o_ref.dtype)
                else:
                    o_ref[tile, :, cs] = z.astype(o_ref.dtype)
            pieces.append(piece)
    return pieces


def _inproj_kernel(*refs, n_out, acts):
    x_ref, mod_ref, ng_ref = refs[:3]
    _run_staggered([_inproj_pieces(x_ref, mod_ref, ng_ref, None, refs[3:3 + n_out], refs[3 + n_out:],
                                   acts=acts, n_piece=D_MODEL, tile=i)
                    for i in range(x_ref.shape[0])], INPROJ_STAGGER)


def _inproj_setup(x, weights, order):
    b, t, _ = x.shape
    if order == 'col_major':
        rows = t // GRID_W
        tm = rows * COL_BLOCK
        return _tok_view(x, rows), _tok_spec(D_MODEL, tm, rows), tm, _perm_matrix(rows, COL_BLOCK)
    tm = _lru_chunk(t)
    perm = _interleave_matrix(tm) if order == 'interleaved' else None
    return x, _tok_spec(D_MODEL, tm, None), tm, perm


def _inproj(x, mod, norm_g, weights, acts, *, per_batch_mod, name):
    b, t, _ = x.shape
    tm = _lru_chunk(t)
    n_tiles = b * t // tm
    spec, mod_spec, const = _tile_specs(n_tiles, tm, t, per_batch_mod)
    outs = pl.pallas_call(
        functools.partial(_inproj_kernel, n_out=len(weights), acts=tuple(acts)),
        grid=(n_tiles // TILES_PER_STEP,),
        in_specs=[spec, mod_spec, const((4, D_MODEL))] + [const(w.shape) for w in weights],
        out_specs=[pl.BlockSpec((TILES_PER_STEP, tm, w.shape[1]), lambda i: (i, 0, 0)) for w in weights],
        out_shape=[jax.ShapeDtypeStruct((n_tiles, tm, w.shape[1]), BF16) for w in weights],
        compiler_params=_cparams(("parallel",)),
        name=name,
    )(x.reshape(n_tiles, tm, D_MODEL), mod, norm_g, *weights)
    return [o.reshape(b, t, o.shape[-1]) for o in outs]


def _lru_chunk(t):
    return min(LRU_CHUNK, t)


def _interleave_matrix(tc):
    return _perm_matrix(V7X_SUBLANES, tc // V7X_SUBLANES)


def _lru_pieces(zf_ref, zb_ref, rf0, rb0, il_ref, cw_ref, cb_ref, wa_ref, wx_ref, ba_ref, bx_ref, l_ref,
                hf_ref, hb_ref, fin_ref, a_s, u_s, carry_s, zi_s, *, tc, period):
    ns = V7X_SUBLANES
    sl = tc // ns
    sub_c = lax.broadcasted_iota(jnp.int32, (ns, LRU_COLB), 0)
    prev_ok = jnp.bitwise_and(sub_c * sl, period - 1) != 0
    next_ok = jnp.bitwise_and((sub_c + 1) * sl, period - 1) != 0

    def reorder(d, z_ref, r0):
        zi_s[d] = _bdot(il_ref[...], z_ref[r0:r0 + tc, :]).astype(BF16)

    def gates(d, cb):
        cs = slice(cb * LRU_COLB, (cb + 1) * LRU_COLB)
        x = zi_s[d, :, cs].astype(F32)
        e_prev = jnp.where(prev_ok, pltpu.roll(x[tc - ns:], 1, 0), 0.0)
        e_next0 = jnp.where(next_ok, pltpu.roll(x[:ns], ns - 1, 0), 0.0)
        e_next1 = jnp.where(next_ok, pltpu.roll(x[ns:2 * ns], ns - 1, 0), 0.0)
        xm1 = jnp.concatenate([e_prev, x[:tc - ns]], axis=0)
        xp1 = jnp.concatenate([x[ns:], e_next0], axis=0)
        xp2 = jnp.concatenate([x[2 * ns:], e_next0, e_next1], axis=0)
        cw = 0.5 * cw_ref[:, cs]
        xh = 0.5 * cb_ref[0:1, cs] + xm1 * cw[0:1] + x * cw[1:2] + xp1 * cw[2:3] + xp2 * cw[3:4]
        xhb = xh.astype(BF16)
        tr = jnp.tanh(_bdot(xhb, wa_ref[d, cb]) + 0.5 * ba_ref[d:d + 1, cs])
        ti = jnp.tanh(_bdot(xhb, wx_ref[d, cb]) + 0.5 * bx_ref[d:d + 1, cs])
        k2 = (-0.5 * LRU_C * LOG2_E) * _softplus(-l_ref[d:d + 1, cs])
        a = jnp.exp2(tr * k2 + k2)
        y = 1.0 - a * a
        u = jnp.where(y > 0.0, y * lax.rsqrt(y), 0.0) * ((ti + 1.0) * xh)
        a_s[d, :, cs] = a
        u_s[d, :, cs] = u

    state = {}
    n_cut = LRU_SCAN_CUTS

    def pass1(part):
        if part == 0:
            zeros = jnp.zeros((ns, D_MODEL), F32)
            ones = jnp.ones((ns, D_MODEL), F32)
            state['c'] = (zeros, ones, zeros, ones)
        hf, af, hb, ab = state['c']
        for i in range(part * sl // n_cut, (part + 1) * sl // n_cut):
            rf = slice(i * ns, (i + 1) * ns)
            rb = slice((sl - 1 - i) * ns, (sl - i) * ns)
            a = a_s[0, rf, :]
            hf = a * hf + u_s[0, rf, :]
            af = a * af
            u_s[0, rf, :] = hf
            a_s[0, rf, :] = af
            a = a_s[1, rb, :]
            hb = a * hb + u_s[1, rb, :]
            ab = a * ab
            u_s[1, rb, :] = hb
            a_s[1, rb, :] = ab
        state['c'] = (hf, af, hb, ab)

    def chain():
        hf, af, hb, ab = state['c']
        sub = lax.broadcasted_iota(jnp.int32, (ns, D_MODEL), 0)

        def sublane_scan(a, u, forward):
            for s in (1, 2, 4):
                if forward:
                    m = sub >= s
                    shift = s
                else:
                    m = sub < ns - s
                    shift = ns - s
                a_sh = jnp.where(m, pltpu.roll(a, shift, 0), 1.0)
                u_sh = jnp.where(m, pltpu.roll(u, shift, 0), 0.0)
                u = a * u_sh + u
                a = a * a_sh
            return a, u

        cin_f = carry_s[0:1, :]
        cin_b = carry_s[1:2, :]
        af, hf = sublane_scan(af, hf, True)
        ab, hb = sublane_scan(ab, hb, False)
        fin_f = hf + af * cin_f
        fin_b = hb + ab * cin_b
        enter_f = jnp.where(sub == 0, cin_f, pltpu.roll(fin_f, 1, 0))
        enter_b = jnp.where(sub == ns - 1, cin_b, pltpu.roll(fin_b, ns - 1, 0))
        out_f = fin_f[ns - 1:ns, :]
        out_b = fin_b[0:1, :]
        carry_s[0:1, :] = out_f
        carry_s[1:2, :] = out_b
        fin_ref[0:1, :] = out_f
        fin_ref[1:2, :] = out_b
        state['e'] = (jnp.concatenate([enter_f, enter_f], axis=0), jnp.concatenate([enter_b, enter_b], axis=0))

    def pass2(part):
        pk = 2 * ns
        enter_f2, enter_b2 = state['e']
        for i in range(part * (sl // 2) // n_cut, (part + 1) * (sl // 2) // n_cut):
            rr = slice(i * pk, (i + 1) * pk)
            hf_ref[rf0 + i * pk:rf0 + (i + 1) * pk, :] = (
                u_s[0, rr, :] + a_s[0, rr, :] * enter_f2).astype(hf_ref.dtype)
            hb_ref[rb0 + i * pk:rb0 + (i + 1) * pk, :] = (
                u_s[1, rr, :] + a_s[1, rr, :] * enter_b2).astype(hb_ref.dtype)

    scan = ([functools.partial(pass1, part) for part in range(n_cut)] + [chain]
            + [functools.partial(pass2, part) for part in range(n_cut)])

    pieces = [functools.partial(reorder, 0, zf_ref, rf0), functools.partial(reorder, 1, zb_ref, rb0)]
    pieces += [functools.partial(gates, d, cb) for d in range(2) for cb in range(LRU_NCOLB)]
    return pieces + scan


def _interleave(major, minor):
    out, done = [], 0
    for i, p in enumerate(major):
        want = -(-(i + 1) * len(minor) // len(major))
        out.extend(minor[done:want])
        done = want
        out.append(p)
    return out


def _cm_lru_kernel(*refs, n_cm, permute, acts, tc, period, cps):
    x_ref, mod_ref, ng_ref = refs[:3]
    k = 4 if permute else 3
    perm_ref = refs[3] if permute else None
    w_refs = refs[k:k + n_cm]
    k += n_cm
    zf_ref, zb_ref, h0_ref, il_ref, cw_ref, cb_ref, wa_ref, wx_ref, ba_ref, bx_ref, l_ref = refs[k:k + 11]
    k += 11
    o_refs = refs[k:k + n_cm]
    k += n_cm
    hf_ref, hb_ref, fin_ref, a_s, u_s, carry_s, zi_s = refs[k:]

    @pl.when(pl.program_id(1) == 0)
    def _():
        carry_s[...] = h0_ref[...]

    cm = _inproj_pieces(x_ref, mod_ref, ng_ref, perm_ref, w_refs, o_refs, acts=acts, n_piece=CM_PIECE)
    lru = []
    for s in range(cps):
        lru += _lru_pieces(zf_ref, zb_ref, s * tc, (cps - 1 - s) * tc, il_ref, cw_ref, cb_ref, wa_ref, wx_ref,
                           ba_ref, bx_ref, l_ref, hf_ref, hb_ref, fin_ref, a_s, u_s, carry_s, zi_s,
                           tc=tc, period=period)
    cm[0]()
    for piece in _interleave(lru, cm[1:]):
        piece()


def _cm_lru(x, mod, norm_g, weights, acts, zx, h0, conv_w, conv_b, wa_bd, wx_bd, ba, bx, lru_l, *,
            per_batch_mod, order, period, name):
    b, t, _ = x.shape
    x_in, x_spec, tm, perm = _inproj_setup(x, weights, order)
    extra = [] if perm is None else [jnp.asarray(perm, BF16)]
    n_steps = t // tm
    tc = _lru_chunk(t)
    cps = t // tc // n_steps
    assert cps >= 1 and cps * tc * n_steps == t
    assert period % (tc // V7X_SUBLANES) == 0 and tc % period == 0
    blk = cps * tc
    res = pl.pallas_call(
        functools.partial(_cm_lru_kernel, n_cm=len(weights), permute=perm is not None, acts=tuple(acts),
                          tc=tc, period=period, cps=cps),
        grid=(b, n_steps),
        in_specs=[x_spec, _mod_spec(per_batch_mod), _const_spec((4, D_MODEL))]
        + [_const_spec(e.shape) for e in extra] + [_const_spec(w.shape) for w in weights]
        + [
            pl.BlockSpec((None, blk, D_MODEL), lambda bb, j: (bb, j, 0)),
            pl.BlockSpec((None, blk, D_MODEL), lambda bb, j: (bb, n_steps - 1 - j, 0)),
            pl.BlockSpec((None, 2, D_MODEL), lambda bb, j: (bb, 0, 0)),
            _const_spec((tc, tc)), _const_spec((4, D_MODEL)), _const_spec((1, D_MODEL)),
            _const_spec((2, LRU_NCOLB, LRU_COLB, LRU_COLB)), _const_spec((2, LRU_NCOLB, LRU_COLB, LRU_COLB)),
            _const_spec((2, D_MODEL)), _const_spec((2, D_MODEL)), _const_spec((2, D_MODEL)),
        ],
        out_specs=[_tok_spec(w.shape[1], tm, None) for w in weights] + [
            pl.BlockSpec((None, blk, D_MODEL), lambda bb, j: (bb, j, 0)),
            pl.BlockSpec((None, blk, D_MODEL), lambda bb, j: (bb, n_steps - 1 - j, 0)),
            pl.BlockSpec((None, 2, D_MODEL), lambda bb, j: (bb, 0, 0)),
        ],
        out_shape=[jax.ShapeDtypeStruct((b, t, w.shape[1]), BF16) for w in weights] + [
            jax.ShapeDtypeStruct((b, t, D_MODEL), BF16),
            jax.ShapeDtypeStruct((b, t, D_MODEL), BF16),
            jax.ShapeDtypeStruct((b, 2, D_MODEL), F32),
        ],
        scratch_shapes=[
            pltpu.VMEM((2, tc, D_MODEL), F32),
            pltpu.VMEM((2, tc, D_MODEL), F32),
            pltpu.VMEM((2, D_MODEL), F32),
            pltpu.VMEM((2, tc, D_MODEL), BF16),
        ],
        compiler_params=_cparams(("parallel", "arbitrary")),
        name=name,
    )(x_in, mod, norm_g, *extra, *weights, zx, zx, h0, jnp.asarray(_interleave_matrix(tc), BF16),
      conv_w, conv_b, wa_bd, wx_bd, ba, bx, lru_l)
    return res[:len(weights)], res[len(weights)], res[len(weights) + 1], res[len(weights) + 2]


def _gla_kernel(*refs, t, hps, has_s0, want_final, grid_rows):
    q_ref, k_ref, v_ref, g_ref, lr_ref, w2_ref, b2_ref, gn_ref = refs[:8]
    n = 8
    if grid_rows is not None:
        perm_ref = refs[n]
        n += 1
    if has_s0:
        s0_ref = refs[n]
        n += 1
    on_ref = refs[n]
    n += 1
    if want_final:
        fin_ref = refs[n]
        n += 1
    oacc, st, q2_s, k2_s, bt_s, pp_q, pp_k, pp_qx, pp_kx = refs[n:]

    c = GLA_CHUNK
    blk = min(GLA_BLOCK, t)
    for d in range(2):
        for hh in range(hps):
            if has_s0:
                st[d, hh] = s0_ref[d, hh].T
            else:
                st[d, hh] = jnp.zeros((GLA_DV, GLA_DK), F32)

    ti = lax.broadcasted_iota(jnp.int32, (blk, blk), 0)
    si = lax.broadcasted_iota(jnp.int32, (blk, blk), 1)
    same = (ti // c) == (si // c)
    cums = ((same & (si <= ti)).astype(BF16), (same & (si >= ti)).astype(BF16))
    sb = min(GLA_SCORE_BLOCK, blk)
    tj = lax.broadcasted_iota(jnp.int32, (sb, sb), 0)
    sj = lax.broadcasted_iota(jnp.int32, (sb, sb), 1)
    same_j = (tj // c) == (sj // c)
    masks = (same_j & (sj <= tj), same_j & (sj >= tj))
    scale = GLA_DK ** -0.5

    def step(ia, slot_a, ib, slot_b):
        nd = hps * GLA_DK
        if ia is not None:
            rows_a = pl.ds(pl.multiple_of(ia * blk, blk), blk)
            lrb = lr_ref[rows_a, :]
            pre = [_bdot(lrb, w2_ref[d]) + b2_ref[d:d + 1, :] for d in range(2)]
        if ib is not None:
            rows_b = [pl.ds(pl.multiple_of(ib * blk + g * sb, sb), sb) for g in range(blk // sb)]
            loc_b = [slice(g * sb, (g + 1) * sb) for g in range(blk // sb)]
            nt = (((1,), (1,)), ((), ()))
            raw = [[[lax.dot_general(pp_q[slot_b, d, loc, hh * GLA_DK:(hh + 1) * GLA_DK],
                                     pp_k[slot_b, d, loc, hh * GLA_DK:(hh + 1) * GLA_DK], nt,
                                     preferred_element_type=F32) for d in range(2)]
                    + [lax.dot_general(pp_qx[slot_b, loc, hh * 2 * GLA_DK:(hh + 1) * 2 * GLA_DK],
                                       pp_kx[slot_b, loc, hh * 2 * GLA_DK:(hh + 1) * 2 * GLA_DK], nt,
                                       preferred_element_type=F32)]
                    for hh in range(hps)] for loc in loc_b]
        if ia is not None:
            bcum = []
            for d in range(2):
                soft = jnp.log(1.0 + jnp.exp2(jnp.abs(pre[d]) * (-LOG2_E)))
                la = jnp.minimum(pre[d], 0.0) * (LOG2_E / GLA_TAU) - soft * (LOG2_E / GLA_TAU)
                hi = la.astype(BF16)
                mid = (la - hi.astype(F32)).astype(BF16)
                bcum.append(_bdot(cums[d], hi) + _bdot(cums[d], mid))
        if ib is not None:
            for g, rows in enumerate(rows_b):
                for hh in range(hps):
                    vs = slice(hh * GLA_DV, (hh + 1) * GLA_DV)
                    sd0, sd1, sx = raw[g][hh]
                    sc = jnp.where(masks[0], sd0, 0.0) + jnp.where(masks[1], sd1, 0.0) + sx
                    oacc[rows, vs] = _bdot(sc.astype(BF16), v_ref[rows, vs])
        if ia is not None:
            q = q_ref[rows_a, :].astype(F32)
            k = k_ref[rows_a, :].astype(F32)
            nc = blk // c
            crow = lambda a, n: a[n * c:(n + 1) * c]
            cat = lambda parts: jnp.concatenate(parts, axis=0)
            zero = jnp.zeros((c, nd), BF16)
            qx, kx = [], []
            for d in range(2):
                edge = c - 1 if d == 0 else 0
                tot = [bcum[d][n * c + edge:n * c + edge + 1, :] for n in range(nc)]
                dec = [jnp.exp2(x) for x in tot]
                btot = cat([jnp.broadcast_to(x, (c, nd)) for x in tot])
                qe = (q * scale) * jnp.exp2(bcum[d])
                kt = k * jnp.exp2(btot - bcum[d])
                pp_q[slot_a, d] = qe.astype(BF16)
                pp_k[slot_a, d] = (k * jnp.exp2(-bcum[d])).astype(BF16)
                first = [(n % 2 == 0) == (d == 0) for n in range(nc)]
                q2 = cat([crow(qe, n) if first[n] else crow(qe, n) * dec[n ^ 1] for n in range(nc)])
                k2 = cat([crow(kt, n) * dec[n ^ 1] if first[n] else crow(kt, n) for n in range(nc)])
                q2_s[d, rows_a, :] = q2.astype(BF16)
                k2_s[d, rows_a, :] = k2.astype(BF16)
                bt_s[d, rows_a, :] = cat([jnp.broadcast_to(tot[n - n % 2] + tot[n - n % 2 + 1], (c, nd))
                                          for n in range(nc)])
                qeb, ktb = qe.astype(BF16), kt.astype(BF16)
                qx.append(cat([zero if first[n] else crow(qeb, n) for n in range(nc)]))
                kx.append(cat([crow(ktb, n) if first[n] else zero for n in range(nc)]))
            for hh in range(hps):
                ks = slice(hh * GLA_DK, (hh + 1) * GLA_DK)
                xs = slice(hh * 2 * GLA_DK, (hh + 1) * 2 * GLA_DK)
                pp_qx[slot_a, :, xs] = jnp.concatenate([qx[0][:, ks], qx[1][:, ks]], axis=1)
                pp_kx[slot_a, :, xs] = jnp.concatenate([kx[0][:, ks], kx[1][:, ks]], axis=1)

    n_blocks = t // blk
    step(0, 0, None, None)
    for ib in range(n_blocks - 1):
        step(ib + 1, (ib + 1) % 2, ib, ib % 2)
    step(None, None, n_blocks - 1, (n_blocks - 1) % 2)

    pr = 2 * c
    n_prs = t // pr

    def pair_step(p0, d, hh):
        rows = pl.ds(p0, pr)
        ks = slice(hh * GLA_DK, (hh + 1) * GLA_DK)
        vs = slice(hh * GLA_DV, (hh + 1) * GLA_DV)
        s_t = st[d, hh]
        o = lax.dot_general(q2_s[d, rows, ks], s_t.astype(BF16), (((1,), (1,)), ((), ())),
                            preferred_element_type=F32)
        upd = lax.dot_general(v_ref[rows, vs], k2_s[d, rows, ks], (((0,), (0,)), ((), ())),
                              preferred_element_type=F32)
        st[d, hh] = s_t * jnp.exp2(bt_s[d, pl.ds(p0, 1), ks]) + upd
        return o

    for jj in range(n_prs):
        for hh in range(hps):
            vs = slice(hh * GLA_DV, (hh + 1) * GLA_DV)
            for p0, d in ((jj * pr, 0), ((n_prs - 1 - jj) * pr, 1)):
                oacc[pl.ds(p0, pr), vs] += pair_step(p0, d, hh)

    if grid_rows is None:
        ob = min(GLA_BLOCK, t)
    else:
        ob = GLA_OUT_COLS * grid_rows

    for ib in range(t // ob):
        rows = slice(ib * ob, (ib + 1) * ob)
        parts = []
        for hh in range(hps):
            vs = slice(hh * GLA_DV, (hh + 1) * GLA_DV)
            parts.append((_rms(oacc[rows, vs], gn_ref[...]) * g_ref[rows, vs].astype(F32)).astype(BF16))
        on = jnp.concatenate(parts, axis=1)
        if grid_rows is None:
            on_ref[rows, :] = on
        else:
            on = _bdot(perm_ref[...], on).astype(BF16)
            for r in range(grid_rows):
                dst = r * GRID_W + ib * GLA_OUT_COLS
                on_ref[dst:dst + GLA_OUT_COLS, :] = on[r * GLA_OUT_COLS:(r + 1) * GLA_OUT_COLS]

    if want_final:
        for d in range(2):
            for hh in range(hps):
                fin_ref[d, hh] = st[d, hh].T


def _gla(q, k, v, g, lr, w2p, b2, gn, s0, *, want_final, hps, col_major, name):
    b, t, _ = q.shape
    ng = GLA_HEADS // hps
    has_s0 = s0 is not None
    grid_rows = t // GRID_W if col_major else None
    in_specs = [
        pl.BlockSpec((None, t, hps * GLA_DK), lambda bb, h: (bb, 0, h)),
        pl.BlockSpec((None, t, hps * GLA_DK), lambda bb, h: (bb, 0, h)),
        pl.BlockSpec((None, t, hps * GLA_DV), lambda bb, h: (bb, 0, h)),
        pl.BlockSpec((None, t, hps * GLA_DV), lambda bb, h: (bb, 0, h)),
        pl.BlockSpec((None, t, LR_PAD), lambda bb, h: (bb, 0, 0)),
        pl.BlockSpec((2, LR_PAD, hps * GLA_DK), lambda bb, h: (0, 0, h)),
        pl.BlockSpec((2, hps * GLA_DK), lambda bb, h: (0, h)),
        pl.BlockSpec((1, GLA_DV), lambda bb, h: (0, 0)),
    ]
    args = [q, k, v, g, lr, w2p, b2, gn]
    if col_major:
        ob = GLA_OUT_COLS * grid_rows
        in_specs.append(pl.BlockSpec((ob, ob), lambda bb, h: (0, 0)))
        args.append(jnp.asarray(_perm_matrix(grid_rows, GLA_OUT_COLS).T, BF16))
    state_spec = pl.BlockSpec((None, 2, hps, GLA_DK, GLA_DV), lambda bb, h: (bb, 0, h, 0, 0))
    if has_s0:
        in_specs.append(state_spec)
        args.append(s0)
    out_specs = [pl.BlockSpec((None, t, hps * GLA_DV), lambda bb, h: (bb, 0, h))]
    out_shape = [jax.ShapeDtypeStruct((b, t, GLA_HEADS * GLA_DV), BF16)]
    if want_final:
        out_specs.append(state_spec)
        out_shape.append(jax.ShapeDtypeStruct((b, 2, GLA_HEADS, GLA_DK, GLA_DV), F32))
    return pl.pallas_call(
        functools.partial(_gla_kernel, t=t, hps=hps, has_s0=has_s0, want_final=want_final,
                          grid_rows=grid_rows),
        grid=(b, ng),
        in_specs=in_specs,
        out_specs=out_specs,
        out_shape=out_shape,
        scratch_shapes=[
            pltpu.VMEM((t, hps * GLA_DV), F32),
            pltpu.VMEM((2, hps, GLA_DV, GLA_DK), F32),
            pltpu.VMEM((2, t, hps * GLA_DK), BF16),
            pltpu.VMEM((2, t, hps * GLA_DK), BF16),
            pltpu.VMEM((2, t, hps * GLA_DK), F32),
            pltpu.VMEM((2, 2, min(GLA_BLOCK, t), hps * GLA_DK), BF16),
            pltpu.VMEM((2, 2, min(GLA_BLOCK, t), hps * GLA_DK), BF16),
            pltpu.VMEM((2, min(GLA_BLOCK, t), 2 * hps * GLA_DK), BF16),
            pltpu.VMEM((2, min(GLA_BLOCK, t), 2 * hps * GLA_DK), BF16),
        ],
        compiler_params=_cparams(("parallel", "parallel")),
        name=name,
    )(*args)


def _gelu_tanh(x):
    return x * (0.5 * (1.0 + jnp.tanh(0.7978845608028654 * (x + 0.044715 * (x * x * x)))))


def _run_staggered(tiles, stagger):
    n_st = len(tiles[0])
    for clock in range(n_st + stagger * (len(tiles) - 1)):
        for i, st in enumerate(tiles):
            k = clock - stagger * i
            if 0 <= k < n_st:
                st[k]()


def _tile_specs(n_tiles, tm, t, per_batch_mod):
    tps = TILES_PER_STEP
    assert n_tiles % tps == 0 and (t % (tps * tm) == 0 or not per_batch_mod)
    spec = pl.BlockSpec((tps, tm, D_MODEL), lambda i: (i, 0, 0))
    const = lambda shape: pl.BlockSpec(shape, lambda i: (0,) * len(shape))
    if per_batch_mod:
        mod_spec = pl.BlockSpec((None, N_MOD, D_MODEL), lambda i: ((i * tps * tm) // t + 1, 0, 0))
    else:
        mod_spec = pl.BlockSpec((None, N_MOD, D_MODEL), lambda i: (0, 0, 0))
    return spec, mod_spec, const


def _merge_kernel(x_ref, hf_ref, hb_ref, zg_ref, ga_ref, gb_ref, on_ref, mod_ref, ng_ref,
                  unperm_ref, lup_ref, gup_ref, wo_ref, o_ref):
    def stages(i):
        ld = lambda r: r[i].astype(F32)
        v = {}

        def s0():
            v['hs'] = (ld(hf_ref) + ld(hb_ref)).astype(BF16)
            v['gz'] = _gelu_tanh(ld(zg_ref))

        def s1():
            hs = _bdot(unperm_ref[...], v['hs'])
            v['y_a'] = _bdot((hs * v['gz']).astype(BF16), lup_ref[...])

        def s2():
            v['y_b'] = _bdot(on_ref[i], gup_ref[...])

        def s3():
            v['mm'] = (_sigmoid(ld(ga_ref)) * v['y_a'] + _sigmoid(ld(gb_ref)) * v['y_b']).astype(BF16)

        def s4():
            v['m'] = _bdot(v['mm'], wo_ref[...])

        def s5():
            o_ref[i] = x_ref[i] + mod_ref[2:3, :] * _rms(v['m'], ng_ref[1:2, :])

        return [s0, s1, s2, s3, s4, s5]

    _run_staggered([stages(i) for i in range(x_ref.shape[0])], MERGE_STAGGER)


def _merge(x, hf, hb, zg, ga, gb, on, mod, norm_g, lru_up, gla_up, w_out, *, per_batch_mod, name):
    b, t, _ = x.shape
    tm = _lru_chunk(t)
    n_tiles = b * t // tm
    spec, mod_spec, const = _tile_specs(n_tiles, tm, t, per_batch_mod)
    tiled = lambda a: a.reshape(n_tiles, tm, a.shape[-1])
    out = pl.pallas_call(
        _merge_kernel,
        grid=(n_tiles // TILES_PER_STEP,),
        in_specs=[spec] * 7 + [mod_spec, const((4, D_MODEL)), const((tm, tm))]
        + [const((D_MODEL, D_MODEL))] * 3,
        out_specs=spec,
        out_shape=jax.ShapeDtypeStruct((n_tiles, tm, D_MODEL), F32),
        compiler_params=_cparams(("parallel",)),
        name=name,
    )(*(tiled(a) for a in (x, hf, hb, zg, ga, gb, on)), mod, norm_g,
      jnp.asarray(_interleave_matrix(tm).T, BF16), lru_up, gla_up, w_out)
    return out.reshape(b, t, D_MODEL)


def _ffn_kernel(x_ref, mod_ref, ng_ref, w1_ref, w2_ref, o_ref):
    fs = D_FF // FFN_SPLIT

    def stages(i):
        v = {}

        def norm():
            x = x_ref[i]
            v['h'] = (_rms(x, ng_ref[2:3, :]) * (1.0 + mod_ref[4:5, :]) + mod_ref[3:4, :]).astype(BF16)

        def hidden(kf):
            hid = jnp.maximum(_bdot(v['h'], w1_ref[:, kf * fs:(kf + 1) * fs]), 0.0)
            part = _bdot((hid * hid).astype(BF16), w2_ref[kf * fs:(kf + 1) * fs, :])
            v['acc'] = part if kf == 0 else v['acc'] + part

        def out():
            o_ref[i] = x_ref[i] + mod_ref[5:6, :] * _rms(v['acc'], ng_ref[3:4, :])

        return [norm] + [functools.partial(hidden, kf) for kf in range(FFN_SPLIT)] + [out]

    _run_staggered([stages(i) for i in range(x_ref.shape[0])], FFN_STAGGER)


def _ffn(x, mod, norm_g, w1, w2, *, per_batch_mod, name):
    b, t, _ = x.shape
    tm = _lru_chunk(t)
    n_tiles = b * t // tm
    spec, mod_spec, const = _tile_specs(n_tiles, tm, t, per_batch_mod)
    out = pl.pallas_call(
        _ffn_kernel,
        grid=(n_tiles // TILES_PER_STEP,),
        in_specs=[spec, mod_spec, const((4, D_MODEL)), const((D_MODEL, D_FF)), const((D_FF, D_MODEL))],
        out_specs=spec,
        out_shape=jax.ShapeDtypeStruct((n_tiles, tm, D_MODEL), F32),
        compiler_params=_cparams(("parallel",)),
        name=name,
    )(x.reshape(n_tiles, tm, D_MODEL), mod, norm_g, w1, w2)
    return out.reshape(b, t, D_MODEL)


def _blockdiag_slabs(w):
    per = LRU_COLB // LRU_BW
    w = w.reshape(2, LRU_NCOLB, per, LRU_BW, LRU_BW)
    eye = jnp.eye(per, dtype=w.dtype)
    slab = jnp.einsum('dspij,pq->dspiqj', w, eye)
    return slab.reshape(2, LRU_NCOLB, LRU_COLB, LRU_COLB).astype(BF16)


def _layer_group(x, mod, p, lru_h0, gla_s0, *, latent, want_state, tag):
    b, t, _ = x.shape
    per_batch = latent
    ga, gb, zg, zx = _inproj(x, mod, p['norm_g'], p['w_rm'], (None,) * 4,
                             per_batch_mod=per_batch, name=f"inproj_rm_{tag}")
    period = GRID_W if latent else t
    (q, k, v, g, lr), hf, hb, s_lru = _cm_lru(
        x, mod, p['norm_g'], p['w_cm'], (None, None, None, _silu, None), zx, lru_h0, p['conv_w'],
        p['conv_b'], p['wa_bd'], p['wx_bd'], p['lru_ba'], p['lru_bx'], p['lru_L'], per_batch_mod=per_batch,
        order='col_major' if latent else 'natural', period=period, name=f"cm_lru_{tag}")
    res = _gla(q, k, v, g, lr, p['w2p'], p['gla_b2'], p['gla_norm_g'], gla_s0,
               want_final=want_state, hps=GLA_HEADS_PER_STEP if latent else GLA_HEADS, col_major=latent,
               name=f"gla_{tag}")
    on = res[0]
    s_gla = res[1] if want_state else None
    x1 = _merge(x, hf, hb, zg, ga, gb, on, mod, p['norm_g'], p['lru_up'], p['gla_up'], p['w_out'],
                per_batch_mod=per_batch, name=f"merge_{tag}")
    y = _ffn(x1, mod, p['norm_g'], p['mlp_w1'], p['mlp_w2'], per_batch_mod=per_batch, name=f"ffn_{tag}")
    return y, s_lru, s_gla


def kernel(x_prompt, x_sample, state_lru, state_gla, c, c_ctx, w_mod, b_mod, norm_g, w_in, conv_w, conv_b,
           lru_wa, lru_ba, lru_wx, lru_bx, lru_L, lru_up, gla_w2, gla_b2, gla_norm_g, gla_up, w_out,
           mlp_w1, mlp_w2):
    depth = w_in.shape[0]
    assert depth == 1
    dec_b = x_sample.shape[0]
    b0 = x_prompt.shape[0]
    xp, xs = x_prompt, x_sample
    lru_states, gla_states = [], []
    mod_rows = -(-(1 + dec_b) // V7X_SUBLANES) * V7X_SUBLANES
    cc = jnp.zeros((mod_rows, D_MODEL), F32).at[0].set(c_ctx).at[1:1 + dec_b].set(c)
    for l in range(depth):
        wl = w_in[l]
        dk_tot, dv_tot = GLA_HEADS * GLA_DK, GLA_HEADS * GLA_DV
        o = np.cumsum((0, D_MODEL, D_MODEL, dk_tot, dk_tot, dv_tot, dv_tot, 2 * GLA_RANK, D_MODEL, D_MODEL))
        col = lambda i: wl[:, o[i]:o[i + 1]].astype(BF16)
        lr_w = jnp.pad(col(6), ((0, 0), (0, LR_PAD - 2 * GLA_RANK)))
        w2 = gla_w2[l].astype(BF16)
        w2p = jnp.zeros((2, LR_PAD, GLA_HEADS * GLA_DK), BF16)
        w2p = w2p.at[0, 0:GLA_RANK].set(w2[0]).at[1, GLA_RANK:2 * GLA_RANK].set(w2[1])
        p = {
            'norm_g': norm_g[l],
            'w_rm': [col(7), col(8), col(1), col(0)],
            'w_cm': [col(2), col(3), col(4), col(5), lr_w],
            'conv_w': conv_w[l], 'conv_b': conv_b[l].reshape(1, D_MODEL),
            'wa_bd': _blockdiag_slabs(lru_wa[l]), 'wx_bd': _blockdiag_slabs(lru_wx[l]),
            'lru_ba': lru_ba[l], 'lru_bx': lru_bx[l], 'lru_L': lru_L[l],
            'lru_up': lru_up[l].astype(BF16),
            'w2p': w2p, 'gla_b2': gla_b2[l], 'gla_norm_g': gla_norm_g[l].reshape(1, GLA_DV),
            'gla_up': gla_up[l].astype(BF16), 'w_out': w_out[l].astype(BF16),
            'mlp_w1': mlp_w1[l].astype(BF16), 'mlp_w2': mlp_w2[l].astype(BF16),
        }
        mod = _modulation(cc, w_mod[l], b_mod[l])
        h0_ctx = jnp.zeros((b0, 2, D_MODEL), F32)
        xp, s_lru, s_gla = _layer_group(xp, mod, p, h0_ctx, None, latent=False, want_state=True,
                                        tag="ctx")
        lru_states.append(s_lru)
        gla_states.append(s_gla)
        xs, _, _ = _layer_group(xs, mod, p, state_lru[:, l], state_gla[:, l], latent=True,
                                want_state=False, tag="lat")
    return (xp, xs, jnp.stack(lru_states, axis=1), jnp.stack(gla_states, axis=1))
```

```python
import functools

import numpy as np
import jax
import jax.numpy as jnp
from jax import lax
from jax.experimental import pallas as pl
from jax.experimental.pallas import tpu as pltpu

F32 = jnp.float32
BF16 = jnp.bfloat16

D_MODEL = 1024
GRID_W = 64
LRU_BLOCKS = 16
LRU_BW = D_MODEL // LRU_BLOCKS
LRU_C = 8.0
GLA_HEADS = 4
GLA_DK = 128
GLA_DV = 256
GLA_RANK = 16
GLA_TAU = 16.0
GLA_CHUNK = 64
D_FF = 4 * D_MODEL
N_MOD = 6
EPS = 1e-6
LOG2_E = 1.4426950408889634

V7X_LANES = 128
V7X_SUBLANES = 8
V7X_MXU_DIM = 256
V7X_VMEM_BYTES = 64 * 1024 * 1024
VMEM_LIMIT = V7X_VMEM_BYTES - 8 * 1024 * 1024

LRU_COLB = V7X_MXU_DIM
LRU_NCOLB = D_MODEL // LRU_COLB
LR_PAD = V7X_LANES
COL_BLOCK = 16
GLA_BLOCK = V7X_MXU_DIM
GLA_HEADS_PER_STEP = 2
GLA_SCORE_BLOCK = 2 * GLA_CHUNK
GLA_OUT_COLS = 16
LRU_CHUNK = 256
TILES_PER_STEP = 2
MERGE_STAGGER = 2
INPROJ_STAGGER = 2
FFN_SPLIT = 4
FFN_STAGGER = 3
LRU_SCAN_CUTS = 1
CM_PIECE = 256


def _cparams(sem):
    return pltpu.CompilerParams(dimension_semantics=sem, vmem_limit_bytes=VMEM_LIMIT)


def _rms(x, g):
    ms = jnp.mean(x * x, axis=-1, keepdims=True)
    return x * lax.rsqrt(ms + EPS) * g


def _sigmoid(x):
    return 0.5 * jnp.tanh(0.5 * x) + 0.5


def _softplus(x):
    return jnp.maximum(x, 0.0) + jnp.log1p(jnp.exp(-jnp.abs(x)))


def _bdot(a, b):
    return jnp.dot(a, b, preferred_element_type=F32)


def _mod_kernel(c_ref, w_ref, b_ref, o_ref):
    c = c_ref[...]
    s = (c * _sigmoid(c)).astype(BF16)
    o_ref[...] = _bdot(s, w_ref[...].astype(BF16)) + b_ref[...]


def _modulation(cc, w_mod, b_mod):
    rows = cc.shape[0]
    return pl.pallas_call(
        _mod_kernel,
        grid=(N_MOD,),
        in_specs=[
            pl.BlockSpec((rows, D_MODEL), lambda n: (0, 0)),
            pl.BlockSpec((D_MODEL, D_MODEL), lambda n: (0, n)),
            pl.BlockSpec((1, D_MODEL), lambda n: (0, n)),
        ],
        out_specs=pl.BlockSpec((rows, D_MODEL), lambda n: (0, n)),
        out_shape=jax.ShapeDtypeStruct((rows, N_MOD * D_MODEL), F32),
        compiler_params=_cparams(("arbitrary",)),
        name="modulation",
    )(cc, w_mod, b_mod.reshape(1, N_MOD * D_MODEL)).reshape(rows, N_MOD, D_MODEL)


def _tok_view(a, grid_tile):
    if grid_tile is None:
        return a
    b, t, n = a.shape
    rows = t // GRID_W
    return a.reshape(b, rows, GRID_W // COL_BLOCK, COL_BLOCK, n)


def _tok_spec(n, tm, grid_tile):
    if grid_tile is None:
        return pl.BlockSpec((None, tm, n), lambda b, j: (b, j, 0))
    rows = grid_tile
    return pl.BlockSpec((None, rows, None, COL_BLOCK, n), lambda b, j: (b, 0, j, 0, 0))


def _const_spec(shape):
    nd = len(shape)
    return pl.BlockSpec(shape, lambda b, j: (0,) * nd)


def _mod_spec(per_batch):
    if per_batch:
        return pl.BlockSpec((None, N_MOD, D_MODEL), lambda b, j: (b + 1, 0, 0))
    return pl.BlockSpec((None, N_MOD, D_MODEL), lambda b, j: (0, 0, 0))


def _perm_matrix(rows, cols):
    n = rows * cols
    p = np.zeros((n, n), np.float32)
    for r in range(rows):
        for c in range(cols):
            p[c * rows + r, r * cols + c] = 1.0
    return p


def _silu(x):
    return x * _sigmoid(x)


def _inproj_pieces(x_ref, mod_ref, ng_ref, perm_ref, w_refs, o_refs, *, acts, n_piece, tile=None):
    v = {}

    def norm():
        x = (x_ref[...] if tile is None else x_ref[tile]).reshape(-1, D_MODEL)
        h = _rms(x, ng_ref[0:1, :]) * (1.0 + mod_ref[1:2, :]) + mod_ref[0:1, :]
        v['h'] = h.astype(BF16)
        if perm_ref is not None:
            v['h'] = _bdot(perm_ref[...], v['h']).astype(BF16)

    pieces = [norm]
    for i, (w_ref, o_ref) in enumerate(zip(w_refs, o_refs)):
        n = w_ref.shape[1]
        width = min(n, n_piece)
        for c0 in range(0, n, width):
            def piece(i=i, w_ref=w_ref, o_ref=o_ref, cs=slice(c0, c0 + width)):
                z = _bdot(v['h'], w_ref[:, cs])
                if acts[i] is not None:
                    z = acts[i](z)
                if tile is None:
                    o_ref[:, cs] = z.astype(o_ref.dtype)
                else:
                    o_ref[tile, :, cs] = z.astype(o_ref.dtype)
            pieces.append(piece)
    return pieces


def _inproj_kernel(*refs, n_out, acts):
    x_ref, mod_ref, ng_ref = refs[:3]
    _run_staggered([_inproj_pieces(x_ref, mod_ref, ng_ref, None, refs[3:3 + n_out], refs[3 + n_out:],
                                   acts=acts, n_piece=D_MODEL, tile=i)
                    for i in range(x_ref.shape[0])], INPROJ_STAGGER)


def _inproj_setup(x, weights, order):
    b, t, _ = x.shape
    if order == 'col_major':
        rows = t // GRID_W
        tm = rows * COL_BLOCK
        return _tok_view(x, rows), _tok_spec(D_MODEL, tm, rows), tm, _perm_matrix(rows, COL_BLOCK)
    tm = _lru_chunk(t)
    perm = _interleave_matrix(tm) if order == 'interleaved' else None
    return x, _tok_spec(D_MODEL, tm, None), tm, perm


def _inproj(x, mod, norm_g, weights, acts, *, per_batch_mod, name):
    b, t, _ = x.shape
    tm = _lru_chunk(t)
    n_tiles = b * t // tm
    spec, mod_spec, const = _tile_specs(n_tiles, tm, t, per_batch_mod)
    outs = pl.pallas_call(
        functools.partial(_inproj_kernel, n_out=len(weights), acts=tuple(acts)),
        grid=(n_tiles // TILES_PER_STEP,),
        in_specs=[spec, mod_spec, const((4, D_MODEL))] + [const(w.shape) for w in weights],
        out_specs=[pl.BlockSpec((TILES_PER_STEP, tm, w.shape[1]), lambda i: (i, 0, 0)) for w in weights],
        out_shape=[jax.ShapeDtypeStruct((n_tiles, tm, w.shape[1]), BF16) for w in weights],
        compiler_params=_cparams(("parallel",)),
        name=name,
    )(x.reshape(n_tiles, tm, D_MODEL), mod, norm_g, *weights)
    return [o.reshape(b, t, o.shape[-1]) for o in outs]


def _lru_chunk(t):
    return min(LRU_CHUNK, t)


def _interleave_matrix(tc):
    return _perm_matrix(V7X_SUBLANES, tc // V7X_SUBLANES)


def _lru_pieces(zf_ref, zb_ref, rf0, rb0, il_ref, cw_ref, cb_ref, wa_ref, wx_ref, ba_ref, bx_ref, l_ref,
                hf_ref, hb_ref, fin_ref, a_s, u_s, carry_s, zi_s, *, tc, period):
    ns = V7X_SUBLANES
    sl = tc // ns
    sub_c = lax.broadcasted_iota(jnp.int32, (ns, LRU_COLB), 0)
    prev_ok = jnp.bitwise_and(sub_c * sl, period - 1) != 0
    next_ok = jnp.bitwise_and((sub_c + 1) * sl, period - 1) != 0

    def reorder(d, z_ref, r0):
        zi_s[d] = _bdot(il_ref[...], z_ref[r0:r0 + tc, :]).astype(BF16)

    def gates(d, cb):
        cs = slice(cb * LRU_COLB, (cb + 1) * LRU_COLB)
        x = zi_s[d, :, cs].astype(F32)
        e_prev = jnp.where(prev_ok, pltpu.roll(x[tc - ns:], 1, 0), 0.0)
        e_next0 = jnp.where(next_ok, pltpu.roll(x[:ns], ns - 1, 0), 0.0)
        e_next1 = jnp.where(next_ok, pltpu.roll(x[ns:2 * ns], ns - 1, 0), 0.0)
        xm1 = jnp.concatenate([e_prev, x[:tc - ns]], axis=0)
        xp1 = jnp.concatenate([x[ns:], e_next0], axis=0)
        xp2 = jnp.concatenate([x[2 * ns:], e_next0, e_next1], axis=0)
        cw = 0.5 * cw_ref[:, cs]
        xh = 0.5 * cb_ref[0:1, cs] + xm1 * cw[0:1] + x * cw[1:2] + xp1 * cw[2:3] + xp2 * cw[3:4]
        xhb = xh.astype(BF16)
        tr = jnp.tanh(_bdot(xhb, wa_ref[d, cb]) + 0.5 * ba_ref[d:d + 1, cs])
        ti = jnp.tanh(_bdot(xhb, wx_ref[d, cb]) + 0.5 * bx_ref[d:d + 1, cs])
        k2 = (-0.5 * LRU_C * LOG2_E) * _softplus(-l_ref[d:d + 1, cs])
        a = jnp.exp2(tr * k2 + k2)
        y = 1.0 - a * a
        u = jnp.where(y > 0.0, y * lax.rsqrt(y), 0.0) * ((ti + 1.0) * xh)
        a_s[d, :, cs] = a
        u_s[d, :, cs] = u

    state = {}
    n_cut = LRU_SCAN_CUTS

    def pass1(part):
        if part == 0:
            zeros = jnp.zeros((ns, D_MODEL), F32)
            ones = jnp.ones((ns, D_MODEL), F32)
            state['c'] = (zeros, ones, zeros, ones)
        hf, af, hb, ab = state['c']
        for i in range(part * sl // n_cut, (part + 1) * sl // n_cut):
            rf = slice(i * ns, (i + 1) * ns)
            rb = slice((sl - 1 - i) * ns, (sl - i) * ns)
            a = a_s[0, rf, :]
            hf = a * hf + u_s[0, rf, :]
            af = a * af
            u_s[0, rf, :] = hf
            a_s[0, rf, :] = af
            a = a_s[1, rb, :]
            hb = a * hb + u_s[1, rb, :]
            ab = a * ab
            u_s[1, rb, :] = hb
            a_s[1, rb, :] = ab
        state['c'] = (hf, af, hb, ab)

    def chain():
        hf, af, hb, ab = state['c']
        sub = lax.broadcasted_iota(jnp.int32, (ns, D_MODEL), 0)

        def sublane_scan(a, u, forward):
            for s in (1, 2, 4):
                if forward:
                    m = sub >= s
                    shift = s
                else:
                    m = sub < ns - s
                    shift = ns - s
                a_sh = jnp.where(m, pltpu.roll(a, shift, 0), 1.0)
                u_sh = jnp.where(m, pltpu.roll(u, shift, 0), 0.0)
                u = a * u_sh + u
                a = a * a_sh
            return a, u

        cin_f = carry_s[0:1, :]
        cin_b = carry_s[1:2, :]
        af, hf = sublane_scan(af, hf, True)
        ab, hb = sublane_scan(ab, hb, False)
        fin_f = hf + af * cin_f
        fin_b = hb + ab * cin_b
        enter_f = jnp.where(sub == 0, cin_f, pltpu.roll(fin_f, 1, 0))
        enter_b = jnp.where(sub == ns - 1, cin_b, pltpu.roll(fin_b, ns - 1, 0))
        out_f = fin_f[ns - 1:ns, :]
        out_b = fin_b[0:1, :]
        carry_s[0:1, :] = out_f
        carry_s[1:2, :] = out_b
        fin_ref[0:1, :] = out_f
        fin_ref[1:2, :] = out_b
        state['e'] = (jnp.concatenate([enter_f, enter_f], axis=0), jnp.concatenate([enter_b, enter_b], axis=0))

    def pass2(part):
        pk = 2 * ns
        enter_f2, enter_b2 = state['e']
        for i in range(part * (sl // 2) // n_cut, (part + 1) * (sl // 2) // n_cut):
            rr = slice(i * pk, (i + 1) * pk)
            hf_ref[rf0 + i * pk:rf0 + (i + 1) * pk, :] = (
                u_s[0, rr, :] + a_s[0, rr, :] * enter_f2).astype(hf_ref.dtype)
            hb_ref[rb0 + i * pk:rb0 + (i + 1) * pk, :] = (
                u_s[1, rr, :] + a_s[1, rr, :] * enter_b2).astype(hb_ref.dtype)

    scan = ([functools.partial(pass1, part) for part in range(n_cut)] + [chain]
            + [functools.partial(pass2, part) for part in range(n_cut)])

    pieces = [functools.partial(reorder, 0, zf_ref, rf0), functools.partial(reorder, 1, zb_ref, rb0)]
    pieces += [functools.partial(gates, d, cb) for d in range(2) for cb in range(LRU_NCOLB)]
    return pieces + scan


def _interleave(major, minor):
    out, done = [], 0
    for i, p in enumerate(major):
        want = -(-(i + 1) * len(minor) // len(major))
        out.extend(minor[done:want])
        done = want
        out.append(p)
    return out


def _cm_lru_kernel(*refs, n_cm, permute, acts, tc, period, cps):
    x_ref, mod_ref, ng_ref = refs[:3]
    k = 4 if permute else 3
    perm_ref = refs[3] if permute else None
    w_refs = refs[k:k + n_cm]
    k += n_cm
    zf_ref, zb_ref, h0_ref, il_ref, cw_ref, cb_ref, wa_ref, wx_ref, ba_ref, bx_ref, l_ref = refs[k:k + 11]
    k += 11
    o_refs = refs[k:k + n_cm]
    k += n_cm
    hf_ref, hb_ref, fin_ref, a_s, u_s, carry_s, zi_s = refs[k:]

    @pl.when(pl.program_id(1) == 0)
    def _():
        carry_s[...] = h0_ref[...]

    cm = _inproj_pieces(x_ref, mod_ref, ng_ref, perm_ref, w_refs, o_refs, acts=acts, n_piece=CM_PIECE)
    lru = []
    for s in range(cps):
        lru += _lru_pieces(zf_ref, zb_ref, s * tc, (cps - 1 - s) * tc, il_ref, cw_ref, cb_ref, wa_ref, wx_ref,
                           ba_ref, bx_ref, l_ref, hf_ref, hb_ref, fin_ref, a_s, u_s, carry_s, zi_s,
                           tc=tc, period=period)
    cm[0]()
    for piece in _interleave(lru, cm[1:]):
        piece()


def _cm_lru(x, mod, norm_g, weights, acts, zx, h0, conv_w, conv_b, wa_bd, wx_bd, ba, bx, lru_l, *,
            per_batch_mod, order, period, name):
    b, t, _ = x.shape
    x_in, x_spec, tm, perm = _inproj_setup(x, weights, order)
    extra = [] if perm is None else [jnp.asarray(perm, BF16)]
    n_steps = t // tm
    tc = _lru_chunk(t)
    cps = t // tc // n_steps
    assert cps >= 1 and cps * tc * n_steps == t
    assert period % (tc // V7X_SUBLANES) == 0 and tc % period == 0
    blk = cps * tc
    res = pl.pallas_call(
        functools.partial(_cm_lru_kernel, n_cm=len(weights), permute=perm is not None, acts=tuple(acts),
                          tc=tc, period=period, cps=cps),
        grid=(b, n_steps),
        in_specs=[x_spec, _mod_spec(per_batch_mod), _const_spec((4, D_MODEL))]
        + [_const_spec(e.shape) for e in extra] + [_const_spec(w.shape) for w in weights]
        + [
            pl.BlockSpec((None, blk, D_MODEL), lambda bb, j: (bb, j, 0)),
            pl.BlockSpec((None, blk, D_MODEL), lambda bb, j: (bb, n_steps - 1 - j, 0)),
            pl.BlockSpec((None, 2, D_MODEL), lambda bb, j: (bb, 0, 0)),
            _const_spec((tc, tc)), _const_spec((4, D_MODEL)), _const_spec((1, D_MODEL)),
            _const_spec((2, LRU_NCOLB, LRU_COLB, LRU_COLB)), _const_spec((2, LRU_NCOLB, LRU_COLB, LRU_COLB)),
            _const_spec((2, D_MODEL)), _const_spec((2, D_MODEL)), _const_spec((2, D_MODEL)),
        ],
        out_specs=[_tok_spec(w.shape[1], tm, None) for w in weights] + [
            pl.BlockSpec((None, blk, D_MODEL), lambda bb, j: (bb, j, 0)),
            pl.BlockSpec((None, blk, D_MODEL), lambda bb, j: (bb, n_steps - 1 - j, 0)),
            pl.BlockSpec((None, 2, D_MODEL), lambda bb, j: (bb, 0, 0)),
        ],
        out_shape=[jax.ShapeDtypeStruct((b, t, w.shape[1]), BF16) for w in weights] + [
            jax.ShapeDtypeStruct((b, t, D_MODEL), BF16),
            jax.ShapeDtypeStruct((b, t, D_MODEL), BF16),
            jax.ShapeDtypeStruct((b, 2, D_MODEL), F32),
        ],
        scratch_shapes=[
            pltpu.VMEM((2, tc, D_MODEL), F32),
            pltpu.VMEM((2, tc, D_MODEL), F32),
            pltpu.VMEM((2, D_MODEL), F32),
            pltpu.VMEM((2, tc, D_MODEL), BF16),
        ],
        compiler_params=_cparams(("parallel", "arbitrary")),
        name=name,
    )(x_in, mod, norm_g, *extra, *weights, zx, zx, h0, jnp.asarray(_interleave_matrix(tc), BF16),
      conv_w, conv_b, wa_bd, wx_bd, ba, bx, lru_l)
    return res[:len(weights)], res[len(weights)], res[len(weights) + 1], res[len(weights) + 2]


def _gla_kernel(*refs, t, hps, has_s0, want_final, grid_rows):
    q_ref, k_ref, v_ref, g_ref, lr_ref, w2_ref, b2_ref, gn_ref = refs[:8]
    n = 8
    if grid_rows is not None:
        perm_ref = refs[n]
        n += 1
    if has_s0:
        s0_ref = refs[n]
        n += 1
    on_ref = refs[n]
    n += 1
    if want_final:
        fin_ref = refs[n]
        n += 1
    oacc, st, q2_s, k2_s, dc_s, pp_q, pp_k, pp_qx, pp_kx = refs[n:]

    c = GLA_CHUNK
    blk = min(GLA_BLOCK, t)
    for d in range(2):
        for hh in range(hps):
            if has_s0:
                st[d, hh] = s0_ref[d, hh]
            else:
                st[d, hh] = jnp.zeros((GLA_DK, GLA_DV), F32)

    ti = lax.broadcasted_iota(jnp.int32, (blk, blk), 0)
    si = lax.broadcasted_iota(jnp.int32, (blk, blk), 1)
    same = (ti // c) == (si // c)
    cums = ((same & (si <= ti)).astype(BF16), (same & (si >= ti)).astype(BF16))
    sb = min(GLA_SCORE_BLOCK, blk)
    tj = lax.broadcasted_iota(jnp.int32, (sb, sb), 0)
    sj = lax.broadcasted_iota(jnp.int32, (sb, sb), 1)
    same_j = (tj // c) == (sj // c)
    masks = (same_j & (sj <= tj), same_j & (sj >= tj))
    scale = GLA_DK ** -0.5

    def step(ia, slot_a, ib, slot_b):
        nd = hps * GLA_DK
        if ia is not None:
            rows_a = pl.ds(pl.multiple_of(ia * blk, blk), blk)
            lrb = lr_ref[rows_a, :]
            pre = [_bdot(lrb, w2_ref[d]) + b2_ref[d:d + 1, :] for d in range(2)]
        if ib is not None:
            rows_b = [pl.ds(pl.multiple_of(ib * blk + g * sb, sb), sb) for g in range(blk // sb)]
            loc_b = [slice(g * sb, (g + 1) * sb) for g in range(blk // sb)]
            nt = (((1,), (1,)), ((), ()))
            raw = [[[lax.dot_general(pp_q[slot_b, d, loc, hh * GLA_DK:(hh + 1) * GLA_DK],
                                     pp_k[slot_b, d, loc, hh * GLA_DK:(hh + 1) * GLA_DK], nt,
                                     preferred_element_type=F32) for d in range(2)]
                    + [lax.dot_general(pp_qx[slot_b, loc, hh * 2 * GLA_DK:(hh + 1) * 2 * GLA_DK],
                                       pp_kx[slot_b, loc, hh * 2 * GLA_DK:(hh + 1) * 2 * GLA_DK], nt,
                                       preferred_element_type=F32)]
                    for hh in range(hps)] for loc in loc_b]
        if ia is not None:
            bcum = []
            for d in range(2):
                soft = jnp.log(1.0 + jnp.exp2(jnp.abs(pre[d]) * (-LOG2_E)))
                la = jnp.minimum(pre[d], 0.0) * (LOG2_E / GLA_TAU) - soft * (LOG2_E / GLA_TAU)
                hi = la.astype(BF16)
                mid = (la - hi.astype(F32)).astype(BF16)
                bcum.append(_bdot(cums[d], hi) + _bdot(cums[d], mid))
        if ib is not None:
            for g, rows in enumerate(rows_b):
                for hh in range(hps):
                    vs = slice(hh * GLA_DV, (hh + 1) * GLA_DV)
                    sd0, sd1, sx = raw[g][hh]
                    sc = jnp.where(masks[0], sd0, 0.0) + jnp.where(masks[1], sd1, 0.0) + sx
                    oacc[rows, vs] = _bdot(sc.astype(BF16), v_ref[rows, vs])
        if ia is not None:
            q = q_ref[rows_a, :].astype(F32)
            k = k_ref[rows_a, :].astype(F32)
            nc = blk // c
            crow = lambda a, n: a[n * c:(n + 1) * c]
            cat = lambda parts: jnp.concatenate(parts, axis=0)
            zero = jnp.zeros((c, nd), BF16)
            qx, kx = [], []
            for d in range(2):
                edge = c - 1 if d == 0 else 0
                tot = [bcum[d][n * c + edge:n * c + edge + 1, :] for n in range(nc)]
                dec = [jnp.exp2(x) for x in tot]
                btot = cat([jnp.broadcast_to(x, (c, nd)) for x in tot])
                qe = (q * scale) * jnp.exp2(bcum[d])
                kt = k * jnp.exp2(btot - bcum[d])
                pp_q[slot_a, d] = qe.astype(BF16)
                pp_k[slot_a, d] = (k * jnp.exp2(-bcum[d])).astype(BF16)
                first = [(n % 2 == 0) == (d == 0) for n in range(nc)]
                q2 = cat([crow(qe, n) if first[n] else crow(qe, n) * dec[n ^ 1] for n in range(nc)])
                k2 = cat([crow(kt, n) * dec[n ^ 1] if first[n] else crow(kt, n) for n in range(nc)])
                q2_s[d, rows_a, :] = q2.astype(BF16)
                k2_s[d, rows_a, :] = k2.astype(BF16)
                for m in range(nc // 2):
                    dec2 = jnp.exp2(tot[2 * m] + tot[2 * m + 1])
                    for hh in range(hps):
                        row = dec2[:, hh * GLA_DK:(hh + 1) * GLA_DK]
                        dc_s[d, ia * (nc // 2) + m, hh] = jnp.broadcast_to(row, (GLA_DK, GLA_DK)).T
                qeb, ktb = qe.astype(BF16), kt.astype(BF16)
                qx.append(cat([zero if first[n] else crow(qeb, n) for n in range(nc)]))
                kx.append(cat([crow(ktb, n) if first[n] else zero for n in range(nc)]))
            for hh in range(hps):
                ks = slice(hh * GLA_DK, (hh + 1) * GLA_DK)
                xs = slice(hh * 2 * GLA_DK, (hh + 1) * 2 * GLA_DK)
                pp_qx[slot_a, :, xs] = jnp.concatenate([qx[0][:, ks], qx[1][:, ks]], axis=1)
                pp_kx[slot_a, :, xs] = jnp.concatenate([kx[0][:, ks], kx[1][:, ks]], axis=1)

    n_blocks = t // blk
    step(0, 0, None, None)
    for ib in range(n_blocks - 1):
        step(ib + 1, (ib + 1) % 2, ib, ib % 2)
    step(None, None, n_blocks - 1, (n_blocks - 1) % 2)

    pr = 2 * c
    n_prs = t // pr

    def pair_step(p0, d, hh):
        rows = pl.ds(p0, pr)
        ks = slice(hh * GLA_DK, (hh + 1) * GLA_DK)
        vs = slice(hh * GLA_DV, (hh + 1) * GLA_DV)
        s_t = st[d, hh]
        o = _bdot(q2_s[d, rows, ks], s_t.astype(BF16))
        upd = lax.dot_general(k2_s[d, rows, ks], v_ref[rows, vs], (((0,), (0,)), ((), ())),
                              preferred_element_type=F32)
        dcol = dc_s[d, p0 // pr, hh]
        st[d, hh] = s_t * jnp.concatenate([dcol] * (GLA_DV // GLA_DK), axis=1) + upd
        return o

    for jj in range(n_prs):
        for hh in range(hps):
            vs = slice(hh * GLA_DV, (hh + 1) * GLA_DV)
            for p0, d in ((jj * pr, 0), ((n_prs - 1 - jj) * pr, 1)):
                oacc[pl.ds(p0, pr), vs] += pair_step(p0, d, hh)

    if grid_rows is None:
        ob = min(GLA_BLOCK, t)
    else:
        ob = GLA_OUT_COLS * grid_rows

    for ib in range(t // ob):
        rows = slice(ib * ob, (ib + 1) * ob)
        parts = []
        for hh in range(hps):
            vs = slice(hh * GLA_DV, (hh + 1) * GLA_DV)
            parts.append((_rms(oacc[rows, vs], gn_ref[...]) * g_ref[rows, vs].astype(F32)).astype(BF16))
        on = jnp.concatenate(parts, axis=1)
        if grid_rows is None:
            on_ref[rows, :] = on
        else:
            on = _bdot(perm_ref[...], on).astype(BF16)
            for r in range(grid_rows):
                dst = r * GRID_W + ib * GLA_OUT_COLS
                on_ref[dst:dst + GLA_OUT_COLS, :] = on[r * GLA_OUT_COLS:(r + 1) * GLA_OUT_COLS]

    if want_final:
        for d in range(2):
            for hh in range(hps):
                fin_ref[d, hh] = st[d, hh]


def _gla(q, k, v, g, lr, w2p, b2, gn, s0, *, want_final, hps, col_major, name):
    b, t, _ = q.shape
    ng = GLA_HEADS // hps
    has_s0 = s0 is not None
    grid_rows = t // GRID_W if col_major else None
    in_specs = [
        pl.BlockSpec((None, t, hps * GLA_DK), lambda bb, h: (bb, 0, h)),
        pl.BlockSpec((None, t, hps * GLA_DK), lambda bb, h: (bb, 0, h)),
        pl.BlockSpec((None, t, hps * GLA_DV), lambda bb, h: (bb, 0, h)),
        pl.BlockSpec((None, t, hps * GLA_DV), lambda bb, h: (bb, 0, h)),
        pl.BlockSpec((None, t, LR_PAD), lambda bb, h: (bb, 0, 0)),
        pl.BlockSpec((2, LR_PAD, hps * GLA_DK), lambda bb, h: (0, 0, h)),
        pl.BlockSpec((2, hps * GLA_DK), lambda bb, h: (0, h)),
        pl.BlockSpec((1, GLA_DV), lambda bb, h: (0, 0)),
    ]
    args = [q, k, v, g, lr, w2p, b2, gn]
    if col_major:
        ob = GLA_OUT_COLS * grid_rows
        in_specs.append(pl.BlockSpec((ob, ob), lambda bb, h: (0, 0)))
        args.append(jnp.asarray(_perm_matrix(grid_rows, GLA_OUT_COLS).T, BF16))
    state_spec = pl.BlockSpec((None, 2, hps, GLA_DK, GLA_DV), lambda bb, h: (bb, 0, h, 0, 0))
    if has_s0:
        in_specs.append(state_spec)
        args.append(s0)
    out_specs = [pl.BlockSpec((None, t, hps * GLA_DV), lambda bb, h: (bb, 0, h))]
    out_shape = [jax.ShapeDtypeStruct((b, t, GLA_HEADS * GLA_DV), BF16)]
    if want_final:
        out_specs.append(state_spec)
        out_shape.append(jax.ShapeDtypeStruct((b, 2, GLA_HEADS, GLA_DK, GLA_DV), F32))
    return pl.pallas_call(
        functools.partial(_gla_kernel, t=t, hps=hps, has_s0=has_s0, want_final=want_final,
                          grid_rows=grid_rows),
        grid=(b, ng),
        in_specs=in_specs,
        out_specs=out_specs,
        out_shape=out_shape,
        scratch_shapes=[
            pltpu.VMEM((t, hps * GLA_DV), F32),
            pltpu.VMEM((2, hps, GLA_DK, GLA_DV), F32),
            pltpu.VMEM((2, t, hps * GLA_DK), BF16),
            pltpu.VMEM((2, t, hps * GLA_DK), BF16),
            pltpu.VMEM((2, t // (2 * GLA_CHUNK), hps, GLA_DK, GLA_DK), F32),
            pltpu.VMEM((2, 2, min(GLA_BLOCK, t), hps * GLA_DK), BF16),
            pltpu.VMEM((2, 2, min(GLA_BLOCK, t), hps * GLA_DK), BF16),
            pltpu.VMEM((2, min(GLA_BLOCK, t), 2 * hps * GLA_DK), BF16),
            pltpu.VMEM((2, min(GLA_BLOCK, t), 2 * hps * GLA_DK), BF16),
        ],
        compiler_params=_cparams(("parallel", "parallel")),
        name=name,
    )(*args)


def _gelu_tanh(x):
    return x * (0.5 * (1.0 + jnp.tanh(0.7978845608028654 * (x + 0.044715 * (x * x * x)))))


def _run_staggered(tiles, stagger):
    n_st = len(tiles[0])
    for clock in range(n_st + stagger * (len(tiles) - 1)):
        for i, st in enumerate(tiles):
            k = clock - stagger * i
            if 0 <= k < n_st:
                st[k]()


def _tile_specs(n_tiles, tm, t, per_batch_mod):
    tps = TILES_PER_STEP
    assert n_tiles % tps == 0 and (t % (tps * tm) == 0 or not per_batch_mod)
    spec = pl.BlockSpec((tps, tm, D_MODEL), lambda i: (i, 0, 0))
    const = lambda shape: pl.BlockSpec(shape, lambda i: (0,) * len(shape))
    if per_batch_mod:
        mod_spec = pl.BlockSpec((None, N_MOD, D_MODEL), lambda i: ((i * tps * tm) // t + 1, 0, 0))
    else:
        mod_spec = pl.BlockSpec((None, N_MOD, D_MODEL), lambda i: (0, 0, 0))
    return spec, mod_spec, const


def _merge_kernel(x_ref, hf_ref, hb_ref, zg_ref, ga_ref, gb_ref, on_ref, mod_ref, ng_ref,
                  unperm_ref, lup_ref, gup_ref, wo_ref, o_ref):
    def stages(i):
        ld = lambda r: r[i].astype(F32)
        v = {}

        def s0():
            v['hs'] = (ld(hf_ref) + ld(hb_ref)).astype(BF16)
            v['gz'] = _gelu_tanh(ld(zg_ref))

        def s1():
            hs = _bdot(unperm_ref[...], v['hs'])
            v['y_a'] = _bdot((hs * v['gz']).astype(BF16), lup_ref[...])

        def s2():
            v['y_b'] = _bdot(on_ref[i], gup_ref[...])

        def s3():
            v['mm'] = (_sigmoid(ld(ga_ref)) * v['y_a'] + _sigmoid(ld(gb_ref)) * v['y_b']).astype(BF16)

        def s4():
            v['m'] = _bdot(v['mm'], wo_ref[...])

        def s5():
            o_ref[i] = x_ref[i] + mod_ref[2:3, :] * _rms(v['m'], ng_ref[1:2, :])

        return [s0, s1, s2, s3, s4, s5]

    _run_staggered([stages(i) for i in range(x_ref.shape[0])], MERGE_STAGGER)


def _merge(x, hf, hb, zg, ga, gb, on, mod, norm_g, lru_up, gla_up, w_out, *, per_batch_mod, name):
    b, t, _ = x.shape
    tm = _lru_chunk(t)
    n_tiles = b * t // tm
    spec, mod_spec, const = _tile_specs(n_tiles, tm, t, per_batch_mod)
    tiled = lambda a: a.reshape(n_tiles, tm, a.shape[-1])
    out = pl.pallas_call(
        _merge_kernel,
        grid=(n_tiles // TILES_PER_STEP,),
        in_specs=[spec] * 7 + [mod_spec, const((4, D_MODEL)), const((tm, tm))]
        + [const((D_MODEL, D_MODEL))] * 3,
        out_specs=spec,
        out_shape=jax.ShapeDtypeStruct((n_tiles, tm, D_MODEL), F32),
        compiler_params=_cparams(("parallel",)),
        name=name,
    )(*(tiled(a) for a in (x, hf, hb, zg, ga, gb, on)), mod, norm_g,
      jnp.asarray(_interleave_matrix(tm).T, BF16), lru_up, gla_up, w_out)
    return out.reshape(b, t, D_MODEL)


def _ffn_kernel(x_ref, mod_ref, ng_ref, w1_ref, w2_ref, o_ref):
    fs = D_FF // FFN_SPLIT

    def stages(i):
        v = {}

        def norm():
            x = x_ref[i]
            v['h'] = (_rms(x, ng_ref[2:3, :]) * (1.0 + mod_ref[4:5, :]) + mod_ref[3:4, :]).astype(BF16)

        def hidden(kf):
            hid = jnp.maximum(_bdot(v['h'], w1_ref[:, kf * fs:(kf + 1) * fs]), 0.0)
            part = _bdot((hid * hid).astype(BF16), w2_ref[kf * fs:(kf + 1) * fs, :])
            v['acc'] = part if kf == 0 else v['acc'] + part

        def out():
            o_ref[i] = x_ref[i] + mod_ref[5:6, :] * _rms(v['acc'], ng_ref[3:4, :])

        return [norm] + [functools.partial(hidden, kf) for kf in range(FFN_SPLIT)] + [out]

    _run_staggered([stages(i) for i in range(x_ref.shape[0])], FFN_STAGGER)


def _ffn(x, mod, norm_g, w1, w2, *, per_batch_mod, name):
    b, t, _ = x.shape
    tm = _lru_chunk(t)
    n_tiles = b * t // tm
    spec, mod_spec, const = _tile_specs(n_tiles, tm, t, per_batch_mod)
    out = pl.pallas_call(
        _ffn_kernel,
        grid=(n_tiles // TILES_PER_STEP,),
        in_specs=[spec, mod_spec, const((4, D_MODEL)), const((D_MODEL, D_FF)), const((D_FF, D_MODEL))],
        out_specs=spec,
        out_shape=jax.ShapeDtypeStruct((n_tiles, tm, D_MODEL), F32),
        compiler_params=_cparams(("parallel",)),
        name=name,
    )(x.reshape(n_tiles, tm, D_MODEL), mod, norm_g, w1, w2)
    return out.reshape(b, t, D_MODEL)


def _blockdiag_slabs(w):
    per = LRU_COLB // LRU_BW
    w = w.reshape(2, LRU_NCOLB, per, LRU_BW, LRU_BW)
    eye = jnp.eye(per, dtype=w.dtype)
    slab = jnp.einsum('dspij,pq->dspiqj', w, eye)
    return slab.reshape(2, LRU_NCOLB, LRU_COLB, LRU_COLB).astype(BF16)


def _layer_group(x, mod, p, lru_h0, gla_s0, *, latent, want_state, tag):
    b, t, _ = x.shape
    per_batch = latent
    ga, gb, zg, zx = _inproj(x, mod, p['norm_g'], p['w_rm'], (None,) * 4,
                             per_batch_mod=per_batch, name=f"inproj_rm_{tag}")
    period = GRID_W if latent else t
    (q, k, v, g, lr), hf, hb, s_lru = _cm_lru(
        x, mod, p['norm_g'], p['w_cm'], (None, None, None, _silu, None), zx, lru_h0, p['conv_w'],
        p['conv_b'], p['wa_bd'], p['wx_bd'], p['lru_ba'], p['lru_bx'], p['lru_L'], per_batch_mod=per_batch,
        order='col_major' if latent else 'natural', period=period, name=f"cm_lru_{tag}")
    res = _gla(q, k, v, g, lr, p['w2p'], p['gla_b2'], p['gla_norm_g'], gla_s0,
               want_final=want_state, hps=GLA_HEADS_PER_STEP if latent else GLA_HEADS, col_major=latent,
               name=f"gla_{tag}")
    on = res[0]
    s_gla = res[1] if want_state else None
    x1 = _merge(x, hf, hb, zg, ga, gb, on, mod, p['norm_g'], p['lru_up'], p['gla_up'], p['w_out'],
                per_batch_mod=per_batch, name=f"merge_{tag}")
    y = _ffn(x1, mod, p['norm_g'], p['mlp_w1'], p['mlp_w2'], per_batch_mod=per_batch, name=f"ffn_{tag}")
    return y, s_lru, s_gla


def kernel(x_prompt, x_sample, state_lru, state_gla, c, c_ctx, w_mod, b_mod, norm_g, w_in, conv_w, conv_b,
           lru_wa, lru_ba, lru_wx, lru_bx, lru_L, lru_up, gla_w2, gla_b2, gla_norm_g, gla_up, w_out,
           mlp_w1, mlp_w2):
    depth = w_in.shape[0]
    assert depth == 1
    dec_b = x_sample.shape[0]
    b0 = x_prompt.shape[0]
    xp, xs = x_prompt, x_sample
    lru_states, gla_states = [], []
    mod_rows = -(-(1 + dec_b) // V7X_SUBLANES) * V7X_SUBLANES
    cc = jnp.zeros((mod_rows, D_MODEL), F32).at[0].set(c_ctx).at[1:1 + dec_b].set(c)
    for l in range(depth):
        wl = w_in[l]
        dk_tot, dv_tot = GLA_HEADS * GLA_DK, GLA_HEADS * GLA_DV
        o = np.cumsum((0, D_MODEL, D_MODEL, dk_tot, dk_tot, dv_tot, dv_tot, 2 * GLA_RANK, D_MODEL, D_MODEL))
        col = lambda i: wl[:, o[i]:o[i + 1]].astype(BF16)
        lr_w = jnp.pad(col(6), ((0, 0), (0, LR_PAD - 2 * GLA_RANK)))
        w2 = gla_w2[l].astype(BF16)
        w2p = jnp.zeros((2, LR_PAD, GLA_HEADS * GLA_DK), BF16)
        w2p = w2p.at[0, 0:GLA_RANK].set(w2[0]).at[1, GLA_RANK:2 * GLA_RANK].set(w2[1])
        p = {
            'norm_g': norm_g[l],
            'w_rm': [col(7), col(8), col(1), col(0)],
            'w_cm': [col(2), col(3), col(4), col(5), lr_w],
            'conv_w': conv_w[l], 'conv_b': conv_b[l].reshape(1, D_MODEL),
            'wa_bd': _blockdiag_slabs(lru_wa[l]), 'wx_bd': _blockdiag_slabs(lru_wx[l]),
            'lru_ba': lru_ba[l], 'lru_bx': lru_bx[l], 'lru_L': lru_L[l],
            'lru_up': lru_up[l].astype(BF16),
            'w2p': w2p, 'gla_b2': gla_b2[l], 'gla_norm_g': gla_norm_g[l].reshape(1, GLA_DV),
            'gla_up': gla_up[l].astype(BF16), 'w_out': w_out[l].astype(BF16),
            'mlp_w1': mlp_w1[l].astype(BF16), 'mlp_w2': mlp_w2[l].astype(BF16),
        }
        mod = _modulation(cc, w_mod[l], b_mod[l])
        h0_ctx = jnp.zeros((b0, 2, D_MODEL), F32)
        xp, s_lru, s_gla = _layer_group(xp, mod, p, h0_ctx, None, latent=False, want_state=True,
                                        tag="ctx")
        lru_states.append(s_lru)
        gla_states.append(s_gla)
        xs, _, _ = _layer_group(xs, mod, p, state_lru[:, l], state_gla[:, l], latent=True,
                                want_state=False, tag="lat")
    return (xp, xs, jnp.stack(lru_states, axis=1), jnp.stack(gla_states, axis=1))
```

```python
import functools

import numpy as np
import jax
import jax.numpy as jnp
from jax import lax
from jax.experimental import pallas as pl
from jax.experimental.pallas import tpu as pltpu

F32 = jnp.float32
BF16 = jnp.bfloat16

D_MODEL = 1024
GRID_W = 64
LRU_BLOCKS = 16
LRU_BW = D_MODEL // LRU_BLOCKS
LRU_C = 8.0
GLA_HEADS = 4
GLA_DK = 128
GLA_DV = 256
GLA_RANK = 16
GLA_TAU = 16.0
GLA_CHUNK = 64
D_FF = 4 * D_MODEL
N_MOD = 6
EPS = 1e-6
LOG2_E = 1.4426950408889634

V7X_LANES = 128
V7X_SUBLANES = 8
V7X_MXU_DIM = 256
V7X_VMEM_BYTES = 64 * 1024 * 1024
VMEM_LIMIT = V7X_VMEM_BYTES - 8 * 1024 * 1024

LRU_COLB = V7X_MXU_DIM
LRU_NCOLB = D_MODEL // LRU_COLB
LR_PAD = V7X_LANES
COL_BLOCK = 16
GLA_BLOCK = V7X_MXU_DIM
GLA_HEADS_PER_STEP = 2
GLA_SCORE_BLOCK = 2 * GLA_CHUNK
GLA_OUT_COLS = 16
LRU_CHUNK = 256
TILES_PER_STEP = 2
MERGE_STAGGER = 2
INPROJ_STAGGER = 2
FFN_SPLIT = 4
FFN_STAGGER = 3
LRU_SCAN_CUTS = 1
CM_PIECE = 256


def _cparams(sem):
    return pltpu.CompilerParams(dimension_semantics=sem, vmem_limit_bytes=VMEM_LIMIT)


def _rms(x, g):
    ms = jnp.mean(x * x, axis=-1, keepdims=True)
    return x * lax.rsqrt(ms + EPS) * g


def _sigmoid(x):
    return 0.5 * jnp.tanh(0.5 * x) + 0.5


def _softplus(x):
    return jnp.maximum(x, 0.0) + jnp.log1p(jnp.exp(-jnp.abs(x)))


def _bdot(a, b):
    return jnp.dot(a, b, preferred_element_type=F32)


def _mod_kernel(c_ref, w_ref, b_ref, o_ref):
    c = c_ref[...]
    s = (c * _sigmoid(c)).astype(BF16)
    o_ref[...] = _bdot(s, w_ref[...].astype(BF16)) + b_ref[...]


def _modulation(cc, w_mod, b_mod):
    rows = cc.shape[0]
    return pl.pallas_call(
        _mod_kernel,
        grid=(N_MOD,),
        in_specs=[
            pl.BlockSpec((rows, D_MODEL), lambda n: (0, 0)),
            pl.BlockSpec((D_MODEL, D_MODEL), lambda n: (0, n)),
            pl.BlockSpec((1, D_MODEL), lambda n: (0, n)),
        ],
        out_specs=pl.BlockSpec((rows, D_MODEL), lambda n: (0, n)),
        out_shape=jax.ShapeDtypeStruct((rows, N_MOD * D_MODEL), F32),
        compiler_params=_cparams(("arbitrary",)),
        name="modulation",
    )(cc, w_mod, b_mod.reshape(1, N_MOD * D_MODEL)).reshape(rows, N_MOD, D_MODEL)


def _tok_view(a, grid_tile):
    if grid_tile is None:
        return a
    b, t, n = a.shape
    rows = t // GRID_W
    return a.reshape(b, rows, GRID_W // COL_BLOCK, COL_BLOCK, n)


def _tok_spec(n, tm, grid_tile):
    if grid_tile is None:
        return pl.BlockSpec((None, tm, n), lambda b, j: (b, j, 0))
    rows = grid_tile
    return pl.BlockSpec((None, rows, None, COL_BLOCK, n), lambda b, j: (b, 0, j, 0, 0))


def _const_spec(shape):
    nd = len(shape)
    return pl.BlockSpec(shape, lambda b, j: (0,) * nd)


def _mod_spec(per_batch):
    if per_batch:
        return pl.BlockSpec((None, N_MOD, D_MODEL), lambda b, j: (b + 1, 0, 0))
    return pl.BlockSpec((None, N_MOD, D_MODEL), lambda b, j: (0, 0, 0))


def _perm_matrix(rows, cols):
    n = rows * cols
    p = np.zeros((n, n), np.float32)
    for r in range(rows):
        for c in range(cols):
            p[c * rows + r, r * cols + c] = 1.0
    return p


def _silu(x):
    return x * _sigmoid(x)


def _inproj_pieces(x_ref, mod_ref, ng_ref, perm_ref, w_refs, o_refs, *, acts, n_piece, tile=None):
    v = {}

    def norm():
        x = (x_ref[...] if tile is None else x_ref[tile]).reshape(-1, D_MODEL)
        h = _rms(x, ng_ref[0:1, :]) * (1.0 + mod_ref[1:2, :]) + mod_ref[0:1, :]
        v['h'] = h.astype(BF16)
        if perm_ref is not None:
            v['h'] = _bdot(perm_ref[...], v['h']).astype(BF16)

    pieces = [norm]
    for i, (w_ref, o_ref) in enumerate(zip(w_refs, o_refs)):
        n = w_ref.shape[1]
        width = min(n, n_piece)
        for c0 in range(0, n, width):
            def piece(i=i, w_ref=w_ref, o_ref=o_ref, cs=slice(c0, c0 + width)):
                z = _bdot(v['h'], w_ref[:, cs])
                if acts[i] is not None:
                    z = acts[i](z)
                if tile is None:
                    o_ref[:, cs] = z.astype(o_ref.dtype)
                else:
                    o_ref[tile, :, cs] = z.astype(o_ref.dtype)
            pieces.append(piece)
    return pieces


def _inproj_kernel(*refs, n_out, acts):
    x_ref, mod_ref, ng_ref = refs[:3]
    _run_staggered([_inproj_pieces(x_ref, mod_ref, ng_ref, None, refs[3:3 + n_out], refs[3 + n_out:],
                                   acts=acts, n_piece=D_MODEL, tile=i)
                    for i in range(x_ref.shape[0])], INPROJ_STAGGER)


def _inproj_setup(x, weights, order):
    b, t, _ = x.shape
    if order == 'col_major':
        rows = t // GRID_W
        tm = rows * COL_BLOCK
        return _tok_view(x, rows), _tok_spec(D_MODEL, tm, rows), tm, _perm_matrix(rows, COL_BLOCK)
    tm = _lru_chunk(t)
    perm = _interleave_matrix(tm) if order == 'interleaved' else None
    return x, _tok_spec(D_MODEL, tm, None), tm, perm


def _lru_chunk(t):
    return min(LRU_CHUNK, t)


def _interleave_matrix(tc):
    return _perm_matrix(V7X_SUBLANES, tc // V7X_SUBLANES)


def _lru_pieces(zf_ref, zb_ref, rf0, rb0, il_ref, cw_ref, cb_ref, wa_ref, wx_ref, ba_ref, bx_ref, l_ref,
                hf_ref, hb_ref, fin_ref, a_s, u_s, carry_s, zi_s, *, tc, period):
    ns = V7X_SUBLANES
    sl = tc // ns
    sub_c = lax.broadcasted_iota(jnp.int32, (ns, LRU_COLB), 0)
    prev_ok = jnp.bitwise_and(sub_c * sl, period - 1) != 0
    next_ok = jnp.bitwise_and((sub_c + 1) * sl, period - 1) != 0

    def reorder(d, z_ref, r0):
        zi_s[d] = _bdot(il_ref[...], z_ref[r0:r0 + tc, :]).astype(BF16)

    def gates(d, cb):
        cs = slice(cb * LRU_COLB, (cb + 1) * LRU_COLB)
        x = zi_s[d, :, cs].astype(F32)
        e_prev = jnp.where(prev_ok, pltpu.roll(x[tc - ns:], 1, 0), 0.0)
        e_next0 = jnp.where(next_ok, pltpu.roll(x[:ns], ns - 1, 0), 0.0)
        e_next1 = jnp.where(next_ok, pltpu.roll(x[ns:2 * ns], ns - 1, 0), 0.0)
        xm1 = jnp.concatenate([e_prev, x[:tc - ns]], axis=0)
        xp1 = jnp.concatenate([x[ns:], e_next0], axis=0)
        xp2 = jnp.concatenate([x[2 * ns:], e_next0, e_next1], axis=0)
        cw = 0.5 * cw_ref[:, cs]
        xh = 0.5 * cb_ref[0:1, cs] + xm1 * cw[0:1] + x * cw[1:2] + xp1 * cw[2:3] + xp2 * cw[3:4]
        xhb = xh.astype(BF16)
        tr = jnp.tanh(_bdot(xhb, wa_ref[d, cb]) + 0.5 * ba_ref[d:d + 1, cs])
        ti = jnp.tanh(_bdot(xhb, wx_ref[d, cb]) + 0.5 * bx_ref[d:d + 1, cs])
        k2 = (-0.5 * LRU_C * LOG2_E) * _softplus(-l_ref[d:d + 1, cs])
        a = jnp.exp2(tr * k2 + k2)
        y = 1.0 - a * a
        u = jnp.where(y > 0.0, y * lax.rsqrt(y), 0.0) * ((ti + 1.0) * xh)
        a_s[d, :, cs] = a
        u_s[d, :, cs] = u

    state = {}
    n_cut = LRU_SCAN_CUTS

    def pass1(part):
        if part == 0:
            zeros = jnp.zeros((ns, D_MODEL), F32)
            ones = jnp.ones((ns, D_MODEL), F32)
            state['c'] = (zeros, ones, zeros, ones)
        hf, af, hb, ab = state['c']
        for i in range(part * sl // n_cut, (part + 1) * sl // n_cut):
            rf = slice(i * ns, (i + 1) * ns)
            rb = slice((sl - 1 - i) * ns, (sl - i) * ns)
            a = a_s[0, rf, :]
            hf = a * hf + u_s[0, rf, :]
            af = a * af
            u_s[0, rf, :] = hf
            a_s[0, rf, :] = af
            a = a_s[1, rb, :]
            hb = a * hb + u_s[1, rb, :]
            ab = a * ab
            u_s[1, rb, :] = hb
            a_s[1, rb, :] = ab
        state['c'] = (hf, af, hb, ab)

    def chain():
        hf, af, hb, ab = state['c']
        sub = lax.broadcasted_iota(jnp.int32, (ns, D_MODEL), 0)

        def sublane_scan(a, u, forward):
            for s in (1, 2, 4):
                if forward:
                    m = sub >= s
                    shift = s
                else:
                    m = sub < ns - s
                    shift = ns - s
                a_sh = jnp.where(m, pltpu.roll(a, shift, 0), 1.0)
                u_sh = jnp.where(m, pltpu.roll(u, shift, 0), 0.0)
                u = a * u_sh + u
                a = a * a_sh
            return a, u

        cin_f = carry_s[0:1, :]
        cin_b = carry_s[1:2, :]
        af, hf = sublane_scan(af, hf, True)
        ab, hb = sublane_scan(ab, hb, False)
        fin_f = hf + af * cin_f
        fin_b = hb + ab * cin_b
        enter_f = jnp.where(sub == 0, cin_f, pltpu.roll(fin_f, 1, 0))
        enter_b = jnp.where(sub == ns - 1, cin_b, pltpu.roll(fin_b, ns - 1, 0))
        out_f = fin_f[ns - 1:ns, :]
        out_b = fin_b[0:1, :]
        carry_s[0:1, :] = out_f
        carry_s[1:2, :] = out_b
        fin_ref[0:1, :] = out_f
        fin_ref[1:2, :] = out_b
        state['e'] = (jnp.concatenate([enter_f, enter_f], axis=0), jnp.concatenate([enter_b, enter_b], axis=0))

    def pass2(part):
        pk = 2 * ns
        enter_f2, enter_b2 = state['e']
        for i in range(part * (sl // 2) // n_cut, (part + 1) * (sl // 2) // n_cut):
            rr = slice(i * pk, (i + 1) * pk)
            hf_ref[rf0 + i * pk:rf0 + (i + 1) * pk, :] = (
                u_s[0, rr, :] + a_s[0, rr, :] * enter_f2).astype(hf_ref.dtype)
            hb_ref[rb0 + i * pk:rb0 + (i + 1) * pk, :] = (
                u_s[1, rr, :] + a_s[1, rr, :] * enter_b2).astype(hb_ref.dtype)

    scan = ([functools.partial(pass1, part) for part in range(n_cut)] + [chain]
            + [functools.partial(pass2, part) for part in range(n_cut)])

    pieces = [functools.partial(reorder, 0, zf_ref, rf0), functools.partial(reorder, 1, zb_ref, rb0)]
    pieces += [functools.partial(gates, d, cb) for d in range(2) for cb in range(LRU_NCOLB)]
    return pieces + scan


def _interleave(major, minor):
    out, done = [], 0
    for i, p in enumerate(major):
        want = -(-(i + 1) * len(minor) // len(major))
        out.extend(minor[done:want])
        done = want
        out.append(p)
    return out


def _cm_lru_kernel(*refs, n_cm, permute, acts, tc, period, cps):
    x_ref, mod_ref, ng_ref = refs[:3]
    k = 4 if permute else 3
    perm_ref = refs[3] if permute else None
    w_refs = refs[k:k + n_cm]
    k += n_cm
    zf_ref, zb_ref, h0_ref, il_ref, cw_ref, cb_ref, wa_ref, wx_ref, ba_ref, bx_ref, l_ref = refs[k:k + 11]
    k += 11
    o_refs = refs[k:k + n_cm]
    k += n_cm
    hf_ref, hb_ref, fin_ref, a_s, u_s, carry_s, zi_s = refs[k:]

    @pl.when(pl.program_id(1) == 0)
    def _():
        carry_s[...] = h0_ref[...]

    cm = _inproj_pieces(x_ref, mod_ref, ng_ref, perm_ref, w_refs, o_refs, acts=acts, n_piece=CM_PIECE)
    lru = []
    for s in range(cps):
        lru += _lru_pieces(zf_ref, zb_ref, s * tc, (cps - 1 - s) * tc, il_ref, cw_ref, cb_ref, wa_ref, wx_ref,
                           ba_ref, bx_ref, l_ref, hf_ref, hb_ref, fin_ref, a_s, u_s, carry_s, zi_s,
                           tc=tc, period=period)
    cm[0]()
    for piece in _interleave(lru, cm[1:]):
        piece()


def _cm_lru(x, mod, norm_g, weights, acts, zx, h0, conv_w, conv_b, wa_bd, wx_bd, ba, bx, lru_l, *,
            per_batch_mod, order, period, name):
    b, t, _ = x.shape
    x_in, x_spec, tm, perm = _inproj_setup(x, weights, order)
    extra = [] if perm is None else [jnp.asarray(perm, BF16)]
    n_steps = t // tm
    tc = _lru_chunk(t)
    cps = t // tc // n_steps
    assert cps >= 1 and cps * tc * n_steps == t
    assert period % (tc // V7X_SUBLANES) == 0 and tc % period == 0
    blk = cps * tc
    res = pl.pallas_call(
        functools.partial(_cm_lru_kernel, n_cm=len(weights), permute=perm is not None, acts=tuple(acts),
                          tc=tc, period=period, cps=cps),
        grid=(b, n_steps),
        in_specs=[x_spec, _mod_spec(per_batch_mod), _const_spec((4, D_MODEL))]
        + [_const_spec(e.shape) for e in extra] + [_const_spec(w.shape) for w in weights]
        + [
            pl.BlockSpec((None, blk, D_MODEL), lambda bb, j: (bb, j, 0)),
            pl.BlockSpec((None, blk, D_MODEL), lambda bb, j: (bb, n_steps - 1 - j, 0)),
            pl.BlockSpec((None, 2, D_MODEL), lambda bb, j: (bb, 0, 0)),
            _const_spec((tc, tc)), _const_spec((4, D_MODEL)), _const_spec((1, D_MODEL)),
            _const_spec((2, LRU_NCOLB, LRU_COLB, LRU_COLB)), _const_spec((2, LRU_NCOLB, LRU_COLB, LRU_COLB)),
            _const_spec((2, D_MODEL)), _const_spec((2, D_MODEL)), _const_spec((2, D_MODEL)),
        ],
        out_specs=[_tok_spec(w.shape[1], tm, None) for w in weights] + [
            pl.BlockSpec((None, blk, D_MODEL), lambda bb, j: (bb, j, 0)),
            pl.BlockSpec((None, blk, D_MODEL), lambda bb, j: (bb, n_steps - 1 - j, 0)),
            pl.BlockSpec((None, 2, D_MODEL), lambda bb, j: (bb, 0, 0)),
        ],
        out_shape=[jax.ShapeDtypeStruct((b, t, w.shape[1]), BF16) for w in weights] + [
            jax.ShapeDtypeStruct((b, t, D_MODEL), BF16),
            jax.ShapeDtypeStruct((b, t, D_MODEL), BF16),
            jax.ShapeDtypeStruct((b, 2, D_MODEL), F32),
        ],
        scratch_shapes=[
            pltpu.VMEM((2, tc, D_MODEL), F32),
            pltpu.VMEM((2, tc, D_MODEL), F32),
            pltpu.VMEM((2, D_MODEL), F32),
            pltpu.VMEM((2, tc, D_MODEL), BF16),
        ],
        compiler_params=_cparams(("parallel", "arbitrary")),
        name=name,
    )(x_in, mod, norm_g, *extra, *weights, zx, zx, h0, jnp.asarray(_interleave_matrix(tc), BF16),
      conv_w, conv_b, wa_bd, wx_bd, ba, bx, lru_l)
    return res[:len(weights)], res[len(weights)], res[len(weights) + 1], res[len(weights) + 2]


def _gla_kernel(*refs, t, hps, has_s0, want_final, grid_rows):
    q_ref, k_ref, v_ref, g_ref, lr_ref, w2_ref, b2_ref, gn_ref = refs[:8]
    n = 8
    if grid_rows is not None:
        perm_ref = refs[n]
        n += 1
    if has_s0:
        s0_ref = refs[n]
        n += 1
    on_ref = refs[n]
    n += 1
    if want_final:
        fin_ref = refs[n]
        n += 1
    oacc, st, q2_s, k2_s, dc_s, pp_q, pp_k, pp_qx, pp_kx = refs[n:]

    c = GLA_CHUNK
    blk = min(GLA_BLOCK, t)
    for d in range(2):
        for hh in range(hps):
            if has_s0:
                st[d, hh] = s0_ref[d, hh]
            else:
                st[d, hh] = jnp.zeros((GLA_DK, GLA_DV), F32)

    ti = lax.broadcasted_iota(jnp.int32, (blk, blk), 0)
    si = lax.broadcasted_iota(jnp.int32, (blk, blk), 1)
    same = (ti // c) == (si // c)
    cums = ((same & (si <= ti)).astype(BF16), (same & (si >= ti)).astype(BF16))
    sb = min(GLA_SCORE_BLOCK, blk)
    tj = lax.broadcasted_iota(jnp.int32, (sb, sb), 0)
    sj = lax.broadcasted_iota(jnp.int32, (sb, sb), 1)
    same_j = (tj // c) == (sj // c)
    masks = (same_j & (sj <= tj), same_j & (sj >= tj))
    scale = GLA_DK ** -0.5

    def step(ia, slot_a, ib, slot_b):
        nd = hps * GLA_DK
        if ia is not None:
            rows_a = pl.ds(pl.multiple_of(ia * blk, blk), blk)
            lrb = lr_ref[rows_a, :]
            pre = [_bdot(lrb, w2_ref[d]) + b2_ref[d:d + 1, :] for d in range(2)]
        if ib is not None:
            rows_b = [pl.ds(pl.multiple_of(ib * blk + g * sb, sb), sb) for g in range(blk // sb)]
            loc_b = [slice(g * sb, (g + 1) * sb) for g in range(blk // sb)]
            nt = (((1,), (1,)), ((), ()))
            raw = [[[lax.dot_general(pp_q[slot_b, d, loc, hh * GLA_DK:(hh + 1) * GLA_DK],
                                     pp_k[slot_b, d, loc, hh * GLA_DK:(hh + 1) * GLA_DK], nt,
                                     preferred_element_type=F32) for d in range(2)]
                    + [lax.dot_general(pp_qx[slot_b, loc, hh * 2 * GLA_DK:(hh + 1) * 2 * GLA_DK],
                                       pp_kx[slot_b, loc, hh * 2 * GLA_DK:(hh + 1) * 2 * GLA_DK], nt,
                                       preferred_element_type=F32)]
                    for hh in range(hps)] for loc in loc_b]
        if ia is not None:
            bcum = []
            for d in range(2):
                soft = jnp.log(1.0 + jnp.exp2(jnp.abs(pre[d]) * (-LOG2_E)))
                la = jnp.minimum(pre[d], 0.0) * (LOG2_E / GLA_TAU) - soft * (LOG2_E / GLA_TAU)
                hi = la.astype(BF16)
                mid = (la - hi.astype(F32)).astype(BF16)
                bcum.append(_bdot(cums[d], hi) + _bdot(cums[d], mid))
        if ib is not None:
            for g, rows in enumerate(rows_b):
                for hh in range(hps):
                    vs = slice(hh * GLA_DV, (hh + 1) * GLA_DV)
                    sd0, sd1, sx = raw[g][hh]
                    sc = jnp.where(masks[0], sd0, 0.0) + jnp.where(masks[1], sd1, 0.0) + sx
                    oacc[rows, vs] = _bdot(sc.astype(BF16), v_ref[rows, vs])
        if ia is not None:
            q = q_ref[rows_a, :].astype(F32)
            k = k_ref[rows_a, :].astype(F32)
            nc = blk // c
            crow = lambda a, n: a[n * c:(n + 1) * c]
            cat = lambda parts: jnp.concatenate(parts, axis=0)
            zero = jnp.zeros((c, nd), BF16)
            qx, kx = [], []
            for d in range(2):
                edge = c - 1 if d == 0 else 0
                tot = [bcum[d][n * c + edge:n * c + edge + 1, :] for n in range(nc)]
                dec = [jnp.exp2(x) for x in tot]
                btot = cat([jnp.broadcast_to(x, (c, nd)) for x in tot])
                qe = (q * scale) * jnp.exp2(bcum[d])
                kt = k * jnp.exp2(btot - bcum[d])
                pp_q[slot_a, d] = qe.astype(BF16)
                pp_k[slot_a, d] = (k * jnp.exp2(-bcum[d])).astype(BF16)
                first = [(n % 2 == 0) == (d == 0) for n in range(nc)]
                q2 = cat([crow(qe, n) if first[n] else crow(qe, n) * dec[n ^ 1] for n in range(nc)])
                k2 = cat([crow(kt, n) * dec[n ^ 1] if first[n] else crow(kt, n) for n in range(nc)])
                q2_s[d, rows_a, :] = q2.astype(BF16)
                k2_s[d, rows_a, :] = k2.astype(BF16)
                for m in range(nc // 2):
                    dec2 = jnp.exp2(tot[2 * m] + tot[2 * m + 1])
                    for hh in range(hps):
                        row = dec2[:, hh * GLA_DK:(hh + 1) * GLA_DK]
                        dc_s[d, ia * (nc // 2) + m, hh] = jnp.broadcast_to(row, (GLA_DK, GLA_DK)).T
                qeb, ktb = qe.astype(BF16), kt.astype(BF16)
                qx.append(cat([zero if first[n] else crow(qeb, n) for n in range(nc)]))
                kx.append(cat([crow(ktb, n) if first[n] else zero for n in range(nc)]))
            for hh in range(hps):
                ks = slice(hh * GLA_DK, (hh + 1) * GLA_DK)
                xs = slice(hh * 2 * GLA_DK, (hh + 1) * 2 * GLA_DK)
                pp_qx[slot_a, :, xs] = jnp.concatenate([qx[0][:, ks], qx[1][:, ks]], axis=1)
                pp_kx[slot_a, :, xs] = jnp.concatenate([kx[0][:, ks], kx[1][:, ks]], axis=1)

    n_blocks = t // blk
    step(0, 0, None, None)
    for ib in range(n_blocks - 1):
        step(ib + 1, (ib + 1) % 2, ib, ib % 2)
    step(None, None, n_blocks - 1, (n_blocks - 1) % 2)

    pr = 2 * c
    n_prs = t // pr

    def pair_step(p0, d, hh):
        rows = pl.ds(p0, pr)
        ks = slice(hh * GLA_DK, (hh + 1) * GLA_DK)
        vs = slice(hh * GLA_DV, (hh + 1) * GLA_DV)
        s_t = st[d, hh]
        o = _bdot(q2_s[d, rows, ks], s_t.astype(BF16))
        upd = lax.dot_general(k2_s[d, rows, ks], v_ref[rows, vs], (((0,), (0,)), ((), ())),
                              preferred_element_type=F32)
        dcol = dc_s[d, p0 // pr, hh]
        st[d, hh] = s_t * jnp.concatenate([dcol] * (GLA_DV // GLA_DK), axis=1) + upd
        return o

    for jj in range(n_prs):
        for hh in range(hps):
            vs = slice(hh * GLA_DV, (hh + 1) * GLA_DV)
            for p0, d in ((jj * pr, 0), ((n_prs - 1 - jj) * pr, 1)):
                oacc[pl.ds(p0, pr), vs] += pair_step(p0, d, hh)

    if grid_rows is None:
        ob = min(GLA_BLOCK, t)
    else:
        ob = GLA_OUT_COLS * grid_rows

    for ib in range(t // ob):
        rows = slice(ib * ob, (ib + 1) * ob)
        parts = []
        for hh in range(hps):
            vs = slice(hh * GLA_DV, (hh + 1) * GLA_DV)
            parts.append((_rms(oacc[rows, vs], gn_ref[...]) * g_ref[rows, vs].astype(F32)).astype(BF16))
        on = jnp.concatenate(parts, axis=1)
        if grid_rows is None:
            on_ref[rows, :] = on
        else:
            on = _bdot(perm_ref[...], on).astype(BF16)
            for r in range(grid_rows):
                dst = r * GRID_W + ib * GLA_OUT_COLS
                on_ref[dst:dst + GLA_OUT_COLS, :] = on[r * GLA_OUT_COLS:(r + 1) * GLA_OUT_COLS]

    if want_final:
        for d in range(2):
            for hh in range(hps):
                fin_ref[d, hh] = st[d, hh]


def _gla(q, k, v, g, lr, w2p, b2, gn, s0, *, want_final, hps, col_major, name):
    b, t, _ = q.shape
    ng = GLA_HEADS // hps
    has_s0 = s0 is not None
    grid_rows = t // GRID_W if col_major else None
    in_specs = [
        pl.BlockSpec((None, t, hps * GLA_DK), lambda bb, h: (bb, 0, h)),
        pl.BlockSpec((None, t, hps * GLA_DK), lambda bb, h: (bb, 0, h)),
        pl.BlockSpec((None, t, hps * GLA_DV), lambda bb, h: (bb, 0, h)),
        pl.BlockSpec((None, t, hps * GLA_DV), lambda bb, h: (bb, 0, h)),
        pl.BlockSpec((None, t, LR_PAD), lambda bb, h: (bb, 0, 0)),
        pl.BlockSpec((2, LR_PAD, hps * GLA_DK), lambda bb, h: (0, 0, h)),
        pl.BlockSpec((2, hps * GLA_DK), lambda bb, h: (0, h)),
        pl.BlockSpec((1, GLA_DV), lambda bb, h: (0, 0)),
    ]
    args = [q, k, v, g, lr, w2p, b2, gn]
    if col_major:
        ob = GLA_OUT_COLS * grid_rows
        in_specs.append(pl.BlockSpec((ob, ob), lambda bb, h: (0, 0)))
        args.append(jnp.asarray(_perm_matrix(grid_rows, GLA_OUT_COLS).T, BF16))
    state_spec = pl.BlockSpec((None, 2, hps, GLA_DK, GLA_DV), lambda bb, h: (bb, 0, h, 0, 0))
    if has_s0:
        in_specs.append(state_spec)
        args.append(s0)
    out_specs = [pl.BlockSpec((None, t, hps * GLA_DV), lambda bb, h: (bb, 0, h))]
    out_shape = [jax.ShapeDtypeStruct((b, t, GLA_HEADS * GLA_DV), BF16)]
    if want_final:
        out_specs.append(state_spec)
        out_shape.append(jax.ShapeDtypeStruct((b, 2, GLA_HEADS, GLA_DK, GLA_DV), F32))
    return pl.pallas_call(
        functools.partial(_gla_kernel, t=t, hps=hps, has_s0=has_s0, want_final=want_final,
                          grid_rows=grid_rows),
        grid=(b, ng),
        in_specs=in_specs,
        out_specs=out_specs,
        out_shape=out_shape,
        scratch_shapes=[
            pltpu.VMEM((t, hps * GLA_DV), F32),
            pltpu.VMEM((2, hps, GLA_DK, GLA_DV), F32),
            pltpu.VMEM((2, t, hps * GLA_DK), BF16),
            pltpu.VMEM((2, t, hps * GLA_DK), BF16),
            pltpu.VMEM((2, t // (2 * GLA_CHUNK), hps, GLA_DK, GLA_DK), F32),
            pltpu.VMEM((2, 2, min(GLA_BLOCK, t), hps * GLA_DK), BF16),
            pltpu.VMEM((2, 2, min(GLA_BLOCK, t), hps * GLA_DK), BF16),
            pltpu.VMEM((2, min(GLA_BLOCK, t), 2 * hps * GLA_DK), BF16),
            pltpu.VMEM((2, min(GLA_BLOCK, t), 2 * hps * GLA_DK), BF16),
        ],
        compiler_params=_cparams(("parallel", "parallel")),
        name=name,
    )(*args)


def _gelu_tanh(x):
    return x * (0.5 * (1.0 + jnp.tanh(0.7978845608028654 * (x + 0.044715 * (x * x * x)))))


def _run_staggered(tiles, stagger):
    n_st = len(tiles[0])
    for clock in range(n_st + stagger * (len(tiles) - 1)):
        for i, st in enumerate(tiles):
            k = clock - stagger * i
            if 0 <= k < n_st:
                st[k]()


def _tile_specs(n_tiles, tm, t, per_batch_mod):
    tps = TILES_PER_STEP
    assert n_tiles % tps == 0 and (t % (tps * tm) == 0 or not per_batch_mod)
    spec = pl.BlockSpec((tps, tm, D_MODEL), lambda i: (i, 0, 0))
    const = lambda shape: pl.BlockSpec(shape, lambda i: (0,) * len(shape))
    if per_batch_mod:
        mod_spec = pl.BlockSpec((None, N_MOD, D_MODEL), lambda i: ((i * tps * tm) // t + 1, 0, 0))
    else:
        mod_spec = pl.BlockSpec((None, N_MOD, D_MODEL), lambda i: (0, 0, 0))
    return spec, mod_spec, const


def _merge_kernel(x_ref, hf_ref, hb_ref, zg_ref, ga_ref, gb_ref, on_ref, mod_ref, ng_ref,
                  unperm_ref, lup_ref, gup_ref, wo_ref, o_ref):
    def stages(i):
        ld = lambda r: r[i].astype(F32)
        v = {}

        def s0():
            v['hs'] = (ld(hf_ref) + ld(hb_ref)).astype(BF16)
            v['gz'] = _gelu_tanh(ld(zg_ref))

        def s1():
            hs = _bdot(unperm_ref[...], v['hs'])
            v['y_a'] = _bdot((hs * v['gz']).astype(BF16), lup_ref[...])

        def s2():
            v['y_b'] = _bdot(on_ref[i], gup_ref[...])

        def s3():
            v['mm'] = (_sigmoid(ld(ga_ref)) * v['y_a'] + _sigmoid(ld(gb_ref)) * v['y_b']).astype(BF16)

        def s4():
            v['m'] = _bdot(v['mm'], wo_ref[...])

        def s5():
            o_ref[i] = x_ref[i] + mod_ref[2:3, :] * _rms(v['m'], ng_ref[1:2, :])

        return [s0, s1, s2, s3, s4, s5]

    _run_staggered([stages(i) for i in range(x_ref.shape[0])], MERGE_STAGGER)


def _merge(x, hf, hb, zg, ga, gb, on, mod, norm_g, lru_up, gla_up, w_out, *, per_batch_mod, name):
    b, t, _ = x.shape
    tm = _lru_chunk(t)
    n_tiles = b * t // tm
    spec, mod_spec, const = _tile_specs(n_tiles, tm, t, per_batch_mod)
    tiled = lambda a: a.reshape(n_tiles, tm, a.shape[-1])
    out = pl.pallas_call(
        _merge_kernel,
        grid=(n_tiles // TILES_PER_STEP,),
        in_specs=[spec] * 7 + [mod_spec, const((4, D_MODEL)), const((tm, tm))]
        + [const((D_MODEL, D_MODEL))] * 3,
        out_specs=spec,
        out_shape=jax.ShapeDtypeStruct((n_tiles, tm, D_MODEL), F32),
        compiler_params=_cparams(("parallel",)),
        name=name,
    )(*(tiled(a) for a in (x, hf, hb, zg, ga, gb, on)), mod, norm_g,
      jnp.asarray(_interleave_matrix(tm).T, BF16), lru_up, gla_up, w_out)
    return out.reshape(b, t, D_MODEL)


def _ffn_kernel(x_ref, mod_ref, ng_ref, w1_ref, w2_ref, o_ref):
    fs = D_FF // FFN_SPLIT

    def stages(i):
        v = {}

        def norm():
            x = x_ref[i]
            v['h'] = (_rms(x, ng_ref[2:3, :]) * (1.0 + mod_ref[4:5, :]) + mod_ref[3:4, :]).astype(BF16)

        def hidden(kf):
            hid = jnp.maximum(_bdot(v['h'], w1_ref[:, kf * fs:(kf + 1) * fs]), 0.0)
            part = _bdot((hid * hid).astype(BF16), w2_ref[kf * fs:(kf + 1) * fs, :])
            v['acc'] = part if kf == 0 else v['acc'] + part

        def out():
            o_ref[i] = x_ref[i] + mod_ref[5:6, :] * _rms(v['acc'], ng_ref[3:4, :])

        return [norm] + [functools.partial(hidden, kf) for kf in range(FFN_SPLIT)] + [out]

    _run_staggered([stages(i) for i in range(x_ref.shape[0])], FFN_STAGGER)


def _both_kernel(x0_ref, x1_ref, mod_ref, *rest, inner, n_const, n_out, off):
    consts = rest[:n_const]
    outs = rest[n_const:]
    i = pl.program_id(0)

    @pl.when(i < off)
    def _():
        inner(x0_ref, mod_ref, *consts, *outs[:n_out])

    @pl.when(i >= off)
    def _():
        inner(x1_ref, mod_ref, *consts, *outs[n_out:])


def _both_groups(inner, x_ctx, x_lat, mod, consts, widths, dtype, name):
    tm = _lru_chunk(x_ctx.shape[1])
    assert tm == _lru_chunk(x_lat.shape[1])
    tps = TILES_PER_STEP
    t1 = x_lat.shape[1]
    tiles = [x.shape[0] * x.shape[1] // tm for x in (x_ctx, x_lat)]
    assert all(n % tps == 0 for n in tiles) and t1 % (tps * tm) == 0
    steps = [n // tps for n in tiles]
    off = steps[0]
    idx = (lambda i: jnp.minimum(i, off - 1), lambda i: jnp.maximum(i - off, 0))
    const = lambda shape: pl.BlockSpec(shape, lambda i: (0,) * len(shape))
    mod_spec = pl.BlockSpec((None, N_MOD, D_MODEL),
                            lambda i: (jnp.where(i < off, 0, (jnp.maximum(i - off, 0) * tps * tm) // t1 + 1), 0, 0))
    tile_spec = lambda g, n: pl.BlockSpec((tps, tm, n), lambda i: (idx[g](i), 0, 0))
    outs = pl.pallas_call(
        functools.partial(_both_kernel, inner=inner, n_const=len(consts), n_out=len(widths), off=off),
        grid=(steps[0] + steps[1],),
        in_specs=[tile_spec(0, D_MODEL), tile_spec(1, D_MODEL), mod_spec] + [const(c.shape) for c in consts],
        out_specs=[tile_spec(g, n) for g in range(2) for n in widths],
        out_shape=[jax.ShapeDtypeStruct((tiles[g], tm, n), dtype) for g in range(2) for n in widths],
        compiler_params=_cparams(("arbitrary",)),
        name=name,
    )(x_ctx.reshape(tiles[0], tm, D_MODEL), x_lat.reshape(tiles[1], tm, D_MODEL), mod, *consts)
    shaped = [o.reshape(x.shape[0], x.shape[1], o.shape[-1])
              for x, group in ((x_ctx, outs[:len(widths)]), (x_lat, outs[len(widths):])) for o in group]
    return shaped[:len(widths)], shaped[len(widths):]


def _blockdiag_slabs(w):
    per = LRU_COLB // LRU_BW
    w = w.reshape(2, LRU_NCOLB, per, LRU_BW, LRU_BW)
    eye = jnp.eye(per, dtype=w.dtype)
    slab = jnp.einsum('dspij,pq->dspiqj', w, eye)
    return slab.reshape(2, LRU_NCOLB, LRU_COLB, LRU_COLB).astype(BF16)


def _mixers(x, proj, mod, p, lru_h0, gla_s0, *, latent, want_state, tag):
    b, t, _ = x.shape
    ga, gb, zg, zx = proj
    period = GRID_W if latent else t
    (q, k, v, g, lr), hf, hb, s_lru = _cm_lru(
        x, mod, p['norm_g'], p['w_cm'], (None, None, None, _silu, None), zx, lru_h0, p['conv_w'],
        p['conv_b'], p['wa_bd'], p['wx_bd'], p['lru_ba'], p['lru_bx'], p['lru_L'], per_batch_mod=latent,
        order='col_major' if latent else 'natural', period=period, name=f"cm_lru_{tag}")
    res = _gla(q, k, v, g, lr, p['w2p'], p['gla_b2'], p['gla_norm_g'], gla_s0,
               want_final=want_state, hps=GLA_HEADS_PER_STEP if latent else GLA_HEADS, col_major=latent,
               name=f"gla_{tag}")
    x1 = _merge(x, hf, hb, zg, ga, gb, res[0], mod, p['norm_g'], p['lru_up'], p['gla_up'], p['w_out'],
                per_batch_mod=latent, name=f"merge_{tag}")
    return x1, s_lru, (res[1] if want_state else None)


def kernel(x_prompt, x_sample, state_lru, state_gla, c, c_ctx, w_mod, b_mod, norm_g, w_in, conv_w, conv_b,
           lru_wa, lru_ba, lru_wx, lru_bx, lru_L, lru_up, gla_w2, gla_b2, gla_norm_g, gla_up, w_out,
           mlp_w1, mlp_w2):
    depth = w_in.shape[0]
    assert depth == 1
    dec_b = x_sample.shape[0]
    b0 = x_prompt.shape[0]
    xp, xs = x_prompt, x_sample
    lru_states, gla_states = [], []
    mod_rows = -(-(1 + dec_b) // V7X_SUBLANES) * V7X_SUBLANES
    cc = jnp.zeros((mod_rows, D_MODEL), F32).at[0].set(c_ctx).at[1:1 + dec_b].set(c)
    for l in range(depth):
        wl = w_in[l]
        dk_tot, dv_tot = GLA_HEADS * GLA_DK, GLA_HEADS * GLA_DV
        o = np.cumsum((0, D_MODEL, D_MODEL, dk_tot, dk_tot, dv_tot, dv_tot, 2 * GLA_RANK, D_MODEL, D_MODEL))
        col = lambda i: wl[:, o[i]:o[i + 1]].astype(BF16)
        lr_w = jnp.pad(col(6), ((0, 0), (0, LR_PAD - 2 * GLA_RANK)))
        w2 = gla_w2[l].astype(BF16)
        w2p = jnp.zeros((2, LR_PAD, GLA_HEADS * GLA_DK), BF16)
        w2p = w2p.at[0, 0:GLA_RANK].set(w2[0]).at[1, GLA_RANK:2 * GLA_RANK].set(w2[1])
        p = {
            'norm_g': norm_g[l],
            'w_rm': [col(7), col(8), col(1), col(0)],
            'w_cm': [col(2), col(3), col(4), col(5), lr_w],
            'conv_w': conv_w[l], 'conv_b': conv_b[l].reshape(1, D_MODEL),
            'wa_bd': _blockdiag_slabs(lru_wa[l]), 'wx_bd': _blockdiag_slabs(lru_wx[l]),
            'lru_ba': lru_ba[l], 'lru_bx': lru_bx[l], 'lru_L': lru_L[l],
            'lru_up': lru_up[l].astype(BF16),
            'w2p': w2p, 'gla_b2': gla_b2[l], 'gla_norm_g': gla_norm_g[l].reshape(1, GLA_DV),
            'gla_up': gla_up[l].astype(BF16), 'w_out': w_out[l].astype(BF16),
            'mlp_w1': mlp_w1[l].astype(BF16), 'mlp_w2': mlp_w2[l].astype(BF16),
        }
        mod = _modulation(cc, w_mod[l], b_mod[l])
        h0_ctx = jnp.zeros((b0, 2, D_MODEL), F32)
        rm_widths = [w.shape[1] for w in p['w_rm']]
        proj_p, proj_s = _both_groups(functools.partial(_inproj_kernel, n_out=len(rm_widths), acts=(None,) * 4),
                                      xp, xs, mod, [p['norm_g']] + p['w_rm'], rm_widths, BF16, "inproj_rm")
        xp, s_lru, s_gla = _mixers(xp, proj_p, mod, p, h0_ctx, None, latent=False, want_state=True, tag="ctx")
        lru_states.append(s_lru)
        gla_states.append(s_gla)
        xs, _, _ = _mixers(xs, proj_s, mod, p, state_lru[:, l], state_gla[:, l], latent=True,
                           want_state=False, tag="lat")
        (xp,), (xs,) = _both_groups(_ffn_kernel, xp, xs, mod, [p['norm_g'], p['mlp_w1'], p['mlp_w2']],
                                    [D_MODEL], F32, "ffn")
    return (xp, xs, jnp.stack(lru_states, axis=1), jnp.stack(gla_states, axis=1))
```
